```python
import math
import jax, jax.numpy as jnp
from jax import lax
import numpy as np

D_MODEL = 1024
BATCH = 2
SEQ = 8192
DEPTH = 1
DEC_BATCH = 16
DEC_SEQ = 4096
PAST_LEN = 128

DA_HEADS = 4
DA_HEAD_DIM = 64
DA_VDIM = 2 * DA_HEAD_DIM
DA_QK_WIDTH = DA_HEADS * 2 * DA_HEAD_DIM
DA_WIDTH = DA_HEADS * DA_VDIM
Q_BLOCK = 128
ROPE_THETA = 10000.0

GLA_HEADS = 4
GLA_KDIM = 64
GLA_VDIM = 128
GLA_KEY_WIDTH = GLA_HEADS * GLA_KDIM
GLA_WIDTH = GLA_HEADS * GLA_VDIM
GLA_GATE_RANK = 16
GLA_GATE_NORMALIZER = 16.0
GLA_CHUNK = 64

N_EXPERTS = 32
TOP_K = 4
D_FF = D_MODEL
SWIGLU_ALPHA = 1.702
SWIGLU_LIMIT = 7.0
EXPERT_BLOCK = 128

NORM_EPS = 1e-5

SPLIT_SIZES = (DA_QK_WIDTH, DA_QK_WIDTH, DA_WIDTH,
               GLA_KEY_WIDTH, GLA_KEY_WIDTH, GLA_WIDTH, GLA_WIDTH,
               GLA_GATE_RANK, GLA_GATE_RANK,
               D_MODEL, D_MODEL)
IN_WIDTH = sum(SPLIT_SIZES)
SPLIT_POINTS = tuple(sum(SPLIT_SIZES[:i + 1]) for i in range(len(SPLIT_SIZES) - 1))

kernel_name = 'hybrid_diffattn_gla_moe_encoder'


def rmsnorm(x, g):
    xf = x.astype(jnp.float32)
    y = xf * lax.rsqrt(jnp.mean(xf * xf, axis=-1, keepdims=True) + NORM_EPS)
    return (y * g.astype(jnp.float32)).astype(x.dtype)


def apply_rope(x):
    L, d = x.shape[1], x.shape[-1]
    inv = ROPE_THETA ** (-jnp.arange(0, d, 2, dtype=jnp.float32) / d)
    ang = jnp.arange(L, dtype=jnp.float32)[:, None] * inv[None, :]
    cos = jnp.concatenate([jnp.cos(ang), jnp.cos(ang)], -1)[None, :, None, None, :]
    sin = jnp.concatenate([jnp.sin(ang), jnp.sin(ang)], -1)[None, :, None, None, :]
    xf = x.astype(jnp.float32)
    x1, x2 = xf[..., : d // 2], xf[..., d // 2:]
    rot = jnp.concatenate([-x2, x1], -1)
    return (xf * cos + rot * sin).astype(x.dtype)


def diff_attention(q, k, v, lam):
    B, L, H, M, d = q.shape
    nb = L // Q_BLOCK
    qb = q.reshape(B, nb, Q_BLOCK, H, M, d).transpose(1, 0, 2, 3, 4, 5)
    scale = d ** -0.5

    def block(qi):
        s = jnp.einsum('bqhmd,bkhmd->bhmqk', qi, k).astype(jnp.float32) * scale
        p = jax.nn.softmax(s, axis=-1)
        a = p[:, :, 0] - lam * p[:, :, 1]
        return jnp.einsum('bhqk,bkhe->bqhe', a.astype(v.dtype), v)

    o = lax.map(block, qb)
    return o.transpose(1, 0, 2, 3, 4).reshape(B, L, H, v.shape[-1])


def gla_chunked(q, k, v, g):
    dtype = v.dtype
    B, L, H, dk = q.shape
    dv = v.shape[-1]
    C = GLA_CHUNK
    N = L // C
    qf = q.astype(jnp.float32).reshape(B, N, C, H, dk) * (dk ** -0.5)
    kf = k.astype(jnp.float32).reshape(B, N, C, H, dk)
    vf = v.astype(jnp.float32).reshape(B, N, C, H, dv)
    gf = g.astype(jnp.float32).reshape(B, N, C, H, dk)
    b = jnp.cumsum(gf, axis=2)
    b_last = b[:, :, -1]
    r = b[:, :, C // 2 - 1: C // 2]
    a = jnp.einsum('bnihd,bnjhd->bnhij', qf * jnp.exp(b - r), kf * jnp.exp(r - b))
    mask = jnp.tril(jnp.ones((C, C), dtype=bool))
    a = jnp.where(mask, a, 0.0)
    o_intra = jnp.einsum('bnhij,bnjhe->bnihe', a, vf)
    kv = jnp.einsum('bnjhd,bnjhe->bnhde', kf * jnp.exp(b_last[:, :, None] - b), vf)

    def step(S, inp):
        dec, kv_n = inp
        return dec[..., None] * S + kv_n, S

    _, S_prev = lax.scan(step, jnp.zeros((B, H, dk, dv), jnp.float32),
                         (jnp.moveaxis(jnp.exp(b_last), 1, 0), jnp.moveaxis(kv, 1, 0)))
    S_prev = jnp.moveaxis(S_prev, 0, 1)
    o_inter = jnp.einsum('bnihd,bnhde->bnihe', qf * jnp.exp(b), S_prev)
    return (o_intra + o_inter).reshape(B, L, H, dv).astype(dtype)


def mixer(h, lam_init, w_in, lambda_q1, lambda_k1, lambda_q2, lambda_k2, diff_subln_g,
          gla_gate_up_f, gla_gate_bias_f, gla_gate_up_b, gla_gate_bias_b, gla_norm_g,
          w_proj_a, w_proj_b, w_out):
    B, L, _ = h.shape
    z = h @ w_in
    dq, dk, dv, lq, lk, lv, lr, laf, lab, gate_a, gate_b = jnp.split(z, SPLIT_POINTS, axis=-1)

    dq = apply_rope(dq.reshape(B, L, DA_HEADS, 2, DA_HEAD_DIM))
    dk = apply_rope(dk.reshape(B, L, DA_HEADS, 2, DA_HEAD_DIM))
    dv = dv.reshape(B, L, DA_HEADS, DA_VDIM)
    lam = (jnp.exp(jnp.sum(lambda_q1.astype(jnp.float32) * lambda_k1.astype(jnp.float32)))
           - jnp.exp(jnp.sum(lambda_q2.astype(jnp.float32) * lambda_k2.astype(jnp.float32)))
           + lam_init)
    oa = diff_attention(dq, dk, dv, lam)
    oa = (rmsnorm(oa, diff_subln_g) * (1.0 - lam_init)).reshape(B, L, DA_WIDTH)

    lq = lq.reshape(B, L, GLA_HEADS, GLA_KDIM)
    lk = lk.reshape(B, L, GLA_HEADS, GLA_KDIM)
    lv = lv.reshape(B, L, GLA_HEADS, GLA_VDIM)
    g_f = (jax.nn.log_sigmoid((laf @ gla_gate_up_f + gla_gate_bias_f).astype(jnp.float32))
           / GLA_GATE_NORMALIZER).reshape(B, L, GLA_HEADS, GLA_KDIM)
    g_b = (jax.nn.log_sigmoid((lab @ gla_gate_up_b + gla_gate_bias_b).astype(jnp.float32))
           / GLA_GATE_NORMALIZER).reshape(B, L, GLA_HEADS, GLA_KDIM)
    o_f = gla_chunked(lq, lk, lv, g_f)
    o_b = jnp.flip(gla_chunked(jnp.flip(lq, 1), jnp.flip(lk, 1), jnp.flip(lv, 1), jnp.flip(g_b, 1)), 1)
    ob = rmsnorm(o_f + o_b, gla_norm_g) * jax.nn.silu(lr.reshape(B, L, GLA_HEADS, GLA_VDIM))
    ob = ob.reshape(B, L, GLA_WIDTH)

    merged = jax.nn.sigmoid(gate_a) * (oa @ w_proj_a) + jax.nn.sigmoid(gate_b) * (ob @ w_proj_b)
    return merged @ w_out


def moe_ffn(h, router_w, router_b, w1, b1, w2, b2):
    nbat, L, D = h.shape
    T = nbat * L
    xt = h.reshape(T, D)
    logits = (xt @ router_w + router_b).astype(jnp.float32)
    top_v, top_i = lax.top_k(logits, TOP_K)
    gates = jax.nn.softmax(top_v, axis=-1)
    A = T * TOP_K
    e_flat = top_i.reshape(A).astype(jnp.int32)
    t_flat = jnp.arange(A, dtype=jnp.int32) // TOP_K
    w_flat = gates.reshape(A)
    order = jnp.argsort(e_flat)
    e_s, t_s, w_s = e_flat[order], t_flat[order], w_flat[order]
    counts = jnp.bincount(e_flat, length=N_EXPERTS).astype(jnp.int32)
    starts = jnp.cumsum(counts) - counts
    padded = (counts + EXPERT_BLOCK - 1) // EXPERT_BLOCK * EXPERT_BLOCK
    pad_end = jnp.cumsum(padded)
    pad_start = pad_end - padded
    dest = pad_start[e_s] + jnp.arange(A, dtype=jnp.int32) - starts[e_s]
    P = -(-(A + N_EXPERTS * (EXPERT_BLOCK - 1)) // EXPERT_BLOCK) * EXPERT_BLOCK
    NB = P // EXPERT_BLOCK
    tok_buf = jnp.full((P,), T, jnp.int32).at[dest].set(t_s)
    w_buf = jnp.zeros((P,), h.dtype).at[dest].set(w_s.astype(h.dtype))
    blk_e = jnp.minimum(jnp.searchsorted(pad_end, jnp.arange(NB, dtype=jnp.int32) * EXPERT_BLOCK,
                                         side='right'), N_EXPERTS - 1).astype(jnp.int32)
    x_pad = jnp.concatenate([xt, jnp.zeros((1, D), h.dtype)], axis=0)
    xs = x_pad[tok_buf].reshape(NB, EXPERT_BLOCK, D)

    def expert_block(args):
        xb, e = args
        hid = xb @ w1[e] + b1[e]
        gate = jnp.minimum(hid[:, 0::2], SWIGLU_LIMIT)
        up = jnp.clip(hid[:, 1::2], -SWIGLU_LIMIT, SWIGLU_LIMIT)
        act = (up + 1.0) * (gate * jax.nn.sigmoid(SWIGLU_ALPHA * gate))
        return act @ w2[e] + b2[e]

    ys = lax.map(expert_block, (xs, blk_e)).reshape(P, D)
    out = jnp.zeros((T + 1, D), h.dtype).at[tok_buf].add(ys * w_buf[:, None])
    return out[:T].reshape(nbat, L, D)


def setup_inputs(seed: int = 0) -> dict:
    key = jax.random.key(seed)
    ks = jax.random.split(key, 32)

    def nrm(k, shape, scale):
        return jax.random.normal(k, shape, jnp.float32) * scale

    return {
        'x_prompt': nrm(ks[0], (BATCH, SEQ, D_MODEL), 1.0),
        'x_sample': nrm(ks[1], (DEC_BATCH, DEC_SEQ, D_MODEL), 1.0),
        'norm_mix_g': 1.0 + nrm(ks[2], (DEPTH, D_MODEL), 0.02),
        'w_in': nrm(ks[3], (DEPTH, D_MODEL, IN_WIDTH), D_MODEL ** -0.5),
        'lambda_q1': nrm(ks[4], (DEPTH, DA_HEAD_DIM), 0.1),
        'lambda_k1': nrm(ks[5], (DEPTH, DA_HEAD_DIM), 0.1),
        'lambda_q2': nrm(ks[6], (DEPTH, DA_HEAD_DIM), 0.1),
        'lambda_k2': nrm(ks[7], (DEPTH, DA_HEAD_DIM), 0.1),
        'diff_subln_g': 1.0 + nrm(ks[8], (DEPTH, DA_VDIM), 0.02),
        'gla_gate_up_f': nrm(ks[9], (DEPTH, GLA_GATE_RANK, GLA_KEY_WIDTH), GLA_GATE_RANK ** -0.5),
        'gla_gate_bias_f': nrm(ks[10], (DEPTH, GLA_KEY_WIDTH), 0.1),
        'gla_gate_up_b': nrm(ks[11], (DEPTH, GLA_GATE_RANK, GLA_KEY_WIDTH), GLA_GATE_RANK ** -0.5),
        'gla_gate_bias_b': nrm(ks[12], (DEPTH, GLA_KEY_WIDTH), 0.1),
        'gla_norm_g': 1.0 + nrm(ks[13], (DEPTH, GLA_VDIM), 0.02),
        'w_proj_a': nrm(ks[14], (DEPTH, DA_WIDTH, D_MODEL), DA_WIDTH ** -0.5),
        'w_proj_b': nrm(ks[15], (DEPTH, GLA_WIDTH, D_MODEL), GLA_WIDTH ** -0.5),
        'w_out': nrm(ks[16], (DEPTH, D_MODEL, D_MODEL), D_MODEL ** -0.5),
        'norm_ffn_g': 1.0 + nrm(ks[17], (DEPTH, D_MODEL), 0.02),
        'router_w': nrm(ks[18], (DEPTH, D_MODEL, N_EXPERTS), D_MODEL ** -0.5),
        'router_b': nrm(ks[19], (DEPTH, N_EXPERTS), 0.01),
        'moe_w1': nrm(ks[20], (DEPTH, N_EXPERTS, D_MODEL, 2 * D_FF), D_MODEL ** -0.5),
        'moe_b1': nrm(ks[21], (DEPTH, N_EXPERTS, 2 * D_FF), 0.01),
        'moe_w2': nrm(ks[22], (DEPTH, N_EXPERTS, D_FF, D_MODEL), D_FF ** -0.5),
        'moe_b2': nrm(ks[23], (DEPTH, N_EXPERTS, D_MODEL), 0.01),
        'norm_final_g': 1.0 + nrm(ks[24], (D_MODEL,), 0.02),
    }


def reference(x_prompt, x_sample, norm_mix_g, w_in, lambda_q1, lambda_k1, lambda_q2, lambda_k2,
              diff_subln_g, gla_gate_up_f, gla_gate_bias_f, gla_gate_up_b, gla_gate_bias_b,
              gla_norm_g, w_proj_a, w_proj_b, w_out, norm_ffn_g, router_w, router_b,
              moe_w1, moe_b1, moe_w2, moe_b2, norm_final_g):
    def trunk(x):
        for l in range(DEPTH):
            lam_init = 0.8 - 0.6 * math.exp(-0.3 * l)
            h = rmsnorm(x, norm_mix_g[l])
            x = x + mixer(h, lam_init, w_in[l], lambda_q1[l], lambda_k1[l], lambda_q2[l], lambda_k2[l],
                          diff_subln_g[l], gla_gate_up_f[l], gla_gate_bias_f[l], gla_gate_up_b[l],
                          gla_gate_bias_b[l], gla_norm_g[l], w_proj_a[l], w_proj_b[l], w_out[l])
            h = rmsnorm(x, norm_ffn_g[l])
            x = x + moe_ffn(h, router_w[l], router_b[l], moe_w1[l], moe_b1[l], moe_w2[l], moe_b2[l])
        return rmsnorm(x, norm_final_g)

    y_prompt = trunk(x_prompt)
    y_sample = trunk(x_sample)
    return (y_prompt, y_sample)
```

```python
import functools
import math

import jax
import jax.numpy as jnp
from jax import lax
from jax.experimental import pallas as pl
from jax.experimental.pallas import tpu as pltpu

F32 = jnp.float32
BF16 = jnp.bfloat16
I32 = jnp.int32

D_MODEL = 1024
DA_HEADS = 4
DA_HEAD_DIM = 64
DA_VDIM = 128
DA_QK_WIDTH = DA_HEADS * 2 * DA_HEAD_DIM
DA_WIDTH = DA_HEADS * DA_VDIM
ROPE_THETA = 10000.0
GLA_HEADS = 4
GLA_KDIM = 64
GLA_VDIM = 128
GLA_KEY_WIDTH = GLA_HEADS * GLA_KDIM
GLA_WIDTH = GLA_HEADS * GLA_VDIM
GLA_GATE_RANK = 16
GLA_GATE_NORMALIZER = 16.0
GLA_CHUNK = 64
N_EXPERTS = 32
TOP_K = 4
D_FF = D_MODEL
SWIGLU_ALPHA = 1.702
SWIGLU_LIMIT = 7.0
NORM_EPS = 1e-5
LANES = 128
SUBLANES = 8

VMEM_LIMIT = 56 * 1024 * 1024

TOKEN_TILE = 512
EXPERT_ROWS = 256

NT_DIMS = (((1,), (1,)), ((), ()))
TN_DIMS = (((0,), (0,)), ((), ()))


def _cparams(sem):
    return pltpu.CompilerParams(dimension_semantics=sem, vmem_limit_bytes=VMEM_LIMIT)


def _full(shape):
    nd = len(shape)
    return pl.BlockSpec(shape, lambda *_: (0,) * nd)


def _rms(x, g):
    return x * lax.rsqrt(jnp.mean(x * x, axis=-1, keepdims=True) + NORM_EPS) * g


def _log_sigmoid(x):
    return jnp.minimum(x, 0.0) - jnp.log1p(jnp.exp(-jnp.abs(x)))


def _inproj_kernel(x_ref, g_ref, cos_ref, sin_ref, wqk_ref, wvt_ref, wlqk_ref, wlvr_ref,
                   wg_ref, wlow_ref, gup_ref, gbias_ref,
                   qk_ref, vt_ref, lqk_ref, lvr_ref, gfb_ref, sg_ref):
    h = _rms(x_ref[...], g_ref[...]).astype(BF16)

    cos = cos_ref[...]
    sin_signed = sin_ref[...]
    first_half = (lax.broadcasted_iota(I32, (1, LANES), 1) % DA_HEAD_DIM) < (DA_HEAD_DIM // 2)
    for c in range(2 * DA_QK_WIDTH // LANES):
        z = jnp.dot(h, wqk_ref[:, c * LANES:(c + 1) * LANES], preferred_element_type=F32)
        partner = jnp.where(first_half, pltpu.roll(z, LANES - DA_HEAD_DIM // 2, 1),
                            pltpu.roll(z, DA_HEAD_DIM // 2, 1))
        qk_ref[:, c * LANES:(c + 1) * LANES] = (z * cos + partner * sin_signed).astype(BF16)

    vt_ref[0] = lax.dot_general(wvt_ref[...], h, NT_DIMS, preferred_element_type=F32).astype(BF16)
    lqk_ref[...] = jnp.dot(h, wlqk_ref[...], preferred_element_type=F32)
    lvr_ref[...] = jnp.dot(h, wlvr_ref[...], preferred_element_type=F32).astype(BF16)
    sg_ref[...] = jax.nn.sigmoid(jnp.dot(h, wg_ref[...], preferred_element_type=F32)).astype(BF16)
    low = jnp.dot(h, wlow_ref[...], preferred_element_type=F32).astype(BF16)
    pre = jnp.dot(low, gup_ref[...], preferred_element_type=F32) + gbias_ref[...]
    gfb_ref[...] = _log_sigmoid(pre) * (1.0 / GLA_GATE_NORMALIZER)


def _inproj(x2, seq_len, norm_g, wqk, wvt, wlqk, wlvr, wg, wlow, gup, gbias, cos_t, sin_t):
    T, D = x2.shape
    tm = TOKEN_TILE
    assert T % tm == 0 and seq_len % tm == 0
    nl = seq_len // tm
    row = lambda n: pl.BlockSpec((tm, n), lambda i: (i, 0))
    out_shape = (
        jax.ShapeDtypeStruct((T, 2 * DA_QK_WIDTH), BF16),
        jax.ShapeDtypeStruct((T // tm, DA_WIDTH, tm), BF16),
        jax.ShapeDtypeStruct((T, 2 * GLA_KEY_WIDTH), F32),
        jax.ShapeDtypeStruct((T, 2 * GLA_WIDTH), BF16),
        jax.ShapeDtypeStruct((T, 2 * GLA_KEY_WIDTH), F32),
        jax.ShapeDtypeStruct((T, 2 * D_MODEL), BF16),
    )
    return pl.pallas_call(
        _inproj_kernel,
        grid=(T // tm,),
        in_specs=[row(D), _full(norm_g.shape),
                  pl.BlockSpec((tm, LANES), lambda i: (i % nl, 0)),
                  pl.BlockSpec((tm, LANES), lambda i: (i % nl, 0)),
                  _full(wqk.shape), _full(wvt.shape), _full(wlqk.shape), _full(wlvr.shape),
                  _full(wg.shape), _full(wlow.shape), _full(gup.shape), _full(gbias.shape)],
        out_specs=(row(2 * DA_QK_WIDTH), pl.BlockSpec((1, DA_WIDTH, tm), lambda i: (i, 0, 0)),
                   row(2 * GLA_KEY_WIDTH), row(2 * GLA_WIDTH), row(2 * GLA_KEY_WIDTH), row(2 * D_MODEL)),
        out_shape=out_shape,
        compiler_params=_cparams(("parallel",)),
        name="inproj",
    )(x2, norm_g, cos_t, sin_t, wqk, wvt, wlqk, wlvr, wg, wlow, gup, gbias)


def _rope_tables(seq_len):
    d = DA_HEAD_DIM
    inv = ROPE_THETA ** (-jnp.arange(0, d, 2, dtype=F32) / d)
    ang = jnp.arange(seq_len, dtype=F32)[:, None] * inv[None, :]
    cos = jnp.concatenate([jnp.cos(ang)] * (2 * LANES // d), -1)
    sin = jnp.sin(ang)
    sin_signed = jnp.concatenate([-sin, sin] * (LANES // d), -1)
    return cos, sin_signed


def _attn_kernel(lamv_ref, q_ref, k_ref, vt_ref, o_ref, m_sc, l_sc, acc_sc, *, lam_init):
    nk, _, tk = vt_ref.shape
    q = q_ref[0]
    lane = lax.broadcasted_iota(I32, (1, LANES), 1)
    outs = []
    for mp in range(2):
        keep = (lane < DA_HEAD_DIM) if mp == 0 else (lane >= DA_HEAD_DIM)
        qm = jnp.where(keep, q, jnp.zeros_like(q))
        m_sc[...] = jnp.full(m_sc.shape, -jnp.inf, F32)
        l_sc[...] = jnp.zeros(l_sc.shape, F32)
        acc_sc[...] = jnp.zeros(acc_sc.shape, F32)

        def body(j, carry, qm=qm):
            off = pl.multiple_of(j * tk, tk)
            kj = k_ref[0, pl.ds(off, tk), :]
            st = lax.dot_general(kj, qm, NT_DIMS, preferred_element_type=F32)
            m_prev = m_sc[...]
            m_new = jnp.maximum(m_prev, jnp.max(st, axis=0, keepdims=True))
            alpha = jnp.exp(m_prev - m_new)
            p = jnp.exp(st - m_new)
            l_sc[...] = alpha * l_sc[...] + jnp.sum(p, axis=0, keepdims=True)
            acc_sc[...] = acc_sc[...] * alpha + jnp.dot(vt_ref[j], p.astype(BF16), preferred_element_type=F32)
            m_sc[...] = m_new
            return carry

        lax.fori_loop(0, nk, body, 0)
        outs.append(acc_sc[...] / l_sc[...])

    lv = lamv_ref[...]
    lam = (jnp.exp(jnp.sum(lv[0:1] * lv[1:2], axis=-1, keepdims=True))
           - jnp.exp(jnp.sum(lv[2:3] * lv[3:4], axis=-1, keepdims=True)) + lam_init)
    o_ref[0] = (outs[0] - lam * outs[1]).T


def _diff_attention(qk3, vt3, lamv, lam_init, tq=512):
    B, L, _ = qk3.shape
    tk = vt3.shape[-1]
    nkb = L // tk
    return pl.pallas_call(
        functools.partial(_attn_kernel, lam_init=lam_init),
        grid=(B, DA_HEADS, L // tq),
        in_specs=[_full(lamv.shape),
                  pl.BlockSpec((1, tq, LANES), lambda b, h, i: (b, i, h)),
                  pl.BlockSpec((1, L, LANES), lambda b, h, i: (b, 0, DA_HEADS + h)),
                  pl.BlockSpec((nkb, DA_VDIM, tk), lambda b, h, i: (b, h, 0))],
        out_specs=pl.BlockSpec((1, tq, DA_VDIM), lambda b, h, i: (b, i, h)),
        out_shape=jax.ShapeDtypeStruct((B, L, DA_WIDTH), F32),
        scratch_shapes=[pltpu.VMEM((1, tq), F32), pltpu.VMEM((1, tq), F32), pltpu.VMEM((DA_VDIM, tq), F32)],
        compiler_params=_cparams(("parallel", "parallel", "arbitrary")),
        name="diff_attention",
    )(lamv, qk3, qk3, vt3)


def _gla_direction(qk_ref, vr_ref, g_ref, o_ref, st_ref, *, reverse):
    C = GLA_CHUNK
    nchunk = qk_ref.shape[1] // C
    row = lax.broadcasted_iota(I32, (C, C), 0)
    col = lax.broadcasted_iota(I32, (C, C), 1)
    tri = (col >= row) if reverse else (col <= row)
    tri_f = tri.astype(F32)
    lane = lax.broadcasted_iota(I32, (1, LANES), 1)
    goff = GLA_KEY_WIDTH if reverse else 0
    scale = GLA_KDIM ** -0.5
    order = range(nchunk - 1, -1, -1) if reverse else range(nchunk)
    for c in order:
        rows = slice(c * C, (c + 1) * C)
        g = g_ref[0, rows, goff:goff + GLA_KEY_WIDTH]
        q = qk_ref[0, rows, 0:GLA_KEY_WIDTH] * scale
        k = qk_ref[0, rows, GLA_KEY_WIDTH:2 * GLA_KEY_WIDTH]
        v = vr_ref[0, rows, :]
        b = jnp.dot(tri_f, g, preferred_element_type=F32, precision=lax.Precision.HIGHEST)
        if reverse:
            b_last, r = b[0:1], b[C // 2:C // 2 + 1]
        else:
            b_last, r = b[C - 1:C], b[C // 2 - 1:C // 2]
        q_in = q * jnp.exp(b - r)
        k_in = (k * jnp.exp(r - b)).astype(BF16)
        k_dec = k * jnp.exp(b_last - b)
        q_dec = q * jnp.exp(b)
        dec = jnp.exp(b_last)
        for h in range(GLA_HEADS):
            ps = slice((h // 2) * LANES, (h // 2 + 1) * LANES)
            vs = slice(h * GLA_VDIM, (h + 1) * GLA_VDIM)
            keep = (lane < GLA_KDIM) if h % 2 == 0 else (lane >= GLA_KDIM)
            zero = jnp.zeros((C, LANES), F32)
            q_in_h = jnp.where(keep, q_in[:, ps], zero).astype(BF16)
            q_dec_h = jnp.where(keep, q_dec[:, ps], zero).astype(BF16)
            k_dec_h = jnp.where(keep, k_dec[:, ps], zero).astype(BF16)
            a = lax.dot_general(q_in_h, k_in[:, ps], NT_DIMS, preferred_element_type=F32)
            a = jnp.where(tri, a, 0.0).astype(BF16)
            s_t = st_ref[h]
            o = jnp.dot(a, v[:, vs], preferred_element_type=F32)
            o = o + lax.dot_general(q_dec_h, s_t.astype(BF16), NT_DIMS, preferred_element_type=F32)
            o_ref[0, rows, vs] = o
            kv_t = lax.dot_general(v[:, vs], k_dec_h, TN_DIMS, preferred_element_type=F32)
            st_ref[h] = s_t * dec[:, ps] + kv_t


def _gla_kernel(qkf_ref, vrf_ref, gf_ref, qkb_ref, vrb_ref, gb_ref, of_ref, ob_ref, sf_sc, sb_sc):
    @pl.when(pl.program_id(1) == 0)
    def _():
        sf_sc[...] = jnp.zeros(sf_sc.shape, F32)
        sb_sc[...] = jnp.zeros(sb_sc.shape, F32)

    _gla_direction(qkf_ref, vrf_ref, gf_ref, of_ref, sf_sc, reverse=False)
    _gla_direction(qkb_ref, vrb_ref, gb_ref, ob_ref, sb_sc, reverse=True)


def _gla(lqk3, lvr3, gfb3, tb=256):
    B, L, _ = lqk3.shape
    nb = L // tb
    fwd = lambda n: pl.BlockSpec((1, tb, n), lambda b, i: (b, i, 0))
    bwd = lambda n: pl.BlockSpec((1, tb, n), lambda b, i: (b, nb - 1 - i, 0))
    out = jax.ShapeDtypeStruct((B, L, GLA_WIDTH), F32)
    return pl.pallas_call(
        _gla_kernel,
        grid=(B, nb),
        in_specs=[fwd(2 * GLA_KEY_WIDTH), fwd(GLA_WIDTH), fwd(2 * GLA_KEY_WIDTH),
                  bwd(2 * GLA_KEY_WIDTH), bwd(GLA_WIDTH), bwd(2 * GLA_KEY_WIDTH)],
        out_specs=(fwd(GLA_WIDTH), bwd(GLA_WIDTH)),
        out_shape=(out, out),
        scratch_shapes=[pltpu.VMEM((GLA_HEADS, GLA_VDIM, LANES), F32),
                        pltpu.VMEM((GLA_HEADS, GLA_VDIM, LANES), F32)],
        compiler_params=_cparams(("parallel", "arbitrary")),
        name="gla",
    )(lqk3, lvr3, gfb3, lqk3, lvr3, gfb3)


def _head_norm(z, g):
    return jnp.concatenate(
        [_rms(z[:, h * LANES:(h + 1) * LANES], g) for h in range(z.shape[1] // LANES)], axis=1)


def _merge_kernel(x_ref, oa_ref, of_ref, ob_ref, r_ref, sg_ref, subg_ref, glag_ref, wpa_ref, wpb_ref,
                  wout_ref, nffn_ref, rwt_ref, rb_ref,
                  x1_ref, h2_ref, ti_ref, rk_ref, tg_ref, cnt_ref, carry_sc, *, lam_init):
    tm = x_ref.shape[0]

    @pl.when(pl.program_id(0) == 0)
    def _():
        carry_sc[...] = jnp.zeros(carry_sc.shape, F32)

    oa = (_head_norm(oa_ref[...], subg_ref[...]) * (1.0 - lam_init)).astype(BF16)
    ob = _head_norm(of_ref[...] + ob_ref[...], glag_ref[...]) * jax.nn.silu(r_ref[...].astype(F32))
    pa = jnp.dot(oa, wpa_ref[...], preferred_element_type=F32)
    pb = jnp.dot(ob.astype(BF16), wpb_ref[...], preferred_element_type=F32)
    sg = sg_ref[...].astype(F32)
    merged = sg[:, :D_MODEL] * pa + sg[:, D_MODEL:] * pb
    x1 = x_ref[...] + jnp.dot(merged.astype(BF16), wout_ref[...], preferred_element_type=F32)
    x1_ref[...] = x1
    h2 = _rms(x1, nffn_ref[...])
    h2_ref[...] = h2

    logits = lax.dot_general(rwt_ref[...], h2.astype(BF16), NT_DIMS, preferred_element_type=F32) + rb_ref[...]
    eidx = lax.broadcasted_iota(I32, logits.shape, 0)
    vals, sels = [], []
    for k in range(TOP_K):
        mk = jnp.max(logits, axis=0, keepdims=True)
        ik = jnp.min(jnp.where(logits == mk, eidx, N_EXPERTS), axis=0, keepdims=True)
        sel = eidx == ik
        logits = jnp.where(sel, -jnp.inf, logits)
        vals.append(mk)
        sels.append(sel)
        ti_ref[k:k + 1, :] = ik
    ex = [jnp.exp(v - vals[0]) for v in vals]
    denom = ex[0] + ex[1] + ex[2] + ex[3]
    for k in range(TOP_K):
        tg_ref[k:k + 1, :] = ex[k] / denom
    tg_ref[TOP_K:, :] = jnp.zeros((tg_ref.shape[0] - TOP_K, tm), F32)

    multi = (sels[0] | sels[1] | sels[2] | sels[3])
    multi_f = jnp.where(multi, 1.0, 0.0)
    before = lax.broadcasted_iota(I32, (tm, tm), 0) < lax.broadcasted_iota(I32, (tm, tm), 1)
    cum = jnp.dot(multi_f.astype(BF16), jnp.where(before, 1.0, 0.0).astype(BF16), preferred_element_type=F32)
    tot = carry_sc[:, 0:1] + cum
    for k in range(TOP_K):
        rk_ref[k:k + 1, :] = jnp.sum(jnp.where(sels[k], tot, 0.0), axis=0, keepdims=True).astype(I32)
    carry_sc[...] = carry_sc[...] + jnp.sum(multi_f, axis=1, keepdims=True)
    cnt_ref[...] = carry_sc[...]


def _merge(x2, oa, of, ob, lvr, sg, subg, glag, wpa, wpb, wout, nffn, rwt, rb, lam_init):
    T, D = x2.shape
    tm = TOKEN_TILE
    row = lambda n: pl.BlockSpec((tm, n), lambda i: (i, 0))
    col = lambda n: pl.BlockSpec((n, tm), lambda i: (0, i))
    out_shape = (
        jax.ShapeDtypeStruct((T, D), F32),
        jax.ShapeDtypeStruct((T, D), F32),
        jax.ShapeDtypeStruct((TOP_K, T), I32),
        jax.ShapeDtypeStruct((TOP_K, T), I32),
        jax.ShapeDtypeStruct((SUBLANES, T), F32),
        jax.ShapeDtypeStruct((N_EXPERTS, LANES), F32),
    )
    return pl.pallas_call(
        functools.partial(_merge_kernel, lam_init=lam_init),
        grid=(T // tm,),
        in_specs=[row(D), row(DA_WIDTH), row(GLA_WIDTH), row(GLA_WIDTH),
                  pl.BlockSpec((tm, GLA_WIDTH), lambda i: (i, 1)), row(2 * D),
                  _full(subg.shape), _full(glag.shape), _full(wpa.shape), _full(wpb.shape),
                  _full(wout.shape), _full(nffn.shape), _full(rwt.shape), _full(rb.shape)],
        out_specs=(row(D), row(D), col(TOP_K), col(TOP_K), col(SUBLANES), _full((N_EXPERTS, LANES))),
        out_shape=out_shape,
        scratch_shapes=[pltpu.VMEM((N_EXPERTS, LANES), F32)],
        compiler_params=_cparams(("arbitrary",)),
        name="merge_router",
    )(x2, oa, of, ob, lvr, sg, subg, glag, wpa, wpb, wout, nffn, rwt, rb)


def _dispatch_kernel(pend_ref, dest_ref, h_ref, xs_ref, zero_sc, sem):
    tm = h_ref.shape[0]
    rows = zero_sc.shape[0]

    @pl.when(pl.program_id(0) == 0)
    def _():
        zero_sc[...] = jnp.zeros(zero_sc.shape, F32)
        for e in range(N_EXPERTS):
            end = pend_ref[e]
            begin = pend_ref[e - 1] if e else 0

            @pl.when(end > begin)
            def _():
                last = pl.multiple_of(end - rows, rows)
                cp = pltpu.make_async_copy(zero_sc, xs_ref.at[pl.ds(last, rows)], sem)
                cp.start()
                cp.wait()

        def zero_unused(blk, c):
            cp = pltpu.make_async_copy(zero_sc, xs_ref.at[pl.ds(pl.multiple_of(blk * rows, rows), rows)], sem)
            cp.start()
            cp.wait()
            return c

        lax.fori_loop(pend_ref[N_EXPERTS - 1] // rows, xs_ref.shape[0] // rows, zero_unused, 0)

    def row_copy(t, k):
        return pltpu.make_async_copy(h_ref.at[pl.ds(t, 1)], xs_ref.at[pl.ds(dest_ref[k, t], 1)], sem)

    def start(t, c):
        for k in range(TOP_K):
            row_copy(t, k).start()
        return c

    def wait(t, c):
        for k in range(TOP_K):
            row_copy(t, k).wait()
        return c

    lax.fori_loop(0, tm, start, 0)
    lax.fori_loop(0, tm, wait, 0)


def _dispatch(h2, dest, pad_end, n_slots):
    T, D = h2.shape
    tm = TOKEN_TILE
    return pl.pallas_call(
        _dispatch_kernel,
        grid_spec=pltpu.PrefetchScalarGridSpec(
            num_scalar_prefetch=1,
            grid=(T // tm,),
            in_specs=[pl.BlockSpec((TOP_K, tm), lambda i, pe: (0, i), memory_space=pltpu.SMEM),
                      pl.BlockSpec((tm, D), lambda i, pe: (i, 0))],
            out_specs=pl.BlockSpec(memory_space=pl.ANY),
            scratch_shapes=[pltpu.VMEM((EXPERT_ROWS, D), F32), pltpu.SemaphoreType.DMA(())],
        ),
        out_shape=jax.ShapeDtypeStruct((n_slots, D), F32),
        compiler_params=_cparams(("arbitrary",)),
        name="moe_dispatch",
    )(pad_end, dest, h2)


def _expert_kernel(blk_ref, nused_ref, xs_ref, w1g_ref, w1u_ref, b1g_ref, b1u_ref, w2_ref, b2_ref, ys_ref):
    @pl.when(pl.program_id(0) < nused_ref[0])
    def _():
        x = xs_ref[...].astype(BF16)
        gate = jnp.dot(x, w1g_ref[0], preferred_element_type=F32) + b1g_ref[0]
        up = jnp.dot(x, w1u_ref[0], preferred_element_type=F32) + b1u_ref[0]
        gate = jnp.minimum(gate, SWIGLU_LIMIT)
        up = jnp.clip(up, -SWIGLU_LIMIT, SWIGLU_LIMIT)
        act = (up + 1.0) * (gate * jax.nn.sigmoid(SWIGLU_ALPHA * gate))
        ys_ref[...] = jnp.dot(act.astype(BF16), w2_ref[0], preferred_element_type=F32) + b2_ref[0]

    @pl.when(pl.program_id(0) >= nused_ref[0])
    def _():
        ys_ref[...] = jnp.zeros(ys_ref.shape, F32)


def _experts(xs, blk_e, nused, w1g, w1u, b1g, b1u, w2, b2):
    P, D = xs.shape
    bm = EXPERT_ROWS
    nb = P // bm
    rows = pl.BlockSpec((bm, D), lambda i, blk, nu: (jnp.minimum(i, nu[0] - 1), 0))
    per_expert = lambda a: pl.BlockSpec((1,) + a.shape[1:], lambda i, blk, nu: (blk[i], 0, 0))
    return pl.pallas_call(
        _expert_kernel,
        grid_spec=pltpu.PrefetchScalarGridSpec(
            num_scalar_prefetch=2,
            grid=(nb,),
            in_specs=[rows, per_expert(w1g), per_expert(w1u), per_expert(b1g), per_expert(b1u),
                      per_expert(w2), per_expert(b2)],
            out_specs=pl.BlockSpec((bm, D), lambda i, blk, nu: (i, 0)),
        ),
        out_shape=jax.ShapeDtypeStruct((P, D), F32),
        compiler_params=_cparams(("arbitrary",)),
        name="moe_experts",
    )(blk_e, nused, xs, w1g, w1u, b1g, b1u, w2, b2)


def _combine_kernel(dest_ref, ys_ref, tg_ref, x1_ref, gfin_ref, y_ref, buf, sem, *, final):
    tc = x1_ref.shape[0]

    def row_copy(t, k):
        return pltpu.make_async_copy(ys_ref.at[pl.ds(dest_ref[k, t], 1)], buf.at[k, pl.ds(t, 1)], sem)

    def start(t, c):
        for k in range(TOP_K):
            row_copy(t, k).start()
        return c

    def wait(t, c):
        for k in range(TOP_K):
            row_copy(t, k).wait()
        return c

    lax.fori_loop(0, tc, start, 0)
    lax.fori_loop(0, tc, wait, 0)
    gates = tg_ref[...].T
    x = x1_ref[...]
    for k in range(TOP_K):
        x = x + buf[k] * gates[:, k:k + 1]
    y_ref[...] = _rms(x, gfin_ref[...]) if final else x


def _combine(ys, dest, tg, x1, gfin, final, tc=256):
    T, D = x1.shape
    return pl.pallas_call(
        functools.partial(_combine_kernel, final=final),
        grid=(T // tc,),
        in_specs=[pl.BlockSpec((TOP_K, tc), lambda i: (0, i), memory_space=pltpu.SMEM),
                  pl.BlockSpec(memory_space=pl.ANY),
                  pl.BlockSpec((SUBLANES, tc), lambda i: (0, i)),
                  pl.BlockSpec((tc, D), lambda i: (i, 0)),
                  _full(gfin.shape)],
        out_specs=pl.BlockSpec((tc, D), lambda i: (i, 0)),
        out_shape=jax.ShapeDtypeStruct((T, D), F32),
        scratch_shapes=[pltpu.VMEM((TOP_K, tc, D), F32), pltpu.SemaphoreType.DMA(())],
        compiler_params=_cparams(("parallel",)),
        name="moe_combine",
    )(dest, ys, tg, x1, gfin)


def _moe(h2, ti, rk, tg, cnt, x1, moe_w, gfin, final):
    T, D = h2.shape
    bm = EXPERT_ROWS
    n_assign = T * TOP_K
    nb = -(-(n_assign + N_EXPERTS * (bm - 1)) // bm)
    counts = cnt[:, 0].astype(I32)
    padded = (counts + bm - 1) // bm * bm
    pad_end = jnp.cumsum(padded).astype(I32)
    pad_start = pad_end - padded
    dest = jnp.take(pad_start, ti) + rk
    blk_e = jnp.minimum(jnp.searchsorted(pad_end, jnp.arange(nb, dtype=I32) * bm, side='right'),
                        N_EXPERTS - 1).astype(I32)
    nused = (pad_end[-1:] // bm).astype(I32)
    blk_e = jnp.where(jnp.arange(nb) < nused[0], blk_e, blk_e[jnp.maximum(nused[0] - 1, 0)])
    xs = _dispatch(h2, dest, pad_end, nb * bm)
    ys = _experts(xs, blk_e, nused, *moe_w)
    return _combine(ys, dest, tg, x1, gfin, final)


def _split_in_proj(w):
    sizes = (DA_QK_WIDTH, DA_QK_WIDTH, DA_WIDTH, GLA_KEY_WIDTH, GLA_KEY_WIDTH, GLA_WIDTH, GLA_WIDTH,
             GLA_GATE_RANK, GLA_GATE_RANK, D_MODEL, D_MODEL)
    assert w.shape[-1] == sum(sizes)
    out, o = [], 0
    for s in sizes:
        out.append(w[:, o:o + s])
        o += s
    return out


def _prep_layer(l, norm_mix_g, w_in, lambda_q1, lambda_k1, lambda_q2, lambda_k2, diff_subln_g,
                gla_gate_up_f, gla_gate_bias_f, gla_gate_up_b, gla_gate_bias_b, gla_norm_g,
                w_proj_a, w_proj_b, w_out, norm_ffn_g, router_w, router_b, moe_w1, moe_b1, moe_w2, moe_b2):
    dq, dk, dv, lq, lk, lv, lr, laf, lab, ga, gb = _split_in_proj(w_in[l])
    bf = lambda a: a.astype(BF16)
    r = GLA_GATE_RANK
    gup = jnp.zeros((LANES, 2 * GLA_KEY_WIDTH), F32)
    gup = gup.at[0:r, 0:GLA_KEY_WIDTH].set(gla_gate_up_f[l]).at[r:2 * r, GLA_KEY_WIDTH:].set(gla_gate_up_b[l])
    lamv = jnp.zeros((SUBLANES, LANES), F32)
    for i, v in enumerate((lambda_q1, lambda_k1, lambda_q2, lambda_k2)):
        lamv = lamv.at[i, 0:DA_HEAD_DIM].set(v[l].astype(F32))
    inproj = dict(
        norm_g=norm_mix_g[l][None, :],
        wqk=bf(jnp.concatenate([dq * (DA_HEAD_DIM ** -0.5), dk], 1)),
        wvt=bf(dv.T),
        wlqk=bf(jnp.concatenate([lq, lk], 1)),
        wlvr=bf(jnp.concatenate([lv, lr], 1)),
        wg=bf(jnp.concatenate([ga, gb], 1)),
        wlow=bf(jnp.concatenate([laf, lab, jnp.zeros((D_MODEL, LANES - 2 * r), F32)], 1)),
        gup=bf(gup),
        gbias=jnp.concatenate([gla_gate_bias_f[l], gla_gate_bias_b[l]])[None, :],
    )
    merge = dict(
        subg=diff_subln_g[l][None, :], glag=gla_norm_g[l][None, :],
        wpa=bf(w_proj_a[l]), wpb=bf(w_proj_b[l]), wout=bf(w_out[l]),
        nffn=norm_ffn_g[l][None, :], rwt=bf(router_w[l].T), rb=router_b[l][:, None],
    )
    w1 = moe_w1[l]
    b1 = moe_b1[l]
    moe_w = (bf(w1[:, :, 0::2]), bf(w1[:, :, 1::2]), b1[:, None, 0::2], b1[:, None, 1::2],
             bf(moe_w2[l]), moe_b2[l][:, None, :])
    return inproj, lamv, merge, moe_w


def _trunk(x, layers, norm_final_g):
    B, L, D = x.shape
    T = B * L
    cos_t, sin_t = _rope_tables(L)
    x2 = x.reshape(T, D)
    gfin = norm_final_g[None, :]
    for l, (inproj, lamv, merge, moe_w) in enumerate(layers):
        lam_init = 0.8 - 0.6 * math.exp(-0.3 * l)
        qk, vt, lqk, lvr, gfb, sg = _inproj(x2, L, cos_t=cos_t, sin_t=sin_t, **inproj)
        oa = _diff_attention(qk.reshape(B, L, -1), vt, lamv, lam_init)
        of, ob = _gla(lqk.reshape(B, L, -1), lvr.reshape(B, L, -1), gfb.reshape(B, L, -1))
        x1, h2, ti, rk, tg, cnt = _merge(x2, oa.reshape(T, -1), of.reshape(T, -1), ob.reshape(T, -1),
                                         lvr, sg, lam_init=lam_init, **merge)
        x2 = _moe(h2, ti, rk, tg, cnt, x1, moe_w, gfin, final=(l == len(layers) - 1))
    return x2.reshape(B, L, D)


def kernel(x_prompt, x_sample, norm_mix_g, w_in, lambda_q1, lambda_k1, lambda_q2, lambda_k2, diff_subln_g,
           gla_gate_up_f, gla_gate_bias_f, gla_gate_up_b, gla_gate_bias_b, gla_norm_g, w_proj_a, w_proj_b,
           w_out, norm_ffn_g, router_w, router_b, moe_w1, moe_b1, moe_w2, moe_b2, norm_final_g):
    depth = w_in.shape[0]
    layers = [_prep_layer(l, norm_mix_g, w_in, lambda_q1, lambda_k1, lambda_q2, lambda_k2, diff_subln_g,
                          gla_gate_up_f, gla_gate_bias_f, gla_gate_up_b, gla_gate_bias_b, gla_norm_g,
                          w_proj_a, w_proj_b, w_out, norm_ffn_g, router_w, router_b,
                          moe_w1, moe_b1, moe_w2, moe_b2) for l in range(depth)]
    return _trunk(x_prompt, layers, norm_final_g), _trunk(x_sample, layers, norm_final_g)
```

```python
import functools
import math

import jax
import jax.numpy as jnp
from jax import lax
from jax.experimental import pallas as pl
from jax.experimental.pallas import tpu as pltpu

F32 = jnp.float32
BF16 = jnp.bfloat16
I32 = jnp.int32

D_MODEL = 1024
DA_HEADS = 4
DA_HEAD_DIM = 64
DA_VDIM = 128
DA_QK_WIDTH = DA_HEADS * 2 * DA_HEAD_DIM
DA_WIDTH = DA_HEADS * DA_VDIM
ROPE_THETA = 10000.0
GLA_HEADS = 4
GLA_KDIM = 64
GLA_VDIM = 128
GLA_KEY_WIDTH = GLA_HEADS * GLA_KDIM
GLA_WIDTH = GLA_HEADS * GLA_VDIM
GLA_GATE_RANK = 16
GLA_GATE_NORMALIZER = 16.0
GLA_CHUNK = 64
N_EXPERTS = 32
TOP_K = 4
D_FF = D_MODEL
SWIGLU_ALPHA = 1.702
SWIGLU_LIMIT = 7.0
NORM_EPS = 1e-5
LANES = 128
SUBLANES = 8

BF16_SUBLANES = 16
LOG2_E = 1.4426950408889634
VT_ROWS = DA_VDIM + BF16_SUBLANES

VMEM_LIMIT = 56 * 1024 * 1024

TOKEN_TILE = 512
EXPERT_ROWS = 256

NT_DIMS = (((1,), (1,)), ((), ()))
TN_DIMS = (((0,), (0,)), ((), ()))


def _cparams(sem):
    return pltpu.CompilerParams(dimension_semantics=sem, vmem_limit_bytes=VMEM_LIMIT)


def _full(shape):
    nd = len(shape)
    return pl.BlockSpec(shape, lambda *_: (0,) * nd)


def _rms(x, g):
    return x * lax.rsqrt(jnp.mean(x * x, axis=-1, keepdims=True) + NORM_EPS) * g


def _log_sigmoid(x):
    return jnp.minimum(x, 0.0) - jnp.log1p(jnp.exp(-jnp.abs(x)))


def _inproj_kernel(x_ref, g_ref, cos_ref, sin_ref, wqk_ref, wvt_ref, wlqk_ref, wlvr_ref,
                   wg_ref, wlow_ref, gup_ref, gbias_ref,
                   qk_ref, vt_ref, lqk_ref, lvr_ref, gfb_ref, sg_ref):
    h = _rms(x_ref[...], g_ref[...]).astype(BF16)

    cos = cos_ref[...]
    sin_signed = sin_ref[...]
    first_half = (lax.broadcasted_iota(I32, (1, LANES), 1) % DA_HEAD_DIM) < (DA_HEAD_DIM // 2)
    for c in range(2 * DA_QK_WIDTH // LANES):
        z = jnp.dot(h, wqk_ref[:, c * LANES:(c + 1) * LANES], preferred_element_type=F32)
        if c < DA_QK_WIDTH // LANES:
            z = z * LOG2_E
        partner = jnp.where(first_half, pltpu.roll(z, LANES - DA_HEAD_DIM // 2, 1),
                            pltpu.roll(z, DA_HEAD_DIM // 2, 1))
        qk_ref[:, c * LANES:(c + 1) * LANES] = (z * cos + partner * sin_signed).astype(BF16)

    vt = lax.dot_general(wvt_ref[...], h, NT_DIMS, preferred_element_type=F32).astype(BF16)
    for hd in range(DA_HEADS):
        vt_ref[0, hd * VT_ROWS:hd * VT_ROWS + DA_VDIM, :] = vt[hd * DA_VDIM:(hd + 1) * DA_VDIM]
        vt_ref[0, hd * VT_ROWS + DA_VDIM:(hd + 1) * VT_ROWS, :] = jnp.ones((VT_ROWS - DA_VDIM, vt.shape[1]), BF16)
    lqk_ref[...] = jnp.dot(h, wlqk_ref[...], preferred_element_type=F32)
    lvr_ref[...] = jnp.dot(h, wlvr_ref[...], preferred_element_type=F32).astype(BF16)
    sg_ref[...] = jax.nn.sigmoid(jnp.dot(h, wg_ref[...], preferred_element_type=F32)).astype(BF16)
    low = jnp.dot(h, wlow_ref[...], preferred_element_type=F32).astype(BF16)
    pre = jnp.dot(low, gup_ref[...], preferred_element_type=F32) + gbias_ref[...]
    gfb_ref[...] = _log_sigmoid(pre) * (1.0 / GLA_GATE_NORMALIZER)


def _inproj(x2, seq_len, norm_g, wqk, wvt, wlqk, wlvr, wg, wlow, gup, gbias, cos_t, sin_t):
    T, D = x2.shape
    tm = TOKEN_TILE
    assert T % tm == 0 and seq_len % tm == 0
    nl = seq_len // tm
    row = lambda n: pl.BlockSpec((tm, n), lambda i: (i, 0))
    out_shape = (
        jax.ShapeDtypeStruct((T, 2 * DA_QK_WIDTH), BF16),
        jax.ShapeDtypeStruct((T // tm, DA_HEADS * VT_ROWS, tm), BF16),
        jax.ShapeDtypeStruct((T, 2 * GLA_KEY_WIDTH), F32),
        jax.ShapeDtypeStruct((T, 2 * GLA_WIDTH), BF16),
        jax.ShapeDtypeStruct((T, 2 * GLA_KEY_WIDTH), F32),
        jax.ShapeDtypeStruct((T, 2 * D_MODEL), BF16),
    )
    return pl.pallas_call(
        _inproj_kernel,
        grid=(T // tm,),
        in_specs=[row(D), _full(norm_g.shape),
                  pl.BlockSpec((tm, LANES), lambda i: (i % nl, 0)),
                  pl.BlockSpec((tm, LANES), lambda i: (i % nl, 0)),
                  _full(wqk.shape), _full(wvt.shape), _full(wlqk.shape), _full(wlvr.shape),
                  _full(wg.shape), _full(wlow.shape), _full(gup.shape), _full(gbias.shape)],
        out_specs=(row(2 * DA_QK_WIDTH), pl.BlockSpec((1, DA_HEADS * VT_ROWS, tm), lambda i: (i, 0, 0)),
                   row(2 * GLA_KEY_WIDTH), row(2 * GLA_WIDTH), row(2 * GLA_KEY_WIDTH), row(2 * D_MODEL)),
        out_shape=out_shape,
        compiler_params=_cparams(("parallel",)),
        name="inproj",
    )(x2, norm_g, cos_t, sin_t, wqk, wvt, wlqk, wlvr, wg, wlow, gup, gbias)


def _rope_tables(seq_len):
    d = DA_HEAD_DIM
    inv = ROPE_THETA ** (-jnp.arange(0, d, 2, dtype=F32) / d)
    ang = jnp.arange(seq_len, dtype=F32)[:, None] * inv[None, :]
    cos = jnp.concatenate([jnp.cos(ang)] * (2 * LANES // d), -1)
    sin = jnp.sin(ang)
    sin_signed = jnp.concatenate([-sin, sin] * (LANES // d), -1)
    return cos, sin_signed


def _attn_kernel(lamv_ref, q_ref, k_ref, vt_ref, o_ref, acc_sc, sa_sc, sb_sc, *, lam_init):
    nk, _, tk = vt_ref.shape
    tq = q_ref.shape[1]
    q = q_ref[0]
    lane = lax.broadcasted_iota(I32, (1, LANES), 1)
    zeros = jnp.zeros_like(q)
    qms = (jnp.where(lane < DA_HEAD_DIM, q, zeros), jnp.where(lane >= DA_HEAD_DIM, q, zeros))
    acc_sc[...] = jnp.zeros(acc_sc.shape, F32)

    def scores(j, dst_ref):
        kj = k_ref[0, pl.ds(pl.multiple_of(j * tk, tk), tk), :]
        for mp in range(2):
            dst_ref[mp] = lax.dot_general(kj, qms[mp], NT_DIMS, preferred_element_type=F32)

    def softmax_pv(j, src_ref, ms):
        vj = vt_ref[j]
        new = []
        for mp in range(2):
            st = src_ref[mp]
            m_new = jnp.maximum(ms[mp], jnp.max(st, axis=0, keepdims=True))
            p = jnp.exp2((st - m_new).astype(BF16))
            acc_sc[mp] = acc_sc[mp] * jnp.exp2(ms[mp] - m_new) + jnp.dot(vj, p, preferred_element_type=F32)
            new.append(m_new)
        return tuple(new)

    def pair(i, ms):
        j = 2 * i
        scores(j + 1, sb_sc)
        ms = softmax_pv(j, sa_sc, ms)
        scores(j + 2, sa_sc)
        return softmax_pv(j + 1, sb_sc, ms)

    assert nk % 2 == 0
    scores(0, sa_sc)
    ms = tuple(jnp.full((1, tq), -jnp.inf, F32) for _ in range(2))
    ms = lax.fori_loop(0, (nk - 2) // 2, pair, ms)
    scores(nk - 1, sb_sc)
    ms = softmax_pv(nk - 2, sa_sc, ms)
    softmax_pv(nk - 1, sb_sc, ms)

    lv = lamv_ref[...]
    lam = (jnp.exp(jnp.sum(lv[0:1] * lv[1:2], axis=-1, keepdims=True))
           - jnp.exp(jnp.sum(lv[2:3] * lv[3:4], axis=-1, keepdims=True)) + lam_init)
    a0, a1 = acc_sc[0], acc_sc[1]
    o_ref[0] = (a0[:DA_VDIM] / a0[DA_VDIM:DA_VDIM + 1] - lam * (a1[:DA_VDIM] / a1[DA_VDIM:DA_VDIM + 1])).T


def _diff_attention(qk3, vt3, lamv, lam_init, tq=512):
    B, L, _ = qk3.shape
    tk = vt3.shape[-1]
    nkb = L // tk
    return pl.pallas_call(
        functools.partial(_attn_kernel, lam_init=lam_init),
        grid=(B, DA_HEADS, L // tq),
        in_specs=[_full(lamv.shape),
                  pl.BlockSpec((1, tq, LANES), lambda b, h, i: (b, i, h)),
                  pl.BlockSpec((1, L, LANES), lambda b, h, i: (b, 0, DA_HEADS + h)),
                  pl.BlockSpec((nkb, VT_ROWS, tk), lambda b, h, i: (b, h, 0))],
        out_specs=pl.BlockSpec((1, tq, DA_VDIM), lambda b, h, i: (b, i, h)),
        out_shape=jax.ShapeDtypeStruct((B, L, DA_WIDTH), F32),
        scratch_shapes=[pltpu.VMEM((2, VT_ROWS, tq), F32), pltpu.VMEM((2, tk, tq), F32),
                        pltpu.VMEM((2, tk, tq), F32)],
        compiler_params=_cparams(("parallel", "parallel", "arbitrary")),
        name="diff_attention",
    )(lamv, qk3, qk3, vt3)


def _gla_direction(qk_ref, vr_ref, g_ref, o_ref, st_ref, *, reverse):
    C = GLA_CHUNK
    nchunk = qk_ref.shape[1] // C
    row = lax.broadcasted_iota(I32, (C, C), 0)
    col = lax.broadcasted_iota(I32, (C, C), 1)
    tri = (col >= row) if reverse else (col <= row)
    tri_f = tri.astype(F32)
    lane = lax.broadcasted_iota(I32, (1, LANES), 1)
    goff = GLA_KEY_WIDTH if reverse else 0
    scale = GLA_KDIM ** -0.5
    order = range(nchunk - 1, -1, -1) if reverse else range(nchunk)
    for c in order:
        rows = slice(c * C, (c + 1) * C)
        g = g_ref[0, rows, goff:goff + GLA_KEY_WIDTH]
        q = qk_ref[0, rows, 0:GLA_KEY_WIDTH] * scale
        k = qk_ref[0, rows, GLA_KEY_WIDTH:2 * GLA_KEY_WIDTH]
        v = vr_ref[0, rows, :]
        b = jnp.dot(tri_f, g, preferred_element_type=F32, precision=lax.Precision.HIGHEST)
        if reverse:
            b_last, r = b[0:1], b[C // 2:C // 2 + 1]
        else:
            b_last, r = b[C - 1:C], b[C // 2 - 1:C // 2]
        q_in = q * jnp.exp(b - r)
        k_in = (k * jnp.exp(r - b)).astype(BF16)
        k_dec = k * jnp.exp(b_last - b)
        q_dec = q * jnp.exp(b)
        dec = jnp.exp(b_last)
        for h in range(GLA_HEADS):
            ps = slice((h // 2) * LANES, (h // 2 + 1) * LANES)
            vs = slice(h * GLA_VDIM, (h + 1) * GLA_VDIM)
            keep = (lane < GLA_KDIM) if h % 2 == 0 else (lane >= GLA_KDIM)
            zero = jnp.zeros((C, LANES), F32)
            q_in_h = jnp.where(keep, q_in[:, ps], zero).astype(BF16)
            q_dec_h = jnp.where(keep, q_dec[:, ps], zero).astype(BF16)
            k_dec_h = jnp.where(keep, k_dec[:, ps], zero).astype(BF16)
            a = lax.dot_general(q_in_h, k_in[:, ps], NT_DIMS, preferred_element_type=F32)
            a = jnp.where(tri, a, 0.0).astype(BF16)
            s_t = st_ref[h]
            o = jnp.dot(a, v[:, vs], preferred_element_type=F32)
            o = o + lax.dot_general(q_dec_h, s_t.astype(BF16), NT_DIMS, preferred_element_type=F32)
            o_ref[0, rows, vs] = o
            kv_t = lax.dot_general(v[:, vs], k_dec_h, TN_DIMS, preferred_element_type=F32)
            st_ref[h] = s_t * dec[:, ps] + kv_t


def _gla_kernel(qkf_ref, vrf_ref, gf_ref, qkb_ref, vrb_ref, gb_ref, of_ref, ob_ref, sf_sc, sb_sc):
    @pl.when(pl.program_id(1) == 0)
    def _():
        sf_sc[...] = jnp.zeros(sf_sc.shape, F32)
        sb_sc[...] = jnp.zeros(sb_sc.shape, F32)

    _gla_direction(qkf_ref, vrf_ref, gf_ref, of_ref, sf_sc, reverse=False)
    _gla_direction(qkb_ref, vrb_ref, gb_ref, ob_ref, sb_sc, reverse=True)


def _gla(lqk3, lvr3, gfb3, tb=256):
    B, L, _ = lqk3.shape
    nb = L // tb
    fwd = lambda n: pl.BlockSpec((1, tb, n), lambda b, i: (b, i, 0))
    bwd = lambda n: pl.BlockSpec((1, tb, n), lambda b, i: (b, nb - 1 - i, 0))
    out = jax.ShapeDtypeStruct((B, L, GLA_WIDTH), F32)
    return pl.pallas_call(
        _gla_kernel,
        grid=(B, nb),
        in_specs=[fwd(2 * GLA_KEY_WIDTH), fwd(GLA_WIDTH), fwd(2 * GLA_KEY_WIDTH),
                  bwd(2 * GLA_KEY_WIDTH), bwd(GLA_WIDTH), bwd(2 * GLA_KEY_WIDTH)],
        out_specs=(fwd(GLA_WIDTH), bwd(GLA_WIDTH)),
        out_shape=(out, out),
        scratch_shapes=[pltpu.VMEM((GLA_HEADS, GLA_VDIM, LANES), F32),
                        pltpu.VMEM((GLA_HEADS, GLA_VDIM, LANES), F32)],
        compiler_params=_cparams(("parallel", "arbitrary")),
        name="gla",
    )(lqk3, lvr3, gfb3, lqk3, lvr3, gfb3)


def _head_norm(z, g):
    return jnp.concatenate(
        [_rms(z[:, h * LANES:(h + 1) * LANES], g) for h in range(z.shape[1] // LANES)], axis=1)


def _merge_kernel(x_ref, oa_ref, of_ref, ob_ref, r_ref, sg_ref, subg_ref, glag_ref, wpa_ref, wpb_ref,
                  wout_ref, nffn_ref, rwt_ref, rb_ref,
                  x1_ref, h2_ref, ti_ref, rk_ref, tg_ref, cnt_ref, carry_sc, *, lam_init):
    tm = x_ref.shape[0]

    @pl.when(pl.program_id(0) == 0)
    def _():
        carry_sc[...] = jnp.zeros(carry_sc.shape, F32)

    oa = (_head_norm(oa_ref[...], subg_ref[...]) * (1.0 - lam_init)).astype(BF16)
    ob = _head_norm(of_ref[...] + ob_ref[...], glag_ref[...]) * jax.nn.silu(r_ref[...].astype(F32))
    pa = jnp.dot(oa, wpa_ref[...], preferred_element_type=F32)
    pb = jnp.dot(ob.astype(BF16), wpb_ref[...], preferred_element_type=F32)
    sg = sg_ref[...].astype(F32)
    merged = sg[:, :D_MODEL] * pa + sg[:, D_MODEL:] * pb
    x1 = x_ref[...] + jnp.dot(merged.astype(BF16), wout_ref[...], preferred_element_type=F32)
    x1_ref[...] = x1
    h2 = _rms(x1, nffn_ref[...])
    h2_ref[...] = h2

    logits = lax.dot_general(rwt_ref[...], h2.astype(BF16), NT_DIMS, preferred_element_type=F32) + rb_ref[...]
    eidx = lax.broadcasted_iota(I32, logits.shape, 0)
    vals, sels = [], []
    for k in range(TOP_K):
        mk = jnp.max(logits, axis=0, keepdims=True)
        ik = jnp.min(jnp.where(logits == mk, eidx, N_EXPERTS), axis=0, keepdims=True)
        sel = eidx == ik
        logits = jnp.where(sel, -jnp.inf, logits)
        vals.append(mk)
        sels.append(sel)
        ti_ref[k:k + 1, :] = ik
    ex = [jnp.exp(v - vals[0]) for v in vals]
    denom = ex[0] + ex[1] + ex[2] + ex[3]
    for k in range(TOP_K):
        tg_ref[k:k + 1, :] = ex[k] / denom
    tg_ref[TOP_K:, :] = jnp.zeros((tg_ref.shape[0] - TOP_K, tm), F32)

    multi = (sels[0] | sels[1] | sels[2] | sels[3])
    multi_f = jnp.where(multi, 1.0, 0.0)
    before = lax.broadcasted_iota(I32, (tm, tm), 0) < lax.broadcasted_iota(I32, (tm, tm), 1)
    cum = jnp.dot(multi_f.astype(BF16), jnp.where(before, 1.0, 0.0).astype(BF16), preferred_element_type=F32)
    tot = carry_sc[:, 0:1] + cum
    for k in range(TOP_K):
        rk_ref[k:k + 1, :] = jnp.sum(jnp.where(sels[k], tot, 0.0), axis=0, keepdims=True).astype(I32)
    carry_sc[...] = carry_sc[...] + jnp.sum(multi_f, axis=1, keepdims=True)
    cnt_ref[...] = carry_sc[...]


def _merge(x2, oa, of, ob, lvr, sg, subg, glag, wpa, wpb, wout, nffn, rwt, rb, lam_init):
    T, D = x2.shape
    tm = TOKEN_TILE
    row = lambda n: pl.BlockSpec((tm, n), lambda i: (i, 0))
    col = lambda n: pl.BlockSpec((n, tm), lambda i: (0, i))
    out_shape = (
        jax.ShapeDtypeStruct((T, D), F32),
        jax.ShapeDtypeStruct((T, D), F32),
        jax.ShapeDtypeStruct((TOP_K, T), I32),
        jax.ShapeDtypeStruct((TOP_K, T), I32),
        jax.ShapeDtypeStruct((SUBLANES, T), F32),
        jax.ShapeDtypeStruct((N_EXPERTS, LANES), F32),
    )
    return pl.pallas_call(
        functools.partial(_merge_kernel, lam_init=lam_init),
        grid=(T // tm,),
        in_specs=[row(D), row(DA_WIDTH), row(GLA_WIDTH), row(GLA_WIDTH),
                  pl.BlockSpec((tm, GLA_WIDTH), lambda i: (i, 1)), row(2 * D),
                  _full(subg.shape), _full(glag.shape), _full(wpa.shape), _full(wpb.shape),
                  _full(wout.shape), _full(nffn.shape), _full(rwt.shape), _full(rb.shape)],
        out_specs=(row(D), row(D), col(TOP_K), col(TOP_K), col(SUBLANES), _full((N_EXPERTS, LANES))),
        out_shape=out_shape,
        scratch_shapes=[pltpu.VMEM((N_EXPERTS, LANES), F32)],
        compiler_params=_cparams(("arbitrary",)),
        name="merge_router",
    )(x2, oa, of, ob, lvr, sg, subg, glag, wpa, wpb, wout, nffn, rwt, rb)


def _dispatch_kernel(pend_ref, dest_ref, h_ref, xs_ref, zero_sc, sem):
    tm = h_ref.shape[0]
    rows = zero_sc.shape[0]

    @pl.when(pl.program_id(0) == 0)
    def _():
        zero_sc[...] = jnp.zeros(zero_sc.shape, F32)
        for e in range(N_EXPERTS):
            end = pend_ref[e]
            begin = pend_ref[e - 1] if e else 0

            @pl.when(end > begin)
            def _():
                last = pl.multiple_of(end - rows, rows)
                cp = pltpu.make_async_copy(zero_sc, xs_ref.at[pl.ds(last, rows)], sem)
                cp.start()
                cp.wait()

        def zero_unused(blk, c):
            cp = pltpu.make_async_copy(zero_sc, xs_ref.at[pl.ds(pl.multiple_of(blk * rows, rows), rows)], sem)
            cp.start()
            cp.wait()
            return c

        lax.fori_loop(pend_ref[N_EXPERTS - 1] // rows, xs_ref.shape[0] // rows, zero_unused, 0)

    def row_copy(t, k):
        return pltpu.make_async_copy(h_ref.at[pl.ds(t, 1)], xs_ref.at[pl.ds(dest_ref[k, t], 1)], sem)

    def start(t, c):
        for k in range(TOP_K):
            row_copy(t, k).start(priority=k % 2)
        return c

    def wait(t, c):
        for k in range(TOP_K):
            row_copy(t, k).wait()
        return c

    lax.fori_loop(0, tm, start, 0)
    lax.fori_loop(0, tm, wait, 0)


def _dispatch(h2, dest, pad_end, n_slots):
    T, D = h2.shape
    tm = TOKEN_TILE
    return pl.pallas_call(
        _dispatch_kernel,
        grid_spec=pltpu.PrefetchScalarGridSpec(
            num_scalar_prefetch=1,
            grid=(T // tm,),
            in_specs=[pl.BlockSpec((TOP_K, tm), lambda i, pe: (0, i), memory_space=pltpu.SMEM),
                      pl.BlockSpec((tm, D), lambda i, pe: (i, 0))],
            out_specs=pl.BlockSpec(memory_space=pl.ANY),
            scratch_shapes=[pltpu.VMEM((EXPERT_ROWS, D), F32), pltpu.SemaphoreType.DMA(())],
        ),
        out_shape=jax.ShapeDtypeStruct((n_slots, D), F32),
        compiler_params=_cparams(("arbitrary",)),
        name="moe_dispatch",
    )(pad_end, dest, h2)


def _deinterleave_kernel(w_ref, o_ref):
    n = 2 * LANES
    src = lax.broadcasted_iota(I32, (n, n), 0)
    dst = lax.broadcasted_iota(I32, (n, n), 1)
    perm = jnp.where(src == jnp.where(dst < LANES, 2 * dst, 2 * (dst - LANES) + 1), 1.0, 0.0).astype(BF16)
    for c in range(w_ref.shape[2] // n):
        w = w_ref[0, :, c * n:(c + 1) * n].astype(BF16)
        o_ref[0, :, c * n:(c + 1) * n] = jnp.dot(w, perm, preferred_element_type=F32).astype(BF16)


def _deinterleave_w1(w1, cols=512):
    E, K, N = w1.shape
    spec = pl.BlockSpec((1, K, cols), lambda e, j: (e, 0, j))
    return pl.pallas_call(
        _deinterleave_kernel,
        grid=(E, N // cols),
        in_specs=[spec],
        out_specs=spec,
        out_shape=jax.ShapeDtypeStruct(w1.shape, BF16),
        compiler_params=_cparams(("parallel", "parallel")),
        name="moe_w1_layout",
    )(w1)


def _expert_kernel(blk_ref, nused_ref, xs_ref, w1_ref, b1_ref, w2_ref, b2_ref, ys_ref):
    @pl.when(pl.program_id(0) < nused_ref[0])
    def _():
        x = xs_ref[...].astype(BF16)
        hid = jnp.dot(x, w1_ref[0], preferred_element_type=F32) + b1_ref[0]
        acts = []
        for c in range(hid.shape[1] // (2 * LANES)):
            gate = jnp.minimum(hid[:, 2 * c * LANES:(2 * c + 1) * LANES], SWIGLU_LIMIT)
            up = jnp.clip(hid[:, (2 * c + 1) * LANES:(2 * c + 2) * LANES], -SWIGLU_LIMIT, SWIGLU_LIMIT)
            acts.append(((up + 1.0) * (gate * jax.nn.sigmoid(SWIGLU_ALPHA * gate))).astype(BF16))
        act = jnp.concatenate(acts, axis=1)
        ys_ref[...] = jnp.dot(act, w2_ref[0], preferred_element_type=F32) + b2_ref[0]

    @pl.when(pl.program_id(0) >= nused_ref[0])
    def _():
        ys_ref[...] = jnp.zeros(ys_ref.shape, F32)


def _experts(xs, blk_e, nused, w1, b1, w2, b2):
    P, D = xs.shape
    bm = EXPERT_ROWS
    nb = P // bm
    rows = pl.BlockSpec((bm, D), lambda i, blk, nu: (jnp.minimum(i, nu[0] - 1), 0))
    per_expert = lambda a: pl.BlockSpec((1,) + a.shape[1:], lambda i, blk, nu: (blk[i], 0, 0))
    return pl.pallas_call(
        _expert_kernel,
        grid_spec=pltpu.PrefetchScalarGridSpec(
            num_scalar_prefetch=2,
            grid=(nb,),
            in_specs=[rows, per_expert(w1), per_expert(b1), per_expert(w2), per_expert(b2)],
            out_specs=pl.BlockSpec((bm, D), lambda i, blk, nu: (i, 0)),
        ),
        out_shape=jax.ShapeDtypeStruct((P, D), F32),
        compiler_params=_cparams(("arbitrary",)),
        name="moe_experts",
    )(blk_e, nused, xs, w1, b1, w2, b2)


def _combine_kernel(dest_ref, ys_ref, tg_ref, x1_ref, gfin_ref, y_ref, buf, sem, *, final):
    tc = x1_ref.shape[0]

    def row_copy(t, k):
        return pltpu.make_async_copy(ys_ref.at[pl.ds(dest_ref[k, t], 1)], buf.at[k, pl.ds(t, 1)], sem)

    def start(t, c):
        for k in range(TOP_K):
            row_copy(t, k).start(priority=k % 2)
        return c

    def wait(t, c):
        for k in range(TOP_K):
            row_copy(t, k).wait()
        return c

    lax.fori_loop(0, tc, start, 0)
    lax.fori_loop(0, tc, wait, 0)
    gates = tg_ref[...].T
    x = x1_ref[...]
    for k in range(TOP_K):
        x = x + buf[k] * gates[:, k:k + 1]
    y_ref[...] = _rms(x, gfin_ref[...]) if final else x


def _combine(ys, dest, tg, x1, gfin, final, tc=256):
    T, D = x1.shape
    return pl.pallas_call(
        functools.partial(_combine_kernel, final=final),
        grid=(T // tc,),
        in_specs=[pl.BlockSpec((TOP_K, tc), lambda i: (0, i), memory_space=pltpu.SMEM),
                  pl.BlockSpec(memory_space=pl.ANY),
                  pl.BlockSpec((SUBLANES, tc), lambda i: (0, i)),
                  pl.BlockSpec((tc, D), lambda i: (i, 0)),
                  _full(gfin.shape)],
        out_specs=pl.BlockSpec((tc, D), lambda i: (i, 0)),
        out_shape=jax.ShapeDtypeStruct((T, D), F32),
        scratch_shapes=[pltpu.VMEM((TOP_K, tc, D), F32), pltpu.SemaphoreType.DMA(())],
        compiler_params=_cparams(("parallel",)),
        name="moe_combine",
    )(dest, ys, tg, x1, gfin)


def _moe(h2, ti, rk, tg, cnt, x1, moe_w, gfin, final):
    T, D = h2.shape
    bm = EXPERT_ROWS
    n_assign = T * TOP_K
    nb = -(-(n_assign + N_EXPERTS * (bm - 1)) // bm)
    counts = cnt[:, 0].astype(I32)
    padded = (counts + bm - 1) // bm * bm
    pad_end = jnp.cumsum(padded).astype(I32)
    pad_start = pad_end - padded
    experts = jnp.arange(N_EXPERTS, dtype=I32)[:, None, None]
    dest = rk + jnp.sum(jnp.where(ti[None] == experts, pad_start[:, None, None], 0), axis=0)
    nused = (pad_end[-1:] // bm).astype(I32)
    blk_start = jnp.minimum(jnp.arange(nb, dtype=I32), nused[0] - 1) * bm
    blk_e = jnp.minimum(jnp.sum((pad_end[None, :] <= blk_start[:, None]).astype(I32), axis=1), N_EXPERTS - 1)
    xs = _dispatch(h2, dest, pad_end, nb * bm)
    ys = _experts(xs, blk_e, nused, *moe_w)
    return _combine(ys, dest, tg, x1, gfin, final)


def _split_in_proj(w):
    sizes = (DA_QK_WIDTH, DA_QK_WIDTH, DA_WIDTH, GLA_KEY_WIDTH, GLA_KEY_WIDTH, GLA_WIDTH, GLA_WIDTH,
             GLA_GATE_RANK, GLA_GATE_RANK, D_MODEL, D_MODEL)
    assert w.shape[-1] == sum(sizes)
    out, o = [], 0
    for s in sizes:
        out.append(w[:, o:o + s])
        o += s
    return out


def _prep_layer(l, norm_mix_g, w_in, lambda_q1, lambda_k1, lambda_q2, lambda_k2, diff_subln_g,
                gla_gate_up_f, gla_gate_bias_f, gla_gate_up_b, gla_gate_bias_b, gla_norm_g,
                w_proj_a, w_proj_b, w_out, norm_ffn_g, router_w, router_b, moe_w1, moe_b1, moe_w2, moe_b2):
    dq, dk, dv, lq, lk, lv, lr, laf, lab, ga, gb = _split_in_proj(w_in[l])
    bf = lambda a: a.astype(BF16)
    r = GLA_GATE_RANK
    gup = jnp.zeros((LANES, 2 * GLA_KEY_WIDTH), F32)
    gup = gup.at[0:r, 0:GLA_KEY_WIDTH].set(gla_gate_up_f[l]).at[r:2 * r, GLA_KEY_WIDTH:].set(gla_gate_up_b[l])
    lamv = jnp.zeros((SUBLANES, LANES), F32)
    for i, v in enumerate((lambda_q1, lambda_k1, lambda_q2, lambda_k2)):
        lamv = lamv.at[i, 0:DA_HEAD_DIM].set(v[l].astype(F32))
    inproj = dict(
        norm_g=norm_mix_g[l][None, :],
        wqk=bf(jnp.concatenate([dq * (DA_HEAD_DIM ** -0.5), dk], 1)),
        wvt=bf(dv.T),
        wlqk=bf(jnp.concatenate([lq, lk], 1)),
        wlvr=bf(jnp.concatenate([lv, lr], 1)),
        wg=bf(jnp.concatenate([ga, gb], 1)),
        wlow=bf(jnp.concatenate([laf, lab, jnp.zeros((D_MODEL, LANES - 2 * r), F32)], 1)),
        gup=bf(gup),
        gbias=jnp.concatenate([gla_gate_bias_f[l], gla_gate_bias_b[l]])[None, :],
    )
    merge = dict(
        subg=diff_subln_g[l][None, :], glag=gla_norm_g[l][None, :],
        wpa=bf(w_proj_a[l]), wpb=bf(w_proj_b[l]), wout=bf(w_out[l]),
        nffn=norm_ffn_g[l][None, :], rwt=bf(router_w[l].T), rb=router_b[l][:, None],
    )
    b1 = moe_b1[l].reshape(N_EXPERTS, -1, LANES, 2).transpose(0, 1, 3, 2).reshape(N_EXPERTS, 1, -1)
    moe_w = (_deinterleave_w1(moe_w1[l]), b1, bf(moe_w2[l]), moe_b2[l][:, None, :])
    return inproj, lamv, merge, moe_w


def _trunk(x, layers, norm_final_g):
    B, L, D = x.shape
    T = B * L
    cos_t, sin_t = _rope_tables(L)
    x2 = x.reshape(T, D)
    gfin = norm_final_g[None, :]
    for l, (inproj, lamv, merge, moe_w) in enumerate(layers):
        lam_init = 0.8 - 0.6 * math.exp(-0.3 * l)
        qk, vt, lqk, lvr, gfb, sg = _inproj(x2, L, cos_t=cos_t, sin_t=sin_t, **inproj)
        oa = _diff_attention(qk.reshape(B, L, -1), vt, lamv, lam_init)
        of, ob = _gla(lqk.reshape(B, L, -1), lvr.reshape(B, L, -1), gfb.reshape(B, L, -1))
        x1, h2, ti, rk, tg, cnt = _merge(x2, oa.reshape(T, -1), of.reshape(T, -1), ob.reshape(T, -1),
                                         lvr, sg, lam_init=lam_init, **merge)
        x2 = _moe(h2, ti, rk, tg, cnt, x1, moe_w, gfin, final=(l == len(layers) - 1))
    return x2.reshape(B, L, D)


def kernel(x_prompt, x_sample, norm_mix_g, w_in, lambda_q1, lambda_k1, lambda_q2, lambda_k2, diff_subln_g,
           gla_gate_up_f, gla_gate_bias_f, gla_gate_up_b, gla_gate_bias_b, gla_norm_g, w_proj_a, w_proj_b,
           w_out, norm_ffn_g, router_w, router_b, moe_w1, moe_b1, moe_w2, moe_b2, norm_final_g):
    depth = w_in.shape[0]
    layers = [_prep_layer(l, norm_mix_g, w_in, lambda_q1, lambda_k1, lambda_q2, lambda_k2, diff_subln_g,
                          gla_gate_up_f, gla_gate_bias_f, gla_gate_up_b, gla_gate_bias_b, gla_norm_g,
                          w_proj_a, w_proj_b, w_out, norm_ffn_g, router_w, router_b,
                          moe_w1, moe_b1, moe_w2, moe_b2) for l in range(depth)]
    return _trunk(x_prompt, layers, norm_final_g), _trunk(x_sample, layers, norm_final_g)
```

```python
import functools
import math

import jax
import jax.numpy as jnp
from jax import lax
from jax.experimental import pallas as pl
from jax.experimental.pallas import tpu as pltpu

F32 = jnp.float32
BF16 = jnp.bfloat16
I32 = jnp.int32

D_MODEL = 1024
DA_HEADS = 4
DA_HEAD_DIM = 64
DA_VDIM = 128
DA_QK_WIDTH = DA_HEADS * 2 * DA_HEAD_DIM
DA_WIDTH = DA_HEADS * DA_VDIM
ROPE_THETA = 10000.0
GLA_HEADS = 4
GLA_KDIM = 64
GLA_VDIM = 128
GLA_KEY_WIDTH = GLA_HEADS * GLA_KDIM
GLA_WIDTH = GLA_HEADS * GLA_VDIM
GLA_GATE_RANK = 16
GLA_GATE_NORMALIZER = 16.0
GLA_CHUNK = 64
N_EXPERTS = 32
TOP_K = 4
D_FF = D_MODEL
SWIGLU_ALPHA = 1.702
SWIGLU_LIMIT = 7.0
NORM_EPS = 1e-5
LANES = 128
SUBLANES = 8

BF16_SUBLANES = 16
LOG2_E = 1.4426950408889634
VT_ROWS = DA_VDIM + BF16_SUBLANES

VMEM_LIMIT = 56 * 1024 * 1024

TOKEN_TILE = 512
EXPERT_ROWS = 256

NT_DIMS = (((1,), (1,)), ((), ()))
TN_DIMS = (((0,), (0,)), ((), ()))


def _cparams(sem):
    return pltpu.CompilerParams(dimension_semantics=sem, vmem_limit_bytes=VMEM_LIMIT)


def _full(shape):
    nd = len(shape)
    return pl.BlockSpec(shape, lambda *_: (0,) * nd)


def _rms(x, g):
    return x * lax.rsqrt(jnp.mean(x * x, axis=-1, keepdims=True) + NORM_EPS) * g


ROW_TILE = (SUBLANES, LANES)


def _vreg_tile_shape(rows, cols):
    return (rows // SUBLANES, cols // LANES, SUBLANES, LANES)


def _vreg_tile_row(ref, r):
    return ref.at[r // SUBLANES, :, r % SUBLANES, :]


def _store_vreg_tiles(ref, x):
    for b in range(ref.shape[1]):
        ref[:, b, :, :] = x[:, b * LANES:(b + 1) * LANES].reshape(ref.shape[0], SUBLANES, LANES)


def _load_vreg_tiles(ref):
    rows = ref.shape[0] * SUBLANES
    return jnp.concatenate([ref[:, b, :, :].reshape(rows, LANES) for b in range(ref.shape[1])], axis=1)


def _relayout_copies(flat_ref, tiled_ref, sem, to_tiled):
    cps = []
    for c in range(SUBLANES):
        flat = flat_ref.at[:, pl.ds(c * LANES, LANES)]
        tiled = tiled_ref.at[:, c, :]
        cps.append(pltpu.make_async_copy(flat, tiled, sem) if to_tiled else pltpu.make_async_copy(tiled, flat, sem))
    return cps


def _log_sigmoid(x):
    return jnp.minimum(x, 0.0) - jnp.log1p(jnp.exp(-jnp.abs(x)))


def _inproj_kernel(x_ref, g_ref, cos_ref, sin_ref, wqk_ref, wvt_ref, wlqk_ref, wlvr_ref,
                   wg_ref, wlow_ref, gup_ref, gbias_ref,
                   qk_ref, vt_ref, lqk_ref, lvr_ref, gfb_ref, sg_ref):
    h = _rms(x_ref[...], g_ref[...]).astype(BF16)

    cos = cos_ref[...]
    sin_signed = sin_ref[...]
    first_half = (lax.broadcasted_iota(I32, (1, LANES), 1) % DA_HEAD_DIM) < (DA_HEAD_DIM // 2)
    for c in range(2 * DA_QK_WIDTH // LANES):
        z = jnp.dot(h, wqk_ref[:, c * LANES:(c + 1) * LANES], preferred_element_type=F32)
        if c < DA_QK_WIDTH // LANES:
            z = z * LOG2_E
        partner = jnp.where(first_half, pltpu.roll(z, LANES - DA_HEAD_DIM // 2, 1),
                            pltpu.roll(z, DA_HEAD_DIM // 2, 1))
        qk_ref[:, c * LANES:(c + 1) * LANES] = (z * cos + partner * sin_signed).astype(BF16)

    vt = lax.dot_general(wvt_ref[...], h, NT_DIMS, preferred_element_type=F32).astype(BF16)
    for hd in range(DA_HEADS):
        vt_ref[0, hd * VT_ROWS:hd * VT_ROWS + DA_VDIM, :] = vt[hd * DA_VDIM:(hd + 1) * DA_VDIM]
        vt_ref[0, hd * VT_ROWS + DA_VDIM:(hd + 1) * VT_ROWS, :] = jnp.ones((VT_ROWS - DA_VDIM, vt.shape[1]), BF16)
    lqk_ref[...] = jnp.dot(h, wlqk_ref[...], preferred_element_type=F32)
    lvr_ref[...] = jnp.dot(h, wlvr_ref[...], preferred_element_type=F32).astype(BF16)
    sg_ref[...] = jax.nn.sigmoid(jnp.dot(h, wg_ref[...], preferred_element_type=F32)).astype(BF16)
    low = jnp.dot(h, wlow_ref[...], preferred_element_type=F32).astype(BF16)
    pre = jnp.dot(low, gup_ref[...], preferred_element_type=F32) + gbias_ref[...]
    gfb_ref[...] = _log_sigmoid(pre) * (1.0 / GLA_GATE_NORMALIZER)


def _inproj(x2, seq_len, norm_g, wqk, wvt, wlqk, wlvr, wg, wlow, gup, gbias, cos_t, sin_t):
    T, D = x2.shape
    tm = TOKEN_TILE
    assert T % tm == 0 and seq_len % tm == 0
    nl = seq_len // tm
    row = lambda n: pl.BlockSpec((tm, n), lambda i: (i, 0))
    out_shape = (
        jax.ShapeDtypeStruct((T, 2 * DA_QK_WIDTH), BF16),
        jax.ShapeDtypeStruct((T // tm, DA_HEADS * VT_ROWS, tm), BF16),
        jax.ShapeDtypeStruct((T, 2 * GLA_KEY_WIDTH), F32),
        jax.ShapeDtypeStruct((T, 2 * GLA_WIDTH), BF16),
        jax.ShapeDtypeStruct((T, 2 * GLA_KEY_WIDTH), F32),
        jax.ShapeDtypeStruct((T, 2 * D_MODEL), BF16),
    )
    return pl.pallas_call(
        _inproj_kernel,
        grid=(T // tm,),
        in_specs=[row(D), _full(norm_g.shape),
                  pl.BlockSpec((tm, LANES), lambda i: (i % nl, 0)),
                  pl.BlockSpec((tm, LANES), lambda i: (i % nl, 0)),
                  _full(wqk.shape), _full(wvt.shape), _full(wlqk.shape), _full(wlvr.shape),
                  _full(wg.shape), _full(wlow.shape), _full(gup.shape), _full(gbias.shape)],
        out_specs=(row(2 * DA_QK_WIDTH), pl.BlockSpec((1, DA_HEADS * VT_ROWS, tm), lambda i: (i, 0, 0)),
                   row(2 * GLA_KEY_WIDTH), row(2 * GLA_WIDTH), row(2 * GLA_KEY_WIDTH), row(2 * D_MODEL)),
        out_shape=out_shape,
        compiler_params=_cparams(("parallel",)),
        name="inproj",
    )(x2, norm_g, cos_t, sin_t, wqk, wvt, wlqk, wlvr, wg, wlow, gup, gbias)


def _rope_tables(seq_len):
    d = DA_HEAD_DIM
    inv = ROPE_THETA ** (-jnp.arange(0, d, 2, dtype=F32) / d)
    ang = jnp.arange(seq_len, dtype=F32)[:, None] * inv[None, :]
    cos = jnp.concatenate([jnp.cos(ang)] * (2 * LANES // d), -1)
    sin = jnp.sin(ang)
    sin_signed = jnp.concatenate([-sin, sin] * (LANES // d), -1)
    return cos, sin_signed


def _attn_kernel(lamv_ref, q_ref, k_ref, vt_ref, o_ref, acc_sc, sa_sc, sb_sc, *, lam_init):
    nk, _, tk = vt_ref.shape
    tq = q_ref.shape[1]
    q = q_ref[0]
    lane = lax.broadcasted_iota(I32, (1, LANES), 1)
    zeros = jnp.zeros_like(q)
    qms = (jnp.where(lane < DA_HEAD_DIM, q, zeros), jnp.where(lane >= DA_HEAD_DIM, q, zeros))
    acc_sc[...] = jnp.zeros(acc_sc.shape, F32)

    def scores(j, dst_ref):
        kj = k_ref[0, pl.ds(pl.multiple_of(j * tk, tk), tk), :]
        for mp in range(2):
            dst_ref[mp] = lax.dot_general(kj, qms[mp], NT_DIMS, preferred_element_type=F32)

    def softmax_pv(j, src_ref, ms):
        vj = vt_ref[j]
        new = []
        for mp in range(2):
            st = src_ref[mp]
            m_new = jnp.maximum(ms[mp], jnp.max(st, axis=0, keepdims=True))
            p = jnp.exp2((st - m_new).astype(BF16))
            acc_sc[mp] = acc_sc[mp] * jnp.exp2(ms[mp] - m_new) + jnp.dot(vj, p, preferred_element_type=F32)
            new.append(m_new)
        return tuple(new)

    def pair(i, ms):
        j = 2 * i
        scores(j + 1, sb_sc)
        ms = softmax_pv(j, sa_sc, ms)
        scores(j + 2, sa_sc)
        return softmax_pv(j + 1, sb_sc, ms)

    assert nk % 2 == 0
    scores(0, sa_sc)
    ms = tuple(jnp.full((1, tq), -jnp.inf, F32) for _ in range(2))
    ms = lax.fori_loop(0, (nk - 2) // 2, pair, ms)
    scores(nk - 1, sb_sc)
    ms = softmax_pv(nk - 2, sa_sc, ms)
    softmax_pv(nk - 1, sb_sc, ms)

    lv = lamv_ref[...]
    lam = (jnp.exp(jnp.sum(lv[0:1] * lv[1:2], axis=-1, keepdims=True))
           - jnp.exp(jnp.sum(lv[2:3] * lv[3:4], axis=-1, keepdims=True)) + lam_init)
    a0, a1 = acc_sc[0], acc_sc[1]
    o_ref[0] = (a0[:DA_VDIM] / a0[DA_VDIM:DA_VDIM + 1] - lam * (a1[:DA_VDIM] / a1[DA_VDIM:DA_VDIM + 1])).T


def _diff_attention(qk3, vt3, lamv, lam_init, tq=512):
    B, L, _ = qk3.shape
    tk = vt3.shape[-1]
    nkb = L // tk
    return pl.pallas_call(
        functools.partial(_attn_kernel, lam_init=lam_init),
        grid=(B, DA_HEADS, L // tq),
        in_specs=[_full(lamv.shape),
                  pl.BlockSpec((1, tq, LANES), lambda b, h, i: (b, i, h)),
                  pl.BlockSpec((1, L, LANES), lambda b, h, i: (b, 0, DA_HEADS + h)),
                  pl.BlockSpec((nkb, VT_ROWS, tk), lambda b, h, i: (b, h, 0))],
        out_specs=pl.BlockSpec((1, tq, DA_VDIM), lambda b, h, i: (b, i, h)),
        out_shape=jax.ShapeDtypeStruct((B, L, DA_WIDTH), F32),
        scratch_shapes=[pltpu.VMEM((2, VT_ROWS, tq), F32), pltpu.VMEM((2, tk, tq), F32),
                        pltpu.VMEM((2, tk, tq), F32)],
        compiler_params=_cparams(("parallel", "parallel", "arbitrary")),
        name="diff_attention",
    )(lamv, qk3, qk3, vt3)


def _gla_direction(qk_ref, vr_ref, g_ref, o_ref, st_ref, *, reverse):
    C = GLA_CHUNK
    nchunk = qk_ref.shape[1] // C
    row = lax.broadcasted_iota(I32, (C, C), 0)
    col = lax.broadcasted_iota(I32, (C, C), 1)
    tri = (col >= row) if reverse else (col <= row)
    tri_f = tri.astype(F32)
    lane = lax.broadcasted_iota(I32, (1, LANES), 1)
    goff = GLA_KEY_WIDTH if reverse else 0
    scale = GLA_KDIM ** -0.5
    order = range(nchunk - 1, -1, -1) if reverse else range(nchunk)
    for c in order:
        rows = slice(c * C, (c + 1) * C)
        g = g_ref[0, rows, goff:goff + GLA_KEY_WIDTH]
        q = qk_ref[0, rows, 0:GLA_KEY_WIDTH] * scale
        k = qk_ref[0, rows, GLA_KEY_WIDTH:2 * GLA_KEY_WIDTH]
        v = vr_ref[0, rows, :]
        b = jnp.dot(tri_f, g, preferred_element_type=F32, precision=lax.Precision.HIGHEST)
        if reverse:
            b_last, r = b[0:1], b[C // 2:C // 2 + 1]
        else:
            b_last, r = b[C - 1:C], b[C // 2 - 1:C // 2]
        q_in = q * jnp.exp(b - r)
        k_in = (k * jnp.exp(r - b)).astype(BF16)
        k_dec = k * jnp.exp(b_last - b)
        q_dec = q * jnp.exp(b)
        dec = jnp.exp(b_last)
        for h in range(GLA_HEADS):
            ps = slice((h // 2) * LANES, (h // 2 + 1) * LANES)
            vs = slice(h * GLA_VDIM, (h + 1) * GLA_VDIM)
            keep = (lane < GLA_KDIM) if h % 2 == 0 else (lane >= GLA_KDIM)
            zero = jnp.zeros((C, LANES), F32)
            q_in_h = jnp.where(keep, q_in[:, ps], zero).astype(BF16)
            q_dec_h = jnp.where(keep, q_dec[:, ps], zero).astype(BF16)
            k_dec_h = jnp.where(keep, k_dec[:, ps], zero).astype(BF16)
            a = lax.dot_general(q_in_h, k_in[:, ps], NT_DIMS, preferred_element_type=F32)
            a = jnp.where(tri, a, 0.0).astype(BF16)
            s_t = st_ref[h]
            o = jnp.dot(a, v[:, vs], preferred_element_type=F32)
            o = o + lax.dot_general(q_dec_h, s_t.astype(BF16), NT_DIMS, preferred_element_type=F32)
            o_ref[0, rows, vs] = o
            kv_t = lax.dot_general(v[:, vs], k_dec_h, TN_DIMS, preferred_element_type=F32)
            st_ref[h] = s_t * dec[:, ps] + kv_t


def _gla_kernel(qkf_ref, vrf_ref, gf_ref, qkb_ref, vrb_ref, gb_ref, of_ref, ob_ref, sf_sc, sb_sc):
    @pl.when(pl.program_id(1) == 0)
    def _():
        sf_sc[...] = jnp.zeros(sf_sc.shape, F32)
        sb_sc[...] = jnp.zeros(sb_sc.shape, F32)

    _gla_direction(qkf_ref, vrf_ref, gf_ref, of_ref, sf_sc, reverse=False)
    _gla_direction(qkb_ref, vrb_ref, gb_ref, ob_ref, sb_sc, reverse=True)


def _gla(lqk3, lvr3, gfb3, tb=256):
    B, L, _ = lqk3.shape
    nb = L // tb
    fwd = lambda n: pl.BlockSpec((1, tb, n), lambda b, i: (b, i, 0))
    bwd = lambda n: pl.BlockSpec((1, tb, n), lambda b, i: (b, nb - 1 - i, 0))
    out = jax.ShapeDtypeStruct((B, L, GLA_WIDTH), F32)
    return pl.pallas_call(
        _gla_kernel,
        grid=(B, nb),
        in_specs=[fwd(2 * GLA_KEY_WIDTH), fwd(GLA_WIDTH), fwd(2 * GLA_KEY_WIDTH),
                  bwd(2 * GLA_KEY_WIDTH), bwd(GLA_WIDTH), bwd(2 * GLA_KEY_WIDTH)],
        out_specs=(fwd(GLA_WIDTH), bwd(GLA_WIDTH)),
        out_shape=(out, out),
        scratch_shapes=[pltpu.VMEM((GLA_HEADS, GLA_VDIM, LANES), F32),
                        pltpu.VMEM((GLA_HEADS, GLA_VDIM, LANES), F32)],
        compiler_params=_cparams(("parallel", "arbitrary")),
        name="gla",
    )(lqk3, lvr3, gfb3, lqk3, lvr3, gfb3)


def _head_norm(z, g):
    return jnp.concatenate(
        [_rms(z[:, h * LANES:(h + 1) * LANES], g) for h in range(z.shape[1] // LANES)], axis=1)


def _merge_kernel(x_ref, oa_ref, of_ref, ob_ref, r_ref, sg_ref, subg_ref, glag_ref, wpa_ref, wpb_ref,
                  wout_ref, nffn_ref, rwt_ref, rb_ref,
                  x1_ref, h2_ref, ti_ref, rk_ref, tg_ref, cnt_ref, carry_sc, *, lam_init):
    tm = x_ref.shape[0]

    @pl.when(pl.program_id(0) == 0)
    def _():
        carry_sc[...] = jnp.zeros(carry_sc.shape, F32)

    oa = (_head_norm(oa_ref[...], subg_ref[...]) * (1.0 - lam_init)).astype(BF16)
    ob = _head_norm(of_ref[...] + ob_ref[...], glag_ref[...]) * jax.nn.silu(r_ref[...].astype(F32))
    pa = jnp.dot(oa, wpa_ref[...], preferred_element_type=F32)
    pb = jnp.dot(ob.astype(BF16), wpb_ref[...], preferred_element_type=F32)
    sg = sg_ref[...].astype(F32)
    merged = sg[:, :D_MODEL] * pa + sg[:, D_MODEL:] * pb
    x1 = x_ref[...] + jnp.dot(merged.astype(BF16), wout_ref[...], preferred_element_type=F32)
    x1_ref[...] = x1
    h2 = _rms(x1, nffn_ref[...])
    _store_vreg_tiles(h2_ref, h2)

    logits = lax.dot_general(rwt_ref[...], h2.astype(BF16), NT_DIMS, preferred_element_type=F32) + rb_ref[...]
    eidx = lax.broadcasted_iota(I32, logits.shape, 0)
    vals, sels = [], []
    for k in range(TOP_K):
        mk = jnp.max(logits, axis=0, keepdims=True)
        ik = jnp.min(jnp.where(logits == mk, eidx, N_EXPERTS), axis=0, keepdims=True)
        sel = eidx == ik
        logits = jnp.where(sel, -jnp.inf, logits)
        vals.append(mk)
        sels.append(sel)
        ti_ref[k:k + 1, :] = ik
    ex = [jnp.exp(v - vals[0]) for v in vals]
    denom = ex[0] + ex[1] + ex[2] + ex[3]
    for k in range(TOP_K):
        tg_ref[k:k + 1, :] = ex[k] / denom
    tg_ref[TOP_K:, :] = jnp.zeros((tg_ref.shape[0] - TOP_K, tm), F32)

    multi = (sels[0] | sels[1] | sels[2] | sels[3])
    multi_f = jnp.where(multi, 1.0, 0.0)
    before = lax.broadcasted_iota(I32, (tm, tm), 0) < lax.broadcasted_iota(I32, (tm, tm), 1)
    cum = jnp.dot(multi_f.astype(BF16), jnp.where(before, 1.0, 0.0).astype(BF16), preferred_element_type=F32)
    tot = carry_sc[:, 0:1] + cum
    for k in range(TOP_K):
        rk_ref[k:k + 1, :] = jnp.sum(jnp.where(sels[k], tot, 0.0), axis=0, keepdims=True).astype(I32)
    carry_sc[...] = carry_sc[...] + jnp.sum(multi_f, axis=1, keepdims=True)
    cnt_ref[...] = carry_sc[...]


def _merge(x2, oa, of, ob, lvr, sg, subg, glag, wpa, wpb, wout, nffn, rwt, rb, lam_init):
    T, D = x2.shape
    tm = TOKEN_TILE
    row = lambda n: pl.BlockSpec((tm, n), lambda i: (i, 0))
    col = lambda n: pl.BlockSpec((n, tm), lambda i: (0, i))
    out_shape = (
        jax.ShapeDtypeStruct((T, D), F32),
        jax.ShapeDtypeStruct(_vreg_tile_shape(T, D), F32),
        jax.ShapeDtypeStruct((TOP_K, T), I32),
        jax.ShapeDtypeStruct((TOP_K, T), I32),
        jax.ShapeDtypeStruct((SUBLANES, T), F32),
        jax.ShapeDtypeStruct((N_EXPERTS, LANES), F32),
    )
    return pl.pallas_call(
        functools.partial(_merge_kernel, lam_init=lam_init),
        grid=(T // tm,),
        in_specs=[row(D), row(DA_WIDTH), row(GLA_WIDTH), row(GLA_WIDTH),
                  pl.BlockSpec((tm, GLA_WIDTH), lambda i: (i, 1)), row(2 * D),
                  _full(subg.shape), _full(glag.shape), _full(wpa.shape), _full(wpb.shape),
                  _full(wout.shape), _full(nffn.shape), _full(rwt.shape), _full(rb.shape)],
        out_specs=(row(D), pl.BlockSpec(_vreg_tile_shape(tm, D), lambda i: (i, 0, 0, 0)),
                   col(TOP_K), col(TOP_K), col(SUBLANES), _full((N_EXPERTS, LANES))),
        out_shape=out_shape,
        scratch_shapes=[pltpu.VMEM((N_EXPERTS, LANES), F32)],
        compiler_params=_cparams(("arbitrary",)),
        name="merge_router",
    )(x2, oa, of, ob, lvr, sg, subg, glag, wpa, wpb, wout, nffn, rwt, rb)


def _dispatch_kernel(pend_ref, dest_ref, h_ref, xs_ref, zero_sc, sem):
    tm = h_ref.shape[0] * SUBLANES
    rows = zero_sc.shape[0]

    @pl.when(pl.program_id(0) == 0)
    def _():
        zero_sc[...] = jnp.zeros(zero_sc.shape, F32)
        for e in range(N_EXPERTS):
            end = pend_ref[e]
            begin = pend_ref[e - 1] if e else 0

            @pl.when(end > begin)
            def _():
                last = pl.multiple_of(end - rows, rows)
                cp = pltpu.make_async_copy(zero_sc, xs_ref.at[pl.ds(last, rows)], sem)
                cp.start()
                cp.wait()

        def zero_unused(blk, c):
            cp = pltpu.make_async_copy(zero_sc, xs_ref.at[pl.ds(pl.multiple_of(blk * rows, rows), rows)], sem)
            cp.start()
            cp.wait()
            return c

        lax.fori_loop(pend_ref[N_EXPERTS - 1] // rows, xs_ref.shape[0] // rows, zero_unused, 0)

    def row_copy(g, s, k):
        return pltpu.make_async_copy(h_ref.at[g, :, s, :], xs_ref.at[dest_ref[k, g * SUBLANES + s]], sem)

    def start(g, c):
        for s in range(SUBLANES):
            for k in range(TOP_K):
                row_copy(g, s, k).start(priority=k % 2)
        return c

    def wait(g, c):
        for s in range(SUBLANES):
            for k in range(TOP_K):
                row_copy(g, s, k).wait()
        return c

    lax.fori_loop(0, tm // SUBLANES, start, 0)
    lax.fori_loop(0, tm // SUBLANES, wait, 0)


def _dispatch(h2, dest, pad_end, n_slots):
    T = h2.shape[0] * SUBLANES
    tm = TOKEN_TILE
    h_block = (tm // SUBLANES,) + h2.shape[1:]
    return pl.pallas_call(
        _dispatch_kernel,
        grid_spec=pltpu.PrefetchScalarGridSpec(
            num_scalar_prefetch=1,
            grid=(T // tm,),
            in_specs=[pl.BlockSpec((TOP_K, tm), lambda i, pe: (0, i), memory_space=pltpu.SMEM),
                      pl.BlockSpec(h_block, lambda i, pe: (i, 0, 0, 0))],
            out_specs=pl.BlockSpec(memory_space=pl.ANY),
            scratch_shapes=[pltpu.VMEM((EXPERT_ROWS,) + ROW_TILE, F32), pltpu.SemaphoreType.DMA(())],
        ),
        out_shape=jax.ShapeDtypeStruct((n_slots,) + ROW_TILE, F32),
        compiler_params=_cparams(("arbitrary",)),
        name="moe_dispatch",
    )(pad_end, dest, h2)


def _deinterleave_kernel(w_ref, o_ref):
    n = 2 * LANES
    src = lax.broadcasted_iota(I32, (n, n), 0)
    dst = lax.broadcasted_iota(I32, (n, n), 1)
    perm = jnp.where(src == jnp.where(dst < LANES, 2 * dst, 2 * (dst - LANES) + 1), 1.0, 0.0).astype(BF16)
    for c in range(w_ref.shape[2] // n):
        w = w_ref[0, :, c * n:(c + 1) * n].astype(BF16)
        o_ref[0, :, c * n:(c + 1) * n] = jnp.dot(w, perm, preferred_element_type=F32).astype(BF16)


def _deinterleave_w1(w1, cols=512):
    E, K, N = w1.shape
    spec = pl.BlockSpec((1, K, cols), lambda e, j: (e, 0, j))
    return pl.pallas_call(
        _deinterleave_kernel,
        grid=(E, N // cols),
        in_specs=[spec],
        out_specs=spec,
        out_shape=jax.ShapeDtypeStruct(w1.shape, BF16),
        compiler_params=_cparams(("parallel", "parallel")),
        name="moe_w1_layout",
    )(w1)


def _expert_kernel(blk_ref, nused_ref, xs_ref, w1_ref, b1_ref, w2_ref, b2_ref, ys_ref,
                   xbuf, ybuf, in_sem, out_sem):
    i = pl.program_id(0)
    n = pl.num_programs(0)
    bm = xbuf.shape[1]
    slot = i % 2

    def rows_of(ref, blk):
        return ref.at[pl.ds(pl.multiple_of(blk * bm, bm), bm)]

    def loads(blk, s):
        return _relayout_copies(xbuf.at[s], rows_of(xs_ref, blk), in_sem.at[s], to_tiled=False)

    def stores(blk, s):
        return _relayout_copies(ybuf.at[s], rows_of(ys_ref, blk), out_sem.at[s], to_tiled=True)

    @pl.when(i == 0)
    def _():
        for cp in loads(0, 0):
            cp.start()

    @pl.when(i + 1 < n)
    def _():
        for cp in loads(i + 1, 1 - slot):
            cp.start()

    for cp in loads(i, slot):
        cp.wait()

    @pl.when(i >= 2)
    def _():
        for cp in stores(i - 2, slot):
            cp.wait()

    @pl.when(i < nused_ref[0])
    def _():
        x = xbuf[slot].astype(BF16)
        hid = jnp.dot(x, w1_ref[0], preferred_element_type=F32) + b1_ref[0]
        acts = []
        for c in range(hid.shape[1] // (2 * LANES)):
            gate = jnp.minimum(hid[:, 2 * c * LANES:(2 * c + 1) * LANES], SWIGLU_LIMIT)
            up = jnp.clip(hid[:, (2 * c + 1) * LANES:(2 * c + 2) * LANES], -SWIGLU_LIMIT, SWIGLU_LIMIT)
            acts.append(((up + 1.0) * (gate * jax.nn.sigmoid(SWIGLU_ALPHA * gate))).astype(BF16))
        act = jnp.concatenate(acts, axis=1)
        ybuf[slot] = jnp.dot(act, w2_ref[0], preferred_element_type=F32) + b2_ref[0]

    @pl.when(i >= nused_ref[0])
    def _():
        ybuf[slot] = jnp.zeros(ybuf.shape[1:], F32)

    for cp in stores(i, slot):
        cp.start()

    @pl.when(i == n - 1)
    def _():
        for cp in stores(i, slot):
            cp.wait()

    @pl.when((i == n - 1) & (n >= 2))
    def _():
        for cp in stores(i - 1, 1 - slot):
            cp.wait()


def _experts(xs, blk_e, nused, w1, b1, w2, b2):
    P = xs.shape[0]
    D = w1.shape[1]
    bm = EXPERT_ROWS
    nb = P // bm
    per_expert = lambda a: pl.BlockSpec((1,) + a.shape[1:], lambda i, blk, nu: (blk[i], 0, 0))
    return pl.pallas_call(
        _expert_kernel,
        grid_spec=pltpu.PrefetchScalarGridSpec(
            num_scalar_prefetch=2,
            grid=(nb,),
            in_specs=[pl.BlockSpec(memory_space=pl.ANY), per_expert(w1), per_expert(b1), per_expert(w2),
                      per_expert(b2)],
            out_specs=pl.BlockSpec(memory_space=pl.ANY),
            scratch_shapes=[pltpu.VMEM((2, bm, D), F32), pltpu.VMEM((2, bm, D), F32),
                            pltpu.SemaphoreType.DMA((2,)), pltpu.SemaphoreType.DMA((2,))],
        ),
        out_shape=jax.ShapeDtypeStruct((P,) + ROW_TILE, F32),
        compiler_params=_cparams(("arbitrary",)),
        name="moe_experts",
    )(blk_e, nused, xs, w1, b1, w2, b2)


def _combine_kernel(dest_ref, ys_ref, tg_ref, x1_ref, gfin_ref, y_ref, buf, sem, *, final):
    tc = x1_ref.shape[0]

    def row_copy(g, s, k):
        return pltpu.make_async_copy(ys_ref.at[dest_ref[k, g * SUBLANES + s]], buf.at[k, g, :, s, :], sem)

    def start(g, c):
        for s in range(SUBLANES):
            for k in range(TOP_K):
                row_copy(g, s, k).start(priority=k % 2)
        return c

    def wait(g, c):
        for s in range(SUBLANES):
            for k in range(TOP_K):
                row_copy(g, s, k).wait()
        return c

    lax.fori_loop(0, tc // SUBLANES, start, 0)
    lax.fori_loop(0, tc // SUBLANES, wait, 0)
    gates = tg_ref[...].T
    x = x1_ref[...]
    for k in range(TOP_K):
        x = x + _load_vreg_tiles(buf.at[k]) * gates[:, k:k + 1]
    y_ref[...] = _rms(x, gfin_ref[...]) if final else x


def _combine(ys, dest, tg, x1, gfin, final, tc=256):
    T, D = x1.shape
    return pl.pallas_call(
        functools.partial(_combine_kernel, final=final),
        grid=(T // tc,),
        in_specs=[pl.BlockSpec((TOP_K, tc), lambda i: (0, i), memory_space=pltpu.SMEM),
                  pl.BlockSpec(memory_space=pl.ANY),
                  pl.BlockSpec((SUBLANES, tc), lambda i: (0, i)),
                  pl.BlockSpec((tc, D), lambda i: (i, 0)),
                  _full(gfin.shape)],
        out_specs=pl.BlockSpec((tc, D), lambda i: (i, 0)),
        out_shape=jax.ShapeDtypeStruct((T, D), F32),
        scratch_shapes=[pltpu.VMEM((TOP_K,) + _vreg_tile_shape(tc, D), F32), pltpu.SemaphoreType.DMA(())],
        compiler_params=_cparams(("parallel",)),
        name="moe_combine",
    )(dest, ys, tg, x1, gfin)


def _moe(h2, ti, rk, tg, cnt, x1, moe_w, gfin, final):
    T = x1.shape[0]
    bm = EXPERT_ROWS
    n_assign = T * TOP_K
    nb = -(-(n_assign + N_EXPERTS * (bm - 1)) // bm)
    counts = cnt[:, 0].astype(I32)
    padded = (counts + bm - 1) // bm * bm
    pad_end = jnp.cumsum(padded).astype(I32)
    pad_start = pad_end - padded
    experts = jnp.arange(N_EXPERTS, dtype=I32)[:, None, None]
    dest = rk + jnp.sum(jnp.where(ti[None] == experts, pad_start[:, None, None], 0), axis=0)
    nused = (pad_end[-1:] // bm).astype(I32)
    blk_start = jnp.minimum(jnp.arange(nb, dtype=I32), nused[0] - 1) * bm
    blk_e = jnp.minimum(jnp.sum((pad_end[None, :] <= blk_start[:, None]).astype(I32), axis=1), N_EXPERTS - 1)
    xs = _dispatch(h2, dest, pad_end, nb * bm)
    ys = _experts(xs, blk_e, nused, *moe_w)
    return _combine(ys, dest, tg, x1, gfin, final)


def _split_in_proj(w):
    sizes = (DA_QK_WIDTH, DA_QK_WIDTH, DA_WIDTH, GLA_KEY_WIDTH, GLA_KEY_WIDTH, GLA_WIDTH, GLA_WIDTH,
             GLA_GATE_RANK, GLA_GATE_RANK, D_MODEL, D_MODEL)
    assert w.shape[-1] == sum(sizes)
    out, o = [], 0
    for s in sizes:
        out.append(w[:, o:o + s])
        o += s
    return out


def _prep_layer(l, norm_mix_g, w_in, lambda_q1, lambda_k1, lambda_q2, lambda_k2, diff_subln_g,
                gla_gate_up_f, gla_gate_bias_f, gla_gate_up_b, gla_gate_bias_b, gla_norm_g,
                w_proj_a, w_proj_b, w_out, norm_ffn_g, router_w, router_b, moe_w1, moe_b1, moe_w2, moe_b2):
    dq, dk, dv, lq, lk, lv, lr, laf, lab, ga, gb = _split_in_proj(w_in[l])
    bf = lambda a: a.astype(BF16)
    r = GLA_GATE_RANK
    gup = jnp.zeros((LANES, 2 * GLA_KEY_WIDTH), F32)
    gup = gup.at[0:r, 0:GLA_KEY_WIDTH].set(gla_gate_up_f[l]).at[r:2 * r, GLA_KEY_WIDTH:].set(gla_gate_up_b[l])
    lamv = jnp.zeros((SUBLANES, LANES), F32)
    for i, v in enumerate((lambda_q1, lambda_k1, lambda_q2, lambda_k2)):
        lamv = lamv.at[i, 0:DA_HEAD_DIM].set(v[l].astype(F32))
    inproj = dict(
        norm_g=norm_mix_g[l][None, :],
        wqk=bf(jnp.concatenate([dq * (DA_HEAD_DIM ** -0.5), dk], 1)),
        wvt=bf(dv.T),
        wlqk=bf(jnp.concatenate([lq, lk], 1)),
        wlvr=bf(jnp.concatenate([lv, lr], 1)),
        wg=bf(jnp.concatenate([ga, gb], 1)),
        wlow=bf(jnp.concatenate([laf, lab, jnp.zeros((D_MODEL, LANES - 2 * r), F32)], 1)),
        gup=bf(gup),
        gbias=jnp.concatenate([gla_gate_bias_f[l], gla_gate_bias_b[l]])[None, :],
    )
    merge = dict(
        subg=diff_subln_g[l][None, :], glag=gla_norm_g[l][None, :],
        wpa=bf(w_proj_a[l]), wpb=bf(w_proj_b[l]), wout=bf(w_out[l]),
        nffn=norm_ffn_g[l][None, :], rwt=bf(router_w[l].T), rb=router_b[l][:, None],
    )
    b1 = moe_b1[l].reshape(N_EXPERTS, -1, LANES, 2).transpose(0, 1, 3, 2).reshape(N_EXPERTS, 1, -1)
    moe_w = (_deinterleave_w1(moe_w1[l]), b1, bf(moe_w2[l]), moe_b2[l][:, None, :])
    return inproj, lamv, merge, moe_w


def _trunk(x, layers, norm_final_g):
    B, L, D = x.shape
    T = B * L
    cos_t, sin_t = _rope_tables(L)
    x2 = x.reshape(T, D)
    gfin = norm_final_g[None, :]
    for l, (inproj, lamv, merge, moe_w) in enumerate(layers):
        lam_init = 0.8 - 0.6 * math.exp(-0.3 * l)
        qk, vt, lqk, lvr, gfb, sg = _inproj(x2, L, cos_t=cos_t, sin_t=sin_t, **inproj)
        oa = _diff_attention(qk.reshape(B, L, -1), vt, lamv, lam_init)
        of, ob = _gla(lqk.reshape(B, L, -1), lvr.reshape(B, L, -1), gfb.reshape(B, L, -1))
        x1, h2, ti, rk, tg, cnt = _merge(x2, oa.reshape(T, -1), of.reshape(T, -1), ob.reshape(T, -1),
                                         lvr, sg, lam_init=lam_init, **merge)
        x2 = _moe(h2, ti, rk, tg, cnt, x1, moe_w, gfin, final=(l == len(layers) - 1))
    return x2.reshape(B, L, D)


def kernel(x_prompt, x_sample, norm_mix_g, w_in, lambda_q1, lambda_k1, lambda_q2, lambda_k2, diff_subln_g,
           gla_gate_up_f, gla_gate_bias_f, gla_gate_up_b, gla_gate_bias_b, gla_norm_g, w_proj_a, w_proj_b,
           w_out, norm_ffn_g, router_w, router_b, moe_w1, moe_b1, moe_w2, moe_b2, norm_final_g):
    depth = w_in.shape[0]
    layers = [_prep_layer(l, norm_mix_g, w_in, lambda_q1, lambda_k1, lambda_q2, lambda_k2, diff_subln_g,
                          gla_gate_up_f, gla_gate_bias_f, gla_gate_up_b, gla_gate_bias_b, gla_norm_g,
                          w_proj_a, w_proj_b, w_out, norm_ffn_g, router_w, router_b,
                          moe_w1, moe_b1, moe_w2, moe_b2) for l in range(depth)]
    return _trunk(x_prompt, layers, norm_final_g), _trunk(x_sample, layers, norm_final_g)
```

```python
import functools
import math

import jax
import jax.numpy as jnp
from jax import lax
from jax.experimental import pallas as pl
from jax.experimental.pallas import tpu as pltpu

F32 = jnp.float32
BF16 = jnp.bfloat16
I32 = jnp.int32

D_MODEL = 1024
DA_HEADS = 4
DA_HEAD_DIM = 64
DA_VDIM = 128
DA_QK_WIDTH = DA_HEADS * 2 * DA_HEAD_DIM
DA_WIDTH = DA_HEADS * DA_VDIM
ROPE_THETA = 10000.0
GLA_HEADS = 4
GLA_KDIM = 64
GLA_VDIM = 128
GLA_KEY_WIDTH = GLA_HEADS * GLA_KDIM
GLA_WIDTH = GLA_HEADS * GLA_VDIM
GLA_GATE_RANK = 16
GLA_GATE_NORMALIZER = 16.0
GLA_CHUNK = 64
N_EXPERTS = 32
TOP_K = 4
D_FF = D_MODEL
SWIGLU_ALPHA = 1.702
SWIGLU_LIMIT = 7.0
NORM_EPS = 1e-5
LANES = 128
SUBLANES = 8

BF16_SUBLANES = 16
LOG2_E = 1.4426950408889634
VT_ROWS = DA_VDIM + BF16_SUBLANES

VMEM_LIMIT = 56 * 1024 * 1024

TOKEN_TILE = 512
EXPERT_ROWS = 256

NT_DIMS = (((1,), (1,)), ((), ()))
TN_DIMS = (((0,), (0,)), ((), ()))


def _cparams(sem):
    return pltpu.CompilerParams(dimension_semantics=sem, vmem_limit_bytes=VMEM_LIMIT)


def _full(shape):
    nd = len(shape)
    return pl.BlockSpec(shape, lambda *_: (0,) * nd)


def _rms(x, g):
    return x * lax.rsqrt(jnp.mean(x * x, axis=-1, keepdims=True) + NORM_EPS) * g


ROW_TILE = (SUBLANES, LANES)


def _vreg_tile_shape(rows, cols):
    return (rows // SUBLANES, cols // LANES, SUBLANES, LANES)


def _vreg_tile_row(ref, r):
    return ref.at[r // SUBLANES, :, r % SUBLANES, :]


def _store_vreg_tiles(ref, x):
    for b in range(ref.shape[1]):
        ref[:, b, :, :] = x[:, b * LANES:(b + 1) * LANES].reshape(ref.shape[0], SUBLANES, LANES)


def _load_vreg_tiles(ref):
    rows = ref.shape[0] * SUBLANES
    return jnp.concatenate([ref[:, b, :, :].reshape(rows, LANES) for b in range(ref.shape[1])], axis=1)


def _relayout_copies(flat_ref, tiled_ref, sem, to_tiled):
    cps = []
    for c in range(SUBLANES):
        flat = flat_ref.at[:, pl.ds(c * LANES, LANES)]
        tiled = tiled_ref.at[:, c, :]
        cps.append(pltpu.make_async_copy(flat, tiled, sem) if to_tiled else pltpu.make_async_copy(tiled, flat, sem))
    return cps


def _log_sigmoid(x):
    return jnp.minimum(x, 0.0) - jnp.log1p(jnp.exp(-jnp.abs(x)))


def _inproj_kernel(x_ref, g_ref, cos_ref, sin_ref, wqk_ref, wvt_ref, wlqk_ref, wlvr_ref,
                   wg_ref, wlow_ref, gup_ref, gbias_ref,
                   qk_ref, vt_ref, lqk_ref, lvr_ref, gfb_ref, sg_ref):
    h = _rms(x_ref[...], g_ref[...]).astype(BF16)

    cos = cos_ref[...]
    sin_signed = sin_ref[...]
    first_half = (lax.broadcasted_iota(I32, (1, LANES), 1) % DA_HEAD_DIM) < (DA_HEAD_DIM // 2)
    for c in range(2 * DA_QK_WIDTH // LANES):
        z = jnp.dot(h, wqk_ref[:, c * LANES:(c + 1) * LANES], preferred_element_type=F32)
        if c < DA_QK_WIDTH // LANES:
            z = z * LOG2_E
        partner = jnp.where(first_half, pltpu.roll(z, LANES - DA_HEAD_DIM // 2, 1),
                            pltpu.roll(z, DA_HEAD_DIM // 2, 1))
        qk_ref[:, c * LANES:(c + 1) * LANES] = (z * cos + partner * sin_signed).astype(BF16)

    vt = lax.dot_general(wvt_ref[...], h, NT_DIMS, preferred_element_type=F32).astype(BF16)
    for hd in range(DA_HEADS):
        vt_ref[0, hd * VT_ROWS:hd * VT_ROWS + DA_VDIM, :] = vt[hd * DA_VDIM:(hd + 1) * DA_VDIM]
        vt_ref[0, hd * VT_ROWS + DA_VDIM:(hd + 1) * VT_ROWS, :] = jnp.ones((VT_ROWS - DA_VDIM, vt.shape[1]), BF16)
    lqk_ref[...] = jnp.dot(h, wlqk_ref[...], preferred_element_type=F32)
    lvr_ref[...] = jnp.dot(h, wlvr_ref[...], preferred_element_type=F32).astype(BF16)
    sg_ref[...] = jax.nn.sigmoid(jnp.dot(h, wg_ref[...], preferred_element_type=F32)).astype(BF16)
    low = jnp.dot(h, wlow_ref[...], preferred_element_type=F32).astype(BF16)
    pre = jnp.dot(low, gup_ref[...], preferred_element_type=F32) + gbias_ref[...]
    gfb_ref[...] = _log_sigmoid(pre) * (1.0 / GLA_GATE_NORMALIZER)


def _inproj(x2, seq_len, norm_g, wqk, wvt, wlqk, wlvr, wg, wlow, gup, gbias, cos_t, sin_t):
    T, D = x2.shape
    tm = TOKEN_TILE
    assert T % tm == 0 and seq_len % tm == 0
    nl = seq_len // tm
    row = lambda n: pl.BlockSpec((tm, n), lambda i: (i, 0))
    out_shape = (
        jax.ShapeDtypeStruct((T, 2 * DA_QK_WIDTH), BF16),
        jax.ShapeDtypeStruct((T // tm, DA_HEADS * VT_ROWS, tm), BF16),
        jax.ShapeDtypeStruct((T, 2 * GLA_KEY_WIDTH), F32),
        jax.ShapeDtypeStruct((T, 2 * GLA_WIDTH), BF16),
        jax.ShapeDtypeStruct((T, 2 * GLA_KEY_WIDTH), F32),
        jax.ShapeDtypeStruct((T, 2 * D_MODEL), BF16),
    )
    return pl.pallas_call(
        _inproj_kernel,
        grid=(T // tm,),
        in_specs=[row(D), _full(norm_g.shape),
                  pl.BlockSpec((tm, LANES), lambda i: (i % nl, 0)),
                  pl.BlockSpec((tm, LANES), lambda i: (i % nl, 0)),
                  _full(wqk.shape), _full(wvt.shape), _full(wlqk.shape), _full(wlvr.shape),
                  _full(wg.shape), _full(wlow.shape), _full(gup.shape), _full(gbias.shape)],
        out_specs=(row(2 * DA_QK_WIDTH), pl.BlockSpec((1, DA_HEADS * VT_ROWS, tm), lambda i: (i, 0, 0)),
                   row(2 * GLA_KEY_WIDTH), row(2 * GLA_WIDTH), row(2 * GLA_KEY_WIDTH), row(2 * D_MODEL)),
        out_shape=out_shape,
        compiler_params=_cparams(("parallel",)),
        name="inproj",
    )(x2, norm_g, cos_t, sin_t, wqk, wvt, wlqk, wlvr, wg, wlow, gup, gbias)


def _rope_tables(seq_len):
    d = DA_HEAD_DIM
    inv = ROPE_THETA ** (-jnp.arange(0, d, 2, dtype=F32) / d)
    ang = jnp.arange(seq_len, dtype=F32)[:, None] * inv[None, :]
    cos = jnp.concatenate([jnp.cos(ang)] * (2 * LANES // d), -1)
    sin = jnp.sin(ang)
    sin_signed = jnp.concatenate([-sin, sin] * (LANES // d), -1)
    return cos, sin_signed


def _attn_kernel(lamv_ref, q_ref, k_ref, vt_ref, o_ref, acc_sc, sa_sc, sb_sc, *, lam_init):
    nk, _, tk = vt_ref.shape
    tq = q_ref.shape[1]
    q = q_ref[0]
    lane = lax.broadcasted_iota(I32, (1, LANES), 1)
    zeros = jnp.zeros_like(q)
    qms = (jnp.where(lane < DA_HEAD_DIM, q, zeros), jnp.where(lane >= DA_HEAD_DIM, q, zeros))
    acc_sc[...] = jnp.zeros(acc_sc.shape, F32)

    def scores(j, dst_ref):
        kj = k_ref[0, pl.ds(pl.multiple_of(j * tk, tk), tk), :]
        for mp in range(2):
            dst_ref[mp] = lax.dot_general(kj, qms[mp], NT_DIMS, preferred_element_type=F32)

    def softmax_pv(j, src_ref, ms):
        vj = vt_ref[j]
        new = []
        for mp in range(2):
            st = src_ref[mp]
            m_new = jnp.maximum(ms[mp], jnp.max(st, axis=0, keepdims=True))
            p = jnp.exp2((st - m_new).astype(BF16))
            acc_sc[mp] = acc_sc[mp] * jnp.exp2(ms[mp] - m_new) + jnp.dot(vj, p, preferred_element_type=F32)
            new.append(m_new)
        return tuple(new)

    def pair(i, ms):
        j = 2 * i
        scores(j + 1, sb_sc)
        ms = softmax_pv(j, sa_sc, ms)
        scores(j + 2, sa_sc)
        return softmax_pv(j + 1, sb_sc, ms)

    assert nk % 2 == 0
    scores(0, sa_sc)
    ms = tuple(jnp.full((1, tq), -jnp.inf, F32) for _ in range(2))
    ms = lax.fori_loop(0, (nk - 2) // 2, pair, ms)
    scores(nk - 1, sb_sc)
    ms = softmax_pv(nk - 2, sa_sc, ms)
    softmax_pv(nk - 1, sb_sc, ms)

    lv = lamv_ref[...]
    lam = (jnp.exp(jnp.sum(lv[0:1] * lv[1:2], axis=-1, keepdims=True))
           - jnp.exp(jnp.sum(lv[2:3] * lv[3:4], axis=-1, keepdims=True)) + lam_init)
    a0, a1 = acc_sc[0], acc_sc[1]
    o_ref[0] = (a0[:DA_VDIM] / a0[DA_VDIM:DA_VDIM + 1] - lam * (a1[:DA_VDIM] / a1[DA_VDIM:DA_VDIM + 1])).T


def _diff_attention(qk3, vt3, lamv, lam_init, tq=512):
    B, L, _ = qk3.shape
    tk = vt3.shape[-1]
    nkb = L // tk
    return pl.pallas_call(
        functools.partial(_attn_kernel, lam_init=lam_init),
        grid=(B, DA_HEADS, L // tq),
        in_specs=[_full(lamv.shape),
                  pl.BlockSpec((1, tq, LANES), lambda b, h, i: (b, i, h)),
                  pl.BlockSpec((1, L, LANES), lambda b, h, i: (b, 0, DA_HEADS + h)),
                  pl.BlockSpec((nkb, VT_ROWS, tk), lambda b, h, i: (b, h, 0))],
        out_specs=pl.BlockSpec((1, tq, DA_VDIM), lambda b, h, i: (b, i, h)),
        out_shape=jax.ShapeDtypeStruct((B, L, DA_WIDTH), F32),
        scratch_shapes=[pltpu.VMEM((2, VT_ROWS, tq), F32), pltpu.VMEM((2, tk, tq), F32),
                        pltpu.VMEM((2, tk, tq), F32)],
        compiler_params=_cparams(("parallel", "parallel", "arbitrary")),
        name="diff_attention",
    )(lamv, qk3, qk3, vt3)


def _split3_bf16(x):
    hi = x.astype(BF16)
    r1 = x - hi.astype(F32)
    mid = r1.astype(BF16)
    return hi, mid, (r1 - mid.astype(F32)).astype(BF16)


def _gla_local(qk_ref, vr_ref, g_ref, *, reverse):
    C = GLA_CHUNK
    tb = qk_ref.shape[1]
    nchunk = tb // C
    row = lax.broadcasted_iota(I32, (tb, tb), 0)
    col = lax.broadcasted_iota(I32, (tb, tb), 1)
    same_chunk = (row // C) == (col // C)
    tri = same_chunk & ((col >= row) if reverse else (col <= row))
    tri_b = jnp.where(tri, 1.0, 0.0).astype(BF16)
    lane = lax.broadcasted_iota(I32, (1, LANES), 1)
    goff = GLA_KEY_WIDTH if reverse else 0

    g = g_ref[0, :, goff:goff + GLA_KEY_WIDTH]
    q = qk_ref[0, :, 0:GLA_KEY_WIDTH] * (GLA_KDIM ** -0.5)
    k = qk_ref[0, :, GLA_KEY_WIDTH:2 * GLA_KEY_WIDTH]
    v = vr_ref[0]
    b = sum(jnp.dot(tri_b, part, preferred_element_type=F32) for part in _split3_bf16(g))
    last_row, ref_row = (0, C // 2) if reverse else (C - 1, C // 2 - 1)
    per_chunk = lambda r: jnp.concatenate(
        [jnp.broadcast_to(b[c * C + r:c * C + r + 1], (C, GLA_KEY_WIDTH)) for c in range(nchunk)], axis=0)
    b_last, b_ref = per_chunk(last_row), per_chunk(ref_row)
    q_in = q * jnp.exp(b - b_ref)
    k_in = (k * jnp.exp(b_ref - b)).astype(BF16)
    k_dec = k * jnp.exp(b_last - b)
    q_dec = q * jnp.exp(b)
    dec = [jnp.exp(b[c * C + last_row:c * C + last_row + 1]) for c in range(nchunk)]

    heads = []
    for h in range(GLA_HEADS):
        ps = slice((h // 2) * LANES, (h // 2 + 1) * LANES)
        keep = (lane < GLA_KDIM) if h % 2 == 0 else (lane >= GLA_KDIM)
        zero = jnp.zeros((tb, LANES), F32)
        heads.append(dict(
            ps=ps, vs=slice(h * GLA_VDIM, (h + 1) * GLA_VDIM),
            q_in=jnp.where(keep, q_in[:, ps], zero).astype(BF16),
            q_dec=jnp.where(keep, q_dec[:, ps], zero).astype(BF16),
            k_dec=jnp.where(keep, k_dec[:, ps], zero).astype(BF16)))
    scores = [lax.dot_general(hd["q_in"], k_in[:, hd["ps"]], NT_DIMS, preferred_element_type=F32) for hd in heads]
    kv_t = [[lax.dot_general(v[c * C:(c + 1) * C, hd["vs"]], hd["k_dec"][c * C:(c + 1) * C], TN_DIMS,
                             preferred_element_type=F32) for hd in heads] for c in range(nchunk)]
    intra = [jnp.dot(jnp.where(tri, s, 0.0).astype(BF16), v[:, hd["vs"]], preferred_element_type=F32)
             for s, hd in zip(scores, heads)]
    return heads, dec, kv_t, intra


def _gla_scan(heads, dec, kv_t, st_ref, *, reverse):
    nchunk = len(dec)
    state = [st_ref[h] for h in range(GLA_HEADS)]
    entering = [None] * nchunk
    for c in (range(nchunk - 1, -1, -1) if reverse else range(nchunk)):
        entering[c] = [s.astype(BF16) for s in state]
        state = [s * dec[c][:, hd["ps"]] + kv_t[c][h] for h, (s, hd) in enumerate(zip(state, heads))]
    for h in range(GLA_HEADS):
        st_ref[h] = state[h]
    return entering


def _gla_emit(heads, intra, entering, o_ref):
    C = GLA_CHUNK
    for c, states in enumerate(entering):
        rows = slice(c * C, (c + 1) * C)
        for hd, o_intra, s_t in zip(heads, intra, states):
            inter = lax.dot_general(hd["q_dec"][rows], s_t, NT_DIMS, preferred_element_type=F32)
            o_ref[0, rows, hd["vs"]] = o_intra[rows] + inter


def _gla_kernel(qkf_ref, vrf_ref, gf_ref, qkb_ref, vrb_ref, gb_ref, of_ref, ob_ref, sf_sc, sb_sc):
    @pl.when(pl.program_id(1) == 0)
    def _():
        sf_sc[...] = jnp.zeros(sf_sc.shape, F32)
        sb_sc[...] = jnp.zeros(sb_sc.shape, F32)

    hf, df, kf, inf = _gla_local(qkf_ref, vrf_ref, gf_ref, reverse=False)
    hb, db, kb, inb = _gla_local(qkb_ref, vrb_ref, gb_ref, reverse=True)
    ef = _gla_scan(hf, df, kf, sf_sc, reverse=False)
    eb = _gla_scan(hb, db, kb, sb_sc, reverse=True)
    _gla_emit(hf, inf, ef, of_ref)
    _gla_emit(hb, inb, eb, ob_ref)


def _gla(lqk3, lvr3, gfb3, tb=256):
    B, L, _ = lqk3.shape
    nb = L // tb
    fwd = lambda n: pl.BlockSpec((1, tb, n), lambda b, i: (b, i, 0))
    bwd = lambda n: pl.BlockSpec((1, tb, n), lambda b, i: (b, nb - 1 - i, 0))
    out = jax.ShapeDtypeStruct((B, L, GLA_WIDTH), F32)
    return pl.pallas_call(
        _gla_kernel,
        grid=(B, nb),
        in_specs=[fwd(2 * GLA_KEY_WIDTH), fwd(GLA_WIDTH), fwd(2 * GLA_KEY_WIDTH),
                  bwd(2 * GLA_KEY_WIDTH), bwd(GLA_WIDTH), bwd(2 * GLA_KEY_WIDTH)],
        out_specs=(fwd(GLA_WIDTH), bwd(GLA_WIDTH)),
        out_shape=(out, out),
        scratch_shapes=[pltpu.VMEM((GLA_HEADS, GLA_VDIM, LANES), F32),
                        pltpu.VMEM((GLA_HEADS, GLA_VDIM, LANES), F32)],
        compiler_params=_cparams(("parallel", "arbitrary")),
        name="gla",
    )(lqk3, lvr3, gfb3, lqk3, lvr3, gfb3)


def _head_norm(z, g):
    return jnp.concatenate(
        [_rms(z[:, h * LANES:(h + 1) * LANES], g) for h in range(z.shape[1] // LANES)], axis=1)


def _merge_kernel(x_ref, oa_ref, of_ref, ob_ref, r_ref, sg_ref, subg_ref, glag_ref, wpa_ref, wpb_ref,
                  wout_ref, nffn_ref, rwt_ref, rb_ref,
                  x1_ref, h2_ref, ti_ref, rk_ref, tg_ref, cnt_ref, carry_sc, *, lam_init):
    tm = x_ref.shape[0]

    @pl.when(pl.program_id(0) == 0)
    def _():
        carry_sc[...] = jnp.zeros(carry_sc.shape, F32)

    oa = (_head_norm(oa_ref[...], subg_ref[...]) * (1.0 - lam_init)).astype(BF16)
    ob = _head_norm(of_ref[...] + ob_ref[...], glag_ref[...]) * jax.nn.silu(r_ref[...].astype(F32))
    pa = jnp.dot(oa, wpa_ref[...], preferred_element_type=F32)
    pb = jnp.dot(ob.astype(BF16), wpb_ref[...], preferred_element_type=F32)
    sg = sg_ref[...].astype(F32)
    merged = sg[:, :D_MODEL] * pa + sg[:, D_MODEL:] * pb
    x1 = x_ref[...] + jnp.dot(merged.astype(BF16), wout_ref[...], preferred_element_type=F32)
    x1_ref[...] = x1
    h2 = _rms(x1, nffn_ref[...])
    _store_vreg_tiles(h2_ref, h2)

    logits = lax.dot_general(rwt_ref[...], h2.astype(BF16), NT_DIMS, preferred_element_type=F32) + rb_ref[...]
    eidx = lax.broadcasted_iota(I32, logits.shape, 0)
    vals, sels = [], []
    for k in range(TOP_K):
        mk = jnp.max(logits, axis=0, keepdims=True)
        ik = jnp.min(jnp.where(logits == mk, eidx, N_EXPERTS), axis=0, keepdims=True)
        sel = eidx == ik
        logits = jnp.where(sel, -jnp.inf, logits)
        vals.append(mk)
        sels.append(sel)
        ti_ref[k:k + 1, :] = ik
    ex = [jnp.exp(v - vals[0]) for v in vals]
    denom = ex[0] + ex[1] + ex[2] + ex[3]
    for k in range(TOP_K):
        tg_ref[k:k + 1, :] = ex[k] / denom
    tg_ref[TOP_K:, :] = jnp.zeros((tg_ref.shape[0] - TOP_K, tm), F32)

    multi = (sels[0] | sels[1] | sels[2] | sels[3])
    multi_f = jnp.where(multi, 1.0, 0.0)
    before = lax.broadcasted_iota(I32, (tm, tm), 0) < lax.broadcasted_iota(I32, (tm, tm), 1)
    cum = jnp.dot(multi_f.astype(BF16), jnp.where(before, 1.0, 0.0).astype(BF16), preferred_element_type=F32)
    tot = carry_sc[:, 0:1] + cum
    for k in range(TOP_K):
        rk_ref[k:k + 1, :] = jnp.sum(jnp.where(sels[k], tot, 0.0), axis=0, keepdims=True).astype(I32)
    carry_sc[...] = carry_sc[...] + jnp.sum(multi_f, axis=1, keepdims=True)
    cnt_ref[...] = carry_sc[...]


def _merge(x2, oa, of, ob, lvr, sg, subg, glag, wpa, wpb, wout, nffn, rwt, rb, lam_init):
    T, D = x2.shape
    tm = TOKEN_TILE
    row = lambda n: pl.BlockSpec((tm, n), lambda i: (i, 0))
    col = lambda n: pl.BlockSpec((n, tm), lambda i: (0, i))
    out_shape = (
        jax.ShapeDtypeStruct((T, D), F32),
        jax.ShapeDtypeStruct(_vreg_tile_shape(T, D), F32),
        jax.ShapeDtypeStruct((TOP_K, T), I32),
        jax.ShapeDtypeStruct((TOP_K, T), I32),
        jax.ShapeDtypeStruct((SUBLANES, T), F32),
        jax.ShapeDtypeStruct((N_EXPERTS, LANES), F32),
    )
    return pl.pallas_call(
        functools.partial(_merge_kernel, lam_init=lam_init),
        grid=(T // tm,),
        in_specs=[row(D), row(DA_WIDTH), row(GLA_WIDTH), row(GLA_WIDTH),
                  pl.BlockSpec((tm, GLA_WIDTH), lambda i: (i, 1)), row(2 * D),
                  _full(subg.shape), _full(glag.shape), _full(wpa.shape), _full(wpb.shape),
                  _full(wout.shape), _full(nffn.shape), _full(rwt.shape), _full(rb.shape)],
        out_specs=(row(D), pl.BlockSpec(_vreg_tile_shape(tm, D), lambda i: (i, 0, 0, 0)),
                   col(TOP_K), col(TOP_K), col(SUBLANES), _full((N_EXPERTS, LANES))),
        out_shape=out_shape,
        scratch_shapes=[pltpu.VMEM((N_EXPERTS, LANES), F32)],
        compiler_params=_cparams(("arbitrary",)),
        name="merge_router",
    )(x2, oa, of, ob, lvr, sg, subg, glag, wpa, wpb, wout, nffn, rwt, rb)


def _dispatch_kernel(pend_ref, dest_ref, h_ref, xs_ref, zero_sc, sem):
    tm = h_ref.shape[0] * SUBLANES
    rows = zero_sc.shape[0]

    @pl.when(pl.program_id(0) == 0)
    def _():
        zero_sc[...] = jnp.zeros(zero_sc.shape, F32)
        for e in range(N_EXPERTS):
            end = pend_ref[e]
            begin = pend_ref[e - 1] if e else 0

            @pl.when(end > begin)
            def _():
                last = pl.multiple_of(end - rows, rows)
                cp = pltpu.make_async_copy(zero_sc, xs_ref.at[pl.ds(last, rows)], sem)
                cp.start()
                cp.wait()

        def zero_unused(blk, c):
            cp = pltpu.make_async_copy(zero_sc, xs_ref.at[pl.ds(pl.multiple_of(blk * rows, rows), rows)], sem)
            cp.start()
            cp.wait()
            return c

        lax.fori_loop(pend_ref[N_EXPERTS - 1] // rows, xs_ref.shape[0] // rows, zero_unused, 0)

    def row_copy(g, s, k):
        return pltpu.make_async_copy(h_ref.at[g, :, s, :], xs_ref.at[dest_ref[k, g * SUBLANES + s]], sem)

    def start(g, c):
        for s in range(SUBLANES):
            for k in range(TOP_K):
                row_copy(g, s, k).start(priority=k % 2)
        return c

    def wait(g, c):
        for s in range(SUBLANES):
            for k in range(TOP_K):
                row_copy(g, s, k).wait()
        return c

    lax.fori_loop(0, tm // SUBLANES, start, 0)
    lax.fori_loop(0, tm // SUBLANES, wait, 0)


def _dispatch(h2, dest, pad_end, n_slots):
    T = h2.shape[0] * SUBLANES
    tm = TOKEN_TILE
    h_block = (tm // SUBLANES,) + h2.shape[1:]
    return pl.pallas_call(
        _dispatch_kernel,
        grid_spec=pltpu.PrefetchScalarGridSpec(
            num_scalar_prefetch=1,
            grid=(T // tm,),
            in_specs=[pl.BlockSpec((TOP_K, tm), lambda i, pe: (0, i), memory_space=pltpu.SMEM),
                      pl.BlockSpec(h_block, lambda i, pe: (i, 0, 0, 0))],
            out_specs=pl.BlockSpec(memory_space=pl.ANY),
            scratch_shapes=[pltpu.VMEM((EXPERT_ROWS,) + ROW_TILE, F32), pltpu.SemaphoreType.DMA(())],
        ),
        out_shape=jax.ShapeDtypeStruct((n_slots,) + ROW_TILE, F32),
        compiler_params=_cparams(("arbitrary",)),
        name="moe_dispatch",
    )(pad_end, dest, h2)


def _deinterleave_kernel(w_ref, o_ref):
    n = 2 * LANES
    src = lax.broadcasted_iota(I32, (n, n), 0)
    dst = lax.broadcasted_iota(I32, (n, n), 1)
    perm = jnp.where(src == jnp.where(dst < LANES, 2 * dst, 2 * (dst - LANES) + 1), 1.0, 0.0).astype(BF16)
    for c in range(w_ref.shape[2] // n):
        w = w_ref[0, :, c * n:(c + 1) * n].astype(BF16)
        o_ref[0, :, c * n:(c + 1) * n] = jnp.dot(w, perm, preferred_element_type=F32).astype(BF16)


def _deinterleave_w1(w1, cols=512):
    E, K, N = w1.shape
    spec = pl.BlockSpec((1, K, cols), lambda e, j: (e, 0, j))
    return pl.pallas_call(
        _deinterleave_kernel,
        grid=(E, N // cols),
        in_specs=[spec],
        out_specs=spec,
        out_shape=jax.ShapeDtypeStruct(w1.shape, BF16),
        compiler_params=_cparams(("parallel", "parallel")),
        name="moe_w1_layout",
    )(w1)


def _expert_kernel(blk_ref, nused_ref, xs_ref, w1_ref, b1_ref, w2_ref, b2_ref, ys_ref,
                   xbuf, ybuf, in_sem, out_sem):
    i = pl.program_id(0)
    n = pl.num_programs(0)
    bm = xbuf.shape[1]
    slot = i % 2

    def rows_of(ref, blk):
        return ref.at[pl.ds(pl.multiple_of(blk * bm, bm), bm)]

    def loads(blk, s):
        return _relayout_copies(xbuf.at[s], rows_of(xs_ref, blk), in_sem.at[s], to_tiled=False)

    def stores(blk, s):
        return _relayout_copies(ybuf.at[s], rows_of(ys_ref, blk), out_sem.at[s], to_tiled=True)

    @pl.when(i == 0)
    def _():
        for cp in loads(0, 0):
            cp.start()

    @pl.when(i + 1 < n)
    def _():
        for cp in loads(i + 1, 1 - slot):
            cp.start()

    for cp in loads(i, slot):
        cp.wait()

    @pl.when(i >= 2)
    def _():
        for cp in stores(i - 2, slot):
            cp.wait()

    @pl.when(i < nused_ref[0])
    def _():
        x = xbuf[slot].astype(BF16)
        hid = jnp.dot(x, w1_ref[0], preferred_element_type=F32) + b1_ref[0]
        acts = []
        for c in range(hid.shape[1] // (2 * LANES)):
            gate = jnp.minimum(hid[:, 2 * c * LANES:(2 * c + 1) * LANES], SWIGLU_LIMIT)
            up = jnp.clip(hid[:, (2 * c + 1) * LANES:(2 * c + 2) * LANES], -SWIGLU_LIMIT, SWIGLU_LIMIT)
            acts.append(((up + 1.0) * (gate * jax.nn.sigmoid(SWIGLU_ALPHA * gate))).astype(BF16))
        act = jnp.concatenate(acts, axis=1)
        ybuf[slot] = jnp.dot(act, w2_ref[0], preferred_element_type=F32) + b2_ref[0]

    @pl.when(i >= nused_ref[0])
    def _():
        ybuf[slot] = jnp.zeros(ybuf.shape[1:], F32)

    for cp in stores(i, slot):
        cp.start()

    @pl.when(i == n - 1)
    def _():
        for cp in stores(i, slot):
            cp.wait()

    @pl.when((i == n - 1) & (n >= 2))
    def _():
        for cp in stores(i - 1, 1 - slot):
            cp.wait()


def _experts(xs, blk_e, nused, w1, b1, w2, b2):
    P = xs.shape[0]
    D = w1.shape[1]
    bm = EXPERT_ROWS
    nb = P // bm
    per_expert = lambda a: pl.BlockSpec((1,) + a.shape[1:], lambda i, blk, nu: (blk[i], 0, 0))
    return pl.pallas_call(
        _expert_kernel,
        grid_spec=pltpu.PrefetchScalarGridSpec(
            num_scalar_prefetch=2,
            grid=(nb,),
            in_specs=[pl.BlockSpec(memory_space=pl.ANY), per_expert(w1), per_expert(b1), per_expert(w2),
                      per_expert(b2)],
            out_specs=pl.BlockSpec(memory_space=pl.ANY),
            scratch_shapes=[pltpu.VMEM((2, bm, D), F32), pltpu.VMEM((2, bm, D), F32),
                            pltpu.SemaphoreType.DMA((2,)), pltpu.SemaphoreType.DMA((2,))],
        ),
        out_shape=jax.ShapeDtypeStruct((P,) + ROW_TILE, F32),
        compiler_params=_cparams(("arbitrary",)),
        name="moe_experts",
    )(blk_e, nused, xs, w1, b1, w2, b2)


def _combine_kernel(dest_ref, ys_ref, tg_ref, x1_ref, gfin_ref, y_ref, buf, sem, *, final):
    tc = x1_ref.shape[0]

    def row_copy(g, s, k):
        return pltpu.make_async_copy(ys_ref.at[dest_ref[k, g * SUBLANES + s]], buf.at[k, g, :, s, :], sem)

    def start(g, c):
        for s in range(SUBLANES):
            for k in range(TOP_K):
                row_copy(g, s, k).start(priority=k % 2)
        return c

    def wait(g, c):
        for s in range(SUBLANES):
            for k in range(TOP_K):
                row_copy(g, s, k).wait()
        return c

    lax.fori_loop(0, tc // SUBLANES, start, 0)
    lax.fori_loop(0, tc // SUBLANES, wait, 0)
    gates = tg_ref[...].T
    x = x1_ref[...]
    for k in range(TOP_K):
        x = x + _load_vreg_tiles(buf.at[k]) * gates[:, k:k + 1]
    y_ref[...] = _rms(x, gfin_ref[...]) if final else x


def _combine(ys, dest, tg, x1, gfin, final, tc=256):
    T, D = x1.shape
    return pl.pallas_call(
        functools.partial(_combine_kernel, final=final),
        grid=(T // tc,),
        in_specs=[pl.BlockSpec((TOP_K, tc), lambda i: (0, i), memory_space=pltpu.SMEM),
                  pl.BlockSpec(memory_space=pl.ANY),
                  pl.BlockSpec((SUBLANES, tc), lambda i: (0, i)),
                  pl.BlockSpec((tc, D), lambda i: (i, 0)),
                  _full(gfin.shape)],
        out_specs=pl.BlockSpec((tc, D), lambda i: (i, 0)),
        out_shape=jax.ShapeDtypeStruct((T, D), F32),
        scratch_shapes=[pltpu.VMEM((TOP_K,) + _vreg_tile_shape(tc, D), F32), pltpu.SemaphoreType.DMA(())],
        compiler_params=_cparams(("parallel",)),
        name="moe_combine",
    )(dest, ys, tg, x1, gfin)


def _moe(h2, ti, rk, tg, cnt, x1, moe_w, gfin, final):
    T = x1.shape[0]
    bm = EXPERT_ROWS
    n_assign = T * TOP_K
    nb = -(-(n_assign + N_EXPERTS * (bm - 1)) // bm)
    counts = cnt[:, 0].astype(I32)
    padded = (counts + bm - 1) // bm * bm
    pad_end = jnp.cumsum(padded).astype(I32)
    pad_start = pad_end - padded
    experts = jnp.arange(N_EXPERTS, dtype=I32)[:, None, None]
    dest = rk + jnp.sum(jnp.where(ti[None] == experts, pad_start[:, None, None], 0), axis=0)
    nused = (pad_end[-1:] // bm).astype(I32)
    blk_start = jnp.minimum(jnp.arange(nb, dtype=I32), nused[0] - 1) * bm
    blk_e = jnp.minimum(jnp.sum((pad_end[None, :] <= blk_start[:, None]).astype(I32), axis=1), N_EXPERTS - 1)
    xs = _dispatch(h2, dest, pad_end, nb * bm)
    ys = _experts(xs, blk_e, nused, *moe_w)
    return _combine(ys, dest, tg, x1, gfin, final)


def _split_in_proj(w):
    sizes = (DA_QK_WIDTH, DA_QK_WIDTH, DA_WIDTH, GLA_KEY_WIDTH, GLA_KEY_WIDTH, GLA_WIDTH, GLA_WIDTH,
             GLA_GATE_RANK, GLA_GATE_RANK, D_MODEL, D_MODEL)
    assert w.shape[-1] == sum(sizes)
    out, o = [], 0
    for s in sizes:
        out.append(w[:, o:o + s])
        o += s
    return out


def _prep_layer(l, norm_mix_g, w_in, lambda_q1, lambda_k1, lambda_q2, lambda_k2, diff_subln_g,
                gla_gate_up_f, gla_gate_bias_f, gla_gate_up_b, gla_gate_bias_b, gla_norm_g,
                w_proj_a, w_proj_b, w_out, norm_ffn_g, router_w, router_b, moe_w1, moe_b1, moe_w2, moe_b2):
    dq, dk, dv, lq, lk, lv, lr, laf, lab, ga, gb = _split_in_proj(w_in[l])
    bf = lambda a: a.astype(BF16)
    r = GLA_GATE_RANK
    gup = jnp.zeros((LANES, 2 * GLA_KEY_WIDTH), F32)
    gup = gup.at[0:r, 0:GLA_KEY_WIDTH].set(gla_gate_up_f[l]).at[r:2 * r, GLA_KEY_WIDTH:].set(gla_gate_up_b[l])
    lamv = jnp.zeros((SUBLANES, LANES), F32)
    for i, v in enumerate((lambda_q1, lambda_k1, lambda_q2, lambda_k2)):
        lamv = lamv.at[i, 0:DA_HEAD_DIM].set(v[l].astype(F32))
    inproj = dict(
        norm_g=norm_mix_g[l][None, :],
        wqk=bf(jnp.concatenate([dq * (DA_HEAD_DIM ** -0.5), dk], 1)),
        wvt=bf(dv.T),
        wlqk=bf(jnp.concatenate([lq, lk], 1)),
        wlvr=bf(jnp.concatenate([lv, lr], 1)),
        wg=bf(jnp.concatenate([ga, gb], 1)),
        wlow=bf(jnp.concatenate([laf, lab, jnp.zeros((D_MODEL, LANES - 2 * r), F32)], 1)),
        gup=bf(gup),
        gbias=jnp.concatenate([gla_gate_bias_f[l], gla_gate_bias_b[l]])[None, :],
    )
    merge = dict(
        subg=diff_subln_g[l][None, :], glag=gla_norm_g[l][None, :],
        wpa=bf(w_proj_a[l]), wpb=bf(w_proj_b[l]), wout=bf(w_out[l]),
        nffn=norm_ffn_g[l][None, :], rwt=bf(router_w[l].T), rb=router_b[l][:, None],
    )
    b1 = moe_b1[l].reshape(N_EXPERTS, -1, LANES, 2).transpose(0, 1, 3, 2).reshape(N_EXPERTS, 1, -1)
    moe_w = (_deinterleave_w1(moe_w1[l]), b1, bf(moe_w2[l]), moe_b2[l][:, None, :])
    return inproj, lamv, merge, moe_w


def _trunk(x, layers, norm_final_g):
    B, L, D = x.shape
    T = B * L
    cos_t, sin_t = _rope_tables(L)
    x2 = x.reshape(T, D)
    gfin = norm_final_g[None, :]
    for l, (inproj, lamv, merge, moe_w) in enumerate(layers):
        lam_init = 0.8 - 0.6 * math.exp(-0.3 * l)
        qk, vt, lqk, lvr, gfb, sg = _inproj(x2, L, cos_t=cos_t, sin_t=sin_t, **inproj)
        oa = _diff_attention(qk.reshape(B, L, -1), vt, lamv, lam_init)
        of, ob = _gla(lqk.reshape(B, L, -1), lvr.reshape(B, L, -1), gfb.reshape(B, L, -1))
        x1, h2, ti, rk, tg, cnt = _merge(x2, oa.reshape(T, -1), of.reshape(T, -1), ob.reshape(T, -1),
                                         lvr, sg, lam_init=lam_init, **merge)
        x2 = _moe(h2, ti, rk, tg, cnt, x1, moe_w, gfin, final=(l == len(layers) - 1))
    return x2.reshape(B, L, D)


def kernel(x_prompt, x_sample, norm_mix_g, w_in, lambda_q1, lambda_k1, lambda_q2, lambda_k2, diff_subln_g,
           gla_gate_up_f, gla_gate_bias_f, gla_gate_up_b, gla_gate_bias_b, gla_norm_g, w_proj_a, w_proj_b,
           w_out, norm_ffn_g, router_w, router_b, moe_w1, moe_b1, moe_w2, moe_b2, norm_final_g):
    depth = w_in.shape[0]
    layers = [_prep_layer(l, norm_mix_g, w_in, lambda_q1, lambda_k1, lambda_q2, lambda_k2, diff_subln_g,
                          gla_gate_up_f, gla_gate_bias_f, gla_gate_up_b, gla_gate_bias_b, gla_norm_g,
                          w_proj_a, w_proj_b, w_out, norm_ffn_g, router_w, router_b,
                          moe_w1, moe_b1, moe_w2, moe_b2) for l in range(depth)]
    return _trunk(x_prompt, layers, norm_final_g), _trunk(x_sample, layers, norm_final_g)
```

```python
import functools
import math

import jax
import jax.numpy as jnp
from jax import lax
from jax.experimental import pallas as pl
from jax.experimental.pallas import tpu as pltpu

F32 = jnp.float32
BF16 = jnp.bfloat16
I32 = jnp.int32

D_MODEL = 1024
DA_HEADS = 4
DA_HEAD_DIM = 64
DA_VDIM = 128
DA_QK_WIDTH = DA_HEADS * 2 * DA_HEAD_DIM
DA_WIDTH = DA_HEADS * DA_VDIM
ROPE_THETA = 10000.0
GLA_HEADS = 4
GLA_KDIM = 64
GLA_VDIM = 128
GLA_KEY_WIDTH = GLA_HEADS * GLA_KDIM
GLA_WIDTH = GLA_HEADS * GLA_VDIM
GLA_GATE_RANK = 16
GLA_GATE_NORMALIZER = 16.0
GLA_CHUNK = 64
N_EXPERTS = 32
TOP_K = 4
D_FF = D_MODEL
SWIGLU_ALPHA = 1.702
SWIGLU_LIMIT = 7.0
NORM_EPS = 1e-5
LANES = 128
SUBLANES = 8

BF16_SUBLANES = 16
MXU_TILE = 256
LOG2_E = 1.4426950408889634
VT_ROWS = DA_VDIM + BF16_SUBLANES

VMEM_LIMIT = 56 * 1024 * 1024

TOKEN_TILE = 512
EXPERT_ROWS = 256

NT_DIMS = (((1,), (1,)), ((), ()))
TN_DIMS = (((0,), (0,)), ((), ()))


def _cparams(sem):
    return pltpu.CompilerParams(dimension_semantics=sem, vmem_limit_bytes=VMEM_LIMIT)


def _full(shape):
    nd = len(shape)
    return pl.BlockSpec(shape, lambda *_: (0,) * nd)


def _rms(x, g):
    return x * lax.rsqrt(jnp.mean(x * x, axis=-1, keepdims=True) + NORM_EPS) * g


ROW_TILE = (SUBLANES, LANES)


def _vreg_tile_shape(rows, cols):
    return (rows // SUBLANES, cols // LANES, SUBLANES, LANES)


def _vreg_tile_row(ref, r):
    return ref.at[r // SUBLANES, :, r % SUBLANES, :]


def _store_vreg_tiles(ref, x):
    for b in range(ref.shape[1]):
        ref[:, b, :, :] = x[:, b * LANES:(b + 1) * LANES].reshape(ref.shape[0], SUBLANES, LANES)


def _load_vreg_tiles(ref):
    rows = ref.shape[0] * SUBLANES
    return jnp.concatenate([ref[:, b, :, :].reshape(rows, LANES) for b in range(ref.shape[1])], axis=1)


def _relayout_copies(flat_ref, tiled_ref, sem, to_tiled):
    cps = []
    for c in range(SUBLANES):
        flat = flat_ref.at[:, pl.ds(c * LANES, LANES)]
        tiled = tiled_ref.at[:, c, :]
        cps.append(pltpu.make_async_copy(flat, tiled, sem) if to_tiled else pltpu.make_async_copy(tiled, flat, sem))
    return cps


def _log_sigmoid(x):
    return jnp.minimum(x, 0.0) - jnp.log1p(jnp.exp(-jnp.abs(x)))


def _inproj_kernel(x_ref, g_ref, cos_ref, sin_ref, wqk_ref, wvt_ref, wlqk_ref, wlvr_ref,
                   wg_ref, wlow_ref, gup_ref, gbias_ref,
                   qk_ref, vt_ref, lqk_ref, lvr_ref, gfb_ref, sg_ref):
    h = _rms(x_ref[...], g_ref[...]).astype(BF16)

    cos = cos_ref[...]
    sin_signed = sin_ref[...]
    first_half = (lax.broadcasted_iota(I32, (1, LANES), 1) % DA_HEAD_DIM) < (DA_HEAD_DIM // 2)
    for c in range(2 * DA_QK_WIDTH // LANES):
        z = jnp.dot(h, wqk_ref[:, c * LANES:(c + 1) * LANES], preferred_element_type=F32)
        if c < DA_QK_WIDTH // LANES:
            z = z * LOG2_E
        partner = jnp.where(first_half, pltpu.roll(z, LANES - DA_HEAD_DIM // 2, 1),
                            pltpu.roll(z, DA_HEAD_DIM // 2, 1))
        qk_ref[:, c * LANES:(c + 1) * LANES] = (z * cos + partner * sin_signed).astype(BF16)

    vt = lax.dot_general(wvt_ref[...], h, NT_DIMS, preferred_element_type=F32).astype(BF16)
    for hd in range(DA_HEADS):
        vt_ref[0, hd * VT_ROWS:hd * VT_ROWS + DA_VDIM, :] = vt[hd * DA_VDIM:(hd + 1) * DA_VDIM]
        vt_ref[0, hd * VT_ROWS + DA_VDIM:(hd + 1) * VT_ROWS, :] = jnp.ones((VT_ROWS - DA_VDIM, vt.shape[1]), BF16)
    lqk_ref[...] = jnp.dot(h, wlqk_ref[...], preferred_element_type=F32)
    lvr_ref[...] = jnp.dot(h, wlvr_ref[...], preferred_element_type=F32).astype(BF16)
    sg_ref[...] = jax.nn.sigmoid(jnp.dot(h, wg_ref[...], preferred_element_type=F32)).astype(BF16)
    low = jnp.dot(h, wlow_ref[...], preferred_element_type=F32).astype(BF16)
    pre = jnp.dot(low, gup_ref[...], preferred_element_type=F32) + gbias_ref[...]
    gfb_ref[...] = _log_sigmoid(pre) * (1.0 / GLA_GATE_NORMALIZER)


def _inproj(x2, seq_len, norm_g, wqk, wvt, wlqk, wlvr, wg, wlow, gup, gbias, cos_t, sin_t):
    T, D = x2.shape
    tm = TOKEN_TILE
    assert T % tm == 0 and seq_len % tm == 0
    nl = seq_len // tm
    row = lambda n: pl.BlockSpec((tm, n), lambda i: (i, 0))
    out_shape = (
        jax.ShapeDtypeStruct((T, 2 * DA_QK_WIDTH), BF16),
        jax.ShapeDtypeStruct((T // tm, DA_HEADS * VT_ROWS, tm), BF16),
        jax.ShapeDtypeStruct((T, 2 * GLA_KEY_WIDTH), F32),
        jax.ShapeDtypeStruct((T, 2 * GLA_WIDTH), BF16),
        jax.ShapeDtypeStruct((T, 2 * GLA_KEY_WIDTH), F32),
        jax.ShapeDtypeStruct((T, 2 * D_MODEL), BF16),
    )
    return pl.pallas_call(
        _inproj_kernel,
        grid=(T // tm,),
        in_specs=[row(D), _full(norm_g.shape),
                  pl.BlockSpec((tm, LANES), lambda i: (i % nl, 0)),
                  pl.BlockSpec((tm, LANES), lambda i: (i % nl, 0)),
                  _full(wqk.shape), _full(wvt.shape), _full(wlqk.shape), _full(wlvr.shape),
                  _full(wg.shape), _full(wlow.shape), _full(gup.shape), _full(gbias.shape)],
        out_specs=(row(2 * DA_QK_WIDTH), pl.BlockSpec((1, DA_HEADS * VT_ROWS, tm), lambda i: (i, 0, 0)),
                   row(2 * GLA_KEY_WIDTH), row(2 * GLA_WIDTH), row(2 * GLA_KEY_WIDTH), row(2 * D_MODEL)),
        out_shape=out_shape,
        compiler_params=_cparams(("parallel",)),
        name="inproj",
    )(x2, norm_g, cos_t, sin_t, wqk, wvt, wlqk, wlvr, wg, wlow, gup, gbias)


def _rope_tables(seq_len):
    d = DA_HEAD_DIM
    inv = ROPE_THETA ** (-jnp.arange(0, d, 2, dtype=F32) / d)
    ang = jnp.arange(seq_len, dtype=F32)[:, None] * inv[None, :]
    cos = jnp.concatenate([jnp.cos(ang)] * (2 * LANES // d), -1)
    sin = jnp.sin(ang)
    sin_signed = jnp.concatenate([-sin, sin] * (LANES // d), -1)
    return cos, sin_signed


def _attn_kernel(lamv_ref, q_ref, k_ref, vt_ref, o_ref, acc_sc, sa_sc, sb_sc, *, lam_init):
    nk, _, tk = vt_ref.shape
    tq = q_ref.shape[1]
    q = q_ref[0]
    lane = lax.broadcasted_iota(I32, (1, LANES), 1)
    zeros = jnp.zeros_like(q)
    qms = (jnp.where(lane < DA_HEAD_DIM, q, zeros), jnp.where(lane >= DA_HEAD_DIM, q, zeros))
    acc_sc[...] = jnp.zeros(acc_sc.shape, F32)

    def scores(j, dst_ref):
        kj = k_ref[0, pl.ds(pl.multiple_of(j * tk, tk), tk), :]
        for mp in range(2):
            dst_ref[mp] = lax.dot_general(kj, qms[mp], NT_DIMS, preferred_element_type=F32)

    def softmax_pv(j, src_ref, ms):
        vj = vt_ref[j]
        new = []
        for mp in range(2):
            m_new = jnp.maximum(ms[mp], jnp.max(src_ref[mp], axis=0, keepdims=True))
            alpha = jnp.exp2(ms[mp] - m_new)
            for cols in (slice(c, c + MXU_TILE) for c in range(0, tq, MXU_TILE)):
                part = None
                for rows in (slice(r, r + MXU_TILE) for r in range(0, tk, MXU_TILE)):
                    p = jnp.exp2((src_ref[mp, rows, cols] - m_new[:, cols]).astype(BF16))
                    d = jnp.dot(vj[:, rows], p, preferred_element_type=F32)
                    part = d if part is None else part + d
                acc_sc[mp, :, cols] = acc_sc[mp, :, cols] * alpha[:, cols] + part
            new.append(m_new)
        return tuple(new)

    bufs = (sa_sc, sb_sc)
    scores(0, bufs[0])
    ms = tuple(jnp.full((1, tq), -jnp.inf, F32) for _ in range(2))
    for j in range(nk):
        if j + 1 < nk:
            scores(j + 1, bufs[(j + 1) % 2])
        ms = softmax_pv(j, bufs[j % 2], ms)

    lv = lamv_ref[...]
    lam = (jnp.exp(jnp.sum(lv[0:1] * lv[1:2], axis=-1, keepdims=True))
           - jnp.exp(jnp.sum(lv[2:3] * lv[3:4], axis=-1, keepdims=True)) + lam_init)
    a0, a1 = acc_sc[0], acc_sc[1]
    o_ref[0] = (a0[:DA_VDIM] / a0[DA_VDIM:DA_VDIM + 1] - lam * (a1[:DA_VDIM] / a1[DA_VDIM:DA_VDIM + 1])).T


def _diff_attention(qk3, vt3, lamv, lam_init, tq=512):
    B, L, _ = qk3.shape
    tk = vt3.shape[-1]
    nkb = L // tk
    return pl.pallas_call(
        functools.partial(_attn_kernel, lam_init=lam_init),
        grid=(B, DA_HEADS, L // tq),
        in_specs=[_full(lamv.shape),
                  pl.BlockSpec((1, tq, LANES), lambda b, h, i: (b, i, h)),
                  pl.BlockSpec((1, L, LANES), lambda b, h, i: (b, 0, DA_HEADS + h)),
                  pl.BlockSpec((nkb, VT_ROWS, tk), lambda b, h, i: (b, h, 0))],
        out_specs=pl.BlockSpec((1, tq, DA_VDIM), lambda b, h, i: (b, i, h)),
        out_shape=jax.ShapeDtypeStruct((B, L, DA_WIDTH), F32),
        scratch_shapes=[pltpu.VMEM((2, VT_ROWS, tq), F32), pltpu.VMEM((2, tk, tq), F32),
                        pltpu.VMEM((2, tk, tq), F32)],
        compiler_params=_cparams(("parallel", "parallel", "arbitrary")),
        name="diff_attention",
    )(lamv, qk3, qk3, vt3)


def _split3_bf16(x):
    hi = x.astype(BF16)
    r1 = x - hi.astype(F32)
    mid = r1.astype(BF16)
    return hi, mid, (r1 - mid.astype(F32)).astype(BF16)


def _gla_local(qk_ref, vr_ref, g_ref, *, reverse):
    C = GLA_CHUNK
    tb = qk_ref.shape[1]
    nchunk = tb // C
    row = lax.broadcasted_iota(I32, (tb, tb), 0)
    col = lax.broadcasted_iota(I32, (tb, tb), 1)
    same_chunk = (row // C) == (col // C)
    tri = same_chunk & ((col >= row) if reverse else (col <= row))
    tri_b = jnp.where(tri, 1.0, 0.0).astype(BF16)
    lane = lax.broadcasted_iota(I32, (1, LANES), 1)
    goff = GLA_KEY_WIDTH if reverse else 0

    g = g_ref[0, :, goff:goff + GLA_KEY_WIDTH]
    q = qk_ref[0, :, 0:GLA_KEY_WIDTH] * (GLA_KDIM ** -0.5)
    k = qk_ref[0, :, GLA_KEY_WIDTH:2 * GLA_KEY_WIDTH]
    v = vr_ref[0]
    b = sum(jnp.dot(tri_b, part, preferred_element_type=F32) for part in _split3_bf16(g))
    last_row, ref_row = (0, C // 2) if reverse else (C - 1, C // 2 - 1)
    per_chunk = lambda r: jnp.concatenate(
        [jnp.broadcast_to(b[c * C + r:c * C + r + 1], (C, GLA_KEY_WIDTH)) for c in range(nchunk)], axis=0)
    b_last, b_ref = per_chunk(last_row), per_chunk(ref_row)
    q_in = q * jnp.exp(b - b_ref)
    k_in = (k * jnp.exp(b_ref - b)).astype(BF16)
    k_dec = k * jnp.exp(b_last - b)
    q_dec = q * jnp.exp(b)
    dec = [jnp.exp(b[c * C + last_row:c * C + last_row + 1]) for c in range(nchunk)]

    heads = []
    for h in range(GLA_HEADS):
        ps = slice((h // 2) * LANES, (h // 2 + 1) * LANES)
        keep = (lane < GLA_KDIM) if h % 2 == 0 else (lane >= GLA_KDIM)
        zero = jnp.zeros((tb, LANES), F32)
        heads.append(dict(
            ps=ps, vs=slice(h * GLA_VDIM, (h + 1) * GLA_VDIM),
            q_in=jnp.where(keep, q_in[:, ps], zero).astype(BF16),
            q_dec=jnp.where(keep, q_dec[:, ps], zero).astype(BF16),
            k_dec=jnp.where(keep, k_dec[:, ps], zero).astype(BF16)))
    scores = [lax.dot_general(hd["q_in"], k_in[:, hd["ps"]], NT_DIMS, preferred_element_type=F32) for hd in heads]
    kv_t = [[lax.dot_general(v[c * C:(c + 1) * C, hd["vs"]], hd["k_dec"][c * C:(c + 1) * C], TN_DIMS,
                             preferred_element_type=F32) for hd in heads] for c in range(nchunk)]
    intra = [jnp.dot(jnp.where(tri, s, 0.0).astype(BF16), v[:, hd["vs"]], preferred_element_type=F32)
             for s, hd in zip(scores, heads)]
    return heads, dec, kv_t, intra


def _gla_scan(heads, dec, kv_t, st_ref, *, reverse):
    nchunk = len(dec)
    state = [st_ref[h] for h in range(GLA_HEADS)]
    entering = [None] * nchunk
    for c in (range(nchunk - 1, -1, -1) if reverse else range(nchunk)):
        entering[c] = [s.astype(BF16) for s in state]
        state = [s * dec[c][:, hd["ps"]] + kv_t[c][h] for h, (s, hd) in enumerate(zip(state, heads))]
    for h in range(GLA_HEADS):
        st_ref[h] = state[h]
    return entering


def _gla_emit(heads, intra, entering, o_ref):
    C = GLA_CHUNK
    for c, states in enumerate(entering):
        rows = slice(c * C, (c + 1) * C)
        for hd, o_intra, s_t in zip(heads, intra, states):
            inter = lax.dot_general(hd["q_dec"][rows], s_t, NT_DIMS, preferred_element_type=F32)
            o_ref[0, rows, hd["vs"]] = o_intra[rows] + inter


def _gla_kernel(qkf_ref, vrf_ref, gf_ref, qkb_ref, vrb_ref, gb_ref, of_ref, ob_ref, sf_sc, sb_sc):
    @pl.when(pl.program_id(1) == 0)
    def _():
        sf_sc[...] = jnp.zeros(sf_sc.shape, F32)
        sb_sc[...] = jnp.zeros(sb_sc.shape, F32)

    hf, df, kf, inf = _gla_local(qkf_ref, vrf_ref, gf_ref, reverse=False)
    hb, db, kb, inb = _gla_local(qkb_ref, vrb_ref, gb_ref, reverse=True)
    ef = _gla_scan(hf, df, kf, sf_sc, reverse=False)
    eb = _gla_scan(hb, db, kb, sb_sc, reverse=True)
    _gla_emit(hf, inf, ef, of_ref)
    _gla_emit(hb, inb, eb, ob_ref)


def _gla(lqk3, lvr3, gfb3, tb=256):
    B, L, _ = lqk3.shape
    nb = L // tb
    fwd = lambda n: pl.BlockSpec((1, tb, n), lambda b, i: (b, i, 0))
    bwd = lambda n: pl.BlockSpec((1, tb, n), lambda b, i: (b, nb - 1 - i, 0))
    out = jax.ShapeDtypeStruct((B, L, GLA_WIDTH), F32)
    return pl.pallas_call(
        _gla_kernel,
        grid=(B, nb),
        in_specs=[fwd(2 * GLA_KEY_WIDTH), fwd(GLA_WIDTH), fwd(2 * GLA_KEY_WIDTH),
                  bwd(2 * GLA_KEY_WIDTH), bwd(GLA_WIDTH), bwd(2 * GLA_KEY_WIDTH)],
        out_specs=(fwd(GLA_WIDTH), bwd(GLA_WIDTH)),
        out_shape=(out, out),
        scratch_shapes=[pltpu.VMEM((GLA_HEADS, GLA_VDIM, LANES), F32),
                        pltpu.VMEM((GLA_HEADS, GLA_VDIM, LANES), F32)],
        compiler_params=_cparams(("parallel", "arbitrary")),
        name="gla",
    )(lqk3, lvr3, gfb3, lqk3, lvr3, gfb3)


def _head_norm(z, g):
    return jnp.concatenate(
        [_rms(z[:, h * LANES:(h + 1) * LANES], g) for h in range(z.shape[1] // LANES)], axis=1)


def _merge_kernel(x_ref, oa_ref, of_ref, ob_ref, r_ref, sg_ref, subg_ref, glag_ref, wpa_ref, wpb_ref,
                  wout_ref, nffn_ref, rwt_ref, rb_ref,
                  x1_ref, h2_ref, ti_ref, rk_ref, tg_ref, cnt_ref, carry_sc, *, lam_init):
    tm = x_ref.shape[0]

    @pl.when(pl.program_id(0) == 0)
    def _():
        carry_sc[...] = jnp.zeros(carry_sc.shape, F32)

    oa = (_head_norm(oa_ref[...], subg_ref[...]) * (1.0 - lam_init)).astype(BF16)
    ob = _head_norm(of_ref[...] + ob_ref[...], glag_ref[...]) * jax.nn.silu(r_ref[...].astype(F32))
    pa = jnp.dot(oa, wpa_ref[...], preferred_element_type=F32)
    pb = jnp.dot(ob.astype(BF16), wpb_ref[...], preferred_element_type=F32)
    sg = sg_ref[...].astype(F32)
    merged = sg[:, :D_MODEL] * pa + sg[:, D_MODEL:] * pb
    x1 = x_ref[...] + jnp.dot(merged.astype(BF16), wout_ref[...], preferred_element_type=F32)
    x1_ref[...] = x1
    h2 = _rms(x1, nffn_ref[...])
    _store_vreg_tiles(h2_ref, h2)

    logits = lax.dot_general(rwt_ref[...], h2.astype(BF16), NT_DIMS, preferred_element_type=F32) + rb_ref[...]
    eidx = lax.broadcasted_iota(I32, logits.shape, 0)
    vals, sels = [], []
    for k in range(TOP_K):
        mk = jnp.max(logits, axis=0, keepdims=True)
        ik = jnp.min(jnp.where(logits == mk, eidx, N_EXPERTS), axis=0, keepdims=True)
        sel = eidx == ik
        logits = jnp.where(sel, -jnp.inf, logits)
        vals.append(mk)
        sels.append(sel)
        ti_ref[k:k + 1, :] = ik
    ex = [jnp.exp(v - vals[0]) for v in vals]
    denom = ex[0] + ex[1] + ex[2] + ex[3]
    for k in range(TOP_K):
        tg_ref[k:k + 1, :] = ex[k] / denom
    tg_ref[TOP_K:, :] = jnp.zeros((tg_ref.shape[0] - TOP_K, tm), F32)

    multi = (sels[0] | sels[1] | sels[2] | sels[3])
    multi_f = jnp.where(multi, 1.0, 0.0)
    before = lax.broadcasted_iota(I32, (tm, tm), 0) < lax.broadcasted_iota(I32, (tm, tm), 1)
    cum = jnp.dot(multi_f.astype(BF16), jnp.where(before, 1.0, 0.0).astype(BF16), preferred_element_type=F32)
    tot = carry_sc[:, 0:1] + cum
    for k in range(TOP_K):
        rk_ref[k:k + 1, :] = jnp.sum(jnp.where(sels[k], tot, 0.0), axis=0, keepdims=True).astype(I32)
    carry_sc[...] = carry_sc[...] + jnp.sum(multi_f, axis=1, keepdims=True)
    cnt_ref[...] = carry_sc[...]


def _merge(x2, oa, of, ob, lvr, sg, subg, glag, wpa, wpb, wout, nffn, rwt, rb, lam_init):
    T, D = x2.shape
    tm = TOKEN_TILE
    row = lambda n: pl.BlockSpec((tm, n), lambda i: (i, 0))
    col = lambda n: pl.BlockSpec((n, tm), lambda i: (0, i))
    out_shape = (
        jax.ShapeDtypeStruct((T, D), F32),
        jax.ShapeDtypeStruct(_vreg_tile_shape(T, D), F32),
        jax.ShapeDtypeStruct((TOP_K, T), I32),
        jax.ShapeDtypeStruct((TOP_K, T), I32),
        jax.ShapeDtypeStruct((SUBLANES, T), F32),
        jax.ShapeDtypeStruct((N_EXPERTS, LANES), F32),
    )
    return pl.pallas_call(
        functools.partial(_merge_kernel, lam_init=lam_init),
        grid=(T // tm,),
        in_specs=[row(D), row(DA_WIDTH), row(GLA_WIDTH), row(GLA_WIDTH),
                  pl.BlockSpec((tm, GLA_WIDTH), lambda i: (i, 1)), row(2 * D),
                  _full(subg.shape), _full(glag.shape), _full(wpa.shape), _full(wpb.shape),
                  _full(wout.shape), _full(nffn.shape), _full(rwt.shape), _full(rb.shape)],
        out_specs=(row(D), pl.BlockSpec(_vreg_tile_shape(tm, D), lambda i: (i, 0, 0, 0)),
                   col(TOP_K), col(TOP_K), col(SUBLANES), _full((N_EXPERTS, LANES))),
        out_shape=out_shape,
        scratch_shapes=[pltpu.VMEM((N_EXPERTS, LANES), F32)],
        compiler_params=_cparams(("arbitrary",)),
        name="merge_router",
    )(x2, oa, of, ob, lvr, sg, subg, glag, wpa, wpb, wout, nffn, rwt, rb)


def _dispatch_kernel(pend_ref, dest_ref, h_ref, xs_ref, zero_sc, sem):
    tm = h_ref.shape[0] * SUBLANES
    rows = zero_sc.shape[0]

    @pl.when(pl.program_id(0) == 0)
    def _():
        zero_sc[...] = jnp.zeros(zero_sc.shape, F32)
        for e in range(N_EXPERTS):
            end = pend_ref[e]
            begin = pend_ref[e - 1] if e else 0

            @pl.when(end > begin)
            def _():
                last = pl.multiple_of(end - rows, rows)
                cp = pltpu.make_async_copy(zero_sc, xs_ref.at[pl.ds(last, rows)], sem)
                cp.start()
                cp.wait()

        def zero_unused(blk, c):
            cp = pltpu.make_async_copy(zero_sc, xs_ref.at[pl.ds(pl.multiple_of(blk * rows, rows), rows)], sem)
            cp.start()
            cp.wait()
            return c

        lax.fori_loop(pend_ref[N_EXPERTS - 1] // rows, xs_ref.shape[0] // rows, zero_unused, 0)

    def row_copy(g, s, k):
        return pltpu.make_async_copy(h_ref.at[g, :, s, :], xs_ref.at[dest_ref[k, g * SUBLANES + s]], sem)

    def start(g, c):
        for s in range(SUBLANES):
            for k in range(TOP_K):
                row_copy(g, s, k).start(priority=k % 2)
        return c

    def wait(g, c):
        for s in range(SUBLANES):
            for k in range(TOP_K):
                row_copy(g, s, k).wait()
        return c

    lax.fori_loop(0, tm // SUBLANES, start, 0)
    lax.fori_loop(0, tm // SUBLANES, wait, 0)


def _dispatch(h2, dest, pad_end, n_slots):
    T = h2.shape[0] * SUBLANES
    tm = TOKEN_TILE
    h_block = (tm // SUBLANES,) + h2.shape[1:]
    return pl.pallas_call(
        _dispatch_kernel,
        grid_spec=pltpu.PrefetchScalarGridSpec(
            num_scalar_prefetch=1,
            grid=(T // tm,),
            in_specs=[pl.BlockSpec((TOP_K, tm), lambda i, pe: (0, i), memory_space=pltpu.SMEM),
                      pl.BlockSpec(h_block, lambda i, pe: (i, 0, 0, 0))],
            out_specs=pl.BlockSpec(memory_space=pl.ANY),
            scratch_shapes=[pltpu.VMEM((EXPERT_ROWS,) + ROW_TILE, F32), pltpu.SemaphoreType.DMA(())],
        ),
        out_shape=jax.ShapeDtypeStruct((n_slots,) + ROW_TILE, F32),
        compiler_params=_cparams(("arbitrary",)),
        name="moe_dispatch",
    )(pad_end, dest, h2)


def _deinterleave_kernel(w_ref, o_ref):
    n = 2 * LANES
    src = lax.broadcasted_iota(I32, (n, n), 0)
    dst = lax.broadcasted_iota(I32, (n, n), 1)
    perm = jnp.where(src == jnp.where(dst < LANES, 2 * dst, 2 * (dst - LANES) + 1), 1.0, 0.0).astype(BF16)
    for c in range(w_ref.shape[2] // n):
        w = w_ref[0, :, c * n:(c + 1) * n].astype(BF16)
        o_ref[0, :, c * n:(c + 1) * n] = jnp.dot(w, perm, preferred_element_type=F32).astype(BF16)


def _deinterleave_w1(w1, cols=512):
    E, K, N = w1.shape
    spec = pl.BlockSpec((1, K, cols), lambda e, j: (e, 0, j))
    return pl.pallas_call(
        _deinterleave_kernel,
        grid=(E, N // cols),
        in_specs=[spec],
        out_specs=spec,
        out_shape=jax.ShapeDtypeStruct(w1.shape, BF16),
        compiler_params=_cparams(("parallel", "parallel")),
        name="moe_w1_layout",
    )(w1)


def _expert_kernel(blk_ref, nused_ref, xs_ref, w1_ref, b1_ref, w2_ref, b2_ref, ys_ref,
                   xbuf, ybuf, in_sem, out_sem):
    i = pl.program_id(0)
    n = pl.num_programs(0)
    bm = xbuf.shape[1]
    slot = i % 2

    def rows_of(ref, blk):
        return ref.at[pl.ds(pl.multiple_of(blk * bm, bm), bm)]

    def loads(blk, s):
        return _relayout_copies(xbuf.at[s], rows_of(xs_ref, blk), in_sem.at[s], to_tiled=False)

    def stores(blk, s):
        return _relayout_copies(ybuf.at[s], rows_of(ys_ref, blk), out_sem.at[s], to_tiled=True)

    @pl.when(i == 0)
    def _():
        for cp in loads(0, 0):
            cp.start()

    @pl.when(i + 1 < n)
    def _():
        for cp in loads(i + 1, 1 - slot):
            cp.start()

    for cp in loads(i, slot):
        cp.wait()

    @pl.when(i >= 2)
    def _():
        for cp in stores(i - 2, slot):
            cp.wait()

    @pl.when(i < nused_ref[0])
    def _():
        x = xbuf[slot].astype(BF16)
        hid = jnp.dot(x, w1_ref[0], preferred_element_type=F32) + b1_ref[0]
        acts = []
        for c in range(hid.shape[1] // (2 * LANES)):
            gate = jnp.minimum(hid[:, 2 * c * LANES:(2 * c + 1) * LANES], SWIGLU_LIMIT)
            up = jnp.clip(hid[:, (2 * c + 1) * LANES:(2 * c + 2) * LANES], -SWIGLU_LIMIT, SWIGLU_LIMIT)
            acts.append(((up + 1.0) * (gate * jax.nn.sigmoid(SWIGLU_ALPHA * gate))).astype(BF16))
        act = jnp.concatenate(acts, axis=1)
        ybuf[slot] = jnp.dot(act, w2_ref[0], preferred_element_type=F32) + b2_ref[0]

    @pl.when(i >= nused_ref[0])
    def _():
        ybuf[slot] = jnp.zeros(ybuf.shape[1:], F32)

    for cp in stores(i, slot):
        cp.start()

    @pl.when(i == n - 1)
    def _():
        for cp in stores(i, slot):
            cp.wait()

    @pl.when((i == n - 1) & (n >= 2))
    def _():
        for cp in stores(i - 1, 1 - slot):
            cp.wait()


def _experts(xs, blk_e, nused, w1, b1, w2, b2):
    P = xs.shape[0]
    D = w1.shape[1]
    bm = EXPERT_ROWS
    nb = P // bm
    per_expert = lambda a: pl.BlockSpec((1,) + a.shape[1:], lambda i, blk, nu: (blk[i], 0, 0))
    return pl.pallas_call(
        _expert_kernel,
        grid_spec=pltpu.PrefetchScalarGridSpec(
            num_scalar_prefetch=2,
            grid=(nb,),
            in_specs=[pl.BlockSpec(memory_space=pl.ANY), per_expert(w1), per_expert(b1), per_expert(w2),
                      per_expert(b2)],
            out_specs=pl.BlockSpec(memory_space=pl.ANY),
            scratch_shapes=[pltpu.VMEM((2, bm, D), F32), pltpu.VMEM((2, bm, D), F32),
                            pltpu.SemaphoreType.DMA((2,)), pltpu.SemaphoreType.DMA((2,))],
        ),
        out_shape=jax.ShapeDtypeStruct((P,) + ROW_TILE, F32),
        compiler_params=_cparams(("arbitrary",)),
        name="moe_experts",
    )(blk_e, nused, xs, w1, b1, w2, b2)


def _combine_kernel(dest_ref, ys_ref, tg_ref, x1_ref, gfin_ref, y_ref, buf, sem, *, final):
    tc = x1_ref.shape[0]

    def row_copy(g, s, k):
        return pltpu.make_async_copy(ys_ref.at[dest_ref[k, g * SUBLANES + s]], buf.at[k, g, :, s, :], sem)

    def start(g, c):
        for s in range(SUBLANES):
            for k in range(TOP_K):
                row_copy(g, s, k).start(priority=k % 2)
        return c

    def wait(g, c):
        for s in range(SUBLANES):
            for k in range(TOP_K):
                row_copy(g, s, k).wait()
        return c

    lax.fori_loop(0, tc // SUBLANES, start, 0)
    lax.fori_loop(0, tc // SUBLANES, wait, 0)
    gates = tg_ref[...].T
    x = x1_ref[...]
    for k in range(TOP_K):
        x = x + _load_vreg_tiles(buf.at[k]) * gates[:, k:k + 1]
    y_ref[...] = _rms(x, gfin_ref[...]) if final else x


def _combine(ys, dest, tg, x1, gfin, final, tc=256):
    T, D = x1.shape
    return pl.pallas_call(
        functools.partial(_combine_kernel, final=final),
        grid=(T // tc,),
        in_specs=[pl.BlockSpec((TOP_K, tc), lambda i: (0, i), memory_space=pltpu.SMEM),
                  pl.BlockSpec(memory_space=pl.ANY),
                  pl.BlockSpec((SUBLANES, tc), lambda i: (0, i)),
                  pl.BlockSpec((tc, D), lambda i: (i, 0)),
                  _full(gfin.shape)],
        out_specs=pl.BlockSpec((tc, D), lambda i: (i, 0)),
        out_shape=jax.ShapeDtypeStruct((T, D), F32),
        scratch_shapes=[pltpu.VMEM((TOP_K,) + _vreg_tile_shape(tc, D), F32), pltpu.SemaphoreType.DMA(())],
        compiler_params=_cparams(("parallel",)),
        name="moe_combine",
    )(dest, ys, tg, x1, gfin)


def _moe(h2, ti, rk, tg, cnt, x1, moe_w, gfin, final):
    T = x1.shape[0]
    bm = EXPERT_ROWS
    n_assign = T * TOP_K
    nb = -(-(n_assign + N_EXPERTS * (bm - 1)) // bm)
    counts = cnt[:, 0].astype(I32)
    padded = (counts + bm - 1) // bm * bm
    pad_end = jnp.cumsum(padded).astype(I32)
    pad_start = pad_end - padded
    experts = jnp.arange(N_EXPERTS, dtype=I32)[:, None, None]
    dest = rk + jnp.sum(jnp.where(ti[None] == experts, pad_start[:, None, None], 0), axis=0)
    nused = (pad_end[-1:] // bm).astype(I32)
    blk_start = jnp.minimum(jnp.arange(nb, dtype=I32), nused[0] - 1) * bm
    blk_e = jnp.minimum(jnp.sum((pad_end[None, :] <= blk_start[:, None]).astype(I32), axis=1), N_EXPERTS - 1)
    xs = _dispatch(h2, dest, pad_end, nb * bm)
    ys = _experts(xs, blk_e, nused, *moe_w)
    return _combine(ys, dest, tg, x1, gfin, final)


def _split_in_proj(w):
    sizes = (DA_QK_WIDTH, DA_QK_WIDTH, DA_WIDTH, GLA_KEY_WIDTH, GLA_KEY_WIDTH, GLA_WIDTH, GLA_WIDTH,
             GLA_GATE_RANK, GLA_GATE_RANK, D_MODEL, D_MODEL)
    assert w.shape[-1] == sum(sizes)
    out, o = [], 0
    for s in sizes:
        out.append(w[:, o:o + s])
        o += s
    return out


def _prep_layer(l, norm_mix_g, w_in, lambda_q1, lambda_k1, lambda_q2, lambda_k2, diff_subln_g,
                gla_gate_up_f, gla_gate_bias_f, gla_gate_up_b, gla_gate_bias_b, gla_norm_g,
                w_proj_a, w_proj_b, w_out, norm_ffn_g, router_w, router_b, moe_w1, moe_b1, moe_w2, moe_b2):
    dq, dk, dv, lq, lk, lv, lr, laf, lab, ga, gb = _split_in_proj(w_in[l])
    bf = lambda a: a.astype(BF16)
    r = GLA_GATE_RANK
    gup = jnp.zeros((LANES, 2 * GLA_KEY_WIDTH), F32)
    gup = gup.at[0:r, 0:GLA_KEY_WIDTH].set(gla_gate_up_f[l]).at[r:2 * r, GLA_KEY_WIDTH:].set(gla_gate_up_b[l])
    lamv = jnp.zeros((SUBLANES, LANES), F32)
    for i, v in enumerate((lambda_q1, lambda_k1, lambda_q2, lambda_k2)):
        lamv = lamv.at[i, 0:DA_HEAD_DIM].set(v[l].astype(F32))
    inproj = dict(
        norm_g=norm_mix_g[l][None, :],
        wqk=bf(jnp.concatenate([dq * (DA_HEAD_DIM ** -0.5), dk], 1)),
        wvt=bf(dv.T),
        wlqk=bf(jnp.concatenate([lq, lk], 1)),
        wlvr=bf(jnp.concatenate([lv, lr], 1)),
        wg=bf(jnp.concatenate([ga, gb], 1)),
        wlow=bf(jnp.concatenate([laf, lab, jnp.zeros((D_MODEL, LANES - 2 * r), F32)], 1)),
        gup=bf(gup),
        gbias=jnp.concatenate([gla_gate_bias_f[l], gla_gate_bias_b[l]])[None, :],
    )
    merge = dict(
        subg=diff_subln_g[l][None, :], glag=gla_norm_g[l][None, :],
        wpa=bf(w_proj_a[l]), wpb=bf(w_proj_b[l]), wout=bf(w_out[l]),
        nffn=norm_ffn_g[l][None, :], rwt=bf(router_w[l].T), rb=router_b[l][:, None],
    )
    b1 = moe_b1[l].reshape(N_EXPERTS, -1, LANES, 2).transpose(0, 1, 3, 2).reshape(N_EXPERTS, 1, -1)
    moe_w = (_deinterleave_w1(moe_w1[l]), b1, bf(moe_w2[l]), moe_b2[l][:, None, :])
    return inproj, lamv, merge, moe_w


def _trunk(x, layers, norm_final_g):
    B, L, D = x.shape
    T = B * L
    cos_t, sin_t = _rope_tables(L)
    x2 = x.reshape(T, D)
    gfin = norm_final_g[None, :]
    for l, (inproj, lamv, merge, moe_w) in enumerate(layers):
        lam_init = 0.8 - 0.6 * math.exp(-0.3 * l)
        qk, vt, lqk, lvr, gfb, sg = _inproj(x2, L, cos_t=cos_t, sin_t=sin_t, **inproj)
        oa = _diff_attention(qk.reshape(B, L, -1), vt, lamv, lam_init)
        of, ob = _gla(lqk.reshape(B, L, -1), lvr.reshape(B, L, -1), gfb.reshape(B, L, -1))
        x1, h2, ti, rk, tg, cnt = _merge(x2, oa.reshape(T, -1), of.reshape(T, -1), ob.reshape(T, -1),
                                         lvr, sg, lam_init=lam_init, **merge)
        x2 = _moe(h2, ti, rk, tg, cnt, x1, moe_w, gfin, final=(l == len(layers) - 1))
    return x2.reshape(B, L, D)


def kernel(x_prompt, x_sample, norm_mix_g, w_in, lambda_q1, lambda_k1, lambda_q2, lambda_k2, diff_subln_g,
           gla_gate_up_f, gla_gate_bias_f, gla_gate_up_b, gla_gate_bias_b, gla_norm_g, w_proj_a, w_proj_b,
           w_out, norm_ffn_g, router_w, router_b, moe_w1, moe_b1, moe_w2, moe_b2, norm_final_g):
    depth = w_in.shape[0]
    layers = [_prep_layer(l, norm_mix_g, w_in, lambda_q1, lambda_k1, lambda_q2, lambda_k2, diff_subln_g,
                          gla_gate_up_f, gla_gate_bias_f, gla_gate_up_b, gla_gate_bias_b, gla_norm_g,
                          w_proj_a, w_proj_b, w_out, norm_ffn_g, router_w, router_b,
                          moe_w1, moe_b1, moe_w2, moe_b2) for l in range(depth)]
    return _trunk(x_prompt, layers, norm_final_g), _trunk(x_sample, layers, norm_final_g)
```

```python
import functools
import math

import jax
import jax.numpy as jnp
from jax import lax
from jax.experimental import pallas as pl
from jax.experimental.pallas import tpu as pltpu

F32 = jnp.float32
BF16 = jnp.bfloat16
I32 = jnp.int32

D_MODEL = 1024
DA_HEADS = 4
DA_HEAD_DIM = 64
DA_VDIM = 128
DA_QK_WIDTH = DA_HEADS * 2 * DA_HEAD_DIM
DA_WIDTH = DA_HEADS * DA_VDIM
ROPE_THETA = 10000.0
GLA_HEADS = 4
GLA_KDIM = 64
GLA_VDIM = 128
GLA_KEY_WIDTH = GLA_HEADS * GLA_KDIM
GLA_WIDTH = GLA_HEADS * GLA_VDIM
GLA_GATE_RANK = 16
GLA_GATE_NORMALIZER = 16.0
GLA_CHUNK = 64
N_EXPERTS = 32
TOP_K = 4
D_FF = D_MODEL
SWIGLU_ALPHA = 1.702
SWIGLU_LIMIT = 7.0
NORM_EPS = 1e-5
LANES = 128
SUBLANES = 8

BF16_SUBLANES = 16
MXU_TILE = 256
LOG2_E = 1.4426950408889634
VT_ROWS = DA_VDIM + BF16_SUBLANES

VMEM_LIMIT = 56 * 1024 * 1024

TOKEN_TILE = 512
ATTN_KEY_CHUNK = 512
EXPERT_ROWS = 512

NT_DIMS = (((1,), (1,)), ((), ()))
TN_DIMS = (((0,), (0,)), ((), ()))


def _cparams(sem):
    return pltpu.CompilerParams(dimension_semantics=sem, vmem_limit_bytes=VMEM_LIMIT)


def _full(shape):
    nd = len(shape)
    return pl.BlockSpec(shape, lambda *_: (0,) * nd)


def _rms(x, g):
    return x * lax.rsqrt(jnp.mean(x * x, axis=-1, keepdims=True) + NORM_EPS) * g


ROW_TILE = (SUBLANES, LANES)


def _vreg_tile_shape(rows, cols):
    return (rows // SUBLANES, cols // LANES, SUBLANES, LANES)


def _vreg_tile_row(ref, r):
    return ref.at[r // SUBLANES, :, r % SUBLANES, :]


def _store_vreg_tiles(ref, x):
    for b in range(ref.shape[1]):
        ref[:, b, :, :] = x[:, b * LANES:(b + 1) * LANES].reshape(ref.shape[0], SUBLANES, LANES)


def _load_vreg_tiles(ref):
    rows = ref.shape[0] * SUBLANES
    return jnp.concatenate([ref[:, b, :, :].reshape(rows, LANES) for b in range(ref.shape[1])], axis=1)


def _relayout_copies(flat_ref, tiled_ref, sem, to_tiled):
    cps = []
    for c in range(SUBLANES):
        flat = flat_ref.at[:, pl.ds(c * LANES, LANES)]
        tiled = tiled_ref.at[:, c, :]
        cps.append(pltpu.make_async_copy(flat, tiled, sem) if to_tiled else pltpu.make_async_copy(tiled, flat, sem))
    return cps


def _log_sigmoid(x):
    return jnp.minimum(x, 0.0) - jnp.log1p(jnp.exp(-jnp.abs(x)))


def _inproj_kernel(x_ref, g_ref, cos_ref, sin_ref, wqk_ref, wvt_ref, wlqk_ref, wlvr_ref,
                   wg_ref, wlow_ref, gup_ref, gbias_ref,
                   qk_ref, vt_ref, lqk_ref, lvr_ref, gfb_ref, sg_ref):
    h = _rms(x_ref[...], g_ref[...]).astype(BF16)

    cos = cos_ref[...]
    sin_signed = sin_ref[...]
    first_half = (lax.broadcasted_iota(I32, (1, LANES), 1) % DA_HEAD_DIM) < (DA_HEAD_DIM // 2)
    for c in range(2 * DA_QK_WIDTH // LANES):
        z = jnp.dot(h, wqk_ref[:, c * LANES:(c + 1) * LANES], preferred_element_type=F32)
        if c < DA_QK_WIDTH // LANES:
            z = z * LOG2_E
        partner = jnp.where(first_half, pltpu.roll(z, LANES - DA_HEAD_DIM // 2, 1),
                            pltpu.roll(z, DA_HEAD_DIM // 2, 1))
        qk_ref[:, c * LANES:(c + 1) * LANES] = (z * cos + partner * sin_signed).astype(BF16)

    vt = lax.dot_general(wvt_ref[...], h, NT_DIMS, preferred_element_type=F32).astype(BF16)
    for hd in range(DA_HEADS):
        vt_ref[0, hd * VT_ROWS:hd * VT_ROWS + DA_VDIM, :] = vt[hd * DA_VDIM:(hd + 1) * DA_VDIM]
        vt_ref[0, hd * VT_ROWS + DA_VDIM:(hd + 1) * VT_ROWS, :] = jnp.ones((VT_ROWS - DA_VDIM, vt.shape[1]), BF16)
    lqk_ref[...] = jnp.dot(h, wlqk_ref[...], preferred_element_type=F32)
    lvr_ref[...] = jnp.dot(h, wlvr_ref[...], preferred_element_type=F32).astype(BF16)
    sg_ref[...] = jax.nn.sigmoid(jnp.dot(h, wg_ref[...], preferred_element_type=F32)).astype(BF16)
    low = jnp.dot(h, wlow_ref[...], preferred_element_type=F32).astype(BF16)
    pre = jnp.dot(low, gup_ref[...], preferred_element_type=F32) + gbias_ref[...]
    gfb_ref[...] = _log_sigmoid(pre) * (1.0 / GLA_GATE_NORMALIZER)


def _inproj(x2, seq_len, norm_g, wqk, wvt, wlqk, wlvr, wg, wlow, gup, gbias, cos_t, sin_t):
    T, D = x2.shape
    tm = TOKEN_TILE
    assert T % tm == 0 and seq_len % tm == 0
    nl = seq_len // tm
    row = lambda n: pl.BlockSpec((tm, n), lambda i: (i, 0))
    out_shape = (
        jax.ShapeDtypeStruct((T, 2 * DA_QK_WIDTH), BF16),
        jax.ShapeDtypeStruct((T // tm, DA_HEADS * VT_ROWS, tm), BF16),
        jax.ShapeDtypeStruct((T, 2 * GLA_KEY_WIDTH), F32),
        jax.ShapeDtypeStruct((T, 2 * GLA_WIDTH), BF16),
        jax.ShapeDtypeStruct((T, 2 * GLA_KEY_WIDTH), F32),
        jax.ShapeDtypeStruct((T, 2 * D_MODEL), BF16),
    )
    return pl.pallas_call(
        _inproj_kernel,
        grid=(T // tm,),
        in_specs=[row(D), _full(norm_g.shape),
                  pl.BlockSpec((tm, LANES), lambda i: (i % nl, 0)),
                  pl.BlockSpec((tm, LANES), lambda i: (i % nl, 0)),
                  _full(wqk.shape), _full(wvt.shape), _full(wlqk.shape), _full(wlvr.shape),
                  _full(wg.shape), _full(wlow.shape), _full(gup.shape), _full(gbias.shape)],
        out_specs=(row(2 * DA_QK_WIDTH), pl.BlockSpec((1, DA_HEADS * VT_ROWS, tm), lambda i: (i, 0, 0)),
                   row(2 * GLA_KEY_WIDTH), row(2 * GLA_WIDTH), row(2 * GLA_KEY_WIDTH), row(2 * D_MODEL)),
        out_shape=out_shape,
        compiler_params=_cparams(("parallel",)),
        name="inproj",
    )(x2, norm_g, cos_t, sin_t, wqk, wvt, wlqk, wlvr, wg, wlow, gup, gbias)


def _rope_tables(seq_len):
    d = DA_HEAD_DIM
    inv = ROPE_THETA ** (-jnp.arange(0, d, 2, dtype=F32) / d)
    ang = jnp.arange(seq_len, dtype=F32)[:, None] * inv[None, :]
    cos = jnp.concatenate([jnp.cos(ang)] * (2 * LANES // d), -1)
    sin = jnp.sin(ang)
    sin_signed = jnp.concatenate([-sin, sin] * (LANES // d), -1)
    return cos, sin_signed


def _attn_kernel(lamv_ref, q_ref, k_ref, vt_ref, o_ref, acc_sc, sa_sc, sb_sc, *, lam_init):
    n_vt, _, vt_w = vt_ref.shape
    tk = ATTN_KEY_CHUNK
    nk = n_vt * vt_w // tk
    tq = q_ref.shape[1]
    q = q_ref[0]
    lane = lax.broadcasted_iota(I32, (1, LANES), 1)
    zeros = jnp.zeros_like(q)
    qms = (jnp.where(lane < DA_HEAD_DIM, q, zeros), jnp.where(lane >= DA_HEAD_DIM, q, zeros))
    acc_sc[...] = jnp.zeros(acc_sc.shape, F32)

    def scores(j, dst_ref):
        kj = k_ref[0, j * tk:(j + 1) * tk, :]
        for mp in range(2):
            dst_ref[mp] = lax.dot_general(kj, qms[mp], NT_DIMS, preferred_element_type=F32)

    def softmax_pv(j, src_ref, ms):
        def v_tile(r):
            key = j * tk + r
            return vt_ref[key // vt_w, :, key % vt_w:key % vt_w + MXU_TILE]

        new = []
        for mp in range(2):
            m_new = jnp.maximum(ms[mp], jnp.max(src_ref[mp], axis=0, keepdims=True))
            alpha = jnp.exp2(ms[mp] - m_new)
            for cols in (slice(c, c + MXU_TILE) for c in range(0, tq, MXU_TILE)):
                part = None
                for r in range(0, tk, MXU_TILE):
                    p = jnp.exp2((src_ref[mp, r:r + MXU_TILE, cols] - m_new[:, cols]).astype(BF16))
                    d = jnp.dot(v_tile(r), p, preferred_element_type=F32)
                    part = d if part is None else part + d
                acc_sc[mp, :, cols] = acc_sc[mp, :, cols] * alpha[:, cols] + part
            new.append(m_new)
        return tuple(new)

    bufs = (sa_sc, sb_sc)
    scores(0, bufs[0])
    ms = tuple(jnp.full((1, tq), -jnp.inf, F32) for _ in range(2))
    for j in range(nk):
        if j + 1 < nk:
            scores(j + 1, bufs[(j + 1) % 2])
        ms = softmax_pv(j, bufs[j % 2], ms)

    lv = lamv_ref[...]
    lam = (jnp.exp(jnp.sum(lv[0:1] * lv[1:2], axis=-1, keepdims=True))
           - jnp.exp(jnp.sum(lv[2:3] * lv[3:4], axis=-1, keepdims=True)) + lam_init)
    a0, a1 = acc_sc[0], acc_sc[1]
    o_ref[0] = (a0[:DA_VDIM] / a0[DA_VDIM:DA_VDIM + 1] - lam * (a1[:DA_VDIM] / a1[DA_VDIM:DA_VDIM + 1])).T


def _diff_attention(qk3, vt3, lamv, lam_init, tq=512):
    B, L, _ = qk3.shape
    vt_w = vt3.shape[-1]
    tk = ATTN_KEY_CHUNK
    assert L % tk == 0 and tk % vt_w == 0
    return pl.pallas_call(
        functools.partial(_attn_kernel, lam_init=lam_init),
        grid=(B, DA_HEADS, L // tq),
        in_specs=[_full(lamv.shape),
                  pl.BlockSpec((1, tq, LANES), lambda b, h, i: (b, i, h)),
                  pl.BlockSpec((1, L, LANES), lambda b, h, i: (b, 0, DA_HEADS + h)),
                  pl.BlockSpec((L // vt_w, VT_ROWS, vt_w), lambda b, h, i: (b, h, 0))],
        out_specs=pl.BlockSpec((1, tq, DA_VDIM), lambda b, h, i: (b, i, h)),
        out_shape=jax.ShapeDtypeStruct((B, L, DA_WIDTH), F32),
        scratch_shapes=[pltpu.VMEM((2, VT_ROWS, tq), F32), pltpu.VMEM((2, tk, tq), F32),
                        pltpu.VMEM((2, tk, tq), F32)],
        compiler_params=_cparams(("parallel", "parallel", "arbitrary")),
        name="diff_attention",
    )(lamv, qk3, qk3, vt3)


def _split3_bf16(x):
    hi = x.astype(BF16)
    r1 = x - hi.astype(F32)
    mid = r1.astype(BF16)
    return hi, mid, (r1 - mid.astype(F32)).astype(BF16)


def _gla_local(qk_ref, vr_ref, g_ref, *, reverse):
    C = GLA_CHUNK
    tb = qk_ref.shape[1]
    nchunk = tb // C
    row = lax.broadcasted_iota(I32, (tb, tb), 0)
    col = lax.broadcasted_iota(I32, (tb, tb), 1)
    same_chunk = (row // C) == (col // C)
    tri = same_chunk & ((col >= row) if reverse else (col <= row))
    tri_b = jnp.where(tri, 1.0, 0.0).astype(BF16)
    lane = lax.broadcasted_iota(I32, (1, LANES), 1)
    goff = GLA_KEY_WIDTH if reverse else 0

    g = g_ref[0, :, goff:goff + GLA_KEY_WIDTH]
    q = qk_ref[0, :, 0:GLA_KEY_WIDTH] * (GLA_KDIM ** -0.5)
    k = qk_ref[0, :, GLA_KEY_WIDTH:2 * GLA_KEY_WIDTH]
    v = vr_ref[0]
    b = sum(jnp.dot(tri_b, part, preferred_element_type=F32) for part in _split3_bf16(g))
    last_row, ref_row = (0, C // 2) if reverse else (C - 1, C // 2 - 1)
    per_chunk = lambda r: jnp.concatenate(
        [jnp.broadcast_to(b[c * C + r:c * C + r + 1], (C, GLA_KEY_WIDTH)) for c in range(nchunk)], axis=0)
    b_last, b_ref = per_chunk(last_row), per_chunk(ref_row)
    q_in = q * jnp.exp(b - b_ref)
    k_in = (k * jnp.exp(b_ref - b)).astype(BF16)
    k_dec = k * jnp.exp(b_last - b)
    q_dec = q * jnp.exp(b)
    dec = [jnp.exp(b[c * C + last_row:c * C + last_row + 1]) for c in range(nchunk)]

    heads = []
    for h in range(GLA_HEADS):
        ps = slice((h // 2) * LANES, (h // 2 + 1) * LANES)
        keep = (lane < GLA_KDIM) if h % 2 == 0 else (lane >= GLA_KDIM)
        zero = jnp.zeros((tb, LANES), F32)
        heads.append(dict(
            ps=ps, vs=slice(h * GLA_VDIM, (h + 1) * GLA_VDIM),
            q_in=jnp.where(keep, q_in[:, ps], zero).astype(BF16),
            q_dec=jnp.where(keep, q_dec[:, ps], zero).astype(BF16),
            k_dec=jnp.where(keep, k_dec[:, ps], zero).astype(BF16)))
    scores = [lax.dot_general(hd["q_in"], k_in[:, hd["ps"]], NT_DIMS, preferred_element_type=F32) for hd in heads]
    kv_t = [[lax.dot_general(v[c * C:(c + 1) * C, hd["vs"]], hd["k_dec"][c * C:(c + 1) * C], TN_DIMS,
                             preferred_element_type=F32) for hd in heads] for c in range(nchunk)]
    intra = [jnp.dot(jnp.where(tri, s, 0.0).astype(BF16), v[:, hd["vs"]], preferred_element_type=F32)
             for s, hd in zip(scores, heads)]
    return heads, dec, kv_t, intra


def _gla_scan(heads, dec, kv_t, st_ref, *, reverse):
    nchunk = len(dec)
    state = [st_ref[h] for h in range(GLA_HEADS)]
    entering = [None] * nchunk
    for c in (range(nchunk - 1, -1, -1) if reverse else range(nchunk)):
        entering[c] = [s.astype(BF16) for s in state]
        state = [s * dec[c][:, hd["ps"]] + kv_t[c][h] for h, (s, hd) in enumerate(zip(state, heads))]
    for h in range(GLA_HEADS):
        st_ref[h] = state[h]
    return entering


def _gla_emit(heads, intra, entering, o_ref):
    C = GLA_CHUNK
    for c, states in enumerate(entering):
        rows = slice(c * C, (c + 1) * C)
        for hd, o_intra, s_t in zip(heads, intra, states):
            inter = lax.dot_general(hd["q_dec"][rows], s_t, NT_DIMS, preferred_element_type=F32)
            o_ref[0, rows, hd["vs"]] = o_intra[rows] + inter


def _gla_kernel(qkf_ref, vrf_ref, gf_ref, qkb_ref, vrb_ref, gb_ref, of_ref, ob_ref, sf_sc, sb_sc):
    @pl.when(pl.program_id(1) == 0)
    def _():
        sf_sc[...] = jnp.zeros(sf_sc.shape, F32)
        sb_sc[...] = jnp.zeros(sb_sc.shape, F32)

    hf, df, kf, inf = _gla_local(qkf_ref, vrf_ref, gf_ref, reverse=False)
    hb, db, kb, inb = _gla_local(qkb_ref, vrb_ref, gb_ref, reverse=True)
    ef = _gla_scan(hf, df, kf, sf_sc, reverse=False)
    eb = _gla_scan(hb, db, kb, sb_sc, reverse=True)
    _gla_emit(hf, inf, ef, of_ref)
    _gla_emit(hb, inb, eb, ob_ref)


def _gla(lqk3, lvr3, gfb3, tb=256):
    B, L, _ = lqk3.shape
    nb = L // tb
    fwd = lambda n: pl.BlockSpec((1, tb, n), lambda b, i: (b, i, 0))
    bwd = lambda n: pl.BlockSpec((1, tb, n), lambda b, i: (b, nb - 1 - i, 0))
    out = jax.ShapeDtypeStruct((B, L, GLA_WIDTH), F32)
    return pl.pallas_call(
        _gla_kernel,
        grid=(B, nb),
        in_specs=[fwd(2 * GLA_KEY_WIDTH), fwd(GLA_WIDTH), fwd(2 * GLA_KEY_WIDTH),
                  bwd(2 * GLA_KEY_WIDTH), bwd(GLA_WIDTH), bwd(2 * GLA_KEY_WIDTH)],
        out_specs=(fwd(GLA_WIDTH), bwd(GLA_WIDTH)),
        out_shape=(out, out),
        scratch_shapes=[pltpu.VMEM((GLA_HEADS, GLA_VDIM, LANES), F32),
                        pltpu.VMEM((GLA_HEADS, GLA_VDIM, LANES), F32)],
        compiler_params=_cparams(("parallel", "arbitrary")),
        name="gla",
    )(lqk3, lvr3, gfb3, lqk3, lvr3, gfb3)


def _head_norm(z, g):
    return jnp.concatenate(
        [_rms(z[:, h * LANES:(h + 1) * LANES], g) for h in range(z.shape[1] // LANES)], axis=1)


def _merge_kernel(x_ref, oa_ref, of_ref, ob_ref, r_ref, sg_ref, subg_ref, glag_ref, wpa_ref, wpb_ref,
                  wout_ref, nffn_ref, rwt_ref, rb_ref,
                  x1_ref, h2_ref, ti_ref, rk_ref, tg_ref, cnt_ref, carry_sc, *, lam_init):
    tm = x_ref.shape[0]

    @pl.when(pl.program_id(0) == 0)
    def _():
        carry_sc[...] = jnp.zeros(carry_sc.shape, F32)

    oa = (_head_norm(oa_ref[...], subg_ref[...]) * (1.0 - lam_init)).astype(BF16)
    ob = _head_norm(of_ref[...] + ob_ref[...], glag_ref[...]) * jax.nn.silu(r_ref[...].astype(F32))
    pa = jnp.dot(oa, wpa_ref[...], preferred_element_type=F32)
    pb = jnp.dot(ob.astype(BF16), wpb_ref[...], preferred_element_type=F32)
    sg = sg_ref[...].astype(F32)
    merged = sg[:, :D_MODEL] * pa + sg[:, D_MODEL:] * pb
    x1 = x_ref[...] + jnp.dot(merged.astype(BF16), wout_ref[...], preferred_element_type=F32)
    x1_ref[...] = x1
    h2 = _rms(x1, nffn_ref[...])
    _store_vreg_tiles(h2_ref, h2)

    logits = lax.dot_general(rwt_ref[...], h2.astype(BF16), NT_DIMS, preferred_element_type=F32) + rb_ref[...]
    eidx = lax.broadcasted_iota(I32, logits.shape, 0)
    vals, sels = [], []
    for k in range(TOP_K):
        mk = jnp.max(logits, axis=0, keepdims=True)
        ik = jnp.min(jnp.where(logits == mk, eidx, N_EXPERTS), axis=0, keepdims=True)
        sel = eidx == ik
        logits = jnp.where(sel, -jnp.inf, logits)
        vals.append(mk)
        sels.append(sel)
        ti_ref[k:k + 1, :] = ik
    ex = [jnp.exp(v - vals[0]) for v in vals]
    denom = ex[0] + ex[1] + ex[2] + ex[3]
    for k in range(TOP_K):
        tg_ref[k:k + 1, :] = ex[k] / denom
    tg_ref[TOP_K:, :] = jnp.zeros((tg_ref.shape[0] - TOP_K, tm), F32)

    multi = (sels[0] | sels[1] | sels[2] | sels[3])
    multi_f = jnp.where(multi, 1.0, 0.0)
    before = lax.broadcasted_iota(I32, (tm, tm), 0) < lax.broadcasted_iota(I32, (tm, tm), 1)
    cum = jnp.dot(multi_f.astype(BF16), jnp.where(before, 1.0, 0.0).astype(BF16), preferred_element_type=F32)
    tot = carry_sc[:, 0:1] + cum
    for k in range(TOP_K):
        rk_ref[k:k + 1, :] = jnp.sum(jnp.where(sels[k], tot, 0.0), axis=0, keepdims=True).astype(I32)
    carry_sc[...] = carry_sc[...] + jnp.sum(multi_f, axis=1, keepdims=True)
    cnt_ref[...] = carry_sc[...]


def _merge(x2, oa, of, ob, lvr, sg, subg, glag, wpa, wpb, wout, nffn, rwt, rb, lam_init):
    T, D = x2.shape
    tm = TOKEN_TILE
    row = lambda n: pl.BlockSpec((tm, n), lambda i: (i, 0))
    col = lambda n: pl.BlockSpec((n, tm), lambda i: (0, i))
    out_shape = (
        jax.ShapeDtypeStruct((T, D), F32),
        jax.ShapeDtypeStruct(_vreg_tile_shape(T, D), F32),
        jax.ShapeDtypeStruct((TOP_K, T), I32),
        jax.ShapeDtypeStruct((TOP_K, T), I32),
        jax.ShapeDtypeStruct((SUBLANES, T), F32),
        jax.ShapeDtypeStruct((N_EXPERTS, LANES), F32),
    )
    return pl.pallas_call(
        functools.partial(_merge_kernel, lam_init=lam_init),
        grid=(T // tm,),
        in_specs=[row(D), row(DA_WIDTH), row(GLA_WIDTH), row(GLA_WIDTH),
                  pl.BlockSpec((tm, GLA_WIDTH), lambda i: (i, 1)), row(2 * D),
                  _full(subg.shape), _full(glag.shape), _full(wpa.shape), _full(wpb.shape),
                  _full(wout.shape), _full(nffn.shape), _full(rwt.shape), _full(rb.shape)],
        out_specs=(row(D), pl.BlockSpec(_vreg_tile_shape(tm, D), lambda i: (i, 0, 0, 0)),
                   col(TOP_K), col(TOP_K), col(SUBLANES), _full((N_EXPERTS, LANES))),
        out_shape=out_shape,
        scratch_shapes=[pltpu.VMEM((N_EXPERTS, LANES), F32)],
        compiler_params=_cparams(("arbitrary",)),
        name="merge_router",
    )(x2, oa, of, ob, lvr, sg, subg, glag, wpa, wpb, wout, nffn, rwt, rb)


def _dispatch_kernel(pend_ref, dest_ref, h_ref, xs_ref, zero_sc, sem):
    tm = h_ref.shape[0] * SUBLANES
    rows = zero_sc.shape[0]

    @pl.when(pl.program_id(0) == 0)
    def _():
        zero_sc[...] = jnp.zeros(zero_sc.shape, F32)
        for e in range(N_EXPERTS):
            end = pend_ref[e]
            begin = pend_ref[e - 1] if e else 0

            @pl.when(end > begin)
            def _():
                last = pl.multiple_of(end - rows, rows)
                cp = pltpu.make_async_copy(zero_sc, xs_ref.at[pl.ds(last, rows)], sem)
                cp.start()
                cp.wait()

        def zero_unused(blk, c):
            cp = pltpu.make_async_copy(zero_sc, xs_ref.at[pl.ds(pl.multiple_of(blk * rows, rows), rows)], sem)
            cp.start()
            cp.wait()
            return c

        lax.fori_loop(pend_ref[N_EXPERTS - 1] // rows, xs_ref.shape[0] // rows, zero_unused, 0)

    def row_copy(g, s, k):
        return pltpu.make_async_copy(h_ref.at[g, :, s, :], xs_ref.at[dest_ref[k, g * SUBLANES + s]], sem)

    def start(g, c):
        for s in range(SUBLANES):
            for k in range(TOP_K):
                row_copy(g, s, k).start(priority=k % 2)
        return c

    def wait(g, c):
        for s in range(SUBLANES):
            for k in range(TOP_K):
                row_copy(g, s, k).wait()
        return c

    lax.fori_loop(0, tm // SUBLANES, start, 0)
    lax.fori_loop(0, tm // SUBLANES, wait, 0)


def _dispatch(h2, dest, pad_end, n_slots):
    T = h2.shape[0] * SUBLANES
    tm = TOKEN_TILE
    h_block = (tm // SUBLANES,) + h2.shape[1:]
    return pl.pallas_call(
        _dispatch_kernel,
        grid_spec=pltpu.PrefetchScalarGridSpec(
            num_scalar_prefetch=1,
            grid=(T // tm,),
            in_specs=[pl.BlockSpec((TOP_K, tm), lambda i, pe: (0, i), memory_space=pltpu.SMEM),
                      pl.BlockSpec(h_block, lambda i, pe: (i, 0, 0, 0))],
            out_specs=pl.BlockSpec(memory_space=pl.ANY),
            scratch_shapes=[pltpu.VMEM((EXPERT_ROWS,) + ROW_TILE, F32), pltpu.SemaphoreType.DMA(())],
        ),
        out_shape=jax.ShapeDtypeStruct((n_slots,) + ROW_TILE, F32),
        compiler_params=_cparams(("arbitrary",)),
        name="moe_dispatch",
    )(pad_end, dest, h2)


def _deinterleave_kernel(w_ref, o_ref):
    n = 2 * LANES
    src = lax.broadcasted_iota(I32, (n, n), 0)
    dst = lax.broadcasted_iota(I32, (n, n), 1)
    perm = jnp.where(src == jnp.where(dst < LANES, 2 * dst, 2 * (dst - LANES) + 1), 1.0, 0.0).astype(BF16)
    for c in range(w_ref.shape[2] // n):
        w = w_ref[0, :, c * n:(c + 1) * n].astype(BF16)
        o_ref[0, :, c * n:(c + 1) * n] = jnp.dot(w, perm, preferred_element_type=F32).astype(BF16)


def _deinterleave_w1(w1, cols=512):
    E, K, N = w1.shape
    spec = pl.BlockSpec((1, K, cols), lambda e, j: (e, 0, j))
    return pl.pallas_call(
        _deinterleave_kernel,
        grid=(E, N // cols),
        in_specs=[spec],
        out_specs=spec,
        out_shape=jax.ShapeDtypeStruct(w1.shape, BF16),
        compiler_params=_cparams(("parallel", "parallel")),
        name="moe_w1_layout",
    )(w1)


def _expert_kernel(blk_ref, nused_ref, xs_ref, w1_ref, b1_ref, w2_ref, b2_ref, ys_ref,
                   xbuf, ybuf, in_sem, out_sem):
    i = pl.program_id(0)
    n = pl.num_programs(0)
    bm = xbuf.shape[1]
    slot = i % 2

    def rows_of(ref, blk):
        return ref.at[pl.ds(pl.multiple_of(blk * bm, bm), bm)]

    def loads(blk, s):
        return _relayout_copies(xbuf.at[s], rows_of(xs_ref, blk), in_sem.at[s], to_tiled=False)

    def stores(blk, s):
        return _relayout_copies(ybuf.at[s], rows_of(ys_ref, blk), out_sem.at[s], to_tiled=True)

    @pl.when(i == 0)
    def _():
        for cp in loads(0, 0):
            cp.start()

    @pl.when(i + 1 < n)
    def _():
        for cp in loads(i + 1, 1 - slot):
            cp.start()

    for cp in loads(i, slot):
        cp.wait()

    @pl.when(i >= 2)
    def _():
        for cp in stores(i - 2, slot):
            cp.wait()

    @pl.when(i < nused_ref[0])
    def _():
        x = xbuf[slot].astype(BF16)
        step = 2 * MXU_TILE

        def hidden(c):
            cols = slice(c * step, (c + 1) * step)
            return jnp.dot(x, w1_ref[0, :, cols], preferred_element_type=F32) + b1_ref[0, :, cols]

        def down(c, hid):
            acts = []
            for o in range(0, step, 2 * LANES):
                gate = jnp.minimum(hid[:, o:o + LANES], SWIGLU_LIMIT)
                up = jnp.clip(hid[:, o + LANES:o + 2 * LANES], -SWIGLU_LIMIT, SWIGLU_LIMIT)
                acts.append(((up + 1.0) * (gate * jax.nn.sigmoid(SWIGLU_ALPHA * gate))).astype(BF16))
            rows = slice(c * MXU_TILE, (c + 1) * MXU_TILE)
            return jnp.dot(jnp.concatenate(acts, axis=1), w2_ref[0, rows, :], preferred_element_type=F32)

        n_piece = w1_ref.shape[2] // step
        y = b2_ref[0]
        hid = hidden(0)
        for c in range(n_piece):
            nxt = hidden(c + 1) if c + 1 < n_piece else None
            y = y + down(c, hid)
            hid = nxt
        ybuf[slot] = y

    @pl.when(i >= nused_ref[0])
    def _():
        ybuf[slot] = jnp.zeros(ybuf.shape[1:], F32)

    for cp in stores(i, slot):
        cp.start()

    @pl.when(i == n - 1)
    def _():
        for cp in stores(i, slot):
            cp.wait()

    @pl.when((i == n - 1) & (n >= 2))
    def _():
        for cp in stores(i - 1, 1 - slot):
            cp.wait()


def _experts(xs, blk_e, nused, w1, b1, w2, b2):
    P = xs.shape[0]
    D = w1.shape[1]
    bm = EXPERT_ROWS
    nb = P // bm
    per_expert = lambda a: pl.BlockSpec((1,) + a.shape[1:], lambda i, blk, nu: (blk[i], 0, 0))
    return pl.pallas_call(
        _expert_kernel,
        grid_spec=pltpu.PrefetchScalarGridSpec(
            num_scalar_prefetch=2,
            grid=(nb,),
            in_specs=[pl.BlockSpec(memory_space=pl.ANY), per_expert(w1), per_expert(b1), per_expert(w2),
                      per_expert(b2)],
            out_specs=pl.BlockSpec(memory_space=pl.ANY),
            scratch_shapes=[pltpu.VMEM((2, bm, D), F32), pltpu.VMEM((2, bm, D), F32),
                            pltpu.SemaphoreType.DMA((2,)), pltpu.SemaphoreType.DMA((2,))],
        ),
        out_shape=jax.ShapeDtypeStruct((P,) + ROW_TILE, F32),
        compiler_params=_cparams(("arbitrary",)),
        name="moe_experts",
    )(blk_e, nused, xs, w1, b1, w2, b2)


def _combine_kernel(dest_ref, ys_ref, tg_ref, x1_ref, gfin_ref, y_ref, buf, sem, *, final):
    tc = x1_ref.shape[0]

    def row_copy(g, s, k):
        return pltpu.make_async_copy(ys_ref.at[dest_ref[k, g * SUBLANES + s]], buf.at[k, g, :, s, :], sem)

    def start(g, c):
        for s in range(SUBLANES):
            for k in range(TOP_K):
                row_copy(g, s, k).start(priority=k % 2)
        return c

    def wait(g, c):
        for s in range(SUBLANES):
            for k in range(TOP_K):
                row_copy(g, s, k).wait()
        return c

    lax.fori_loop(0, tc // SUBLANES, start, 0)
    lax.fori_loop(0, tc // SUBLANES, wait, 0)
    gates = tg_ref[...].T
    x = x1_ref[...]
    for k in range(TOP_K):
        x = x + _load_vreg_tiles(buf.at[k]) * gates[:, k:k + 1]
    y_ref[...] = _rms(x, gfin_ref[...]) if final else x


def _combine(ys, dest, tg, x1, gfin, final, tc=256):
    T, D = x1.shape
    return pl.pallas_call(
        functools.partial(_combine_kernel, final=final),
        grid=(T // tc,),
        in_specs=[pl.BlockSpec((TOP_K, tc), lambda i: (0, i), memory_space=pltpu.SMEM),
                  pl.BlockSpec(memory_space=pl.ANY),
                  pl.BlockSpec((SUBLANES, tc), lambda i: (0, i)),
                  pl.BlockSpec((tc, D), lambda i: (i, 0)),
                  _full(gfin.shape)],
        out_specs=pl.BlockSpec((tc, D), lambda i: (i, 0)),
        out_shape=jax.ShapeDtypeStruct((T, D), F32),
        scratch_shapes=[pltpu.VMEM((TOP_K,) + _vreg_tile_shape(tc, D), F32), pltpu.SemaphoreType.DMA(())],
        compiler_params=_cparams(("parallel",)),
        name="moe_combine",
    )(dest, ys, tg, x1, gfin)


def _moe(h2, ti, rk, tg, cnt, x1, moe_w, gfin, final):
    T = x1.shape[0]
    bm = EXPERT_ROWS
    n_assign = T * TOP_K
    nb = -(-(n_assign + N_EXPERTS * (bm - 1)) // bm)
    counts = cnt[:, 0].astype(I32)
    padded = (counts + bm - 1) // bm * bm
    pad_end = jnp.cumsum(padded).astype(I32)
    pad_start = pad_end - padded
    experts = jnp.arange(N_EXPERTS, dtype=I32)[:, None, None]
    dest = rk + jnp.sum(jnp.where(ti[None] == experts, pad_start[:, None, None], 0), axis=0)
    nused = (pad_end[-1:] // bm).astype(I32)
    blk_start = jnp.minimum(jnp.arange(nb, dtype=I32), nused[0] - 1) * bm
    blk_e = jnp.minimum(jnp.sum((pad_end[None, :] <= blk_start[:, None]).astype(I32), axis=1), N_EXPERTS - 1)
    xs = _dispatch(h2, dest, pad_end, nb * bm)
    ys = _experts(xs, blk_e, nused, *moe_w)
    return _combine(ys, dest, tg, x1, gfin, final)


def _split_in_proj(w):
    sizes = (DA_QK_WIDTH, DA_QK_WIDTH, DA_WIDTH, GLA_KEY_WIDTH, GLA_KEY_WIDTH, GLA_WIDTH, GLA_WIDTH,
             GLA_GATE_RANK, GLA_GATE_RANK, D_MODEL, D_MODEL)
    assert w.shape[-1] == sum(sizes)
    out, o = [], 0
    for s in sizes:
        out.append(w[:, o:o + s])
        o += s
    return out


def _prep_layer(l, norm_mix_g, w_in, lambda_q1, lambda_k1, lambda_q2, lambda_k2, diff_subln_g,
                gla_gate_up_f, gla_gate_bias_f, gla_gate_up_b, gla_gate_bias_b, gla_norm_g,
                w_proj_a, w_proj_b, w_out, norm_ffn_g, router_w, router_b, moe_w1, moe_b1, moe_w2, moe_b2):
    dq, dk, dv, lq, lk, lv, lr, laf, lab, ga, gb = _split_in_proj(w_in[l])
    bf = lambda a: a.astype(BF16)
    r = GLA_GATE_RANK
    gup = jnp.zeros((LANES, 2 * GLA_KEY_WIDTH), F32)
    gup = gup.at[0:r, 0:GLA_KEY_WIDTH].set(gla_gate_up_f[l]).at[r:2 * r, GLA_KEY_WIDTH:].set(gla_gate_up_b[l])
    lamv = jnp.zeros((SUBLANES, LANES), F32)
    for i, v in enumerate((lambda_q1, lambda_k1, lambda_q2, lambda_k2)):
        lamv = lamv.at[i, 0:DA_HEAD_DIM].set(v[l].astype(F32))
    inproj = dict(
        norm_g=norm_mix_g[l][None, :],
        wqk=bf(jnp.concatenate([dq * (DA_HEAD_DIM ** -0.5), dk], 1)),
        wvt=bf(dv.T),
        wlqk=bf(jnp.concatenate([lq, lk], 1)),
        wlvr=bf(jnp.concatenate([lv, lr], 1)),
        wg=bf(jnp.concatenate([ga, gb], 1)),
        wlow=bf(jnp.concatenate([laf, lab, jnp.zeros((D_MODEL, LANES - 2 * r), F32)], 1)),
        gup=bf(gup),
        gbias=jnp.concatenate([gla_gate_bias_f[l], gla_gate_bias_b[l]])[None, :],
    )
    merge = dict(
        subg=diff_subln_g[l][None, :], glag=gla_norm_g[l][None, :],
        wpa=bf(w_proj_a[l]), wpb=bf(w_proj_b[l]), wout=bf(w_out[l]),
        nffn=norm_ffn_g[l][None, :], rwt=bf(router_w[l].T), rb=router_b[l][:, None],
    )
    b1 = moe_b1[l].reshape(N_EXPERTS, -1, LANES, 2).transpose(0, 1, 3, 2).reshape(N_EXPERTS, 1, -1)
    moe_w = (_deinterleave_w1(moe_w1[l]), b1, bf(moe_w2[l]), moe_b2[l][:, None, :])
    return inproj, lamv, merge, moe_w


def _trunk(x, layers, norm_final_g):
    B, L, D = x.shape
    T = B * L
    cos_t, sin_t = _rope_tables(L)
    x2 = x.reshape(T, D)
    gfin = norm_final_g[None, :]
    for l, (inproj, lamv, merge, moe_w) in enumerate(layers):
        lam_init = 0.8 - 0.6 * math.exp(-0.3 * l)
        qk, vt, lqk, lvr, gfb, sg = _inproj(x2, L, cos_t=cos_t, sin_t=sin_t, **inproj)
        oa = _diff_attention(qk.reshape(B, L, -1), vt, lamv, lam_init)
        of, ob = _gla(lqk.reshape(B, L, -1), lvr.reshape(B, L, -1), gfb.reshape(B, L, -1))
        x1, h2, ti, rk, tg, cnt = _merge(x2, oa.reshape(T, -1), of.reshape(T, -1), ob.reshape(T, -1),
                                         lvr, sg, lam_init=lam_init, **merge)
        x2 = _moe(h2, ti, rk, tg, cnt, x1, moe_w, gfin, final=(l == len(layers) - 1))
    return x2.reshape(B, L, D)


def kernel(x_prompt, x_sample, norm_mix_g, w_in, lambda_q1, lambda_k1, lambda_q2, lambda_k2, diff_subln_g,
           gla_gate_up_f, gla_gate_bias_f, gla_gate_up_b, gla_gate_bias_b, gla_norm_g, w_proj_a, w_proj_b,
           w_out, norm_ffn_g, router_w, router_b, moe_w1, moe_b1, moe_w2, moe_b2, norm_final_g):
    depth = w_in.shape[0]
    layers = [_prep_layer(l, norm_mix_g, w_in, lambda_q1, lambda_k1, lambda_q2, lambda_k2, diff_subln_g,
                          gla_gate_up_f, gla_gate_bias_f, gla_gate_up_b, gla_gate_bias_b, gla_norm_g,
                          w_proj_a, w_proj_b, w_out, norm_ffn_g, router_w, router_b,
                          moe_w1, moe_b1, moe_w2, moe_b2) for l in range(depth)]
    return _trunk(x_prompt, layers, norm_final_g), _trunk(x_sample, layers, norm_final_g)
```

```python
import functools
import math

import jax
import jax.numpy as jnp
from jax import lax
from jax.experimental import pallas as pl
from jax.experimental.pallas import tpu as pltpu
from jax.experimental.pallas import tpu_sc as plsc

F32 = jnp.float32
BF16 = jnp.bfloat16
I32 = jnp.int32

D_MODEL = 1024
DA_HEADS = 4
DA_HEAD_DIM = 64
DA_VDIM = 128
DA_QK_WIDTH = DA_HEADS * 2 * DA_HEAD_DIM
DA_WIDTH = DA_HEADS * DA_VDIM
ROPE_THETA = 10000.0
GLA_HEADS = 4
GLA_KDIM = 64
GLA_VDIM = 128
GLA_KEY_WIDTH = GLA_HEADS * GLA_KDIM
GLA_WIDTH = GLA_HEADS * GLA_VDIM
GLA_GATE_RANK = 16
GLA_GATE_NORMALIZER = 16.0
GLA_CHUNK = 64
N_EXPERTS = 32
TOP_K = 4
D_FF = D_MODEL
SWIGLU_ALPHA = 1.702
SWIGLU_LIMIT = 7.0
NORM_EPS = 1e-5
LANES = 128
SUBLANES = 8

BF16_SUBLANES = 16
MXU_TILE = 256
LOG2_E = 1.4426950408889634
VT_ROWS = DA_VDIM + BF16_SUBLANES

VMEM_LIMIT = 56 * 1024 * 1024

TOKEN_TILE = 512
ATTN_KEY_CHUNK = 512
EXPERT_ROWS = 512

NT_DIMS = (((1,), (1,)), ((), ()))
TN_DIMS = (((0,), (0,)), ((), ()))


def _cparams(sem):
    return pltpu.CompilerParams(dimension_semantics=sem, vmem_limit_bytes=VMEM_LIMIT)


def _full(shape):
    nd = len(shape)
    return pl.BlockSpec(shape, lambda *_: (0,) * nd)


def _rms(x, g):
    return x * lax.rsqrt(jnp.mean(x * x, axis=-1, keepdims=True) + NORM_EPS) * g


ROW_TILE = (SUBLANES, LANES)


def _vreg_tile_shape(rows, cols):
    return (rows // SUBLANES, cols // LANES, SUBLANES, LANES)


def _vreg_tile_row(ref, r):
    return ref.at[r // SUBLANES, :, r % SUBLANES, :]


def _store_vreg_tiles(ref, x):
    for b in range(ref.shape[1]):
        ref[:, b, :, :] = x[:, b * LANES:(b + 1) * LANES].reshape(ref.shape[0], SUBLANES, LANES)


def _load_vreg_tiles(ref):
    rows = ref.shape[0] * SUBLANES
    return jnp.concatenate([ref[:, b, :, :].reshape(rows, LANES) for b in range(ref.shape[1])], axis=1)


def _relayout_copies(flat_ref, tiled_ref, sem, to_tiled):
    cps = []
    for c in range(SUBLANES):
        flat = flat_ref.at[:, pl.ds(c * LANES, LANES)]
        tiled = tiled_ref.at[:, c, :]
        cps.append(pltpu.make_async_copy(flat, tiled, sem) if to_tiled else pltpu.make_async_copy(tiled, flat, sem))
    return cps


def _log_sigmoid(x):
    return jnp.minimum(x, 0.0) - jnp.log1p(jnp.exp(-jnp.abs(x)))


def _inproj_kernel(x_ref, g_ref, cos_ref, sin_ref, wqk_ref, wvt_ref, wlqk_ref, wlvr_ref,
                   wg_ref, wlow_ref, gup_ref, gbias_ref,
                   qk_ref, vt_ref, lqk_ref, lvr_ref, gfb_ref, sg_ref):
    h = _rms(x_ref[...], g_ref[...]).astype(BF16)

    cos = cos_ref[...]
    sin_signed = sin_ref[...]
    first_half = (lax.broadcasted_iota(I32, (1, LANES), 1) % DA_HEAD_DIM) < (DA_HEAD_DIM // 2)
    for c in range(2 * DA_QK_WIDTH // LANES):
        z = jnp.dot(h, wqk_ref[:, c * LANES:(c + 1) * LANES], preferred_element_type=F32)
        if c < DA_QK_WIDTH // LANES:
            z = z * LOG2_E
        partner = jnp.where(first_half, pltpu.roll(z, LANES - DA_HEAD_DIM // 2, 1),
                            pltpu.roll(z, DA_HEAD_DIM // 2, 1))
        qk_ref[:, c * LANES:(c + 1) * LANES] = (z * cos + partner * sin_signed).astype(BF16)

    vt = lax.dot_general(wvt_ref[...], h, NT_DIMS, preferred_element_type=F32).astype(BF16)
    for hd in range(DA_HEADS):
        vt_ref[0, hd * VT_ROWS:hd * VT_ROWS + DA_VDIM, :] = vt[hd * DA_VDIM:(hd + 1) * DA_VDIM]
        vt_ref[0, hd * VT_ROWS + DA_VDIM:(hd + 1) * VT_ROWS, :] = jnp.ones((VT_ROWS - DA_VDIM, vt.shape[1]), BF16)
    lqk_ref[...] = jnp.dot(h, wlqk_ref[...], preferred_element_type=F32)
    lvr_ref[...] = jnp.dot(h, wlvr_ref[...], preferred_element_type=F32).astype(BF16)
    sg_ref[...] = jax.nn.sigmoid(jnp.dot(h, wg_ref[...], preferred_element_type=F32)).astype(BF16)
    low = jnp.dot(h, wlow_ref[...], preferred_element_type=F32).astype(BF16)
    pre = jnp.dot(low, gup_ref[...], preferred_element_type=F32) + gbias_ref[...]
    gfb_ref[...] = _log_sigmoid(pre) * (1.0 / GLA_GATE_NORMALIZER)


def _inproj(x2, seq_len, norm_g, wqk, wvt, wlqk, wlvr, wg, wlow, gup, gbias, cos_t, sin_t):
    T, D = x2.shape
    tm = TOKEN_TILE
    assert T % tm == 0 and seq_len % tm == 0
    nl = seq_len // tm
    row = lambda n: pl.BlockSpec((tm, n), lambda i: (i, 0))
    out_shape = (
        jax.ShapeDtypeStruct((T, 2 * DA_QK_WIDTH), BF16),
        jax.ShapeDtypeStruct((T // tm, DA_HEADS * VT_ROWS, tm), BF16),
        jax.ShapeDtypeStruct((T, 2 * GLA_KEY_WIDTH), F32),
        jax.ShapeDtypeStruct((T, 2 * GLA_WIDTH), BF16),
        jax.ShapeDtypeStruct((T, 2 * GLA_KEY_WIDTH), F32),
        jax.ShapeDtypeStruct((T, 2 * D_MODEL), BF16),
    )
    return pl.pallas_call(
        _inproj_kernel,
        grid=(T // tm,),
        in_specs=[row(D), _full(norm_g.shape),
                  pl.BlockSpec((tm, LANES), lambda i: (i % nl, 0)),
                  pl.BlockSpec((tm, LANES), lambda i: (i % nl, 0)),
                  _full(wqk.shape), _full(wvt.shape), _full(wlqk.shape), _full(wlvr.shape),
                  _full(wg.shape), _full(wlow.shape), _full(gup.shape), _full(gbias.shape)],
        out_specs=(row(2 * DA_QK_WIDTH), pl.BlockSpec((1, DA_HEADS * VT_ROWS, tm), lambda i: (i, 0, 0)),
                   row(2 * GLA_KEY_WIDTH), row(2 * GLA_WIDTH), row(2 * GLA_KEY_WIDTH), row(2 * D_MODEL)),
        out_shape=out_shape,
        compiler_params=_cparams(("parallel",)),
        name="inproj",
    )(x2, norm_g, cos_t, sin_t, wqk, wvt, wlqk, wlvr, wg, wlow, gup, gbias)


def _rope_tables(seq_len):
    d = DA_HEAD_DIM
    inv = ROPE_THETA ** (-jnp.arange(0, d, 2, dtype=F32) / d)
    ang = jnp.arange(seq_len, dtype=F32)[:, None] * inv[None, :]
    cos = jnp.concatenate([jnp.cos(ang)] * (2 * LANES // d), -1)
    sin = jnp.sin(ang)
    sin_signed = jnp.concatenate([-sin, sin] * (LANES // d), -1)
    return cos, sin_signed


def _attn_kernel(lamv_ref, q_ref, k_ref, vt_ref, o_ref, acc_sc, sa_sc, sb_sc, *, lam_init):
    n_vt, _, vt_w = vt_ref.shape
    tk = ATTN_KEY_CHUNK
    nk = n_vt * vt_w // tk
    tq = q_ref.shape[1]
    q = q_ref[0]
    lane = lax.broadcasted_iota(I32, (1, LANES), 1)
    zeros = jnp.zeros_like(q)
    qms = (jnp.where(lane < DA_HEAD_DIM, q, zeros), jnp.where(lane >= DA_HEAD_DIM, q, zeros))
    acc_sc[...] = jnp.zeros(acc_sc.shape, F32)

    def scores(j, dst_ref):
        kj = k_ref[0, j * tk:(j + 1) * tk, :]
        col_max = []
        for mp in range(2):
            st = lax.dot_general(kj, qms[mp], NT_DIMS, preferred_element_type=F32)
            dst_ref[mp] = st
            col_max.append(jnp.max(st, axis=0, keepdims=True))
        return col_max

    def softmax_pv(j, src_ref, ms, col_max):
        def v_tile(r):
            key = j * tk + r
            return vt_ref[key // vt_w, :, key % vt_w:key % vt_w + MXU_TILE]

        new = []
        for mp in range(2):
            m_new = jnp.maximum(ms[mp], col_max[mp])
            alpha = jnp.exp2(ms[mp] - m_new)
            for cols in (slice(c, c + MXU_TILE) for c in range(0, tq, MXU_TILE)):
                part = None
                for r in range(0, tk, MXU_TILE):
                    p = jnp.exp2((src_ref[mp, r:r + MXU_TILE, cols] - m_new[:, cols]).astype(BF16))
                    d = jnp.dot(v_tile(r), p, preferred_element_type=F32)
                    part = d if part is None else part + d
                acc_sc[mp, :, cols] = acc_sc[mp, :, cols] * alpha[:, cols] + part
            new.append(m_new)
        return tuple(new)

    bufs = (sa_sc, sb_sc)
    col_max = scores(0, bufs[0])
    ms = tuple(jnp.full((1, tq), -jnp.inf, F32) for _ in range(2))
    for j in range(nk):
        nxt_max = scores(j + 1, bufs[(j + 1) % 2]) if j + 1 < nk else None
        ms = softmax_pv(j, bufs[j % 2], ms, col_max)
        col_max = nxt_max

    lv = lamv_ref[...]
    lam = (jnp.exp(jnp.sum(lv[0:1] * lv[1:2], axis=-1, keepdims=True))
           - jnp.exp(jnp.sum(lv[2:3] * lv[3:4], axis=-1, keepdims=True)) + lam_init)
    a0, a1 = acc_sc[0], acc_sc[1]
    o_ref[0] = (a0[:DA_VDIM] / a0[DA_VDIM:DA_VDIM + 1] - lam * (a1[:DA_VDIM] / a1[DA_VDIM:DA_VDIM + 1])).T


def _diff_attention(qk3, vt3, lamv, lam_init, tq=512):
    B, L, _ = qk3.shape
    vt_w = vt3.shape[-1]
    tk = ATTN_KEY_CHUNK
    assert L % tk == 0 and tk % vt_w == 0
    return pl.pallas_call(
        functools.partial(_attn_kernel, lam_init=lam_init),
        grid=(B, DA_HEADS, L // tq),
        in_specs=[_full(lamv.shape),
                  pl.BlockSpec((1, tq, LANES), lambda b, h, i: (b, i, h)),
                  pl.BlockSpec((1, L, LANES), lambda b, h, i: (b, 0, DA_HEADS + h)),
                  pl.BlockSpec((L // vt_w, VT_ROWS, vt_w), lambda b, h, i: (b, h, 0))],
        out_specs=pl.BlockSpec((1, tq, DA_VDIM), lambda b, h, i: (b, i, h)),
        out_shape=jax.ShapeDtypeStruct((B, L, DA_WIDTH), F32),
        scratch_shapes=[pltpu.VMEM((2, VT_ROWS, tq), F32), pltpu.VMEM((2, tk, tq), F32),
                        pltpu.VMEM((2, tk, tq), F32)],
        compiler_params=_cparams(("parallel", "parallel", "arbitrary")),
        name="diff_attention",
    )(lamv, qk3, qk3, vt3)


def _split3_bf16(x):
    hi = x.astype(BF16)
    r1 = x - hi.astype(F32)
    mid = r1.astype(BF16)
    return hi, mid, (r1 - mid.astype(F32)).astype(BF16)


def _gla_local(qk_ref, vr_ref, g_ref, *, reverse):
    C = GLA_CHUNK
    tb = qk_ref.shape[1]
    nchunk = tb // C
    row = lax.broadcasted_iota(I32, (tb, tb), 0)
    col = lax.broadcasted_iota(I32, (tb, tb), 1)
    same_chunk = (row // C) == (col // C)
    tri = same_chunk & ((col >= row) if reverse else (col <= row))
    tri_b = jnp.where(tri, 1.0, 0.0).astype(BF16)
    lane = lax.broadcasted_iota(I32, (1, LANES), 1)
    goff = GLA_KEY_WIDTH if reverse else 0

    g = g_ref[0, :, goff:goff + GLA_KEY_WIDTH]
    q = qk_ref[0, :, 0:GLA_KEY_WIDTH] * (GLA_KDIM ** -0.5)
    k = qk_ref[0, :, GLA_KEY_WIDTH:2 * GLA_KEY_WIDTH]
    v = vr_ref[0]
    b = sum(jnp.dot(tri_b, part, preferred_element_type=F32) for part in _split3_bf16(g))
    last_row, ref_row = (0, C // 2) if reverse else (C - 1, C // 2 - 1)
    per_chunk = lambda r: jnp.concatenate(
        [jnp.broadcast_to(b[c * C + r:c * C + r + 1], (C, GLA_KEY_WIDTH)) for c in range(nchunk)], axis=0)
    b_last, b_ref = per_chunk(last_row), per_chunk(ref_row)
    q_in = q * jnp.exp(b - b_ref)
    k_in = (k * jnp.exp(b_ref - b)).astype(BF16)
    k_dec = k * jnp.exp(b_last - b)
    q_dec = q * jnp.exp(b)
    dec = [jnp.exp(b[c * C + last_row:c * C + last_row + 1]) for c in range(nchunk)]

    heads = []
    for h in range(GLA_HEADS):
        ps = slice((h // 2) * LANES, (h // 2 + 1) * LANES)
        keep = (lane < GLA_KDIM) if h % 2 == 0 else (lane >= GLA_KDIM)
        zero = jnp.zeros((tb, LANES), F32)
        heads.append(dict(
            ps=ps, vs=slice(h * GLA_VDIM, (h + 1) * GLA_VDIM),
            q_in=jnp.where(keep, q_in[:, ps], zero).astype(BF16),
            q_dec=jnp.where(keep, q_dec[:, ps], zero).astype(BF16),
            k_dec=jnp.where(keep, k_dec[:, ps], zero).astype(BF16)))
    scores = [lax.dot_general(hd["q_in"], k_in[:, hd["ps"]], NT_DIMS, preferred_element_type=F32) for hd in heads]
    kv_t = [[lax.dot_general(v[c * C:(c + 1) * C, hd["vs"]], hd["k_dec"][c * C:(c + 1) * C], TN_DIMS,
                             preferred_element_type=F32) for hd in heads] for c in range(nchunk)]
    intra = [jnp.dot(jnp.where(tri, s, 0.0).astype(BF16), v[:, hd["vs"]], preferred_element_type=F32)
             for s, hd in zip(scores, heads)]
    return heads, dec, kv_t, intra


def _gla_scan(heads, dec, kv_t, st_ref, *, reverse):
    nchunk = len(dec)
    state = [st_ref[h] for h in range(GLA_HEADS)]
    entering = [None] * nchunk
    for c in (range(nchunk - 1, -1, -1) if reverse else range(nchunk)):
        entering[c] = [s.astype(BF16) for s in state]
        state = [s * dec[c][:, hd["ps"]] + kv_t[c][h] for h, (s, hd) in enumerate(zip(state, heads))]
    for h in range(GLA_HEADS):
        st_ref[h] = state[h]
    return entering


def _gla_emit(heads, intra, entering, o_ref):
    C = GLA_CHUNK
    for c, states in enumerate(entering):
        rows = slice(c * C, (c + 1) * C)
        for hd, o_intra, s_t in zip(heads, intra, states):
            inter = lax.dot_general(hd["q_dec"][rows], s_t, NT_DIMS, preferred_element_type=F32)
            o_ref[0, rows, hd["vs"]] = o_intra[rows] + inter


def _gla_kernel(qkf_ref, vrf_ref, gf_ref, qkb_ref, vrb_ref, gb_ref, of_ref, ob_ref, sf_sc, sb_sc):
    @pl.when(pl.program_id(1) == 0)
    def _():
        sf_sc[...] = jnp.zeros(sf_sc.shape, F32)
        sb_sc[...] = jnp.zeros(sb_sc.shape, F32)

    hf, df, kf, inf = _gla_local(qkf_ref, vrf_ref, gf_ref, reverse=False)
    hb, db, kb, inb = _gla_local(qkb_ref, vrb_ref, gb_ref, reverse=True)
    ef = _gla_scan(hf, df, kf, sf_sc, reverse=False)
    eb = _gla_scan(hb, db, kb, sb_sc, reverse=True)
    _gla_emit(hf, inf, ef, of_ref)
    _gla_emit(hb, inb, eb, ob_ref)


def _gla(lqk3, lvr3, gfb3, tb=256):
    B, L, _ = lqk3.shape
    nb = L // tb
    fwd = lambda n: pl.BlockSpec((1, tb, n), lambda b, i: (b, i, 0))
    bwd = lambda n: pl.BlockSpec((1, tb, n), lambda b, i: (b, nb - 1 - i, 0))
    out = jax.ShapeDtypeStruct((B, L, GLA_WIDTH), F32)
    return pl.pallas_call(
        _gla_kernel,
        grid=(B, nb),
        in_specs=[fwd(2 * GLA_KEY_WIDTH), fwd(GLA_WIDTH), fwd(2 * GLA_KEY_WIDTH),
                  bwd(2 * GLA_KEY_WIDTH), bwd(GLA_WIDTH), bwd(2 * GLA_KEY_WIDTH)],
        out_specs=(fwd(GLA_WIDTH), bwd(GLA_WIDTH)),
        out_shape=(out, out),
        scratch_shapes=[pltpu.VMEM((GLA_HEADS, GLA_VDIM, LANES), F32),
                        pltpu.VMEM((GLA_HEADS, GLA_VDIM, LANES), F32)],
        compiler_params=_cparams(("parallel", "arbitrary")),
        name="gla",
    )(lqk3, lvr3, gfb3, lqk3, lvr3, gfb3)


def _head_norm(z, g):
    return jnp.concatenate(
        [_rms(z[:, h * LANES:(h + 1) * LANES], g) for h in range(z.shape[1] // LANES)], axis=1)


def _merge_kernel(x_ref, oa_ref, of_ref, ob_ref, r_ref, sg_ref, subg_ref, glag_ref, wpa_ref, wpb_ref,
                  wout_ref, nffn_ref, rwt_ref, rb_ref,
                  x1_ref, h2_ref, ti_ref, rk_ref, tg_ref, cnt_ref, carry_sc, *, lam_init):
    tm = x_ref.shape[0]

    @pl.when(pl.program_id(0) == 0)
    def _():
        carry_sc[...] = jnp.zeros(carry_sc.shape, F32)

    oa = (_head_norm(oa_ref[...], subg_ref[...]) * (1.0 - lam_init)).astype(BF16)
    ob = _head_norm(of_ref[...] + ob_ref[...], glag_ref[...]) * jax.nn.silu(r_ref[...].astype(F32))
    pa = jnp.dot(oa, wpa_ref[...], preferred_element_type=F32)
    pb = jnp.dot(ob.astype(BF16), wpb_ref[...], preferred_element_type=F32)
    sg = sg_ref[...].astype(F32)
    merged = sg[:, :D_MODEL] * pa + sg[:, D_MODEL:] * pb
    x1 = x_ref[...] + jnp.dot(merged.astype(BF16), wout_ref[...], preferred_element_type=F32)
    x1_ref[...] = x1
    h2 = _rms(x1, nffn_ref[...])
    _store_vreg_tiles(h2_ref, h2)

    logits = lax.dot_general(rwt_ref[...], h2.astype(BF16), NT_DIMS, preferred_element_type=F32) + rb_ref[...]
    eidx = lax.broadcasted_iota(I32, logits.shape, 0)
    vals, sels = [], []
    for k in range(TOP_K):
        mk = jnp.max(logits, axis=0, keepdims=True)
        ik = jnp.min(jnp.where(logits == mk, eidx, N_EXPERTS), axis=0, keepdims=True)
        sel = eidx == ik
        logits = jnp.where(sel, -jnp.inf, logits)
        vals.append(mk)
        sels.append(sel)
        ti_ref[k:k + 1, :] = ik
    ex = [jnp.exp(v - vals[0]) for v in vals]
    denom = ex[0] + ex[1] + ex[2] + ex[3]
    for k in range(TOP_K):
        tg_ref[k:k + 1, :] = ex[k] / denom
    tg_ref[TOP_K:, :] = jnp.zeros((tg_ref.shape[0] - TOP_K, tm), F32)

    multi = (sels[0] | sels[1] | sels[2] | sels[3])
    multi_f = jnp.where(multi, 1.0, 0.0)
    before = lax.broadcasted_iota(I32, (tm, tm), 0) < lax.broadcasted_iota(I32, (tm, tm), 1)
    cum = jnp.dot(multi_f.astype(BF16), jnp.where(before, 1.0, 0.0).astype(BF16), preferred_element_type=F32)
    tot = carry_sc[:, 0:1] + cum
    for k in range(TOP_K):
        rk_ref[k:k + 1, :] = jnp.sum(jnp.where(sels[k], tot, 0.0), axis=0, keepdims=True).astype(I32)
    carry_sc[...] = carry_sc[...] + jnp.sum(multi_f, axis=1, keepdims=True)
    cnt_ref[...] = carry_sc[...]


def _merge(x2, oa, of, ob, lvr, sg, subg, glag, wpa, wpb, wout, nffn, rwt, rb, lam_init):
    T, D = x2.shape
    tm = TOKEN_TILE
    row = lambda n: pl.BlockSpec((tm, n), lambda i: (i, 0))
    col = lambda n: pl.BlockSpec((n, tm), lambda i: (0, i))
    out_shape = (
        jax.ShapeDtypeStruct((T, D), F32),
        jax.ShapeDtypeStruct(_vreg_tile_shape(T, D), F32),
        jax.ShapeDtypeStruct((TOP_K, T), I32),
        jax.ShapeDtypeStruct((TOP_K, T), I32),
        jax.ShapeDtypeStruct((SUBLANES, T), F32),
        jax.ShapeDtypeStruct((N_EXPERTS, LANES), F32),
    )
    return pl.pallas_call(
        functools.partial(_merge_kernel, lam_init=lam_init),
        grid=(T // tm,),
        in_specs=[row(D), row(DA_WIDTH), row(GLA_WIDTH), row(GLA_WIDTH),
                  pl.BlockSpec((tm, GLA_WIDTH), lambda i: (i, 1)), row(2 * D),
                  _full(subg.shape), _full(glag.shape), _full(wpa.shape), _full(wpb.shape),
                  _full(wout.shape), _full(nffn.shape), _full(rwt.shape), _full(rb.shape)],
        out_specs=(row(D), pl.BlockSpec(_vreg_tile_shape(tm, D), lambda i: (i, 0, 0, 0)),
                   col(TOP_K), col(TOP_K), col(SUBLANES), _full((N_EXPERTS, LANES))),
        out_shape=out_shape,
        scratch_shapes=[pltpu.VMEM((N_EXPERTS, LANES), F32)],
        compiler_params=_cparams(("arbitrary",)),
        name="merge_router",
    )(x2, oa, of, ob, lvr, sg, subg, glag, wpa, wpb, wout, nffn, rwt, rb)


def _dispatch_kernel(pend_ref, dest_ref, h_ref, xs_ref, zero_sc, sem):
    tm = h_ref.shape[0] * SUBLANES
    rows = zero_sc.shape[0]

    @pl.when(pl.program_id(0) == 0)
    def _():
        zero_sc[...] = jnp.zeros(zero_sc.shape, F32)
        for e in range(N_EXPERTS):
            end = pend_ref[e]
            begin = pend_ref[e - 1] if e else 0

            @pl.when(end > begin)
            def _():
                last = pl.multiple_of(end - rows, rows)
                cp = pltpu.make_async_copy(zero_sc, xs_ref.at[pl.ds(last, rows)], sem)
                cp.start()
                cp.wait()

        def zero_unused(blk, c):
            cp = pltpu.make_async_copy(zero_sc, xs_ref.at[pl.ds(pl.multiple_of(blk * rows, rows), rows)], sem)
            cp.start()
            cp.wait()
            return c

        lax.fori_loop(pend_ref[N_EXPERTS - 1] // rows, xs_ref.shape[0] // rows, zero_unused, 0)

    def row_copy(g, s, k):
        return pltpu.make_async_copy(h_ref.at[g, :, s, :], xs_ref.at[dest_ref[k, g * SUBLANES + s]], sem)

    def start(g, c):
        for s in range(SUBLANES):
            for k in range(TOP_K):
                row_copy(g, s, k).start(priority=k % 2)
        return c

    def wait(g, c):
        for s in range(SUBLANES):
            for k in range(TOP_K):
                row_copy(g, s, k).wait()
        return c

    lax.fori_loop(0, tm // SUBLANES, start, 0)
    lax.fori_loop(0, tm // SUBLANES, wait, 0)


def _dispatch(h2, dest, pad_end, n_slots):
    T = h2.shape[0] * SUBLANES
    tm = TOKEN_TILE
    h_block = (tm // SUBLANES,) + h2.shape[1:]
    return pl.pallas_call(
        _dispatch_kernel,
        grid_spec=pltpu.PrefetchScalarGridSpec(
            num_scalar_prefetch=1,
            grid=(T // tm,),
            in_specs=[pl.BlockSpec((TOP_K, tm), lambda i, pe: (0, i), memory_space=pltpu.SMEM),
                      pl.BlockSpec(h_block, lambda i, pe: (i, 0, 0, 0))],
            out_specs=pl.BlockSpec(memory_space=pl.ANY),
            scratch_shapes=[pltpu.VMEM((EXPERT_ROWS,) + ROW_TILE, F32), pltpu.SemaphoreType.DMA(())],
        ),
        out_shape=jax.ShapeDtypeStruct((n_slots,) + ROW_TILE, F32),
        compiler_params=_cparams(("arbitrary",)),
        name="moe_dispatch",
    )(pad_end, dest, h2)


def _deinterleave_kernel(w_ref, o_ref):
    n = 2 * LANES
    src = lax.broadcasted_iota(I32, (n, n), 0)
    dst = lax.broadcasted_iota(I32, (n, n), 1)
    perm = jnp.where(src == jnp.where(dst < LANES, 2 * dst, 2 * (dst - LANES) + 1), 1.0, 0.0).astype(BF16)
    for c in range(w_ref.shape[2] // n):
        w = w_ref[0, :, c * n:(c + 1) * n].astype(BF16)
        o_ref[0, :, c * n:(c + 1) * n] = jnp.dot(w, perm, preferred_element_type=F32).astype(BF16)


def _deinterleave_w1(w1, cols=512):
    E, K, N = w1.shape
    spec = pl.BlockSpec((1, K, cols), lambda e, j: (e, 0, j))
    return pl.pallas_call(
        _deinterleave_kernel,
        grid=(E, N // cols),
        in_specs=[spec],
        out_specs=spec,
        out_shape=jax.ShapeDtypeStruct(w1.shape, BF16),
        compiler_params=_cparams(("parallel", "parallel")),
        name="moe_w1_layout",
    )(w1)


def _expert_kernel(blk_ref, nused_ref, xs_ref, w1_ref, b1_ref, w2_ref, b2_ref, ys_ref,
                   xbuf, ybuf, in_sem, out_sem):
    i = pl.program_id(0)
    n = pl.num_programs(0)
    bm = xbuf.shape[1]
    slot = i % 2

    def rows_of(ref, blk):
        return ref.at[pl.ds(pl.multiple_of(blk * bm, bm), bm)]

    def loads(blk, s):
        return _relayout_copies(xbuf.at[s], rows_of(xs_ref, blk), in_sem.at[s], to_tiled=False)

    def stores(blk, s):
        return _relayout_copies(ybuf.at[s], rows_of(ys_ref, blk), out_sem.at[s], to_tiled=True)

    @pl.when(i == 0)
    def _():
        for cp in loads(0, 0):
            cp.start()

    @pl.when(i + 1 < n)
    def _():
        for cp in loads(i + 1, 1 - slot):
            cp.start()

    for cp in loads(i, slot):
        cp.wait()

    @pl.when(i >= 2)
    def _():
        for cp in stores(i - 2, slot):
            cp.wait()

    @pl.when(i < nused_ref[0])
    def _():
        x = xbuf[slot].astype(BF16)
        step = 2 * MXU_TILE

        def hidden(c):
            cols = slice(c * step, (c + 1) * step)
            return jnp.dot(x, w1_ref[0, :, cols], preferred_element_type=F32) + b1_ref[0, :, cols]

        def down(c, hid):
            acts = []
            for o in range(0, step, 2 * LANES):
                gate = jnp.minimum(hid[:, o:o + LANES], SWIGLU_LIMIT)
                up = jnp.clip(hid[:, o + LANES:o + 2 * LANES], -SWIGLU_LIMIT, SWIGLU_LIMIT)
                acts.append(((up + 1.0) * (gate * jax.nn.sigmoid(SWIGLU_ALPHA * gate))).astype(BF16))
            rows = slice(c * MXU_TILE, (c + 1) * MXU_TILE)
            return jnp.dot(jnp.concatenate(acts, axis=1), w2_ref[0, rows, :], preferred_element_type=F32)

        n_piece = w1_ref.shape[2] // step
        y = b2_ref[0]
        hid = hidden(0)
        for c in range(n_piece):
            nxt = hidden(c + 1) if c + 1 < n_piece else None
            y = y + down(c, hid)
            hid = nxt
        ybuf[slot] = y

    @pl.when(i >= nused_ref[0])
    def _():
        ybuf[slot] = jnp.zeros(ybuf.shape[1:], F32)

    for cp in stores(i, slot):
        cp.start()

    @pl.when(i == n - 1)
    def _():
        for cp in stores(i, slot):
            cp.wait()

    @pl.when((i == n - 1) & (n >= 2))
    def _():
        for cp in stores(i - 1, 1 - slot):
            cp.wait()


def _experts(xs, blk_e, nused, w1, b1, w2, b2):
    P = xs.shape[0]
    D = w1.shape[1]
    bm = EXPERT_ROWS
    nb = P // bm
    per_expert = lambda a: pl.BlockSpec((1,) + a.shape[1:], lambda i, blk, nu: (blk[i], 0, 0))
    return pl.pallas_call(
        _expert_kernel,
        grid_spec=pltpu.PrefetchScalarGridSpec(
            num_scalar_prefetch=2,
            grid=(nb,),
            in_specs=[pl.BlockSpec(memory_space=pl.ANY), per_expert(w1), per_expert(b1), per_expert(w2),
                      per_expert(b2)],
            out_specs=pl.BlockSpec(memory_space=pl.ANY),
            scratch_shapes=[pltpu.VMEM((2, bm, D), F32), pltpu.VMEM((2, bm, D), F32),
                            pltpu.SemaphoreType.DMA((2,)), pltpu.SemaphoreType.DMA((2,))],
        ),
        out_shape=jax.ShapeDtypeStruct((P,) + ROW_TILE, F32),
        compiler_params=_cparams(("arbitrary",)),
        name="moe_experts",
    )(blk_e, nused, xs, w1, b1, w2, b2)


def _combine_kernel(dest_ref, ys_ref, tg_ref, x1_ref, gfin_ref, y_ref, buf, sem, *, final):
    tc = x1_ref.shape[0]

    def row_copy(g, s, k):
        return pltpu.make_async_copy(ys_ref.at[dest_ref[k, g * SUBLANES + s]], buf.at[k, g, :, s, :], sem)

    def start(g, c):
        for s in range(SUBLANES):
            for k in range(TOP_K):
                row_copy(g, s, k).start(priority=k % 2)
        return c

    def wait(g, c):
        for s in range(SUBLANES):
            for k in range(TOP_K):
                row_copy(g, s, k).wait()
        return c

    lax.fori_loop(0, tc // SUBLANES, start, 0)
    lax.fori_loop(0, tc // SUBLANES, wait, 0)
    gates = tg_ref[...].T
    x = x1_ref[...]
    for k in range(TOP_K):
        x = x + _load_vreg_tiles(buf.at[k]) * gates[:, k:k + 1]
    y_ref[...] = _rms(x, gfin_ref[...]) if final else x


def _combine(ys, dest, tg, x1, gfin, final, tc=256):
    T, D = x1.shape
    return pl.pallas_call(
        functools.partial(_combine_kernel, final=final),
        grid=(T // tc,),
        in_specs=[pl.BlockSpec((TOP_K, tc), lambda i: (0, i), memory_space=pltpu.SMEM),
                  pl.BlockSpec(memory_space=pl.ANY),
                  pl.BlockSpec((SUBLANES, tc), lambda i: (0, i)),
                  pl.BlockSpec((tc, D), lambda i: (i, 0)),
                  _full(gfin.shape)],
        out_specs=pl.BlockSpec((tc, D), lambda i: (i, 0)),
        out_shape=jax.ShapeDtypeStruct((T, D), F32),
        scratch_shapes=[pltpu.VMEM((TOP_K,) + _vreg_tile_shape(tc, D), F32), pltpu.SemaphoreType.DMA(())],
        compiler_params=_cparams(("parallel",)),
        name="moe_combine",
    )(dest, ys, tg, x1, gfin)


SC_INDEX_WINDOW = 128
SC_GATHER_ROWS = 16


def _sc_gather_rows(table, idx):
    n = idx.shape[0]
    info = plsc.get_sparse_core_info()
    workers = info.num_cores * info.num_subcores
    per_worker = n // workers
    assert n % workers == 0 and per_worker % SC_INDEX_WINDOW == 0
    mesh = plsc.VectorSubcoreMesh(core_axis_name="c", subcore_axis_name="s")

    @functools.partial(
        pl.kernel, mesh=mesh,
        out_type=jax.ShapeDtypeStruct((n,) + table.shape[1:], table.dtype),
        scratch_types=[pltpu.VMEM((SC_INDEX_WINDOW,), I32),
                       pltpu.VMEM((SC_GATHER_ROWS,) + table.shape[1:], table.dtype),
                       pltpu.SemaphoreType.DMA])
    def gather(table_hbm, idx_hbm, out_hbm, idx_v, rows_v, sem):
        worker = lax.axis_index("s") * info.num_cores + lax.axis_index("c")
        first = worker * (per_worker // SC_INDEX_WINDOW)

        @pl.loop(0, per_worker // SC_INDEX_WINDOW)
        def _(j):
            window = first + j
            pltpu.sync_copy(idx_hbm.at[window], idx_v)
            for c in range(SC_INDEX_WINDOW // SC_GATHER_ROWS):
                pltpu.async_copy(table_hbm.at[idx_v.at[pl.ds(c * SC_GATHER_ROWS, SC_GATHER_ROWS)]], rows_v, sem).wait()
                pltpu.sync_copy(rows_v, out_hbm.at[pl.ds(window * SC_INDEX_WINDOW + c * SC_GATHER_ROWS,
                                                         SC_GATHER_ROWS)])

    return gather(table, idx.reshape(n // SC_INDEX_WINDOW, SC_INDEX_WINDOW))


def _finish_kernel(rows_ref, tg_ref, x1_ref, gfin_ref, y_ref, buf, sem, *, final):
    i = pl.program_id(0)
    n = pl.num_programs(0)
    tm = x1_ref.shape[0]
    slot = i % 2

    def loads(tile, s):
        cps = []
        for k in range(TOP_K):
            start = pl.multiple_of(k * (n * tm) + tile * tm, tm)
            cps += _relayout_copies(buf.at[s, k], rows_ref.at[pl.ds(start, tm)], sem.at[s], to_tiled=False)
        return cps

    @pl.when(i == 0)
    def _():
        for cp in loads(0, 0):
            cp.start()

    @pl.when(i + 1 < n)
    def _():
        for cp in loads(i + 1, 1 - slot):
            cp.start()

    for cp in loads(i, slot):
        cp.wait()
    gates = tg_ref[...].T
    x = x1_ref[...]
    for k in range(TOP_K):
        x = x + buf[slot, k] * gates[:, k:k + 1]
    y_ref[...] = _rms(x, gfin_ref[...]) if final else x


def _finish(rows, tg, x1, gfin, final, tm=256):
    T, D = x1.shape
    return pl.pallas_call(
        functools.partial(_finish_kernel, final=final),
        grid=(T // tm,),
        in_specs=[pl.BlockSpec(memory_space=pl.ANY),
                  pl.BlockSpec((SUBLANES, tm), lambda i: (0, i)),
                  pl.BlockSpec((tm, D), lambda i: (i, 0)),
                  _full(gfin.shape)],
        out_specs=pl.BlockSpec((tm, D), lambda i: (i, 0)),
        out_shape=jax.ShapeDtypeStruct((T, D), F32),
        scratch_shapes=[pltpu.VMEM((2, TOP_K, tm, D), F32), pltpu.SemaphoreType.DMA((2,))],
        compiler_params=_cparams(("arbitrary",)),
        name="moe_finish",
    )(rows, tg, x1, gfin)


def _moe(h2, ti, rk, tg, cnt, x1, moe_w, gfin, final):
    T = x1.shape[0]
    bm = EXPERT_ROWS
    n_assign = T * TOP_K
    nb = -(-(n_assign + N_EXPERTS * (bm - 1)) // bm)
    counts = cnt[:, 0].astype(I32)
    padded = (counts + bm - 1) // bm * bm
    pad_end = jnp.cumsum(padded).astype(I32)
    pad_start = pad_end - padded
    experts = jnp.arange(N_EXPERTS, dtype=I32)[:, None, None]
    dest = rk + jnp.sum(jnp.where(ti[None] == experts, pad_start[:, None, None], 0), axis=0)
    nused = (pad_end[-1:] // bm).astype(I32)
    blk_start = jnp.minimum(jnp.arange(nb, dtype=I32), nused[0] - 1) * bm
    blk_e = jnp.minimum(jnp.sum((pad_end[None, :] <= blk_start[:, None]).astype(I32), axis=1), N_EXPERTS - 1)
    xs = _dispatch(h2, dest, pad_end, nb * bm)
    ys = _experts(xs, blk_e, nused, *moe_w)
    rows = _sc_gather_rows(ys, dest.reshape(-1))
    return _finish(rows, tg, x1, gfin, final)


def _split_in_proj(w):
    sizes = (DA_QK_WIDTH, DA_QK_WIDTH, DA_WIDTH, GLA_KEY_WIDTH, GLA_KEY_WIDTH, GLA_WIDTH, GLA_WIDTH,
             GLA_GATE_RANK, GLA_GATE_RANK, D_MODEL, D_MODEL)
    assert w.shape[-1] == sum(sizes)
    out, o = [], 0
    for s in sizes:
        out.append(w[:, o:o + s])
        o += s
    return out


def _prep_layer(l, norm_mix_g, w_in, lambda_q1, lambda_k1, lambda_q2, lambda_k2, diff_subln_g,
                gla_gate_up_f, gla_gate_bias_f, gla_gate_up_b, gla_gate_bias_b, gla_norm_g,
                w_proj_a, w_proj_b, w_out, norm_ffn_g, router_w, router_b, moe_w1, moe_b1, moe_w2, moe_b2):
    dq, dk, dv, lq, lk, lv, lr, laf, lab, ga, gb = _split_in_proj(w_in[l])
    bf = lambda a: a.astype(BF16)
    r = GLA_GATE_RANK
    gup = jnp.zeros((LANES, 2 * GLA_KEY_WIDTH), F32)
    gup = gup.at[0:r, 0:GLA_KEY_WIDTH].set(gla_gate_up_f[l]).at[r:2 * r, GLA_KEY_WIDTH:].set(gla_gate_up_b[l])
    lamv = jnp.zeros((SUBLANES, LANES), F32)
    for i, v in enumerate((lambda_q1, lambda_k1, lambda_q2, lambda_k2)):
        lamv = lamv.at[i, 0:DA_HEAD_DIM].set(v[l].astype(F32))
    inproj = dict(
        norm_g=norm_mix_g[l][None, :],
        wqk=bf(jnp.concatenate([dq * (DA_HEAD_DIM ** -0.5), dk], 1)),
        wvt=bf(dv.T),
        wlqk=bf(jnp.concatenate([lq, lk], 1)),
        wlvr=bf(jnp.concatenate([lv, lr], 1)),
        wg=bf(jnp.concatenate([ga, gb], 1)),
        wlow=bf(jnp.concatenate([laf, lab, jnp.zeros((D_MODEL, LANES - 2 * r), F32)], 1)),
        gup=bf(gup),
        gbias=jnp.concatenate([gla_gate_bias_f[l], gla_gate_bias_b[l]])[None, :],
    )
    merge = dict(
        subg=diff_subln_g[l][None, :], glag=gla_norm_g[l][None, :],
        wpa=bf(w_proj_a[l]), wpb=bf(w_proj_b[l]), wout=bf(w_out[l]),
        nffn=norm_ffn_g[l][None, :], rwt=bf(router_w[l].T), rb=router_b[l][:, None],
    )
    b1 = moe_b1[l].reshape(N_EXPERTS, -1, LANES, 2).transpose(0, 1, 3, 2).reshape(N_EXPERTS, 1, -1)
    moe_w = (_deinterleave_w1(moe_w1[l]), b1, bf(moe_w2[l]), moe_b2[l][:, None, :])
    return inproj, lamv, merge, moe_w


def _trunk(x, layers, norm_final_g):
    B, L, D = x.shape
    T = B * L
    cos_t, sin_t = _rope_tables(L)
    x2 = x.reshape(T, D)
    gfin = norm_final_g[None, :]
    for l, (inproj, lamv, merge, moe_w) in enumerate(layers):
        lam_init = 0.8 - 0.6 * math.exp(-0.3 * l)
        qk, vt, lqk, lvr, gfb, sg = _inproj(x2, L, cos_t=cos_t, sin_t=sin_t, **inproj)
        oa = _diff_attention(qk.reshape(B, L, -1), vt, lamv, lam_init)
        of, ob = _gla(lqk.reshape(B, L, -1), lvr.reshape(B, L, -1), gfb.reshape(B, L, -1))
        x1, h2, ti, rk, tg, cnt = _merge(x2, oa.reshape(T, -1), of.reshape(T, -1), ob.reshape(T, -1),
                                         lvr, sg, lam_init=lam_init, **merge)
        x2 = _moe(h2, ti, rk, tg, cnt, x1, moe_w, gfin, final=(l == len(layers) - 1))
    return x2.reshape(B, L, D)


def kernel(x_prompt, x_sample, norm_mix_g, w_in, lambda_q1, lambda_k1, lambda_q2, lambda_k2, diff_subln_g,
           gla_gate_up_f, gla_gate_bias_f, gla_gate_up_b, gla_gate_bias_b, gla_norm_g, w_proj_a, w_proj_b,
           w_out, norm_ffn_g, router_w, router_b, moe_w1, moe_b1, moe_w2, moe_b2, norm_final_g):
    depth = w_in.shape[0]
    layers = [_prep_layer(l, norm_mix_g, w_in, lambda_q1, lambda_k1, lambda_q2, lambda_k2, diff_subln_g,
                          gla_gate_up_f, gla_gate_bias_f, gla_gate_up_b, gla_gate_bias_b, gla_norm_g,
                          w_proj_a, w_proj_b, w_out, norm_ffn_g, router_w, router_b,
                          moe_w1, moe_b1, moe_w2, moe_b2) for l in range(depth)]
    return _trunk(x_prompt, layers, norm_final_g), _trunk(x_sample, layers, norm_final_g)
```

```python
import functools
import math

import jax
import jax.numpy as jnp
from jax import lax
from jax.experimental import pallas as pl
from jax.experimental.pallas import tpu as pltpu
from jax.experimental.pallas import tpu_sc as plsc

F32 = jnp.float32
BF16 = jnp.bfloat16
I32 = jnp.int32

D_MODEL = 1024
DA_HEADS = 4
DA_HEAD_DIM = 64
DA_VDIM = 128
DA_QK_WIDTH = DA_HEADS * 2 * DA_HEAD_DIM
DA_WIDTH = DA_HEADS * DA_VDIM
ROPE_THETA = 10000.0
GLA_HEADS = 4
GLA_KDIM = 64
GLA_VDIM = 128
GLA_KEY_WIDTH = GLA_HEADS * GLA_KDIM
GLA_WIDTH = GLA_HEADS * GLA_VDIM
GLA_GATE_RANK = 16
GLA_GATE_NORMALIZER = 16.0
GLA_CHUNK = 64
N_EXPERTS = 32
TOP_K = 4
D_FF = D_MODEL
SWIGLU_ALPHA = 1.702
SWIGLU_LIMIT = 7.0
NORM_EPS = 1e-5
LANES = 128
SUBLANES = 8

BF16_SUBLANES = 16
MXU_TILE = 256
LOG2_E = 1.4426950408889634
VT_ROWS = DA_VDIM + BF16_SUBLANES

VMEM_LIMIT = 56 * 1024 * 1024

TOKEN_TILE = 512
ATTN_KEY_CHUNK = 512
EXPERT_ROWS = 512

NT_DIMS = (((1,), (1,)), ((), ()))
TN_DIMS = (((0,), (0,)), ((), ()))


def _cparams(sem):
    return pltpu.CompilerParams(dimension_semantics=sem, vmem_limit_bytes=VMEM_LIMIT)


def _full(shape):
    nd = len(shape)
    return pl.BlockSpec(shape, lambda *_: (0,) * nd)


def _rms(x, g):
    return x * lax.rsqrt(jnp.mean(x * x, axis=-1, keepdims=True) + NORM_EPS) * g


ROW_TILE = (SUBLANES, LANES)


def _relayout_copies(flat_ref, tiled_ref, sem, to_tiled):
    cps = []
    for c in range(SUBLANES):
        flat = flat_ref.at[:, pl.ds(c * LANES, LANES)]
        tiled = tiled_ref.at[:, c, :]
        cps.append(pltpu.make_async_copy(flat, tiled, sem) if to_tiled else pltpu.make_async_copy(tiled, flat, sem))
    return cps


def _log_sigmoid(x):
    return jnp.minimum(x, 0.0) - jnp.log1p(jnp.exp(-jnp.abs(x)))


def _inproj_kernel(x_ref, g_ref, cos_ref, sin_ref, wqk_ref, wvt_ref, wlqk_ref, wlvr_ref,
                   wg_ref, wlow_ref, gup_ref, gbias_ref,
                   qk_ref, vt_ref, lqk_ref, lvr_ref, gfb_ref, sg_ref):
    h = _rms(x_ref[...], g_ref[...]).astype(BF16)

    cos = cos_ref[...]
    sin_signed = sin_ref[...]
    first_half = (lax.broadcasted_iota(I32, (1, LANES), 1) % DA_HEAD_DIM) < (DA_HEAD_DIM // 2)
    for c in range(2 * DA_QK_WIDTH // LANES):
        z = jnp.dot(h, wqk_ref[:, c * LANES:(c + 1) * LANES], preferred_element_type=F32)
        if c < DA_QK_WIDTH // LANES:
            z = z * LOG2_E
        partner = jnp.where(first_half, pltpu.roll(z, LANES - DA_HEAD_DIM // 2, 1),
                            pltpu.roll(z, DA_HEAD_DIM // 2, 1))
        qk_ref[:, c * LANES:(c + 1) * LANES] = (z * cos + partner * sin_signed).astype(BF16)

    vt = lax.dot_general(wvt_ref[...], h, NT_DIMS, preferred_element_type=F32).astype(BF16)
    for hd in range(DA_HEADS):
        vt_ref[0, hd * VT_ROWS:hd * VT_ROWS + DA_VDIM, :] = vt[hd * DA_VDIM:(hd + 1) * DA_VDIM]
        vt_ref[0, hd * VT_ROWS + DA_VDIM:(hd + 1) * VT_ROWS, :] = jnp.ones((VT_ROWS - DA_VDIM, vt.shape[1]), BF16)
    lqk_ref[...] = jnp.dot(h, wlqk_ref[...], preferred_element_type=F32)
    lvr_ref[...] = jnp.dot(h, wlvr_ref[...], preferred_element_type=F32).astype(BF16)
    sg_ref[...] = jax.nn.sigmoid(jnp.dot(h, wg_ref[...], preferred_element_type=F32)).astype(BF16)
    low = jnp.dot(h, wlow_ref[...], preferred_element_type=F32).astype(BF16)
    pre = jnp.dot(low, gup_ref[...], preferred_element_type=F32) + gbias_ref[...]
    gfb_ref[...] = _log_sigmoid(pre) * (1.0 / GLA_GATE_NORMALIZER)


def _inproj(x2, seq_len, norm_g, wqk, wvt, wlqk, wlvr, wg, wlow, gup, gbias, cos_t, sin_t):
    T, D = x2.shape
    tm = TOKEN_TILE
    assert T % tm == 0 and seq_len % tm == 0
    nl = seq_len // tm
    row = lambda n: pl.BlockSpec((tm, n), lambda i: (i, 0))
    out_shape = (
        jax.ShapeDtypeStruct((T, 2 * DA_QK_WIDTH), BF16),
        jax.ShapeDtypeStruct((T // tm, DA_HEADS * VT_ROWS, tm), BF16),
        jax.ShapeDtypeStruct((T, 2 * GLA_KEY_WIDTH), F32),
        jax.ShapeDtypeStruct((T, 2 * GLA_WIDTH), BF16),
        jax.ShapeDtypeStruct((T, 2 * GLA_KEY_WIDTH), F32),
        jax.ShapeDtypeStruct((T, 2 * D_MODEL), BF16),
    )
    return pl.pallas_call(
        _inproj_kernel,
        grid=(T // tm,),
        in_specs=[row(D), _full(norm_g.shape),
                  pl.BlockSpec((tm, LANES), lambda i: (i % nl, 0)),
                  pl.BlockSpec((tm, LANES), lambda i: (i % nl, 0)),
                  _full(wqk.shape), _full(wvt.shape), _full(wlqk.shape), _full(wlvr.shape),
                  _full(wg.shape), _full(wlow.shape), _full(gup.shape), _full(gbias.shape)],
        out_specs=(row(2 * DA_QK_WIDTH), pl.BlockSpec((1, DA_HEADS * VT_ROWS, tm), lambda i: (i, 0, 0)),
                   row(2 * GLA_KEY_WIDTH), row(2 * GLA_WIDTH), row(2 * GLA_KEY_WIDTH), row(2 * D_MODEL)),
        out_shape=out_shape,
        compiler_params=_cparams(("parallel",)),
        name="inproj",
    )(x2, norm_g, cos_t, sin_t, wqk, wvt, wlqk, wlvr, wg, wlow, gup, gbias)


def _rope_tables(seq_len):
    d = DA_HEAD_DIM
    inv = ROPE_THETA ** (-jnp.arange(0, d, 2, dtype=F32) / d)
    ang = jnp.arange(seq_len, dtype=F32)[:, None] * inv[None, :]
    cos = jnp.concatenate([jnp.cos(ang)] * (2 * LANES // d), -1)
    sin = jnp.sin(ang)
    sin_signed = jnp.concatenate([-sin, sin] * (LANES // d), -1)
    return cos, sin_signed


def _attn_kernel(lamv_ref, q_ref, k_ref, vt_ref, o_ref, acc_sc, sa_sc, sb_sc, *, lam_init):
    n_vt, _, vt_w = vt_ref.shape
    tk = ATTN_KEY_CHUNK
    nk = n_vt * vt_w // tk
    tq = q_ref.shape[1]
    q = q_ref[0]
    lane = lax.broadcasted_iota(I32, (1, LANES), 1)
    zeros = jnp.zeros_like(q)
    qms = (jnp.where(lane < DA_HEAD_DIM, q, zeros), jnp.where(lane >= DA_HEAD_DIM, q, zeros))
    acc_sc[...] = jnp.zeros(acc_sc.shape, F32)

    def scores(j, dst_ref):
        kj = k_ref[0, j * tk:(j + 1) * tk, :]
        col_max = []
        for mp in range(2):
            st = lax.dot_general(kj, qms[mp], NT_DIMS, preferred_element_type=F32)
            dst_ref[mp] = st
            col_max.append(jnp.max(st, axis=0, keepdims=True))
        return col_max

    def softmax_pv(j, src_ref, ms, col_max):
        def v_tile(r):
            key = j * tk + r
            return vt_ref[key // vt_w, :, key % vt_w:key % vt_w + MXU_TILE]

        new = []
        for mp in range(2):
            m_new = jnp.maximum(ms[mp], col_max[mp])
            alpha = jnp.exp2(ms[mp] - m_new)
            for cols in (slice(c, c + MXU_TILE) for c in range(0, tq, MXU_TILE)):
                part = None
                for r in range(0, tk, MXU_TILE):
                    p = jnp.exp2((src_ref[mp, r:r + MXU_TILE, cols] - m_new[:, cols]).astype(BF16))
                    d = jnp.dot(v_tile(r), p, preferred_element_type=F32)
                    part = d if part is None else part + d
                acc_sc[mp, :, cols] = acc_sc[mp, :, cols] * alpha[:, cols] + part
            new.append(m_new)
        return tuple(new)

    bufs = (sa_sc, sb_sc)
    col_max = scores(0, bufs[0])
    ms = tuple(jnp.full((1, tq), -jnp.inf, F32) for _ in range(2))
    for j in range(nk):
        nxt_max = scores(j + 1, bufs[(j + 1) % 2]) if j + 1 < nk else None
        ms = softmax_pv(j, bufs[j % 2], ms, col_max)
        col_max = nxt_max

    lv = lamv_ref[...]
    lam = (jnp.exp(jnp.sum(lv[0:1] * lv[1:2], axis=-1, keepdims=True))
           - jnp.exp(jnp.sum(lv[2:3] * lv[3:4], axis=-1, keepdims=True)) + lam_init)
    a0, a1 = acc_sc[0], acc_sc[1]
    o_ref[0] = (a0[:DA_VDIM] / a0[DA_VDIM:DA_VDIM + 1] - lam * (a1[:DA_VDIM] / a1[DA_VDIM:DA_VDIM + 1])).T


def _diff_attention(qk3, vt3, lamv, lam_init, tq=512):
    B, L, _ = qk3.shape
    vt_w = vt3.shape[-1]
    tk = ATTN_KEY_CHUNK
    assert L % tk == 0 and tk % vt_w == 0
    return pl.pallas_call(
        functools.partial(_attn_kernel, lam_init=lam_init),
        grid=(B, DA_HEADS, L // tq),
        in_specs=[_full(lamv.shape),
                  pl.BlockSpec((1, tq, LANES), lambda b, h, i: (b, i, h)),
                  pl.BlockSpec((1, L, LANES), lambda b, h, i: (b, 0, DA_HEADS + h)),
                  pl.BlockSpec((L // vt_w, VT_ROWS, vt_w), lambda b, h, i: (b, h, 0))],
        out_specs=pl.BlockSpec((1, tq, DA_VDIM), lambda b, h, i: (b, i, h)),
        out_shape=jax.ShapeDtypeStruct((B, L, DA_WIDTH), F32),
        scratch_shapes=[pltpu.VMEM((2, VT_ROWS, tq), F32), pltpu.VMEM((2, tk, tq), F32),
                        pltpu.VMEM((2, tk, tq), F32)],
        compiler_params=_cparams(("parallel", "parallel", "arbitrary")),
        name="diff_attention",
    )(lamv, qk3, qk3, vt3)


def _split3_bf16(x):
    hi = x.astype(BF16)
    r1 = x - hi.astype(F32)
    mid = r1.astype(BF16)
    return hi, mid, (r1 - mid.astype(F32)).astype(BF16)


def _gla_local(qk_ref, vr_ref, g_ref, *, reverse):
    C = GLA_CHUNK
    tb = qk_ref.shape[1]
    nchunk = tb // C
    row = lax.broadcasted_iota(I32, (tb, tb), 0)
    col = lax.broadcasted_iota(I32, (tb, tb), 1)
    same_chunk = (row // C) == (col // C)
    tri = same_chunk & ((col >= row) if reverse else (col <= row))
    tri_b = jnp.where(tri, 1.0, 0.0).astype(BF16)
    lane = lax.broadcasted_iota(I32, (1, LANES), 1)
    goff = GLA_KEY_WIDTH if reverse else 0

    g = g_ref[0, :, goff:goff + GLA_KEY_WIDTH]
    q = qk_ref[0, :, 0:GLA_KEY_WIDTH] * (GLA_KDIM ** -0.5)
    k = qk_ref[0, :, GLA_KEY_WIDTH:2 * GLA_KEY_WIDTH]
    v = vr_ref[0]
    b = sum(jnp.dot(tri_b, part, preferred_element_type=F32) for part in _split3_bf16(g))
    last_row, ref_row = (0, C // 2) if reverse else (C - 1, C // 2 - 1)
    per_chunk = lambda r: jnp.concatenate(
        [jnp.broadcast_to(b[c * C + r:c * C + r + 1], (C, GLA_KEY_WIDTH)) for c in range(nchunk)], axis=0)
    b_last, b_ref = per_chunk(last_row), per_chunk(ref_row)
    q_in = q * jnp.exp(b - b_ref)
    k_in = (k * jnp.exp(b_ref - b)).astype(BF16)
    k_dec = k * jnp.exp(b_last - b)
    q_dec = q * jnp.exp(b)
    dec = [jnp.exp(b[c * C + last_row:c * C + last_row + 1]) for c in range(nchunk)]

    heads = []
    for h in range(GLA_HEADS):
        ps = slice((h // 2) * LANES, (h // 2 + 1) * LANES)
        keep = (lane < GLA_KDIM) if h % 2 == 0 else (lane >= GLA_KDIM)
        zero = jnp.zeros((tb, LANES), F32)
        heads.append(dict(
            ps=ps, vs=slice(h * GLA_VDIM, (h + 1) * GLA_VDIM),
            q_in=jnp.where(keep, q_in[:, ps], zero).astype(BF16),
            q_dec=jnp.where(keep, q_dec[:, ps], zero).astype(BF16),
            k_dec=jnp.where(keep, k_dec[:, ps], zero).astype(BF16)))
    scores = [lax.dot_general(hd["q_in"], k_in[:, hd["ps"]], NT_DIMS, preferred_element_type=F32) for hd in heads]
    kv_t = [[lax.dot_general(v[c * C:(c + 1) * C, hd["vs"]], hd["k_dec"][c * C:(c + 1) * C], TN_DIMS,
                             preferred_element_type=F32) for hd in heads] for c in range(nchunk)]
    intra = [jnp.dot(jnp.where(tri, s, 0.0).astype(BF16), v[:, hd["vs"]], preferred_element_type=F32)
             for s, hd in zip(scores, heads)]
    return heads, dec, kv_t, intra


def _gla_scan(heads, dec, kv_t, st_ref, *, reverse):
    nchunk = len(dec)
    state = [st_ref[h] for h in range(GLA_HEADS)]
    entering = [None] * nchunk
    for c in (range(nchunk - 1, -1, -1) if reverse else range(nchunk)):
        entering[c] = [s.astype(BF16) for s in state]
        state = [s * dec[c][:, hd["ps"]] + kv_t[c][h] for h, (s, hd) in enumerate(zip(state, heads))]
    for h in range(GLA_HEADS):
        st_ref[h] = state[h]
    return entering


def _gla_emit(heads, intra, entering, o_ref):
    C = GLA_CHUNK
    for c, states in enumerate(entering):
        rows = slice(c * C, (c + 1) * C)
        for hd, o_intra, s_t in zip(heads, intra, states):
            inter = lax.dot_general(hd["q_dec"][rows], s_t, NT_DIMS, preferred_element_type=F32)
            o_ref[0, rows, hd["vs"]] = o_intra[rows] + inter


def _gla_kernel(qkf_ref, vrf_ref, gf_ref, qkb_ref, vrb_ref, gb_ref, of_ref, ob_ref, sf_sc, sb_sc):
    @pl.when(pl.program_id(1) == 0)
    def _():
        sf_sc[...] = jnp.zeros(sf_sc.shape, F32)
        sb_sc[...] = jnp.zeros(sb_sc.shape, F32)

    hf, df, kf, inf = _gla_local(qkf_ref, vrf_ref, gf_ref, reverse=False)
    hb, db, kb, inb = _gla_local(qkb_ref, vrb_ref, gb_ref, reverse=True)
    ef = _gla_scan(hf, df, kf, sf_sc, reverse=False)
    eb = _gla_scan(hb, db, kb, sb_sc, reverse=True)
    _gla_emit(hf, inf, ef, of_ref)
    _gla_emit(hb, inb, eb, ob_ref)


def _gla(lqk3, lvr3, gfb3, tb=256):
    B, L, _ = lqk3.shape
    nb = L // tb
    fwd = lambda n: pl.BlockSpec((1, tb, n), lambda b, i: (b, i, 0))
    bwd = lambda n: pl.BlockSpec((1, tb, n), lambda b, i: (b, nb - 1 - i, 0))
    out = jax.ShapeDtypeStruct((B, L, GLA_WIDTH), F32)
    return pl.pallas_call(
        _gla_kernel,
        grid=(B, nb),
        in_specs=[fwd(2 * GLA_KEY_WIDTH), fwd(GLA_WIDTH), fwd(2 * GLA_KEY_WIDTH),
                  bwd(2 * GLA_KEY_WIDTH), bwd(GLA_WIDTH), bwd(2 * GLA_KEY_WIDTH)],
        out_specs=(fwd(GLA_WIDTH), bwd(GLA_WIDTH)),
        out_shape=(out, out),
        scratch_shapes=[pltpu.VMEM((GLA_HEADS, GLA_VDIM, LANES), F32),
                        pltpu.VMEM((GLA_HEADS, GLA_VDIM, LANES), F32)],
        compiler_params=_cparams(("parallel", "arbitrary")),
        name="gla",
    )(lqk3, lvr3, gfb3, lqk3, lvr3, gfb3)


def _head_norm(z, g):
    return jnp.concatenate(
        [_rms(z[:, h * LANES:(h + 1) * LANES], g) for h in range(z.shape[1] // LANES)], axis=1)


def _merge_kernel(x_ref, oa_ref, of_ref, ob_ref, r_ref, sg_ref, subg_ref, glag_ref, wpa_ref, wpb_ref,
                  wout_ref, nffn_ref, rwt_ref, rb_ref,
                  x1_ref, h2_ref, ti_ref, rk_ref, tg_ref, cnt_ref, carry_sc, h2_sc, h2_sem, *, lam_init):
    tm = x_ref.shape[0]
    step = pl.program_id(0)

    def h2_stores(tile):
        rows = h2_ref.at[pl.ds(pl.multiple_of(tile * tm, tm), tm)]
        return _relayout_copies(h2_sc, rows, h2_sem, to_tiled=True)

    @pl.when(step == 0)
    def _():
        carry_sc[...] = jnp.zeros(carry_sc.shape, F32)

    @pl.when(step > 0)
    def _():
        for cp in h2_stores(step - 1):
            cp.wait()

    oa = (_head_norm(oa_ref[...], subg_ref[...]) * (1.0 - lam_init)).astype(BF16)
    ob = _head_norm(of_ref[...] + ob_ref[...], glag_ref[...]) * jax.nn.silu(r_ref[...].astype(F32))
    pa = jnp.dot(oa, wpa_ref[...], preferred_element_type=F32)
    pb = jnp.dot(ob.astype(BF16), wpb_ref[...], preferred_element_type=F32)
    sg = sg_ref[...].astype(F32)
    merged = sg[:, :D_MODEL] * pa + sg[:, D_MODEL:] * pb
    x1 = x_ref[...] + jnp.dot(merged.astype(BF16), wout_ref[...], preferred_element_type=F32)
    x1_ref[...] = x1
    h2 = _rms(x1, nffn_ref[...])
    h2_sc[...] = h2
    for cp in h2_stores(step):
        cp.start()

    logits = lax.dot_general(rwt_ref[...], h2.astype(BF16), NT_DIMS, preferred_element_type=F32) + rb_ref[...]
    eidx = lax.broadcasted_iota(I32, logits.shape, 0)
    vals, sels = [], []
    for k in range(TOP_K):
        mk = jnp.max(logits, axis=0, keepdims=True)
        ik = jnp.min(jnp.where(logits == mk, eidx, N_EXPERTS), axis=0, keepdims=True)
        sel = eidx == ik
        logits = jnp.where(sel, -jnp.inf, logits)
        vals.append(mk)
        sels.append(sel)
        ti_ref[k:k + 1, :] = ik
    ex = [jnp.exp(v - vals[0]) for v in vals]
    denom = ex[0] + ex[1] + ex[2] + ex[3]
    for k in range(TOP_K):
        tg_ref[k:k + 1, :] = ex[k] / denom
    tg_ref[TOP_K:, :] = jnp.zeros((tg_ref.shape[0] - TOP_K, tm), F32)

    multi = (sels[0] | sels[1] | sels[2] | sels[3])
    multi_f = jnp.where(multi, 1.0, 0.0)
    before = lax.broadcasted_iota(I32, (tm, tm), 0) < lax.broadcasted_iota(I32, (tm, tm), 1)
    cum = jnp.dot(multi_f.astype(BF16), jnp.where(before, 1.0, 0.0).astype(BF16), preferred_element_type=F32)
    tot = carry_sc[:, 0:1] + cum
    for k in range(TOP_K):
        rk_ref[k:k + 1, :] = jnp.sum(jnp.where(sels[k], tot, 0.0), axis=0, keepdims=True).astype(I32)
    carry_sc[...] = carry_sc[...] + jnp.sum(multi_f, axis=1, keepdims=True)
    cnt_ref[...] = carry_sc[...]

    @pl.when(step == pl.num_programs(0) - 1)
    def _():
        for cp in h2_stores(step):
            cp.wait()


def _merge(x2, oa, of, ob, lvr, sg, subg, glag, wpa, wpb, wout, nffn, rwt, rb, lam_init):
    T, D = x2.shape
    tm = TOKEN_TILE
    row = lambda n: pl.BlockSpec((tm, n), lambda i: (i, 0))
    col = lambda n: pl.BlockSpec((n, tm), lambda i: (0, i))
    out_shape = (
        jax.ShapeDtypeStruct((T, D), F32),
        jax.ShapeDtypeStruct((T,) + ROW_TILE, F32),
        jax.ShapeDtypeStruct((TOP_K, T), I32),
        jax.ShapeDtypeStruct((TOP_K, T), I32),
        jax.ShapeDtypeStruct((SUBLANES, T), F32),
        jax.ShapeDtypeStruct((N_EXPERTS, LANES), F32),
    )
    return pl.pallas_call(
        functools.partial(_merge_kernel, lam_init=lam_init),
        grid=(T // tm,),
        in_specs=[row(D), row(DA_WIDTH), row(GLA_WIDTH), row(GLA_WIDTH),
                  pl.BlockSpec((tm, GLA_WIDTH), lambda i: (i, 1)), row(2 * D),
                  _full(subg.shape), _full(glag.shape), _full(wpa.shape), _full(wpb.shape),
                  _full(wout.shape), _full(nffn.shape), _full(rwt.shape), _full(rb.shape)],
        out_specs=(row(D), pl.BlockSpec(memory_space=pl.ANY),
                   col(TOP_K), col(TOP_K), col(SUBLANES), _full((N_EXPERTS, LANES))),
        out_shape=out_shape,
        scratch_shapes=[pltpu.VMEM((N_EXPERTS, LANES), F32), pltpu.VMEM((tm, D), F32),
                        pltpu.SemaphoreType.DMA(())],
        compiler_params=_cparams(("arbitrary",)),
        name="merge_router",
    )(x2, oa, of, ob, lvr, sg, subg, glag, wpa, wpb, wout, nffn, rwt, rb)


SC_INDEX_WINDOW = 128
SC_GATHER_ROWS = 16


def _sc_scatter_rows(rows, dest, n_slots):
    T = rows.shape[0]
    info = plsc.get_sparse_core_info()
    workers = info.num_cores * info.num_subcores
    per_worker = T // workers
    assert T % workers == 0 and per_worker % SC_INDEX_WINDOW == 0
    windows = T // SC_INDEX_WINDOW
    steps = SC_INDEX_WINDOW // SC_GATHER_ROWS
    mesh = plsc.VectorSubcoreMesh(core_axis_name="c", subcore_axis_name="s")

    @functools.partial(
        pl.kernel, mesh=mesh,
        out_type=jax.ShapeDtypeStruct((n_slots,) + rows.shape[1:], rows.dtype),
        scratch_types=[pltpu.VMEM((steps, SC_GATHER_ROWS), I32)] * TOP_K
        + [pltpu.VMEM((SC_GATHER_ROWS,) + rows.shape[1:], rows.dtype), pltpu.SemaphoreType.DMA])
    def scatter(rows_hbm, dest_hbm, out_hbm, *scratch):
        idx_v, rows_v, sem = scratch[:TOP_K], scratch[TOP_K], scratch[TOP_K + 1]
        worker = lax.axis_index("s") * info.num_cores + lax.axis_index("c")
        first = worker * (per_worker // SC_INDEX_WINDOW)

        @pl.loop(0, per_worker // SC_INDEX_WINDOW)
        def _(j):
            window = first + j
            for k in range(TOP_K):
                pltpu.sync_copy(dest_hbm.at[k * windows + window], idx_v[k])
            for c in range(steps):
                pltpu.sync_copy(rows_hbm.at[pl.ds(window * SC_INDEX_WINDOW + c * SC_GATHER_ROWS, SC_GATHER_ROWS)],
                                rows_v)
                copies = [pltpu.async_copy(rows_v, out_hbm.at[idx_v[k].at[c]], sem) for k in range(TOP_K)]
                for cp in copies:
                    cp.wait()

    return scatter(rows, dest.reshape(TOP_K * windows, steps, SC_GATHER_ROWS))


def _deinterleave_kernel(w_ref, o_ref):
    n = 2 * LANES
    src = lax.broadcasted_iota(I32, (n, n), 0)
    dst = lax.broadcasted_iota(I32, (n, n), 1)
    perm = jnp.where(src == jnp.where(dst < LANES, 2 * dst, 2 * (dst - LANES) + 1), 1.0, 0.0).astype(BF16)
    for c in range(w_ref.shape[2] // n):
        w = w_ref[0, :, c * n:(c + 1) * n].astype(BF16)
        o_ref[0, :, c * n:(c + 1) * n] = jnp.dot(w, perm, preferred_element_type=F32).astype(BF16)


def _deinterleave_w1(w1, cols=512):
    E, K, N = w1.shape
    spec = pl.BlockSpec((1, K, cols), lambda e, j: (e, 0, j))
    return pl.pallas_call(
        _deinterleave_kernel,
        grid=(E, N // cols),
        in_specs=[spec],
        out_specs=spec,
        out_shape=jax.ShapeDtypeStruct(w1.shape, BF16),
        compiler_params=_cparams(("parallel", "parallel")),
        name="moe_w1_layout",
    )(w1)


def _expert_kernel(blk_ref, nused_ref, nvalid_ref, xs_ref, w1_ref, b1_ref, w2_ref, b2_ref, ys_ref,
                   xbuf, ybuf, in_sem, out_sem):
    i = pl.program_id(0)
    n = pl.num_programs(0)
    bm = xbuf.shape[1]
    slot = i % 2

    def rows_of(ref, blk):
        return ref.at[pl.ds(pl.multiple_of(blk * bm, bm), bm)]

    def loads(blk, s):
        return _relayout_copies(xbuf.at[s], rows_of(xs_ref, blk), in_sem.at[s], to_tiled=False)

    def stores(blk, s):
        return _relayout_copies(ybuf.at[s], rows_of(ys_ref, blk), out_sem.at[s], to_tiled=True)

    @pl.when(i == 0)
    def _():
        for cp in loads(0, 0):
            cp.start()

    @pl.when(i + 1 < n)
    def _():
        for cp in loads(i + 1, 1 - slot):
            cp.start()

    for cp in loads(i, slot):
        cp.wait()

    @pl.when(i >= 2)
    def _():
        for cp in stores(i - 2, slot):
            cp.wait()

    @pl.when(i < nused_ref[0])
    def _():
        live = lax.broadcasted_iota(I32, (bm, 1), 0) < nvalid_ref[i]
        x = jnp.where(live, xbuf[slot], 0.0).astype(BF16)
        step = 2 * MXU_TILE

        def hidden(c):
            cols = slice(c * step, (c + 1) * step)
            return jnp.dot(x, w1_ref[0, :, cols], preferred_element_type=F32) + b1_ref[0, :, cols]

        def down(c, hid):
            acts = []
            for o in range(0, step, 2 * LANES):
                gate = jnp.minimum(hid[:, o:o + LANES], SWIGLU_LIMIT)
                up = jnp.clip(hid[:, o + LANES:o + 2 * LANES], -SWIGLU_LIMIT, SWIGLU_LIMIT)
                acts.append(((up + 1.0) * (gate * jax.nn.sigmoid(SWIGLU_ALPHA * gate))).astype(BF16))
            rows = slice(c * MXU_TILE, (c + 1) * MXU_TILE)
            return jnp.dot(jnp.concatenate(acts, axis=1), w2_ref[0, rows, :], preferred_element_type=F32)

        n_piece = w1_ref.shape[2] // step
        y = b2_ref[0]
        hid = hidden(0)
        for c in range(n_piece):
            nxt = hidden(c + 1) if c + 1 < n_piece else None
            y = y + down(c, hid)
            hid = nxt
        ybuf[slot] = y

    @pl.when(i >= nused_ref[0])
    def _():
        ybuf[slot] = jnp.zeros(ybuf.shape[1:], F32)

    for cp in stores(i, slot):
        cp.start()

    @pl.when(i == n - 1)
    def _():
        for cp in stores(i, slot):
            cp.wait()

    @pl.when((i == n - 1) & (n >= 2))
    def _():
        for cp in stores(i - 1, 1 - slot):
            cp.wait()


def _experts(xs, blk_e, nused, nvalid, w1, b1, w2, b2):
    P = xs.shape[0]
    D = w1.shape[1]
    bm = EXPERT_ROWS
    nb = P // bm
    per_expert = lambda a: pl.BlockSpec((1,) + a.shape[1:], lambda i, blk, nu, nv: (blk[i], 0, 0))
    return pl.pallas_call(
        _expert_kernel,
        grid_spec=pltpu.PrefetchScalarGridSpec(
            num_scalar_prefetch=3,
            grid=(nb,),
            in_specs=[pl.BlockSpec(memory_space=pl.ANY), per_expert(w1), per_expert(b1), per_expert(w2),
                      per_expert(b2)],
            out_specs=pl.BlockSpec(memory_space=pl.ANY),
            scratch_shapes=[pltpu.VMEM((2, bm, D), F32), pltpu.VMEM((2, bm, D), F32),
                            pltpu.SemaphoreType.DMA((2,)), pltpu.SemaphoreType.DMA((2,))],
        ),
        out_shape=jax.ShapeDtypeStruct((P,) + ROW_TILE, F32),
        compiler_params=_cparams(("arbitrary",)),
        name="moe_experts",
    )(blk_e, nused, nvalid, xs, w1, b1, w2, b2)


def _sc_gather_rows(table, idx):
    n = idx.shape[0]
    info = plsc.get_sparse_core_info()
    workers = info.num_cores * info.num_subcores
    per_worker = n // workers
    assert n % workers == 0 and per_worker % SC_INDEX_WINDOW == 0
    mesh = plsc.VectorSubcoreMesh(core_axis_name="c", subcore_axis_name="s")

    @functools.partial(
        pl.kernel, mesh=mesh,
        out_type=jax.ShapeDtypeStruct((n,) + table.shape[1:], table.dtype),
        scratch_types=[pltpu.VMEM((SC_INDEX_WINDOW,), I32),
                       pltpu.VMEM((SC_GATHER_ROWS,) + table.shape[1:], table.dtype),
                       pltpu.SemaphoreType.DMA])
    def gather(table_hbm, idx_hbm, out_hbm, idx_v, rows_v, sem):
        worker = lax.axis_index("s") * info.num_cores + lax.axis_index("c")
        first = worker * (per_worker // SC_INDEX_WINDOW)

        @pl.loop(0, per_worker // SC_INDEX_WINDOW)
        def _(j):
            window = first + j
            pltpu.sync_copy(idx_hbm.at[window], idx_v)
            for c in range(SC_INDEX_WINDOW // SC_GATHER_ROWS):
                pltpu.async_copy(table_hbm.at[idx_v.at[pl.ds(c * SC_GATHER_ROWS, SC_GATHER_ROWS)]], rows_v, sem).wait()
                pltpu.sync_copy(rows_v, out_hbm.at[pl.ds(window * SC_INDEX_WINDOW + c * SC_GATHER_ROWS,
                                                         SC_GATHER_ROWS)])

    return gather(table, idx.reshape(n // SC_INDEX_WINDOW, SC_INDEX_WINDOW))


def _finish_kernel(rows_ref, tg_ref, x1_ref, gfin_ref, y_ref, buf, sem, *, final):
    i = pl.program_id(0)
    n = pl.num_programs(0)
    tm = x1_ref.shape[0]
    slot = i % 2

    def loads(tile, s):
        cps = []
        for k in range(TOP_K):
            start = pl.multiple_of(k * (n * tm) + tile * tm, tm)
            cps += _relayout_copies(buf.at[s, k], rows_ref.at[pl.ds(start, tm)], sem.at[s], to_tiled=False)
        return cps

    @pl.when(i == 0)
    def _():
        for cp in loads(0, 0):
            cp.start()

    @pl.when(i + 1 < n)
    def _():
        for cp in loads(i + 1, 1 - slot):
            cp.start()

    for cp in loads(i, slot):
        cp.wait()
    gates = tg_ref[...].T
    x = x1_ref[...]
    for k in range(TOP_K):
        x = x + buf[slot, k] * gates[:, k:k + 1]
    y_ref[...] = _rms(x, gfin_ref[...]) if final else x


def _finish(rows, tg, x1, gfin, final, tm=256):
    T, D = x1.shape
    return pl.pallas_call(
        functools.partial(_finish_kernel, final=final),
        grid=(T // tm,),
        in_specs=[pl.BlockSpec(memory_space=pl.ANY),
                  pl.BlockSpec((SUBLANES, tm), lambda i: (0, i)),
                  pl.BlockSpec((tm, D), lambda i: (i, 0)),
                  _full(gfin.shape)],
        out_specs=pl.BlockSpec((tm, D), lambda i: (i, 0)),
        out_shape=jax.ShapeDtypeStruct((T, D), F32),
        scratch_shapes=[pltpu.VMEM((2, TOP_K, tm, D), F32), pltpu.SemaphoreType.DMA((2,))],
        compiler_params=_cparams(("arbitrary",)),
        name="moe_finish",
    )(rows, tg, x1, gfin)


def _moe(h2, ti, rk, tg, cnt, x1, moe_w, gfin, final):
    T = x1.shape[0]
    bm = EXPERT_ROWS
    n_assign = T * TOP_K
    nb = -(-(n_assign + N_EXPERTS * (bm - 1)) // bm)
    counts = cnt[:, 0].astype(I32)
    padded = (counts + bm - 1) // bm * bm
    pad_end = jnp.cumsum(padded).astype(I32)
    pad_start = pad_end - padded
    experts = jnp.arange(N_EXPERTS, dtype=I32)[:, None, None]
    dest = rk + jnp.sum(jnp.where(ti[None] == experts, pad_start[:, None, None], 0), axis=0)
    nused = (pad_end[-1:] // bm).astype(I32)
    blk_start = jnp.minimum(jnp.arange(nb, dtype=I32), nused[0] - 1) * bm
    blk_e = jnp.minimum(jnp.sum((pad_end[None, :] <= blk_start[:, None]).astype(I32), axis=1), N_EXPERTS - 1)
    live_end = pad_start + counts
    nvalid = jnp.clip(live_end[blk_e] - jnp.arange(nb, dtype=I32) * bm, 0, bm).astype(I32)
    xs = _sc_scatter_rows(h2, dest, nb * bm)
    ys = _experts(xs, blk_e, nused, nvalid, *moe_w)
    rows = _sc_gather_rows(ys, dest.reshape(-1))
    return _finish(rows, tg, x1, gfin, final)


def _split_in_proj(w):
    sizes = (DA_QK_WIDTH, DA_QK_WIDTH, DA_WIDTH, GLA_KEY_WIDTH, GLA_KEY_WIDTH, GLA_WIDTH, GLA_WIDTH,
             GLA_GATE_RANK, GLA_GATE_RANK, D_MODEL, D_MODEL)
    assert w.shape[-1] == sum(sizes)
    out, o = [], 0
    for s in sizes:
        out.append(w[:, o:o + s])
        o += s
    return out


def _prep_layer(l, norm_mix_g, w_in, lambda_q1, lambda_k1, lambda_q2, lambda_k2, diff_subln_g,
                gla_gate_up_f, gla_gate_bias_f, gla_gate_up_b, gla_gate_bias_b, gla_norm_g,
                w_proj_a, w_proj_b, w_out, norm_ffn_g, router_w, router_b, moe_w1, moe_b1, moe_w2, moe_b2):
    dq, dk, dv, lq, lk, lv, lr, laf, lab, ga, gb = _split_in_proj(w_in[l])
    bf = lambda a: a.astype(BF16)
    r = GLA_GATE_RANK
    gup = jnp.zeros((LANES, 2 * GLA_KEY_WIDTH), F32)
    gup = gup.at[0:r, 0:GLA_KEY_WIDTH].set(gla_gate_up_f[l]).at[r:2 * r, GLA_KEY_WIDTH:].set(gla_gate_up_b[l])
    lamv = jnp.zeros((SUBLANES, LANES), F32)
    for i, v in enumerate((lambda_q1, lambda_k1, lambda_q2, lambda_k2)):
        lamv = lamv.at[i, 0:DA_HEAD_DIM].set(v[l].astype(F32))
    inproj = dict(
        norm_g=norm_mix_g[l][None, :],
        wqk=bf(jnp.concatenate([dq * (DA_HEAD_DIM ** -0.5), dk], 1)),
        wvt=bf(dv.T),
        wlqk=bf(jnp.concatenate([lq, lk], 1)),
        wlvr=bf(jnp.concatenate([lv, lr], 1)),
        wg=bf(jnp.concatenate([ga, gb], 1)),
        wlow=bf(jnp.concatenate([laf, lab, jnp.zeros((D_MODEL, LANES - 2 * r), F32)], 1)),
        gup=bf(gup),
        gbias=jnp.concatenate([gla_gate_bias_f[l], gla_gate_bias_b[l]])[None, :],
    )
    merge = dict(
        subg=diff_subln_g[l][None, :], glag=gla_norm_g[l][None, :],
        wpa=bf(w_proj_a[l]), wpb=bf(w_proj_b[l]), wout=bf(w_out[l]),
        nffn=norm_ffn_g[l][None, :], rwt=bf(router_w[l].T), rb=router_b[l][:, None],
    )
    b1 = moe_b1[l].reshape(N_EXPERTS, -1, LANES, 2).transpose(0, 1, 3, 2).reshape(N_EXPERTS, 1, -1)
    moe_w = (_deinterleave_w1(moe_w1[l]), b1, bf(moe_w2[l]), moe_b2[l][:, None, :])
    return inproj, lamv, merge, moe_w


def _trunk(x, layers, norm_final_g):
    B, L, D = x.shape
    T = B * L
    cos_t, sin_t = _rope_tables(L)
    x2 = x.reshape(T, D)
    gfin = norm_final_g[None, :]
    for l, (inproj, lamv, merge, moe_w) in enumerate(layers):
        lam_init = 0.8 - 0.6 * math.exp(-0.3 * l)
        qk, vt, lqk, lvr, gfb, sg = _inproj(x2, L, cos_t=cos_t, sin_t=sin_t, **inproj)
        oa = _diff_attention(qk.reshape(B, L, -1), vt, lamv, lam_init)
        of, ob = _gla(lqk.reshape(B, L, -1), lvr.reshape(B, L, -1), gfb.reshape(B, L, -1))
        x1, h2, ti, rk, tg, cnt = _merge(x2, oa.reshape(T, -1), of.reshape(T, -1), ob.reshape(T, -1),
                                         lvr, sg, lam_init=lam_init, **merge)
        x2 = _moe(h2, ti, rk, tg, cnt, x1, moe_w, gfin, final=(l == len(layers) - 1))
    return x2.reshape(B, L, D)


def kernel(x_prompt, x_sample, norm_mix_g, w_in, lambda_q1, lambda_k1, lambda_q2, lambda_k2, diff_subln_g,
           gla_gate_up_f, gla_gate_bias_f, gla_gate_up_b, gla_gate_bias_b, gla_norm_g, w_proj_a, w_proj_b,
           w_out, norm_ffn_g, router_w, router_b, moe_w1, moe_b1, moe_w2, moe_b2, norm_final_g):
    depth = w_in.shape[0]
    layers = [_prep_layer(l, norm_mix_g, w_in, lambda_q1, lambda_k1, lambda_q2, lambda_k2, diff_subln_g,
                          gla_gate_up_f, gla_gate_bias_f, gla_gate_up_b, gla_gate_bias_b, gla_norm_g,
                          w_proj_a, w_proj_b, w_out, norm_ffn_g, router_w, router_b,
                          moe_w1, moe_b1, moe_w2, moe_b2) for l in range(depth)]
    return _trunk(x_prompt, layers, norm_final_g), _trunk(x_sample, layers, norm_final_g)
```

```python
import functools
import math

import jax
import jax.numpy as jnp
from jax import lax
from jax.experimental import pallas as pl
from jax.experimental.pallas import tpu as pltpu
from jax.experimental.pallas import tpu_sc as plsc

F32 = jnp.float32
BF16 = jnp.bfloat16
I32 = jnp.int32

D_MODEL = 1024
DA_HEADS = 4
DA_HEAD_DIM = 64
DA_VDIM = 128
DA_QK_WIDTH = DA_HEADS * 2 * DA_HEAD_DIM
DA_WIDTH = DA_HEADS * DA_VDIM
ROPE_THETA = 10000.0
GLA_HEADS = 4
GLA_KDIM = 64
GLA_VDIM = 128
GLA_KEY_WIDTH = GLA_HEADS * GLA_KDIM
GLA_WIDTH = GLA_HEADS * GLA_VDIM
GLA_GATE_RANK = 16
GLA_GATE_NORMALIZER = 16.0
GLA_CHUNK = 64
N_EXPERTS = 32
TOP_K = 4
D_FF = D_MODEL
SWIGLU_ALPHA = 1.702
SWIGLU_LIMIT = 7.0
NORM_EPS = 1e-5
LANES = 128
SUBLANES = 8

BF16_SUBLANES = 16
MXU_TILE = 256
LOG2_E = 1.4426950408889634
VT_ROWS = DA_VDIM + BF16_SUBLANES

VMEM_LIMIT = 56 * 1024 * 1024

TOKEN_TILE = 512
ATTN_KEY_CHUNK = 512
EXPERT_ROWS = 512

NT_DIMS = (((1,), (1,)), ((), ()))
TN_DIMS = (((0,), (0,)), ((), ()))


def _cparams(sem):
    return pltpu.CompilerParams(dimension_semantics=sem, vmem_limit_bytes=VMEM_LIMIT)


def _full(shape):
    nd = len(shape)
    return pl.BlockSpec(shape, lambda *_: (0,) * nd)


def _rms(x, g):
    return x * lax.rsqrt(jnp.mean(x * x, axis=-1, keepdims=True) + NORM_EPS) * g


ROW_TILE = (SUBLANES, LANES)


def _relayout_copies(flat_ref, tiled_ref, sem, to_tiled):
    cps = []
    for c in range(SUBLANES):
        flat = flat_ref.at[:, pl.ds(c * LANES, LANES)]
        tiled = tiled_ref.at[:, c, :]
        cps.append(pltpu.make_async_copy(flat, tiled, sem) if to_tiled else pltpu.make_async_copy(tiled, flat, sem))
    return cps


def _log_sigmoid(x):
    return jnp.minimum(x, 0.0) - jnp.log1p(jnp.exp(-jnp.abs(x)))


def _inproj_kernel(x_ref, g_ref, cos_ref, sin_ref, wqk_ref, wvt_ref, wlqk_ref, wlvr_ref,
                   wg_ref, wlow_ref, gup_ref, gbias_ref,
                   qk_ref, vt_ref, lqk_ref, lvr_ref, gfb_ref, sg_ref):
    h = _rms(x_ref[...], g_ref[...]).astype(BF16)

    cos = cos_ref[...]
    sin_signed = sin_ref[...]
    first_half = (lax.broadcasted_iota(I32, (1, LANES), 1) % DA_HEAD_DIM) < (DA_HEAD_DIM // 2)
    for c in range(2 * DA_QK_WIDTH // LANES):
        z = jnp.dot(h, wqk_ref[:, c * LANES:(c + 1) * LANES], preferred_element_type=F32)
        if c < DA_QK_WIDTH // LANES:
            z = z * LOG2_E
        partner = jnp.where(first_half, pltpu.roll(z, LANES - DA_HEAD_DIM // 2, 1),
                            pltpu.roll(z, DA_HEAD_DIM // 2, 1))
        qk_ref[:, c * LANES:(c + 1) * LANES] = (z * cos + partner * sin_signed).astype(BF16)

    vt = lax.dot_general(wvt_ref[...], h, NT_DIMS, preferred_element_type=F32).astype(BF16)
    for hd in range(DA_HEADS):
        vt_ref[0, hd * VT_ROWS:hd * VT_ROWS + DA_VDIM, :] = vt[hd * DA_VDIM:(hd + 1) * DA_VDIM]
        vt_ref[0, hd * VT_ROWS + DA_VDIM:(hd + 1) * VT_ROWS, :] = jnp.ones((VT_ROWS - DA_VDIM, vt.shape[1]), BF16)
    lqk_ref[...] = jnp.dot(h, wlqk_ref[...], preferred_element_type=F32)
    lvr_ref[...] = jnp.dot(h, wlvr_ref[...], preferred_element_type=F32).astype(BF16)
    sg_ref[...] = jax.nn.sigmoid(jnp.dot(h, wg_ref[...], preferred_element_type=F32)).astype(BF16)
    low = jnp.dot(h, wlow_ref[...], preferred_element_type=F32).astype(BF16)
    pre = jnp.dot(low, gup_ref[...], preferred_element_type=F32) + gbias_ref[...]
    gfb_ref[...] = _log_sigmoid(pre) * (1.0 / GLA_GATE_NORMALIZER)


def _inproj(x2, seq_len, norm_g, wqk, wvt, wlqk, wlvr, wg, wlow, gup, gbias, cos_t, sin_t):
    T, D = x2.shape
    tm = TOKEN_TILE
    assert T % tm == 0 and seq_len % tm == 0
    nl = seq_len // tm
    row = lambda n: pl.BlockSpec((tm, n), lambda i: (i, 0))
    out_shape = (
        jax.ShapeDtypeStruct((T, 2 * DA_QK_WIDTH), BF16),
        jax.ShapeDtypeStruct((T // tm, DA_HEADS * VT_ROWS, tm), BF16),
        jax.ShapeDtypeStruct((T, 2 * GLA_KEY_WIDTH), F32),
        jax.ShapeDtypeStruct((T, 2 * GLA_WIDTH), BF16),
        jax.ShapeDtypeStruct((T, 2 * GLA_KEY_WIDTH), F32),
        jax.ShapeDtypeStruct((T, 2 * D_MODEL), BF16),
    )
    return pl.pallas_call(
        _inproj_kernel,
        grid=(T // tm,),
        in_specs=[row(D), _full(norm_g.shape),
                  pl.BlockSpec((tm, LANES), lambda i: (i % nl, 0)),
                  pl.BlockSpec((tm, LANES), lambda i: (i % nl, 0)),
                  _full(wqk.shape), _full(wvt.shape), _full(wlqk.shape), _full(wlvr.shape),
                  _full(wg.shape), _full(wlow.shape), _full(gup.shape), _full(gbias.shape)],
        out_specs=(row(2 * DA_QK_WIDTH), pl.BlockSpec((1, DA_HEADS * VT_ROWS, tm), lambda i: (i, 0, 0)),
                   row(2 * GLA_KEY_WIDTH), row(2 * GLA_WIDTH), row(2 * GLA_KEY_WIDTH), row(2 * D_MODEL)),
        out_shape=out_shape,
        compiler_params=_cparams(("parallel",)),
        name="inproj",
    )(x2, norm_g, cos_t, sin_t, wqk, wvt, wlqk, wlvr, wg, wlow, gup, gbias)


def _rope_tables(seq_len):
    d = DA_HEAD_DIM
    inv = ROPE_THETA ** (-jnp.arange(0, d, 2, dtype=F32) / d)
    ang = jnp.arange(seq_len, dtype=F32)[:, None] * inv[None, :]
    cos = jnp.concatenate([jnp.cos(ang)] * (2 * LANES // d), -1)
    sin = jnp.sin(ang)
    sin_signed = jnp.concatenate([-sin, sin] * (LANES // d), -1)
    return cos, sin_signed


def _attn_kernel(lamv_ref, q_ref, k_ref, vt_ref, o_ref, acc_sc, sa_sc, sb_sc, *, lam_init):
    n_vt, _, vt_w = vt_ref.shape
    tk = ATTN_KEY_CHUNK
    nk = n_vt * vt_w // tk
    tq = q_ref.shape[1]
    q = q_ref[0]
    lane = lax.broadcasted_iota(I32, (1, LANES), 1)
    zeros = jnp.zeros_like(q)
    qms = (jnp.where(lane < DA_HEAD_DIM, q, zeros), jnp.where(lane >= DA_HEAD_DIM, q, zeros))
    acc_sc[...] = jnp.zeros(acc_sc.shape, F32)

    def scores(j, dst_ref):
        kj = k_ref[0, j * tk:(j + 1) * tk, :]
        col_max = []
        for mp in range(2):
            st = lax.dot_general(kj, qms[mp], NT_DIMS, preferred_element_type=F32)
            dst_ref[mp] = st
            col_max.append(jnp.max(st, axis=0, keepdims=True))
        return col_max

    def softmax_pv(j, src_ref, ms, col_max):
        def v_tile(r):
            key = j * tk + r
            return vt_ref[key // vt_w, :, key % vt_w:key % vt_w + MXU_TILE]

        new = []
        for mp in range(2):
            m_new = jnp.maximum(ms[mp], col_max[mp])
            alpha = jnp.exp2(ms[mp] - m_new)
            for cols in (slice(c, c + MXU_TILE) for c in range(0, tq, MXU_TILE)):
                part = None
                for r in range(0, tk, MXU_TILE):
                    p = jnp.exp2((src_ref[mp, r:r + MXU_TILE, cols] - m_new[:, cols]).astype(BF16))
                    d = jnp.dot(v_tile(r), p, preferred_element_type=F32)
                    part = d if part is None else part + d
                acc_sc[mp, :, cols] = acc_sc[mp, :, cols] * alpha[:, cols] + part
            new.append(m_new)
        return tuple(new)

    bufs = (sa_sc, sb_sc)
    col_max = scores(0, bufs[0])
    ms = tuple(jnp.full((1, tq), -jnp.inf, F32) for _ in range(2))
    for j in range(nk):
        nxt_max = scores(j + 1, bufs[(j + 1) % 2]) if j + 1 < nk else None
        ms = softmax_pv(j, bufs[j % 2], ms, col_max)
        col_max = nxt_max

    lv = lamv_ref[...]
    lam = (jnp.exp(jnp.sum(lv[0:1] * lv[1:2], axis=-1, keepdims=True))
           - jnp.exp(jnp.sum(lv[2:3] * lv[3:4], axis=-1, keepdims=True)) + lam_init)
    a0, a1 = acc_sc[0], acc_sc[1]
    o_ref[0] = (a0[:DA_VDIM] / a0[DA_VDIM:DA_VDIM + 1] - lam * (a1[:DA_VDIM] / a1[DA_VDIM:DA_VDIM + 1])).T


def _diff_attention(qk3, vt3, lamv, lam_init, tq=512):
    B, L, _ = qk3.shape
    vt_w = vt3.shape[-1]
    tk = ATTN_KEY_CHUNK
    assert L % tk == 0 and tk % vt_w == 0
    return pl.pallas_call(
        functools.partial(_attn_kernel, lam_init=lam_init),
        grid=(B, DA_HEADS, L // tq),
        in_specs=[_full(lamv.shape),
                  pl.BlockSpec((1, tq, LANES), lambda b, h, i: (b, i, h)),
                  pl.BlockSpec((1, L, LANES), lambda b, h, i: (b, 0, DA_HEADS + h)),
                  pl.BlockSpec((L // vt_w, VT_ROWS, vt_w), lambda b, h, i: (b, h, 0))],
        out_specs=pl.BlockSpec((1, tq, DA_VDIM), lambda b, h, i: (b, i, h)),
        out_shape=jax.ShapeDtypeStruct((B, L, DA_WIDTH), F32),
        scratch_shapes=[pltpu.VMEM((2, VT_ROWS, tq), F32), pltpu.VMEM((2, tk, tq), F32),
                        pltpu.VMEM((2, tk, tq), F32)],
        compiler_params=_cparams(("parallel", "parallel", "arbitrary")),
        name="diff_attention",
    )(lamv, qk3, qk3, vt3)


def _split3_bf16(x):
    hi = x.astype(BF16)
    r1 = x - hi.astype(F32)
    mid = r1.astype(BF16)
    return hi, mid, (r1 - mid.astype(F32)).astype(BF16)


def _gla_local(qk_ref, vr_ref, g_ref, *, reverse):
    C = GLA_CHUNK
    tb = qk_ref.shape[1]
    nchunk = tb // C
    row = lax.broadcasted_iota(I32, (tb, tb), 0)
    col = lax.broadcasted_iota(I32, (tb, tb), 1)
    same_chunk = (row // C) == (col // C)
    tri = same_chunk & ((col >= row) if reverse else (col <= row))
    tri_b = jnp.where(tri, 1.0, 0.0).astype(BF16)
    lane = lax.broadcasted_iota(I32, (1, LANES), 1)
    goff = GLA_KEY_WIDTH if reverse else 0

    g = g_ref[0, :, goff:goff + GLA_KEY_WIDTH]
    q = qk_ref[0, :, 0:GLA_KEY_WIDTH] * (GLA_KDIM ** -0.5)
    k = qk_ref[0, :, GLA_KEY_WIDTH:2 * GLA_KEY_WIDTH]
    v = vr_ref[0]
    b = sum(jnp.dot(tri_b, part, preferred_element_type=F32) for part in _split3_bf16(g))
    last_row, ref_row = (0, C // 2) if reverse else (C - 1, C // 2 - 1)
    per_chunk = lambda r: jnp.concatenate(
        [jnp.broadcast_to(b[c * C + r:c * C + r + 1], (C, GLA_KEY_WIDTH)) for c in range(nchunk)], axis=0)
    b_last, b_ref = per_chunk(last_row), per_chunk(ref_row)
    q_in = q * jnp.exp(b - b_ref)
    k_in = (k * jnp.exp(b_ref - b)).astype(BF16)
    k_dec = k * jnp.exp(b_last - b)
    q_dec = q * jnp.exp(b)
    dec = [jnp.exp(b[c * C + last_row:c * C + last_row + 1]) for c in range(nchunk)]

    heads = []
    for h in range(GLA_HEADS):
        ps = slice((h // 2) * LANES, (h // 2 + 1) * LANES)
        keep = (lane < GLA_KDIM) if h % 2 == 0 else (lane >= GLA_KDIM)
        zero = jnp.zeros((tb, LANES), F32)
        heads.append(dict(
            ps=ps, vs=slice(h * GLA_VDIM, (h + 1) * GLA_VDIM),
            q_in=jnp.where(keep, q_in[:, ps], zero).astype(BF16),
            q_dec=jnp.where(keep, q_dec[:, ps], zero).astype(BF16),
            k_dec=jnp.where(keep, k_dec[:, ps], zero).astype(BF16)))
    scores = [lax.dot_general(hd["q_in"], k_in[:, hd["ps"]], NT_DIMS, preferred_element_type=F32) for hd in heads]
    kv_t = [[lax.dot_general(v[c * C:(c + 1) * C, hd["vs"]], hd["k_dec"][c * C:(c + 1) * C], TN_DIMS,
                             preferred_element_type=F32) for hd in heads] for c in range(nchunk)]
    intra = [jnp.dot(jnp.where(tri, s, 0.0).astype(BF16), v[:, hd["vs"]], preferred_element_type=F32)
             for s, hd in zip(scores, heads)]
    return heads, dec, kv_t, intra


def _gla_scan(heads, dec, kv_t, st_ref, *, reverse):
    nchunk = len(dec)
    state = [st_ref[h] for h in range(GLA_HEADS)]
    entering = [None] * nchunk
    for c in (range(nchunk - 1, -1, -1) if reverse else range(nchunk)):
        entering[c] = [s.astype(BF16) for s in state]
        state = [s * dec[c][:, hd["ps"]] + kv_t[c][h] for h, (s, hd) in enumerate(zip(state, heads))]
    for h in range(GLA_HEADS):
        st_ref[h] = state[h]
    return entering


def _gla_emit(heads, intra, entering, o_ref):
    C = GLA_CHUNK
    for c, states in enumerate(entering):
        rows = slice(c * C, (c + 1) * C)
        for hd, o_intra, s_t in zip(heads, intra, states):
            inter = lax.dot_general(hd["q_dec"][rows], s_t, NT_DIMS, preferred_element_type=F32)
            o_ref[0, rows, hd["vs"]] = o_intra[rows] + inter


def _gla_kernel(qkf_ref, vrf_ref, gf_ref, qkb_ref, vrb_ref, gb_ref, of_ref, ob_ref, sf_sc, sb_sc):
    @pl.when(pl.program_id(1) == 0)
    def _():
        sf_sc[...] = jnp.zeros(sf_sc.shape, F32)
        sb_sc[...] = jnp.zeros(sb_sc.shape, F32)

    hf, df, kf, inf = _gla_local(qkf_ref, vrf_ref, gf_ref, reverse=False)
    hb, db, kb, inb = _gla_local(qkb_ref, vrb_ref, gb_ref, reverse=True)
    ef = _gla_scan(hf, df, kf, sf_sc, reverse=False)
    eb = _gla_scan(hb, db, kb, sb_sc, reverse=True)
    _gla_emit(hf, inf, ef, of_ref)
    _gla_emit(hb, inb, eb, ob_ref)


def _gla(lqk3, lvr3, gfb3, tb=256):
    B, L, _ = lqk3.shape
    nb = L // tb
    fwd = lambda n: pl.BlockSpec((1, tb, n), lambda b, i: (b, i, 0))
    bwd = lambda n: pl.BlockSpec((1, tb, n), lambda b, i: (b, nb - 1 - i, 0))
    out = jax.ShapeDtypeStruct((B, L, GLA_WIDTH), F32)
    return pl.pallas_call(
        _gla_kernel,
        grid=(B, nb),
        in_specs=[fwd(2 * GLA_KEY_WIDTH), fwd(GLA_WIDTH), fwd(2 * GLA_KEY_WIDTH),
                  bwd(2 * GLA_KEY_WIDTH), bwd(GLA_WIDTH), bwd(2 * GLA_KEY_WIDTH)],
        out_specs=(fwd(GLA_WIDTH), bwd(GLA_WIDTH)),
        out_shape=(out, out),
        scratch_shapes=[pltpu.VMEM((GLA_HEADS, GLA_VDIM, LANES), F32),
                        pltpu.VMEM((GLA_HEADS, GLA_VDIM, LANES), F32)],
        compiler_params=_cparams(("parallel", "arbitrary")),
        name="gla",
    )(lqk3, lvr3, gfb3, lqk3, lvr3, gfb3)


def _head_norm(z, g):
    return jnp.concatenate(
        [_rms(z[:, h * LANES:(h + 1) * LANES], g) for h in range(z.shape[1] // LANES)], axis=1)


def _merge_kernel(x_ref, oa_ref, of_ref, ob_ref, r_ref, sg_ref, subg_ref, glag_ref, wpa_ref, wpb_ref,
                  wout_ref, nffn_ref, rwt_ref, rb_ref,
                  x1_ref, h2_ref, ti_ref, rk_ref, tg_ref, cnt_ref, carry_sc, h2_sc, h2_sem, *, lam_init):
    tm = x_ref.shape[0]
    step = pl.program_id(0)

    def h2_stores(tile):
        rows = h2_ref.at[pl.ds(pl.multiple_of(tile * tm, tm), tm)]
        return _relayout_copies(h2_sc, rows, h2_sem, to_tiled=True)

    @pl.when(step == 0)
    def _():
        carry_sc[...] = jnp.zeros(carry_sc.shape, F32)

    @pl.when(step > 0)
    def _():
        for cp in h2_stores(step - 1):
            cp.wait()

    oa = (_head_norm(oa_ref[...], subg_ref[...]) * (1.0 - lam_init)).astype(BF16)
    ob = _head_norm(of_ref[...] + ob_ref[...], glag_ref[...]) * jax.nn.silu(r_ref[...].astype(F32))
    pa = jnp.dot(oa, wpa_ref[...], preferred_element_type=F32)
    pb = jnp.dot(ob.astype(BF16), wpb_ref[...], preferred_element_type=F32)
    sg = sg_ref[...].astype(F32)
    merged = sg[:, :D_MODEL] * pa + sg[:, D_MODEL:] * pb
    x1 = x_ref[...] + jnp.dot(merged.astype(BF16), wout_ref[...], preferred_element_type=F32)
    x1_ref[...] = x1
    h2 = _rms(x1, nffn_ref[...])
    h2_sc[...] = h2
    for cp in h2_stores(step):
        cp.start()

    logits = lax.dot_general(rwt_ref[...], h2.astype(BF16), NT_DIMS, preferred_element_type=F32) + rb_ref[...]
    eidx = lax.broadcasted_iota(I32, logits.shape, 0)
    vals, sels = [], []
    for k in range(TOP_K):
        mk = jnp.max(logits, axis=0, keepdims=True)
        ik = jnp.min(jnp.where(logits == mk, eidx, N_EXPERTS), axis=0, keepdims=True)
        sel = eidx == ik
        logits = jnp.where(sel, -jnp.inf, logits)
        vals.append(mk)
        sels.append(sel)
        ti_ref[k:k + 1, :] = ik
    ex = [jnp.exp(v - vals[0]) for v in vals]
    denom = ex[0] + ex[1] + ex[2] + ex[3]
    for k in range(TOP_K):
        tg_ref[k:k + 1, :] = ex[k] / denom
    tg_ref[TOP_K:, :] = jnp.zeros((tg_ref.shape[0] - TOP_K, tm), F32)

    multi = (sels[0] | sels[1] | sels[2] | sels[3])
    multi_f = jnp.where(multi, 1.0, 0.0)
    before = lax.broadcasted_iota(I32, (tm, tm), 0) < lax.broadcasted_iota(I32, (tm, tm), 1)
    cum = jnp.dot(multi_f.astype(BF16), jnp.where(before, 1.0, 0.0).astype(BF16), preferred_element_type=F32)
    tot = carry_sc[:, 0:1] + cum
    for k in range(TOP_K):
        rk_ref[k:k + 1, :] = jnp.sum(jnp.where(sels[k], tot, 0.0), axis=0, keepdims=True).astype(I32)
    carry_sc[...] = carry_sc[...] + jnp.sum(multi_f, axis=1, keepdims=True)
    cnt_ref[...] = carry_sc[...]

    @pl.when(step == pl.num_programs(0) - 1)
    def _():
        for cp in h2_stores(step):
            cp.wait()


def _merge(x2, oa, of, ob, lvr, sg, subg, glag, wpa, wpb, wout, nffn, rwt, rb, lam_init):
    T, D = x2.shape
    tm = TOKEN_TILE
    row = lambda n: pl.BlockSpec((tm, n), lambda i: (i, 0))
    col = lambda n: pl.BlockSpec((n, tm), lambda i: (0, i))
    out_shape = (
        jax.ShapeDtypeStruct((T, D), F32),
        jax.ShapeDtypeStruct((T,) + ROW_TILE, F32),
        jax.ShapeDtypeStruct((TOP_K, T), I32),
        jax.ShapeDtypeStruct((TOP_K, T), I32),
        jax.ShapeDtypeStruct((SUBLANES, T), F32),
        jax.ShapeDtypeStruct((N_EXPERTS, LANES), F32),
    )
    return pl.pallas_call(
        functools.partial(_merge_kernel, lam_init=lam_init),
        grid=(T // tm,),
        in_specs=[row(D), row(DA_WIDTH), row(GLA_WIDTH), row(GLA_WIDTH),
                  pl.BlockSpec((tm, GLA_WIDTH), lambda i: (i, 1)), row(2 * D),
                  _full(subg.shape), _full(glag.shape), _full(wpa.shape), _full(wpb.shape),
                  _full(wout.shape), _full(nffn.shape), _full(rwt.shape), _full(rb.shape)],
        out_specs=(row(D), pl.BlockSpec(memory_space=pl.ANY),
                   col(TOP_K), col(TOP_K), col(SUBLANES), _full((N_EXPERTS, LANES))),
        out_shape=out_shape,
        scratch_shapes=[pltpu.VMEM((N_EXPERTS, LANES), F32), pltpu.VMEM((tm, D), F32),
                        pltpu.SemaphoreType.DMA(())],
        compiler_params=_cparams(("arbitrary",)),
        name="merge_router",
    )(x2, oa, of, ob, lvr, sg, subg, glag, wpa, wpb, wout, nffn, rwt, rb)


SC_INDEX_WINDOW = 128
SC_GATHER_ROWS = 16


def _sc_scatter_rows(rows, dest, n_slots):
    T = rows.shape[0]
    info = plsc.get_sparse_core_info()
    workers = info.num_cores * info.num_subcores
    per_worker = T // workers
    assert T % workers == 0 and per_worker % SC_INDEX_WINDOW == 0
    windows = T // SC_INDEX_WINDOW
    steps = SC_INDEX_WINDOW // SC_GATHER_ROWS
    mesh = plsc.VectorSubcoreMesh(core_axis_name="c", subcore_axis_name="s")

    @functools.partial(
        pl.kernel, mesh=mesh,
        out_type=jax.ShapeDtypeStruct((n_slots,) + rows.shape[1:], rows.dtype),
        scratch_types=[pltpu.VMEM((steps, SC_GATHER_ROWS), I32)] * TOP_K
        + [pltpu.VMEM((SC_GATHER_ROWS,) + rows.shape[1:], rows.dtype)] * 2 + [pltpu.SemaphoreType.DMA] * 4)
    def scatter(rows_hbm, dest_hbm, out_hbm, *scratch):
        idx_v, bufs = scratch[:TOP_K], scratch[TOP_K:TOP_K + 2]
        load_sem, send_sem = scratch[TOP_K + 2:TOP_K + 4], scratch[TOP_K + 4:TOP_K + 6]
        worker = lax.axis_index("s") * info.num_cores + lax.axis_index("c")
        first = worker * (per_worker // SC_INDEX_WINDOW)

        @pl.loop(0, per_worker // SC_INDEX_WINDOW)
        def _(j):
            window = first + j
            for k in range(TOP_K):
                pltpu.sync_copy(dest_hbm.at[k * windows + window], idx_v[k])

            def load(c):
                src = rows_hbm.at[pl.ds(window * SC_INDEX_WINDOW + c * SC_GATHER_ROWS, SC_GATHER_ROWS)]
                return pltpu.async_copy(src, bufs[c % 2], load_sem[c % 2])

            loads = {0: load(0)}
            sends = {}
            for c in range(steps):
                if c + 1 < steps:
                    for cp in sends.pop(c - 1, ()):
                        cp.wait()
                    loads[c + 1] = load(c + 1)
                loads.pop(c).wait()
                sends[c] = [pltpu.async_copy(bufs[c % 2], out_hbm.at[idx_v[k].at[c]], send_sem[c % 2])
                            for k in range(TOP_K)]
            for pending in sends.values():
                for cp in pending:
                    cp.wait()

    return scatter(rows, dest.reshape(TOP_K * windows, steps, SC_GATHER_ROWS))


def _deinterleave_kernel(w_ref, o_ref):
    n = 2 * LANES
    src = lax.broadcasted_iota(I32, (n, n), 0)
    dst = lax.broadcasted_iota(I32, (n, n), 1)
    perm = jnp.where(src == jnp.where(dst < LANES, 2 * dst, 2 * (dst - LANES) + 1), 1.0, 0.0).astype(BF16)
    for c in range(w_ref.shape[2] // n):
        w = w_ref[0, :, c * n:(c + 1) * n].astype(BF16)
        o_ref[0, :, c * n:(c + 1) * n] = jnp.dot(w, perm, preferred_element_type=F32).astype(BF16)


def _deinterleave_w1(w1, cols=512):
    E, K, N = w1.shape
    spec = pl.BlockSpec((1, K, cols), lambda e, j: (e, 0, j))
    return pl.pallas_call(
        _deinterleave_kernel,
        grid=(E, N // cols),
        in_specs=[spec],
        out_specs=spec,
        out_shape=jax.ShapeDtypeStruct(w1.shape, BF16),
        compiler_params=_cparams(("parallel", "parallel")),
        name="moe_w1_layout",
    )(w1)


def _expert_kernel(blk_ref, nused_ref, nvalid_ref, xs_ref, w1_ref, b1_ref, w2_ref, b2_ref, ys_ref,
                   xbuf, ybuf, in_sem, out_sem):
    i = pl.program_id(0)
    n = pl.num_programs(0)
    bm = xbuf.shape[1]
    slot = i % 2

    def rows_of(ref, blk):
        return ref.at[pl.ds(pl.multiple_of(blk * bm, bm), bm)]

    def loads(blk, s):
        return _relayout_copies(xbuf.at[s], rows_of(xs_ref, blk), in_sem.at[s], to_tiled=False)

    def stores(blk, s):
        return _relayout_copies(ybuf.at[s], rows_of(ys_ref, blk), out_sem.at[s], to_tiled=True)

    @pl.when(i == 0)
    def _():
        for cp in loads(0, 0):
            cp.start()

    @pl.when(i + 1 < n)
    def _():
        for cp in loads(i + 1, 1 - slot):
            cp.start()

    for cp in loads(i, slot):
        cp.wait()

    @pl.when(i >= 2)
    def _():
        for cp in stores(i - 2, slot):
            cp.wait()

    @pl.when(i < nused_ref[0])
    def _():
        live = lax.broadcasted_iota(I32, (bm, 1), 0) < nvalid_ref[i]
        x = jnp.where(live, xbuf[slot], 0.0).astype(BF16)
        step = 2 * MXU_TILE

        def hidden(c):
            cols = slice(c * step, (c + 1) * step)
            return jnp.dot(x, w1_ref[0, :, cols], preferred_element_type=F32) + b1_ref[0, :, cols]

        def down(c, hid):
            acts = []
            for o in range(0, step, 2 * LANES):
                gate = jnp.minimum(hid[:, o:o + LANES], SWIGLU_LIMIT)
                up = jnp.clip(hid[:, o + LANES:o + 2 * LANES], -SWIGLU_LIMIT, SWIGLU_LIMIT)
                acts.append(((up + 1.0) * (gate * jax.nn.sigmoid(SWIGLU_ALPHA * gate))).astype(BF16))
            rows = slice(c * MXU_TILE, (c + 1) * MXU_TILE)
            return jnp.dot(jnp.concatenate(acts, axis=1), w2_ref[0, rows, :], preferred_element_type=F32)

        n_piece = w1_ref.shape[2] // step
        y = b2_ref[0]
        hid = hidden(0)
        for c in range(n_piece):
            nxt = hidden(c + 1) if c + 1 < n_piece else None
            y = y + down(c, hid)
            hid = nxt
        ybuf[slot] = y

    @pl.when(i >= nused_ref[0])
    def _():
        ybuf[slot] = jnp.zeros(ybuf.shape[1:], F32)

    for cp in stores(i, slot):
        cp.start()

    @pl.when(i == n - 1)
    def _():
        for cp in stores(i, slot):
            cp.wait()

    @pl.when((i == n - 1) & (n >= 2))
    def _():
        for cp in stores(i - 1, 1 - slot):
            cp.wait()


def _experts(xs, blk_e, nused, nvalid, w1, b1, w2, b2):
    P = xs.shape[0]
    D = w1.shape[1]
    bm = EXPERT_ROWS
    nb = P // bm
    per_expert = lambda a: pl.BlockSpec((1,) + a.shape[1:], lambda i, blk, nu, nv: (blk[i], 0, 0))
    return pl.pallas_call(
        _expert_kernel,
        grid_spec=pltpu.PrefetchScalarGridSpec(
            num_scalar_prefetch=3,
            grid=(nb,),
            in_specs=[pl.BlockSpec(memory_space=pl.ANY), per_expert(w1), per_expert(b1), per_expert(w2),
                      per_expert(b2)],
            out_specs=pl.BlockSpec(memory_space=pl.ANY),
            scratch_shapes=[pltpu.VMEM((2, bm, D), F32), pltpu.VMEM((2, bm, D), F32),
                            pltpu.SemaphoreType.DMA((2,)), pltpu.SemaphoreType.DMA((2,))],
        ),
        out_shape=jax.ShapeDtypeStruct((P,) + ROW_TILE, F32),
        compiler_params=_cparams(("arbitrary",)),
        name="moe_experts",
    )(blk_e, nused, nvalid, xs, w1, b1, w2, b2)


def _sc_gather_rows(table, idx):
    n = idx.shape[0]
    info = plsc.get_sparse_core_info()
    workers = info.num_cores * info.num_subcores
    per_worker = n // workers
    assert n % workers == 0 and per_worker % SC_INDEX_WINDOW == 0
    mesh = plsc.VectorSubcoreMesh(core_axis_name="c", subcore_axis_name="s")

    @functools.partial(
        pl.kernel, mesh=mesh,
        out_type=jax.ShapeDtypeStruct((n,) + table.shape[1:], table.dtype),
        scratch_types=[pltpu.VMEM((SC_INDEX_WINDOW,), I32)]
        + [pltpu.VMEM((SC_GATHER_ROWS,) + table.shape[1:], table.dtype)] * 2 + [pltpu.SemaphoreType.DMA] * 4)
    def gather(table_hbm, idx_hbm, out_hbm, idx_v, *scratch):
        bufs, fetch_sem, store_sem = scratch[0:2], scratch[2:4], scratch[4:6]
        steps = SC_INDEX_WINDOW // SC_GATHER_ROWS
        worker = lax.axis_index("s") * info.num_cores + lax.axis_index("c")
        first = worker * (per_worker // SC_INDEX_WINDOW)

        @pl.loop(0, per_worker // SC_INDEX_WINDOW)
        def _(j):
            window = first + j
            pltpu.sync_copy(idx_hbm.at[window], idx_v)

            def fetch(c):
                src = table_hbm.at[idx_v.at[pl.ds(c * SC_GATHER_ROWS, SC_GATHER_ROWS)]]
                return pltpu.async_copy(src, bufs[c % 2], fetch_sem[c % 2])

            fetches = {0: fetch(0)}
            stores = {}
            for c in range(steps):
                if c + 1 < steps:
                    if c - 1 in stores:
                        stores.pop(c - 1).wait()
                    fetches[c + 1] = fetch(c + 1)
                fetches.pop(c).wait()
                dst = out_hbm.at[pl.ds(window * SC_INDEX_WINDOW + c * SC_GATHER_ROWS, SC_GATHER_ROWS)]
                stores[c] = pltpu.async_copy(bufs[c % 2], dst, store_sem[c % 2])
            for cp in stores.values():
                cp.wait()

    return gather(table, idx.reshape(n // SC_INDEX_WINDOW, SC_INDEX_WINDOW))


def _finish_kernel(rows_ref, tg_ref, x1_ref, gfin_ref, y_ref, buf, sem, *, final):
    i = pl.program_id(0)
    n = pl.num_programs(0)
    tm = x1_ref.shape[0]
    slot = i % 2

    def loads(tile, s):
        cps = []
        for k in range(TOP_K):
            start = pl.multiple_of(k * (n * tm) + tile * tm, tm)
            cps += _relayout_copies(buf.at[s, k], rows_ref.at[pl.ds(start, tm)], sem.at[s], to_tiled=False)
        return cps

    @pl.when(i == 0)
    def _():
        for cp in loads(0, 0):
            cp.start()

    @pl.when(i + 1 < n)
    def _():
        for cp in loads(i + 1, 1 - slot):
            cp.start()

    for cp in loads(i, slot):
        cp.wait()
    gates = tg_ref[...].T
    x = x1_ref[...]
    for k in range(TOP_K):
        x = x + buf[slot, k] * gates[:, k:k + 1]
    y_ref[...] = _rms(x, gfin_ref[...]) if final else x


def _finish(rows, tg, x1, gfin, final, tm=256):
    T, D = x1.shape
    return pl.pallas_call(
        functools.partial(_finish_kernel, final=final),
        grid=(T // tm,),
        in_specs=[pl.BlockSpec(memory_space=pl.ANY),
                  pl.BlockSpec((SUBLANES, tm), lambda i: (0, i)),
                  pl.BlockSpec((tm, D), lambda i: (i, 0)),
                  _full(gfin.shape)],
        out_specs=pl.BlockSpec((tm, D), lambda i: (i, 0)),
        out_shape=jax.ShapeDtypeStruct((T, D), F32),
        scratch_shapes=[pltpu.VMEM((2, TOP_K, tm, D), F32), pltpu.SemaphoreType.DMA((2,))],
        compiler_params=_cparams(("arbitrary",)),
        name="moe_finish",
    )(rows, tg, x1, gfin)


def _moe(h2, ti, rk, tg, cnt, x1, moe_w, gfin, final):
    T = x1.shape[0]
    bm = EXPERT_ROWS
    n_assign = T * TOP_K
    nb = -(-(n_assign + N_EXPERTS * (bm - 1)) // bm)
    counts = cnt[:, 0].astype(I32)
    padded = (counts + bm - 1) // bm * bm
    pad_end = jnp.cumsum(padded).astype(I32)
    pad_start = pad_end - padded
    experts = jnp.arange(N_EXPERTS, dtype=I32)[:, None, None]
    dest = rk + jnp.sum(jnp.where(ti[None] == experts, pad_start[:, None, None], 0), axis=0)
    nused = (pad_end[-1:] // bm).astype(I32)
    blk_start = jnp.minimum(jnp.arange(nb, dtype=I32), nused[0] - 1) * bm
    blk_e = jnp.minimum(jnp.sum((pad_end[None, :] <= blk_start[:, None]).astype(I32), axis=1), N_EXPERTS - 1)
    live_end = pad_start + counts
    nvalid = jnp.clip(live_end[blk_e] - jnp.arange(nb, dtype=I32) * bm, 0, bm).astype(I32)
    xs = _sc_scatter_rows(h2, dest, nb * bm)
    ys = _experts(xs, blk_e, nused, nvalid, *moe_w)
    rows = _sc_gather_rows(ys, dest.reshape(-1))
    return _finish(rows, tg, x1, gfin, final)


def _split_in_proj(w):
    sizes = (DA_QK_WIDTH, DA_QK_WIDTH, DA_WIDTH, GLA_KEY_WIDTH, GLA_KEY_WIDTH, GLA_WIDTH, GLA_WIDTH,
             GLA_GATE_RANK, GLA_GATE_RANK, D_MODEL, D_MODEL)
    assert w.shape[-1] == sum(sizes)
    out, o = [], 0
    for s in sizes:
        out.append(w[:, o:o + s])
        o += s
    return out


def _prep_layer(l, norm_mix_g, w_in, lambda_q1, lambda_k1, lambda_q2, lambda_k2, diff_subln_g,
                gla_gate_up_f, gla_gate_bias_f, gla_gate_up_b, gla_gate_bias_b, gla_norm_g,
                w_proj_a, w_proj_b, w_out, norm_ffn_g, router_w, router_b, moe_w1, moe_b1, moe_w2, moe_b2):
    dq, dk, dv, lq, lk, lv, lr, laf, lab, ga, gb = _split_in_proj(w_in[l])
    bf = lambda a: a.astype(BF16)
    r = GLA_GATE_RANK
    gup = jnp.zeros((LANES, 2 * GLA_KEY_WIDTH), F32)
    gup = gup.at[0:r, 0:GLA_KEY_WIDTH].set(gla_gate_up_f[l]).at[r:2 * r, GLA_KEY_WIDTH:].set(gla_gate_up_b[l])
    lamv = jnp.zeros((SUBLANES, LANES), F32)
    for i, v in enumerate((lambda_q1, lambda_k1, lambda_q2, lambda_k2)):
        lamv = lamv.at[i, 0:DA_HEAD_DIM].set(v[l].astype(F32))
    inproj = dict(
        norm_g=norm_mix_g[l][None, :],
        wqk=bf(jnp.concatenate([dq * (DA_HEAD_DIM ** -0.5), dk], 1)),
        wvt=bf(dv.T),
        wlqk=bf(jnp.concatenate([lq, lk], 1)),
        wlvr=bf(jnp.concatenate([lv, lr], 1)),
        wg=bf(jnp.concatenate([ga, gb], 1)),
        wlow=bf(jnp.concatenate([laf, lab, jnp.zeros((D_MODEL, LANES - 2 * r), F32)], 1)),
        gup=bf(gup),
        gbias=jnp.concatenate([gla_gate_bias_f[l], gla_gate_bias_b[l]])[None, :],
    )
    merge = dict(
        subg=diff_subln_g[l][None, :], glag=gla_norm_g[l][None, :],
        wpa=bf(w_proj_a[l]), wpb=bf(w_proj_b[l]), wout=bf(w_out[l]),
        nffn=norm_ffn_g[l][None, :], rwt=bf(router_w[l].T), rb=router_b[l][:, None],
    )
    b1 = moe_b1[l].reshape(N_EXPERTS, -1, LANES, 2).transpose(0, 1, 3, 2).reshape(N_EXPERTS, 1, -1)
    moe_w = (_deinterleave_w1(moe_w1[l]), b1, bf(moe_w2[l]), moe_b2[l][:, None, :])
    return inproj, lamv, merge, moe_w


def _trunk(x, layers, norm_final_g):
    B, L, D = x.shape
    T = B * L
    cos_t, sin_t = _rope_tables(L)
    x2 = x.reshape(T, D)
    gfin = norm_final_g[None, :]
    for l, (inproj, lamv, merge, moe_w) in enumerate(layers):
        lam_init = 0.8 - 0.6 * math.exp(-0.3 * l)
        qk, vt, lqk, lvr, gfb, sg = _inproj(x2, L, cos_t=cos_t, sin_t=sin_t, **inproj)
        oa = _diff_attention(qk.reshape(B, L, -1), vt, lamv, lam_init)
        of, ob = _gla(lqk.reshape(B, L, -1), lvr.reshape(B, L, -1), gfb.reshape(B, L, -1))
        x1, h2, ti, rk, tg, cnt = _merge(x2, oa.reshape(T, -1), of.reshape(T, -1), ob.reshape(T, -1),
                                         lvr, sg, lam_init=lam_init, **merge)
        x2 = _moe(h2, ti, rk, tg, cnt, x1, moe_w, gfin, final=(l == len(layers) - 1))
    return x2.reshape(B, L, D)


def kernel(x_prompt, x_sample, norm_mix_g, w_in, lambda_q1, lambda_k1, lambda_q2, lambda_k2, diff_subln_g,
           gla_gate_up_f, gla_gate_bias_f, gla_gate_up_b, gla_gate_bias_b, gla_norm_g, w_proj_a, w_proj_b,
           w_out, norm_ffn_g, router_w, router_b, moe_w1, moe_b1, moe_w2, moe_b2, norm_final_g):
    depth = w_in.shape[0]
    layers = [_prep_layer(l, norm_mix_g, w_in, lambda_q1, lambda_k1, lambda_q2, lambda_k2, diff_subln_g,
                          gla_gate_up_f, gla_gate_bias_f, gla_gate_up_b, gla_gate_bias_b, gla_norm_g,
                          w_proj_a, w_proj_b, w_out, norm_ffn_g, router_w, router_b,
                          moe_w1, moe_b1, moe_w2, moe_b2) for l in range(depth)]
    return _trunk(x_prompt, layers, norm_final_g), _trunk(x_sample, layers, norm_final_g)
```

```python
import functools
import math

import jax
import jax.numpy as jnp
from jax import lax
from jax.experimental import pallas as pl
from jax.experimental.pallas import tpu as pltpu
from jax.experimental.pallas import tpu_sc as plsc

F32 = jnp.float32
BF16 = jnp.bfloat16
I32 = jnp.int32

D_MODEL = 1024
DA_HEADS = 4
DA_HEAD_DIM = 64
DA_VDIM = 128
DA_QK_WIDTH = DA_HEADS * 2 * DA_HEAD_DIM
DA_WIDTH = DA_HEADS * DA_VDIM
ROPE_THETA = 10000.0
GLA_HEADS = 4
GLA_KDIM = 64
GLA_VDIM = 128
GLA_KEY_WIDTH = GLA_HEADS * GLA_KDIM
GLA_WIDTH = GLA_HEADS * GLA_VDIM
GLA_GATE_RANK = 16
GLA_GATE_NORMALIZER = 16.0
GLA_CHUNK = 64
N_EXPERTS = 32
TOP_K = 4
D_FF = D_MODEL
SWIGLU_ALPHA = 1.702
SWIGLU_LIMIT = 7.0
NORM_EPS = 1e-5
LANES = 128
SUBLANES = 8

BF16_SUBLANES = 16
MXU_TILE = 256
LOG2_E = 1.4426950408889634
VT_ROWS = DA_VDIM + BF16_SUBLANES

VMEM_LIMIT = 56 * 1024 * 1024

TOKEN_TILE = 512
ATTN_KEY_CHUNK = 512
EXPERT_ROWS = 512

NT_DIMS = (((1,), (1,)), ((), ()))
TN_DIMS = (((0,), (0,)), ((), ()))


def _cparams(sem):
    return pltpu.CompilerParams(dimension_semantics=sem, vmem_limit_bytes=VMEM_LIMIT)


def _full(shape):
    nd = len(shape)
    return pl.BlockSpec(shape, lambda *_: (0,) * nd)


def _rms(x, g):
    return x * lax.rsqrt(jnp.mean(x * x, axis=-1, keepdims=True) + NORM_EPS) * g


ROW_TILE = (SUBLANES, LANES)


def _relayout_copies(flat_ref, tiled_ref, sem, to_tiled):
    cps = []
    for c in range(SUBLANES):
        flat = flat_ref.at[:, pl.ds(c * LANES, LANES)]
        tiled = tiled_ref.at[:, c, :]
        cps.append(pltpu.make_async_copy(flat, tiled, sem) if to_tiled else pltpu.make_async_copy(tiled, flat, sem))
    return cps


def _log_sigmoid(x):
    return jnp.minimum(x, 0.0) - jnp.log1p(jnp.exp(-jnp.abs(x)))


def _inproj_kernel(x_ref, g_ref, cos_ref, sin_ref, wqk_ref, wvt_ref, wlqk_ref, wlvr_ref,
                   wg_ref, wlow_ref, gup_ref, gbias_ref,
                   qk_ref, vt_ref, lqk_ref, lvr_ref, gfb_ref, sg_ref):
    h = _rms(x_ref[...], g_ref[...]).astype(BF16)

    cos = cos_ref[...]
    sin_signed = sin_ref[...]
    first_half = (lax.broadcasted_iota(I32, (1, LANES), 1) % DA_HEAD_DIM) < (DA_HEAD_DIM // 2)
    for c in range(2 * DA_QK_WIDTH // LANES):
        z = jnp.dot(h, wqk_ref[:, c * LANES:(c + 1) * LANES], preferred_element_type=F32)
        if c < DA_QK_WIDTH // LANES:
            z = z * LOG2_E
        partner = jnp.where(first_half, pltpu.roll(z, LANES - DA_HEAD_DIM // 2, 1),
                            pltpu.roll(z, DA_HEAD_DIM // 2, 1))
        qk_ref[:, c * LANES:(c + 1) * LANES] = (z * cos + partner * sin_signed).astype(BF16)

    vt = lax.dot_general(wvt_ref[...], h, NT_DIMS, preferred_element_type=F32).astype(BF16)
    for hd in range(DA_HEADS):
        vt_ref[0, hd * VT_ROWS:hd * VT_ROWS + DA_VDIM, :] = vt[hd * DA_VDIM:(hd + 1) * DA_VDIM]
        vt_ref[0, hd * VT_ROWS + DA_VDIM:(hd + 1) * VT_ROWS, :] = jnp.ones((VT_ROWS - DA_VDIM, vt.shape[1]), BF16)
    lqk_ref[...] = jnp.dot(h, wlqk_ref[...], preferred_element_type=F32)
    lvr_ref[...] = jnp.dot(h, wlvr_ref[...], preferred_element_type=F32).astype(BF16)
    sg_ref[...] = jax.nn.sigmoid(jnp.dot(h, wg_ref[...], preferred_element_type=F32)).astype(BF16)
    low = jnp.dot(h, wlow_ref[...], preferred_element_type=F32).astype(BF16)
    pre = jnp.dot(low, gup_ref[...], preferred_element_type=F32) + gbias_ref[...]
    gfb_ref[...] = _log_sigmoid(pre) * (1.0 / GLA_GATE_NORMALIZER)


def _inproj(x2, seq_len, norm_g, wqk, wvt, wlqk, wlvr, wg, wlow, gup, gbias, cos_t, sin_t):
    T, D = x2.shape
    tm = TOKEN_TILE
    assert T % tm == 0 and seq_len % tm == 0
    nl = seq_len // tm
    row = lambda n: pl.BlockSpec((tm, n), lambda i: (i, 0))
    out_shape = (
        jax.ShapeDtypeStruct((T, 2 * DA_QK_WIDTH), BF16),
        jax.ShapeDtypeStruct((T // tm, DA_HEADS * VT_ROWS, tm), BF16),
        jax.ShapeDtypeStruct((T, 2 * GLA_KEY_WIDTH), F32),
        jax.ShapeDtypeStruct((T, 2 * GLA_WIDTH), BF16),
        jax.ShapeDtypeStruct((T, 2 * GLA_KEY_WIDTH), F32),
        jax.ShapeDtypeStruct((T, 2 * D_MODEL), BF16),
    )
    return pl.pallas_call(
        _inproj_kernel,
        grid=(T // tm,),
        in_specs=[row(D), _full(norm_g.shape),
                  pl.BlockSpec((tm, LANES), lambda i: (i % nl, 0)),
                  pl.BlockSpec((tm, LANES), lambda i: (i % nl, 0)),
                  _full(wqk.shape), _full(wvt.shape), _full(wlqk.shape), _full(wlvr.shape),
                  _full(wg.shape), _full(wlow.shape), _full(gup.shape), _full(gbias.shape)],
        out_specs=(row(2 * DA_QK_WIDTH), pl.BlockSpec((1, DA_HEADS * VT_ROWS, tm), lambda i: (i, 0, 0)),
                   row(2 * GLA_KEY_WIDTH), row(2 * GLA_WIDTH), row(2 * GLA_KEY_WIDTH), row(2 * D_MODEL)),
        out_shape=out_shape,
        compiler_params=_cparams(("parallel",)),
        name="inproj",
    )(x2, norm_g, cos_t, sin_t, wqk, wvt, wlqk, wlvr, wg, wlow, gup, gbias)


def _rope_tables(seq_len):
    d = DA_HEAD_DIM
    inv = ROPE_THETA ** (-jnp.arange(0, d, 2, dtype=F32) / d)
    ang = jnp.arange(seq_len, dtype=F32)[:, None] * inv[None, :]
    cos = jnp.concatenate([jnp.cos(ang)] * (2 * LANES // d), -1)
    sin = jnp.sin(ang)
    sin_signed = jnp.concatenate([-sin, sin] * (LANES // d), -1)
    return cos, sin_signed


def _attn_kernel(lamv_ref, q_ref, k_ref, vt_ref, o_ref, acc_sc, sa_sc, sb_sc, *, lam_init):
    n_vt, _, vt_w = vt_ref.shape
    tk = ATTN_KEY_CHUNK
    nk = n_vt * vt_w // tk
    tq = q_ref.shape[1]
    q = q_ref[0]
    lane = lax.broadcasted_iota(I32, (1, LANES), 1)
    zeros = jnp.zeros_like(q)
    qms = (jnp.where(lane < DA_HEAD_DIM, q, zeros), jnp.where(lane >= DA_HEAD_DIM, q, zeros))
    acc_sc[...] = jnp.zeros(acc_sc.shape, F32)

    def scores(j, dst_ref):
        kj = k_ref[0, j * tk:(j + 1) * tk, :]
        col_max = []
        for mp in range(2):
            st = lax.dot_general(kj, qms[mp], NT_DIMS, preferred_element_type=F32)
            dst_ref[mp] = st
            col_max.append(jnp.max(st, axis=0, keepdims=True))
        return col_max

    def softmax_pv(j, src_ref, ms, col_max):
        def v_tile(r):
            key = j * tk + r
            return vt_ref[key // vt_w, :, key % vt_w:key % vt_w + MXU_TILE]

        new = []
        for mp in range(2):
            m_new = jnp.maximum(ms[mp], col_max[mp])
            alpha = jnp.exp2(ms[mp] - m_new)
            for cols in (slice(c, c + MXU_TILE) for c in range(0, tq, MXU_TILE)):
                part = None
                for r in range(0, tk, MXU_TILE):
                    p = jnp.exp2((src_ref[mp, r:r + MXU_TILE, cols] - m_new[:, cols]).astype(BF16))
                    d = jnp.dot(v_tile(r), p, preferred_element_type=F32)
                    part = d if part is None else part + d
                acc_sc[mp, :, cols] = acc_sc[mp, :, cols] * alpha[:, cols] + part
            new.append(m_new)
        return tuple(new)

    bufs = (sa_sc, sb_sc)
    col_max = scores(0, bufs[0])
    ms = tuple(jnp.full((1, tq), -jnp.inf, F32) for _ in range(2))
    for j in range(nk):
        nxt_max = scores(j + 1, bufs[(j + 1) % 2]) if j + 1 < nk else None
        ms = softmax_pv(j, bufs[j % 2], ms, col_max)
        col_max = nxt_max

    lv = lamv_ref[...]
    lam = (jnp.exp(jnp.sum(lv[0:1] * lv[1:2], axis=-1, keepdims=True))
           - jnp.exp(jnp.sum(lv[2:3] * lv[3:4], axis=-1, keepdims=True)) + lam_init)
    a0, a1 = acc_sc[0], acc_sc[1]
    o_ref[0] = (a0[:DA_VDIM] / a0[DA_VDIM:DA_VDIM + 1] - lam * (a1[:DA_VDIM] / a1[DA_VDIM:DA_VDIM + 1])).T


def _diff_attention(qk3, vt3, lamv, lam_init, tq=512):
    B, L, _ = qk3.shape
    vt_w = vt3.shape[-1]
    tk = ATTN_KEY_CHUNK
    assert L % tk == 0 and tk % vt_w == 0
    return pl.pallas_call(
        functools.partial(_attn_kernel, lam_init=lam_init),
        grid=(B, DA_HEADS, L // tq),
        in_specs=[_full(lamv.shape),
                  pl.BlockSpec((1, tq, LANES), lambda b, h, i: (b, i, h)),
                  pl.BlockSpec((1, L, LANES), lambda b, h, i: (b, 0, DA_HEADS + h)),
                  pl.BlockSpec((L // vt_w, VT_ROWS, vt_w), lambda b, h, i: (b, h, 0))],
        out_specs=pl.BlockSpec((1, tq, DA_VDIM), lambda b, h, i: (b, i, h)),
        out_shape=jax.ShapeDtypeStruct((B, L, DA_WIDTH), F32),
        scratch_shapes=[pltpu.VMEM((2, VT_ROWS, tq), F32), pltpu.VMEM((2, tk, tq), F32),
                        pltpu.VMEM((2, tk, tq), F32)],
        compiler_params=_cparams(("parallel", "parallel", "arbitrary")),
        name="diff_attention",
    )(lamv, qk3, qk3, vt3)


def _split3_bf16(x):
    hi = x.astype(BF16)
    r1 = x - hi.astype(F32)
    mid = r1.astype(BF16)
    return hi, mid, (r1 - mid.astype(F32)).astype(BF16)


def _gla_local(qk_ref, vr_ref, g_ref, *, reverse):
    C = GLA_CHUNK
    tb = qk_ref.shape[1]
    nchunk = tb // C
    row = lax.broadcasted_iota(I32, (tb, tb), 0)
    col = lax.broadcasted_iota(I32, (tb, tb), 1)
    same_chunk = (row // C) == (col // C)
    tri = same_chunk & ((col >= row) if reverse else (col <= row))
    tri_b = jnp.where(tri, 1.0, 0.0).astype(BF16)
    lane = lax.broadcasted_iota(I32, (1, LANES), 1)
    goff = GLA_KEY_WIDTH if reverse else 0

    g = g_ref[0, :, goff:goff + GLA_KEY_WIDTH]
    q = qk_ref[0, :, 0:GLA_KEY_WIDTH] * (GLA_KDIM ** -0.5)
    k = qk_ref[0, :, GLA_KEY_WIDTH:2 * GLA_KEY_WIDTH]
    v = vr_ref[0]
    b = sum(jnp.dot(tri_b, part, preferred_element_type=F32) for part in _split3_bf16(g))
    last_row, ref_row = (0, C // 2) if reverse else (C - 1, C // 2 - 1)
    per_chunk = lambda r: jnp.concatenate(
        [jnp.broadcast_to(b[c * C + r:c * C + r + 1], (C, GLA_KEY_WIDTH)) for c in range(nchunk)], axis=0)
    b_last, b_ref = per_chunk(last_row), per_chunk(ref_row)
    q_in = q * jnp.exp(b - b_ref)
    k_in = (k * jnp.exp(b_ref - b)).astype(BF16)
    k_dec = k * jnp.exp(b_last - b)
    q_dec = q * jnp.exp(b)
    dec = [jnp.exp(b[c * C + last_row:c * C + last_row + 1]) for c in range(nchunk)]

    heads = []
    for h in range(GLA_HEADS):
        ps = slice((h // 2) * LANES, (h // 2 + 1) * LANES)
        keep = (lane < GLA_KDIM) if h % 2 == 0 else (lane >= GLA_KDIM)
        zero = jnp.zeros((tb, LANES), F32)
        heads.append(dict(
            ps=ps, vs=slice(h * GLA_VDIM, (h + 1) * GLA_VDIM),
            q_in=jnp.where(keep, q_in[:, ps], zero).astype(BF16),
            q_dec=jnp.where(keep, q_dec[:, ps], zero).astype(BF16),
            k_dec=jnp.where(keep, k_dec[:, ps], zero).astype(BF16)))
    scores = [lax.dot_general(hd["q_in"], k_in[:, hd["ps"]], NT_DIMS, preferred_element_type=F32) for hd in heads]
    kv_t = [[lax.dot_general(v[c * C:(c + 1) * C, hd["vs"]], hd["k_dec"][c * C:(c + 1) * C], TN_DIMS,
                             preferred_element_type=F32) for hd in heads] for c in range(nchunk)]
    intra = [jnp.dot(jnp.where(tri, s, 0.0).astype(BF16), v[:, hd["vs"]], preferred_element_type=F32)
             for s, hd in zip(scores, heads)]
    return heads, dec, kv_t, intra


def _gla_scan(heads, dec, kv_t, st_ref, *, reverse):
    nchunk = len(dec)
    state = [st_ref[h] for h in range(GLA_HEADS)]
    entering = [None] * nchunk
    for c in (range(nchunk - 1, -1, -1) if reverse else range(nchunk)):
        entering[c] = [s.astype(BF16) for s in state]
        state = [s * dec[c][:, hd["ps"]] + kv_t[c][h] for h, (s, hd) in enumerate(zip(state, heads))]
    for h in range(GLA_HEADS):
        st_ref[h] = state[h]
    return entering


def _gla_emit(heads, intra, entering, o_ref):
    C = GLA_CHUNK
    for c, states in enumerate(entering):
        rows = slice(c * C, (c + 1) * C)
        for hd, o_intra, s_t in zip(heads, intra, states):
            inter = lax.dot_general(hd["q_dec"][rows], s_t, NT_DIMS, preferred_element_type=F32)
            o_ref[0, rows, hd["vs"]] = o_intra[rows] + inter


def _gla_kernel(qkf_ref, vrf_ref, gf_ref, qkb_ref, vrb_ref, gb_ref, of_ref, ob_ref, sf_sc, sb_sc):
    @pl.when(pl.program_id(1) == 0)
    def _():
        sf_sc[...] = jnp.zeros(sf_sc.shape, F32)
        sb_sc[...] = jnp.zeros(sb_sc.shape, F32)

    hf, df, kf, inf = _gla_local(qkf_ref, vrf_ref, gf_ref, reverse=False)
    hb, db, kb, inb = _gla_local(qkb_ref, vrb_ref, gb_ref, reverse=True)
    ef = _gla_scan(hf, df, kf, sf_sc, reverse=False)
    eb = _gla_scan(hb, db, kb, sb_sc, reverse=True)
    _gla_emit(hf, inf, ef, of_ref)
    _gla_emit(hb, inb, eb, ob_ref)


def _gla(lqk3, lvr3, gfb3, tb=256):
    B, L, _ = lqk3.shape
    nb = L // tb
    fwd = lambda n: pl.BlockSpec((1, tb, n), lambda b, i: (b, i, 0))
    bwd = lambda n: pl.BlockSpec((1, tb, n), lambda b, i: (b, nb - 1 - i, 0))
    out = jax.ShapeDtypeStruct((B, L, GLA_WIDTH), F32)
    return pl.pallas_call(
        _gla_kernel,
        grid=(B, nb),
        in_specs=[fwd(2 * GLA_KEY_WIDTH), fwd(GLA_WIDTH), fwd(2 * GLA_KEY_WIDTH),
                  bwd(2 * GLA_KEY_WIDTH), bwd(GLA_WIDTH), bwd(2 * GLA_KEY_WIDTH)],
        out_specs=(fwd(GLA_WIDTH), bwd(GLA_WIDTH)),
        out_shape=(out, out),
        scratch_shapes=[pltpu.VMEM((GLA_HEADS, GLA_VDIM, LANES), F32),
                        pltpu.VMEM((GLA_HEADS, GLA_VDIM, LANES), F32)],
        compiler_params=_cparams(("parallel", "arbitrary")),
        name="gla",
    )(lqk3, lvr3, gfb3, lqk3, lvr3, gfb3)


def _head_norm(z, g):
    return jnp.concatenate(
        [_rms(z[:, h * LANES:(h + 1) * LANES], g) for h in range(z.shape[1] // LANES)], axis=1)


def _merge_kernel(x_ref, oa_ref, of_ref, ob_ref, r_ref, sg_ref, subg_ref, glag_ref, wpa_ref, wpb_ref,
                  wout_ref, nffn_ref, rwt_ref, rb_ref,
                  x1_ref, h2_ref, ti_ref, rk_ref, tg_ref, cnt_ref, carry_sc, h2_sc, h2_sem, *, lam_init):
    tm = x_ref.shape[0]
    step = pl.program_id(0)

    def h2_stores(tile):
        rows = h2_ref.at[pl.ds(pl.multiple_of(tile * tm, tm), tm)]
        return _relayout_copies(h2_sc, rows, h2_sem, to_tiled=True)

    @pl.when(step == 0)
    def _():
        carry_sc[...] = jnp.zeros(carry_sc.shape, F32)

    @pl.when(step > 0)
    def _():
        for cp in h2_stores(step - 1):
            cp.wait()

    oa = (_head_norm(oa_ref[...], subg_ref[...]) * (1.0 - lam_init)).astype(BF16)
    ob = _head_norm(of_ref[...] + ob_ref[...], glag_ref[...]) * jax.nn.silu(r_ref[...].astype(F32))
    pa = jnp.dot(oa, wpa_ref[...], preferred_element_type=F32)
    pb = jnp.dot(ob.astype(BF16), wpb_ref[...], preferred_element_type=F32)
    sg = sg_ref[...].astype(F32)
    merged = sg[:, :D_MODEL] * pa + sg[:, D_MODEL:] * pb
    x1 = x_ref[...] + jnp.dot(merged.astype(BF16), wout_ref[...], preferred_element_type=F32)
    x1_ref[...] = x1
    h2 = _rms(x1, nffn_ref[...])
    h2_sc[...] = h2
    for cp in h2_stores(step):
        cp.start()

    logits = lax.dot_general(rwt_ref[...], h2.astype(BF16), NT_DIMS, preferred_element_type=F32) + rb_ref[...]
    eidx = lax.broadcasted_iota(I32, logits.shape, 0)
    vals, sels = [], []
    for k in range(TOP_K):
        mk = jnp.max(logits, axis=0, keepdims=True)
        ik = jnp.min(jnp.where(logits == mk, eidx, N_EXPERTS), axis=0, keepdims=True)
        sel = eidx == ik
        logits = jnp.where(sel, -jnp.inf, logits)
        vals.append(mk)
        sels.append(sel)
        ti_ref[k:k + 1, :] = ik
    ex = [jnp.exp(v - vals[0]) for v in vals]
    denom = ex[0] + ex[1] + ex[2] + ex[3]
    for k in range(TOP_K):
        tg_ref[k:k + 1, :] = ex[k] / denom
    tg_ref[TOP_K:, :] = jnp.zeros((tg_ref.shape[0] - TOP_K, tm), F32)

    multi = (sels[0] | sels[1] | sels[2] | sels[3])
    multi_f = jnp.where(multi, 1.0, 0.0)
    before = lax.broadcasted_iota(I32, (tm, tm), 0) < lax.broadcasted_iota(I32, (tm, tm), 1)
    cum = jnp.dot(multi_f.astype(BF16), jnp.where(before, 1.0, 0.0).astype(BF16), preferred_element_type=F32)
    tot = carry_sc[:, 0:1] + cum
    for k in range(TOP_K):
        rk_ref[k:k + 1, :] = jnp.sum(jnp.where(sels[k], tot, 0.0), axis=0, keepdims=True).astype(I32)
    carry_sc[...] = carry_sc[...] + jnp.sum(multi_f, axis=1, keepdims=True)
    cnt_ref[...] = carry_sc[...]

    @pl.when(step == pl.num_programs(0) - 1)
    def _():
        for cp in h2_stores(step):
            cp.wait()


def _merge(x2, oa, of, ob, lvr, sg, subg, glag, wpa, wpb, wout, nffn, rwt, rb, lam_init):
    T, D = x2.shape
    tm = TOKEN_TILE
    row = lambda n: pl.BlockSpec((tm, n), lambda i: (i, 0))
    col = lambda n: pl.BlockSpec((n, tm), lambda i: (0, i))
    out_shape = (
        jax.ShapeDtypeStruct((T, D), F32),
        jax.ShapeDtypeStruct((T,) + ROW_TILE, F32),
        jax.ShapeDtypeStruct((TOP_K, T), I32),
        jax.ShapeDtypeStruct((TOP_K, T), I32),
        jax.ShapeDtypeStruct((SUBLANES, T), F32),
        jax.ShapeDtypeStruct((N_EXPERTS, LANES), F32),
    )
    return pl.pallas_call(
        functools.partial(_merge_kernel, lam_init=lam_init),
        grid=(T // tm,),
        in_specs=[row(D), row(DA_WIDTH), row(GLA_WIDTH), row(GLA_WIDTH),
                  pl.BlockSpec((tm, GLA_WIDTH), lambda i: (i, 1)), row(2 * D),
                  _full(subg.shape), _full(glag.shape), _full(wpa.shape), _full(wpb.shape),
                  _full(wout.shape), _full(nffn.shape), _full(rwt.shape), _full(rb.shape)],
        out_specs=(row(D), pl.BlockSpec(memory_space=pl.ANY),
                   col(TOP_K), col(TOP_K), col(SUBLANES), _full((N_EXPERTS, LANES))),
        out_shape=out_shape,
        scratch_shapes=[pltpu.VMEM((N_EXPERTS, LANES), F32), pltpu.VMEM((tm, D), F32),
                        pltpu.SemaphoreType.DMA(())],
        compiler_params=_cparams(("arbitrary",)),
        name="merge_router",
    )(x2, oa, of, ob, lvr, sg, subg, glag, wpa, wpb, wout, nffn, rwt, rb)


SC_INDEX_WINDOW = 128
SC_GATHER_ROWS = 16


def _sc_scatter_rows(rows, dest, n_slots):
    T = rows.shape[0]
    info = plsc.get_sparse_core_info()
    workers = info.num_cores * info.num_subcores
    per_worker = T // workers
    assert T % workers == 0 and per_worker % SC_INDEX_WINDOW == 0
    windows = T // SC_INDEX_WINDOW
    steps = SC_INDEX_WINDOW // SC_GATHER_ROWS
    mesh = plsc.VectorSubcoreMesh(core_axis_name="c", subcore_axis_name="s")

    @functools.partial(
        pl.kernel, mesh=mesh,
        out_type=jax.ShapeDtypeStruct((n_slots,) + rows.shape[1:], rows.dtype),
        scratch_types=[pltpu.VMEM((steps, SC_GATHER_ROWS), I32)] * TOP_K
        + [pltpu.VMEM((SC_GATHER_ROWS,) + rows.shape[1:], rows.dtype)] * 2 + [pltpu.SemaphoreType.DMA] * 4)
    def scatter(rows_hbm, dest_hbm, out_hbm, *scratch):
        idx_v, bufs = scratch[:TOP_K], scratch[TOP_K:TOP_K + 2]
        load_sem, send_sem = scratch[TOP_K + 2:TOP_K + 4], scratch[TOP_K + 4:TOP_K + 6]
        worker = lax.axis_index("s") * info.num_cores + lax.axis_index("c")
        first = worker * (per_worker // SC_INDEX_WINDOW)

        @pl.loop(0, per_worker // SC_INDEX_WINDOW)
        def _(j):
            window = first + j
            for k in range(TOP_K):
                pltpu.sync_copy(dest_hbm.at[k * windows + window], idx_v[k])

            def load(c):
                src = rows_hbm.at[pl.ds(window * SC_INDEX_WINDOW + c * SC_GATHER_ROWS, SC_GATHER_ROWS)]
                return pltpu.async_copy(src, bufs[c % 2], load_sem[c % 2])

            loads = {0: load(0)}
            sends = {}
            for c in range(steps):
                if c + 1 < steps:
                    for cp in sends.pop(c - 1, ()):
                        cp.wait()
                    loads[c + 1] = load(c + 1)
                loads.pop(c).wait()
                sends[c] = [pltpu.async_copy(bufs[c % 2], out_hbm.at[idx_v[k].at[c]], send_sem[c % 2])
                            for k in range(TOP_K)]
            for pending in sends.values():
                for cp in pending:
                    cp.wait()

    return scatter(rows, dest.reshape(TOP_K * windows, steps, SC_GATHER_ROWS))


def _deinterleave_bf16(w_ref, o_ref):
    n = 2 * LANES
    src = lax.broadcasted_iota(I32, (n, n), 0)
    dst = lax.broadcasted_iota(I32, (n, n), 1)
    perm = jnp.where(src == jnp.where(dst < LANES, 2 * dst, 2 * (dst - LANES) + 1), 1.0, 0.0).astype(BF16)
    for c in range(w_ref.shape[1] // n):
        w = w_ref[:, c * n:(c + 1) * n].astype(BF16)
        o_ref[:, c * n:(c + 1) * n] = jnp.dot(w, perm, preferred_element_type=F32).astype(BF16)


def _expert_kernel(blk_ref, nused_ref, nvalid_ref, xs_ref, w1f_ref, b1_ref, w2f_ref, b2_ref, ys_ref,
                   xbuf, ybuf, w1_ref, w2_ref, in_sem, out_sem):
    i = pl.program_id(0)
    n = pl.num_programs(0)
    bm = xbuf.shape[1]
    slot = i % 2

    @pl.when((i == 0) | (blk_ref[i] != blk_ref[jnp.maximum(i - 1, 0)]))
    def _():
        _deinterleave_bf16(w1f_ref.at[0], w1_ref)
        w2_ref[...] = w2f_ref[0].astype(BF16)

    def rows_of(ref, blk):
        return ref.at[pl.ds(pl.multiple_of(blk * bm, bm), bm)]

    def loads(blk, s):
        return _relayout_copies(xbuf.at[s], rows_of(xs_ref, blk), in_sem.at[s], to_tiled=False)

    def stores(blk, s):
        return _relayout_copies(ybuf.at[s], rows_of(ys_ref, blk), out_sem.at[s], to_tiled=True)

    @pl.when(i == 0)
    def _():
        for cp in loads(0, 0):
            cp.start()

    @pl.when(i + 1 < n)
    def _():
        for cp in loads(i + 1, 1 - slot):
            cp.start()

    for cp in loads(i, slot):
        cp.wait()

    @pl.when(i >= 2)
    def _():
        for cp in stores(i - 2, slot):
            cp.wait()

    @pl.when(i < nused_ref[0])
    def _():
        live = lax.broadcasted_iota(I32, (bm, 1), 0) < nvalid_ref[i]
        x = jnp.where(live, xbuf[slot], 0.0).astype(BF16)
        step = 2 * MXU_TILE

        def hidden(c):
            cols = slice(c * step, (c + 1) * step)
            return jnp.dot(x, w1_ref[:, cols], preferred_element_type=F32) + b1_ref[0, :, cols]

        def down(c, hid):
            acts = []
            for o in range(0, step, 2 * LANES):
                gate = jnp.minimum(hid[:, o:o + LANES], SWIGLU_LIMIT)
                up = jnp.clip(hid[:, o + LANES:o + 2 * LANES], -SWIGLU_LIMIT, SWIGLU_LIMIT)
                acts.append(((up + 1.0) * (gate * jax.nn.sigmoid(SWIGLU_ALPHA * gate))).astype(BF16))
            rows = slice(c * MXU_TILE, (c + 1) * MXU_TILE)
            return jnp.dot(jnp.concatenate(acts, axis=1), w2_ref[rows, :], preferred_element_type=F32)

        n_piece = w1_ref.shape[1] // step
        y = b2_ref[0]
        hid = hidden(0)
        for c in range(n_piece):
            nxt = hidden(c + 1) if c + 1 < n_piece else None
            y = y + down(c, hid)
            hid = nxt
        ybuf[slot] = y

    @pl.when(i >= nused_ref[0])
    def _():
        ybuf[slot] = jnp.zeros(ybuf.shape[1:], F32)

    for cp in stores(i, slot):
        cp.start()

    @pl.when(i == n - 1)
    def _():
        for cp in stores(i, slot):
            cp.wait()

    @pl.when((i == n - 1) & (n >= 2))
    def _():
        for cp in stores(i - 1, 1 - slot):
            cp.wait()


def _experts(xs, blk_e, nused, nvalid, w1, b1, w2, b2):
    P = xs.shape[0]
    D = w1.shape[1]
    bm = EXPERT_ROWS
    nb = P // bm
    per_expert = lambda a: pl.BlockSpec((1,) + a.shape[1:], lambda i, blk, nu, nv: (blk[i], 0, 0))
    return pl.pallas_call(
        _expert_kernel,
        grid_spec=pltpu.PrefetchScalarGridSpec(
            num_scalar_prefetch=3,
            grid=(nb,),
            in_specs=[pl.BlockSpec(memory_space=pl.ANY), per_expert(w1), per_expert(b1), per_expert(w2),
                      per_expert(b2)],
            out_specs=pl.BlockSpec(memory_space=pl.ANY),
            scratch_shapes=[pltpu.VMEM((2, bm, D), F32), pltpu.VMEM((2, bm, D), F32),
                            pltpu.VMEM(w1.shape[1:], BF16), pltpu.VMEM(w2.shape[1:], BF16),
                            pltpu.SemaphoreType.DMA((2,)), pltpu.SemaphoreType.DMA((2,))],
        ),
        out_shape=jax.ShapeDtypeStruct((P,) + ROW_TILE, F32),
        compiler_params=_cparams(("arbitrary",)),
        name="moe_experts",
    )(blk_e, nused, nvalid, xs, w1, b1, w2, b2)


def _sc_gather_rows(table, idx):
    n = idx.shape[0]
    info = plsc.get_sparse_core_info()
    workers = info.num_cores * info.num_subcores
    per_worker = n // workers
    assert n % workers == 0 and per_worker % SC_INDEX_WINDOW == 0
    mesh = plsc.VectorSubcoreMesh(core_axis_name="c", subcore_axis_name="s")

    @functools.partial(
        pl.kernel, mesh=mesh,
        out_type=jax.ShapeDtypeStruct((n,) + table.shape[1:], table.dtype),
        scratch_types=[pltpu.VMEM((SC_INDEX_WINDOW,), I32)]
        + [pltpu.VMEM((SC_GATHER_ROWS,) + table.shape[1:], table.dtype)] * 2 + [pltpu.SemaphoreType.DMA] * 4)
    def gather(table_hbm, idx_hbm, out_hbm, idx_v, *scratch):
        bufs, fetch_sem, store_sem = scratch[0:2], scratch[2:4], scratch[4:6]
        steps = SC_INDEX_WINDOW // SC_GATHER_ROWS
        worker = lax.axis_index("s") * info.num_cores + lax.axis_index("c")
        first = worker * (per_worker // SC_INDEX_WINDOW)

        @pl.loop(0, per_worker // SC_INDEX_WINDOW)
        def _(j):
            window = first + j
            pltpu.sync_copy(idx_hbm.at[window], idx_v)

            def fetch(c):
                src = table_hbm.at[idx_v.at[pl.ds(c * SC_GATHER_ROWS, SC_GATHER_ROWS)]]
                return pltpu.async_copy(src, bufs[c % 2], fetch_sem[c % 2])

            fetches = {0: fetch(0)}
            stores = {}
            for c in range(steps):
                if c + 1 < steps:
                    if c - 1 in stores:
                        stores.pop(c - 1).wait()
                    fetches[c + 1] = fetch(c + 1)
                fetches.pop(c).wait()
                dst = out_hbm.at[pl.ds(window * SC_INDEX_WINDOW + c * SC_GATHER_ROWS, SC_GATHER_ROWS)]
                stores[c] = pltpu.async_copy(bufs[c % 2], dst, store_sem[c % 2])
            for cp in stores.values():
                cp.wait()

    return gather(table, idx.reshape(n // SC_INDEX_WINDOW, SC_INDEX_WINDOW))


def _finish_kernel(rows_ref, tg_ref, x1_ref, gfin_ref, y_ref, buf, sem, *, final):
    i = pl.program_id(0)
    n = pl.num_programs(0)
    tm = x1_ref.shape[0]
    slot = i % 2

    def loads(tile, s):
        cps = []
        for k in range(TOP_K):
            start = pl.multiple_of(k * (n * tm) + tile * tm, tm)
            cps += _relayout_copies(buf.at[s, k], rows_ref.at[pl.ds(start, tm)], sem.at[s], to_tiled=False)
        return cps

    @pl.when(i == 0)
    def _():
        for cp in loads(0, 0):
            cp.start()

    @pl.when(i + 1 < n)
    def _():
        for cp in loads(i + 1, 1 - slot):
            cp.start()

    for cp in loads(i, slot):
        cp.wait()
    gates = tg_ref[...].T
    x = x1_ref[...]
    for k in range(TOP_K):
        x = x + buf[slot, k] * gates[:, k:k + 1]
    y_ref[...] = _rms(x, gfin_ref[...]) if final else x


def _finish(rows, tg, x1, gfin, final, tm=256):
    T, D = x1.shape
    return pl.pallas_call(
        functools.partial(_finish_kernel, final=final),
        grid=(T // tm,),
        in_specs=[pl.BlockSpec(memory_space=pl.ANY),
                  pl.BlockSpec((SUBLANES, tm), lambda i: (0, i)),
                  pl.BlockSpec((tm, D), lambda i: (i, 0)),
                  _full(gfin.shape)],
        out_specs=pl.BlockSpec((tm, D), lambda i: (i, 0)),
        out_shape=jax.ShapeDtypeStruct((T, D), F32),
        scratch_shapes=[pltpu.VMEM((2, TOP_K, tm, D), F32), pltpu.SemaphoreType.DMA((2,))],
        compiler_params=_cparams(("arbitrary",)),
        name="moe_finish",
    )(rows, tg, x1, gfin)


def _moe(h2, ti, rk, tg, cnt, x1, moe_w, gfin, final):
    T = x1.shape[0]
    bm = EXPERT_ROWS
    n_assign = T * TOP_K
    nb = -(-(n_assign + N_EXPERTS * (bm - 1)) // bm)
    counts = cnt[:, 0].astype(I32)
    padded = (counts + bm - 1) // bm * bm
    pad_end = jnp.cumsum(padded).astype(I32)
    pad_start = pad_end - padded
    experts = jnp.arange(N_EXPERTS, dtype=I32)[:, None, None]
    dest = rk + jnp.sum(jnp.where(ti[None] == experts, pad_start[:, None, None], 0), axis=0)
    nused = (pad_end[-1:] // bm).astype(I32)
    blk_start = jnp.minimum(jnp.arange(nb, dtype=I32), nused[0] - 1) * bm
    blk_e = jnp.minimum(jnp.sum((pad_end[None, :] <= blk_start[:, None]).astype(I32), axis=1), N_EXPERTS - 1)
    live_end = pad_start + counts
    nvalid = jnp.clip(live_end[blk_e] - jnp.arange(nb, dtype=I32) * bm, 0, bm).astype(I32)
    xs = _sc_scatter_rows(h2, dest, nb * bm)
    ys = _experts(xs, blk_e, nused, nvalid, *moe_w)
    rows = _sc_gather_rows(ys, dest.reshape(-1))
    return _finish(rows, tg, x1, gfin, final)


def _split_in_proj(w):
    sizes = (DA_QK_WIDTH, DA_QK_WIDTH, DA_WIDTH, GLA_KEY_WIDTH, GLA_KEY_WIDTH, GLA_WIDTH, GLA_WIDTH,
             GLA_GATE_RANK, GLA_GATE_RANK, D_MODEL, D_MODEL)
    assert w.shape[-1] == sum(sizes)
    out, o = [], 0
    for s in sizes:
        out.append(w[:, o:o + s])
        o += s
    return out


def _prep_layer(l, norm_mix_g, w_in, lambda_q1, lambda_k1, lambda_q2, lambda_k2, diff_subln_g,
                gla_gate_up_f, gla_gate_bias_f, gla_gate_up_b, gla_gate_bias_b, gla_norm_g,
                w_proj_a, w_proj_b, w_out, norm_ffn_g, router_w, router_b, moe_w1, moe_b1, moe_w2, moe_b2):
    dq, dk, dv, lq, lk, lv, lr, laf, lab, ga, gb = _split_in_proj(w_in[l])
    bf = lambda a: a.astype(BF16)
    r = GLA_GATE_RANK
    gup = jnp.zeros((LANES, 2 * GLA_KEY_WIDTH), F32)
    gup = gup.at[0:r, 0:GLA_KEY_WIDTH].set(gla_gate_up_f[l]).at[r:2 * r, GLA_KEY_WIDTH:].set(gla_gate_up_b[l])
    lamv = jnp.zeros((SUBLANES, LANES), F32)
    for i, v in enumerate((lambda_q1, lambda_k1, lambda_q2, lambda_k2)):
        lamv = lamv.at[i, 0:DA_HEAD_DIM].set(v[l].astype(F32))
    inproj = dict(
        norm_g=norm_mix_g[l][None, :],
        wqk=bf(jnp.concatenate([dq * (DA_HEAD_DIM ** -0.5), dk], 1)),
        wvt=bf(dv.T),
        wlqk=bf(jnp.concatenate([lq, lk], 1)),
        wlvr=bf(jnp.concatenate([lv, lr], 1)),
        wg=bf(jnp.concatenate([ga, gb], 1)),
        wlow=bf(jnp.concatenate([laf, lab, jnp.zeros((D_MODEL, LANES - 2 * r), F32)], 1)),
        gup=bf(gup),
        gbias=jnp.concatenate([gla_gate_bias_f[l], gla_gate_bias_b[l]])[None, :],
    )
    merge = dict(
        subg=diff_subln_g[l][None, :], glag=gla_norm_g[l][None, :],
        wpa=bf(w_proj_a[l]), wpb=bf(w_proj_b[l]), wout=bf(w_out[l]),
        nffn=norm_ffn_g[l][None, :], rwt=bf(router_w[l].T), rb=router_b[l][:, None],
    )
    b1 = moe_b1[l].reshape(N_EXPERTS, -1, LANES, 2).transpose(0, 1, 3, 2).reshape(N_EXPERTS, 1, -1)
    moe_w = (moe_w1[l], b1, moe_w2[l], moe_b2[l][:, None, :])
    return inproj, lamv, merge, moe_w


def _trunk(x, layers, norm_final_g):
    B, L, D = x.shape
    T = B * L
    cos_t, sin_t = _rope_tables(L)
    x2 = x.reshape(T, D)
    gfin = norm_final_g[None, :]
    for l, (inproj, lamv, merge, moe_w) in enumerate(layers):
        lam_init = 0.8 - 0.6 * math.exp(-0.3 * l)
        qk, vt, lqk, lvr, gfb, sg = _inproj(x2, L, cos_t=cos_t, sin_t=sin_t, **inproj)
        oa = _diff_attention(qk.reshape(B, L, -1), vt, lamv, lam_init)
        of, ob = _gla(lqk.reshape(B, L, -1), lvr.reshape(B, L, -1), gfb.reshape(B, L, -1))
        x1, h2, ti, rk, tg, cnt = _merge(x2, oa.reshape(T, -1), of.reshape(T, -1), ob.reshape(T, -1),
                                         lvr, sg, lam_init=lam_init, **merge)
        x2 = _moe(h2, ti, rk, tg, cnt, x1, moe_w, gfin, final=(l == len(layers) - 1))
    return x2.reshape(B, L, D)


def kernel(x_prompt, x_sample, norm_mix_g, w_in, lambda_q1, lambda_k1, lambda_q2, lambda_k2, diff_subln_g,
           gla_gate_up_f, gla_gate_bias_f, gla_gate_up_b, gla_gate_bias_b, gla_norm_g, w_proj_a, w_proj_b,
           w_out, norm_ffn_g, router_w, router_b, moe_w1, moe_b1, moe_w2, moe_b2, norm_final_g):
    depth = w_in.shape[0]
    layers = [_prep_layer(l, norm_mix_g, w_in, lambda_q1, lambda_k1, lambda_q2, lambda_k2, diff_subln_g,
                          gla_gate_up_f, gla_gate_bias_f, gla_gate_up_b, gla_gate_bias_b, gla_norm_g,
                          w_proj_a, w_proj_b, w_out, norm_ffn_g, router_w, router_b,
                          moe_w1, moe_b1, moe_w2, moe_b2) for l in range(depth)]
    return _trunk(x_prompt, layers, norm_final_g), _trunk(x_sample, layers, norm_final_g)
```

```python
import functools
import math

import jax
import jax.numpy as jnp
from jax import lax
from jax.experimental import pallas as pl
from jax.experimental.pallas import tpu as pltpu
from jax.experimental.pallas import tpu_sc as plsc

F32 = jnp.float32
BF16 = jnp.bfloat16
I32 = jnp.int32

D_MODEL = 1024
DA_HEADS = 4
DA_HEAD_DIM = 64
DA_VDIM = 128
DA_QK_WIDTH = DA_HEADS * 2 * DA_HEAD_DIM
DA_WIDTH = DA_HEADS * DA_VDIM
ROPE_THETA = 10000.0
GLA_HEADS = 4
GLA_KDIM = 64
GLA_VDIM = 128
GLA_KEY_WIDTH = GLA_HEADS * GLA_KDIM
GLA_WIDTH = GLA_HEADS * GLA_VDIM
GLA_GATE_RANK = 16
GLA_GATE_NORMALIZER = 16.0
GLA_CHUNK = 64
N_EXPERTS = 32
TOP_K = 4
SWIGLU_ALPHA = 1.702
SWIGLU_LIMIT = 7.0
NORM_EPS = 1e-5
LANES = 128
SUBLANES = 8

BF16_SUBLANES = 16
MXU_TILE = 256
LOG2_E = 1.4426950408889634
VT_ROWS = DA_VDIM + BF16_SUBLANES

VMEM_LIMIT = 56 * 1024 * 1024

TOKEN_TILE = 512
ATTN_KEY_CHUNK = 512
EXPERT_ROWS = 512

NT_DIMS = (((1,), (1,)), ((), ()))
TN_DIMS = (((0,), (0,)), ((), ()))


def _cparams(sem):
    return pltpu.CompilerParams(dimension_semantics=sem, vmem_limit_bytes=VMEM_LIMIT)


def _full(shape):
    nd = len(shape)
    return pl.BlockSpec(shape, lambda *_: (0,) * nd)


def _rms(x, g):
    return x * lax.rsqrt(jnp.mean(x * x, axis=-1, keepdims=True) + NORM_EPS) * g


ROW_TILE = (SUBLANES, LANES)


def _relayout_copies(flat_ref, tiled_ref, sem, to_tiled):
    cps = []
    for c in range(SUBLANES):
        flat = flat_ref.at[:, pl.ds(c * LANES, LANES)]
        tiled = tiled_ref.at[:, c, :]
        cps.append(pltpu.make_async_copy(flat, tiled, sem) if to_tiled else pltpu.make_async_copy(tiled, flat, sem))
    return cps


def _log_sigmoid(x):
    return jnp.minimum(x, 0.0) - jnp.log1p(jnp.exp(-jnp.abs(x)))


def _inproj_kernel(x_ref, g_ref, cos_ref, sin_ref, wqk_ref, wvt_ref, wlqk_ref, wlvr_ref,
                   wg_ref, wlow_ref, gup_ref, gbias_ref,
                   qk_ref, vt_ref, lqk_ref, lvr_ref, gfb_ref, sg_ref):
    h = _rms(x_ref[...], g_ref[...]).astype(BF16)

    cos = cos_ref[...]
    sin_signed = sin_ref[...]
    first_half = (lax.broadcasted_iota(I32, (1, LANES), 1) % DA_HEAD_DIM) < (DA_HEAD_DIM // 2)
    for c in range(2 * DA_QK_WIDTH // LANES):
        z = jnp.dot(h, wqk_ref[:, c * LANES:(c + 1) * LANES], preferred_element_type=F32)
        if c < DA_QK_WIDTH // LANES:
            z = z * LOG2_E
        partner = jnp.where(first_half, pltpu.roll(z, LANES - DA_HEAD_DIM // 2, 1),
                            pltpu.roll(z, DA_HEAD_DIM // 2, 1))
        qk_ref[:, c * LANES:(c + 1) * LANES] = (z * cos + partner * sin_signed).astype(BF16)

    vt = lax.dot_general(wvt_ref[...], h, NT_DIMS, preferred_element_type=F32).astype(BF16)
    for hd in range(DA_HEADS):
        vt_ref[0, hd * VT_ROWS:hd * VT_ROWS + DA_VDIM, :] = vt[hd * DA_VDIM:(hd + 1) * DA_VDIM]
        vt_ref[0, hd * VT_ROWS + DA_VDIM:(hd + 1) * VT_ROWS, :] = jnp.ones((VT_ROWS - DA_VDIM, vt.shape[1]), BF16)
    lqk_ref[...] = jnp.dot(h, wlqk_ref[...], preferred_element_type=F32)
    lvr_ref[...] = jnp.dot(h, wlvr_ref[...], preferred_element_type=F32).astype(BF16)
    sg_ref[...] = jax.nn.sigmoid(jnp.dot(h, wg_ref[...], preferred_element_type=F32)).astype(BF16)
    low = jnp.dot(h, wlow_ref[...], preferred_element_type=F32).astype(BF16)
    pre = jnp.dot(low, gup_ref[...], preferred_element_type=F32) + gbias_ref[...]
    gfb_ref[...] = _log_sigmoid(pre) * (1.0 / GLA_GATE_NORMALIZER)


def _inproj(x2, seq_len, norm_g, wqk, wvt, wlqk, wlvr, wg, wlow, gup, gbias, cos_t, sin_t):
    T, D = x2.shape
    tm = TOKEN_TILE
    assert T % tm == 0 and seq_len % tm == 0
    nl = seq_len // tm
    row = lambda n: pl.BlockSpec((tm, n), lambda i: (i, 0))
    out_shape = (
        jax.ShapeDtypeStruct((T, 2 * DA_QK_WIDTH), BF16),
        jax.ShapeDtypeStruct((T // tm, DA_HEADS * VT_ROWS, tm), BF16),
        jax.ShapeDtypeStruct((T, 2 * GLA_KEY_WIDTH), F32),
        jax.ShapeDtypeStruct((T, 2 * GLA_WIDTH), BF16),
        jax.ShapeDtypeStruct((T, 2 * GLA_KEY_WIDTH), F32),
        jax.ShapeDtypeStruct((T, 2 * D_MODEL), BF16),
    )
    return pl.pallas_call(
        _inproj_kernel,
        grid=(T // tm,),
        in_specs=[row(D), _full(norm_g.shape),
                  pl.BlockSpec((tm, LANES), lambda i: (i % nl, 0)),
                  pl.BlockSpec((tm, LANES), lambda i: (i % nl, 0)),
                  _full(wqk.shape), _full(wvt.shape), _full(wlqk.shape), _full(wlvr.shape),
                  _full(wg.shape), _full(wlow.shape), _full(gup.shape), _full(gbias.shape)],
        out_specs=(row(2 * DA_QK_WIDTH), pl.BlockSpec((1, DA_HEADS * VT_ROWS, tm), lambda i: (i, 0, 0)),
                   row(2 * GLA_KEY_WIDTH), row(2 * GLA_WIDTH), row(2 * GLA_KEY_WIDTH), row(2 * D_MODEL)),
        out_shape=out_shape,
        compiler_params=_cparams(("parallel",)),
        name="inproj",
    )(x2, norm_g, cos_t, sin_t, wqk, wvt, wlqk, wlvr, wg, wlow, gup, gbias)


def _rope_tables(seq_len):
    d = DA_HEAD_DIM
    inv = ROPE_THETA ** (-jnp.arange(0, d, 2, dtype=F32) / d)
    ang = jnp.arange(seq_len, dtype=F32)[:, None] * inv[None, :]
    cos = jnp.concatenate([jnp.cos(ang)] * (2 * LANES // d), -1)
    sin = jnp.sin(ang)
    sin_signed = jnp.concatenate([-sin, sin] * (LANES // d), -1)
    return cos, sin_signed


def _attn_kernel(lamv_ref, q_ref, k_ref, vt_ref, o_ref, acc_sc, sa_sc, sb_sc, *, lam_init):
    n_vt, _, vt_w = vt_ref.shape
    tk = ATTN_KEY_CHUNK
    nk = n_vt * vt_w // tk
    tq = q_ref.shape[1]
    q = q_ref[0]
    lane = lax.broadcasted_iota(I32, (1, LANES), 1)
    zeros = jnp.zeros_like(q)
    qms = (jnp.where(lane < DA_HEAD_DIM, q, zeros), jnp.where(lane >= DA_HEAD_DIM, q, zeros))
    acc_sc[...] = jnp.zeros(acc_sc.shape, F32)

    def scores(j, dst_ref):
        kj = k_ref[0, j * tk:(j + 1) * tk, :]
        col_max = []
        for mp in range(2):
            st = lax.dot_general(kj, qms[mp], NT_DIMS, preferred_element_type=F32)
            dst_ref[mp] = st
            col_max.append(jnp.max(st, axis=0, keepdims=True))
        return col_max

    def softmax_pv(j, src_ref, ms, col_max):
        def v_tile(r):
            key = j * tk + r
            return vt_ref[key // vt_w, :, key % vt_w:key % vt_w + MXU_TILE]

        new = []
        for mp in range(2):
            m_new = jnp.maximum(ms[mp], col_max[mp])
            alpha = jnp.exp2(ms[mp] - m_new)
            for cols in (slice(c, c + MXU_TILE) for c in range(0, tq, MXU_TILE)):
                part = None
                for r in range(0, tk, MXU_TILE):
                    p = jnp.exp2((src_ref[mp, r:r + MXU_TILE, cols] - m_new[:, cols]).astype(BF16))
                    d = jnp.dot(v_tile(r), p, preferred_element_type=F32)
                    part = d if part is None else part + d
                acc_sc[mp, :, cols] = acc_sc[mp, :, cols] * alpha[:, cols] + part
            new.append(m_new)
        return tuple(new)

    bufs = (sa_sc, sb_sc)
    col_max = scores(0, bufs[0])
    ms = tuple(jnp.full((1, tq), -jnp.inf, F32) for _ in range(2))
    for j in range(nk):
        nxt_max = scores(j + 1, bufs[(j + 1) % 2]) if j + 1 < nk else None
        ms = softmax_pv(j, bufs[j % 2], ms, col_max)
        col_max = nxt_max

    lv = lamv_ref[...]
    lam = (jnp.exp(jnp.sum(lv[0:1] * lv[1:2], axis=-1, keepdims=True))
           - jnp.exp(jnp.sum(lv[2:3] * lv[3:4], axis=-1, keepdims=True)) + lam_init)
    a0, a1 = acc_sc[0], acc_sc[1]
    o_ref[0] = (a0[:DA_VDIM] / a0[DA_VDIM:DA_VDIM + 1] - lam * (a1[:DA_VDIM] / a1[DA_VDIM:DA_VDIM + 1])).T


def _diff_attention(qk3, vt3, lamv, lam_init, tq=512):
    B, L, _ = qk3.shape
    vt_w = vt3.shape[-1]
    tk = ATTN_KEY_CHUNK
    assert L % tk == 0 and tk % vt_w == 0
    return pl.pallas_call(
        functools.partial(_attn_kernel, lam_init=lam_init),
        grid=(B, DA_HEADS, L // tq),
        in_specs=[_full(lamv.shape),
                  pl.BlockSpec((1, tq, LANES), lambda b, h, i: (b, i, h)),
                  pl.BlockSpec((1, L, LANES), lambda b, h, i: (b, 0, DA_HEADS + h)),
                  pl.BlockSpec((L // vt_w, VT_ROWS, vt_w), lambda b, h, i: (b, h, 0))],
        out_specs=pl.BlockSpec((1, tq, DA_VDIM), lambda b, h, i: (b, i, h)),
        out_shape=jax.ShapeDtypeStruct((B, L, DA_WIDTH), F32),
        scratch_shapes=[pltpu.VMEM((2, VT_ROWS, tq), F32), pltpu.VMEM((2, tk, tq), F32),
                        pltpu.VMEM((2, tk, tq), F32)],
        compiler_params=_cparams(("parallel", "parallel", "arbitrary")),
        name="diff_attention",
    )(lamv, qk3, qk3, vt3)


def _split3_bf16(x):
    hi = x.astype(BF16)
    r1 = x - hi.astype(F32)
    mid = r1.astype(BF16)
    return hi, mid, (r1 - mid.astype(F32)).astype(BF16)


def _gla_local(qk_ref, vr_ref, g_ref, *, reverse):
    C = GLA_CHUNK
    tb = qk_ref.shape[1]
    nchunk = tb // C
    row = lax.broadcasted_iota(I32, (tb, tb), 0)
    col = lax.broadcasted_iota(I32, (tb, tb), 1)
    same_chunk = (row // C) == (col // C)
    tri = same_chunk & ((col >= row) if reverse else (col <= row))
    tri_b = jnp.where(tri, 1.0, 0.0).astype(BF16)
    lane = lax.broadcasted_iota(I32, (1, LANES), 1)
    goff = GLA_KEY_WIDTH if reverse else 0

    g = g_ref[0, :, goff:goff + GLA_KEY_WIDTH]
    q = qk_ref[0, :, 0:GLA_KEY_WIDTH] * (GLA_KDIM ** -0.5)
    k = qk_ref[0, :, GLA_KEY_WIDTH:2 * GLA_KEY_WIDTH]
    v = vr_ref[0]
    b = sum(jnp.dot(tri_b, part, preferred_element_type=F32) for part in _split3_bf16(g))
    last_row, ref_row = (0, C // 2) if reverse else (C - 1, C // 2 - 1)
    per_chunk = lambda r: jnp.concatenate(
        [jnp.broadcast_to(b[c * C + r:c * C + r + 1], (C, GLA_KEY_WIDTH)) for c in range(nchunk)], axis=0)
    b_last, b_ref = per_chunk(last_row), per_chunk(ref_row)
    q_in = q * jnp.exp(b - b_ref)
    k_in = (k * jnp.exp(b_ref - b)).astype(BF16)
    k_dec = k * jnp.exp(b_last - b)
    q_dec = q * jnp.exp(b)
    dec = [jnp.exp(b[c * C + last_row:c * C + last_row + 1]) for c in range(nchunk)]

    heads = []
    for h in range(GLA_HEADS):
        ps = slice((h // 2) * LANES, (h // 2 + 1) * LANES)
        keep = (lane < GLA_KDIM) if h % 2 == 0 else (lane >= GLA_KDIM)
        zero = jnp.zeros((tb, LANES), F32)
        heads.append(dict(
            ps=ps, vs=slice(h * GLA_VDIM, (h + 1) * GLA_VDIM),
            q_in=jnp.where(keep, q_in[:, ps], zero).astype(BF16),
            q_dec=jnp.where(keep, q_dec[:, ps], zero).astype(BF16),
            k_dec=jnp.where(keep, k_dec[:, ps], zero).astype(BF16)))
    scores = [lax.dot_general(hd["q_in"], k_in[:, hd["ps"]], NT_DIMS, preferred_element_type=F32) for hd in heads]
    kv_t = [[lax.dot_general(v[c * C:(c + 1) * C, hd["vs"]], hd["k_dec"][c * C:(c + 1) * C], TN_DIMS,
                             preferred_element_type=F32) for hd in heads] for c in range(nchunk)]
    intra = [jnp.dot(jnp.where(tri, s, 0.0).astype(BF16), v[:, hd["vs"]], preferred_element_type=F32)
             for s, hd in zip(scores, heads)]
    return heads, dec, kv_t, intra


def _gla_scan(heads, dec, kv_t, st_ref, *, reverse):
    nchunk = len(dec)
    state = [st_ref[h] for h in range(GLA_HEADS)]
    entering = [None] * nchunk
    for c in (range(nchunk - 1, -1, -1) if reverse else range(nchunk)):
        entering[c] = [s.astype(BF16) for s in state]
        state = [s * dec[c][:, hd["ps"]] + kv_t[c][h] for h, (s, hd) in enumerate(zip(state, heads))]
    for h in range(GLA_HEADS):
        st_ref[h] = state[h]
    return entering


def _gla_emit(heads, intra, entering, o_ref):
    C = GLA_CHUNK
    for c, states in enumerate(entering):
        rows = slice(c * C, (c + 1) * C)
        for hd, o_intra, s_t in zip(heads, intra, states):
            inter = lax.dot_general(hd["q_dec"][rows], s_t, NT_DIMS, preferred_element_type=F32)
            o_ref[0, rows, hd["vs"]] = o_intra[rows] + inter


def _gla_kernel(qkf_ref, vrf_ref, gf_ref, qkb_ref, vrb_ref, gb_ref, of_ref, ob_ref, sf_sc, sb_sc):
    @pl.when(pl.program_id(1) == 0)
    def _():
        sf_sc[...] = jnp.zeros(sf_sc.shape, F32)
        sb_sc[...] = jnp.zeros(sb_sc.shape, F32)

    hf, df, kf, inf = _gla_local(qkf_ref, vrf_ref, gf_ref, reverse=False)
    hb, db, kb, inb = _gla_local(qkb_ref, vrb_ref, gb_ref, reverse=True)
    ef = _gla_scan(hf, df, kf, sf_sc, reverse=False)
    eb = _gla_scan(hb, db, kb, sb_sc, reverse=True)
    _gla_emit(hf, inf, ef, of_ref)
    _gla_emit(hb, inb, eb, ob_ref)


def _gla(lqk3, lvr3, gfb3, tb=256):
    B, L, _ = lqk3.shape
    nb = L // tb
    fwd = lambda n: pl.BlockSpec((1, tb, n), lambda b, i: (b, i, 0))
    bwd = lambda n: pl.BlockSpec((1, tb, n), lambda b, i: (b, nb - 1 - i, 0))
    out = jax.ShapeDtypeStruct((B, L, GLA_WIDTH), F32)
    return pl.pallas_call(
        _gla_kernel,
        grid=(B, nb),
        in_specs=[fwd(2 * GLA_KEY_WIDTH), fwd(GLA_WIDTH), fwd(2 * GLA_KEY_WIDTH),
                  bwd(2 * GLA_KEY_WIDTH), bwd(GLA_WIDTH), bwd(2 * GLA_KEY_WIDTH)],
        out_specs=(fwd(GLA_WIDTH), bwd(GLA_WIDTH)),
        out_shape=(out, out),
        scratch_shapes=[pltpu.VMEM((GLA_HEADS, GLA_VDIM, LANES), F32),
                        pltpu.VMEM((GLA_HEADS, GLA_VDIM, LANES), F32)],
        compiler_params=_cparams(("parallel", "arbitrary")),
        name="gla",
    )(lqk3, lvr3, gfb3, lqk3, lvr3, gfb3)


def _head_norm(z, g):
    return jnp.concatenate(
        [_rms(z[:, h * LANES:(h + 1) * LANES], g) for h in range(z.shape[1] // LANES)], axis=1)


def _merge_kernel(x_ref, oa_ref, of_ref, ob_ref, r_ref, sg_ref, subg_ref, glag_ref, wpa_ref, wpb_ref,
                  wout_ref, nffn_ref, rwt_ref, rb_ref,
                  x1_ref, h2_ref, ti_ref, rk_ref, tg_ref, cnt_ref, carry_sc, h2_sc, h2_sem, *, lam_init):
    tm = x_ref.shape[0]
    step = pl.program_id(0)

    def h2_stores(tile):
        rows = h2_ref.at[pl.ds(pl.multiple_of(tile * tm, tm), tm)]
        return _relayout_copies(h2_sc, rows, h2_sem, to_tiled=True)

    @pl.when(step == 0)
    def _():
        carry_sc[...] = jnp.zeros(carry_sc.shape, F32)

    @pl.when(step > 0)
    def _():
        for cp in h2_stores(step - 1):
            cp.wait()

    oa = (_head_norm(oa_ref[...], subg_ref[...]) * (1.0 - lam_init)).astype(BF16)
    ob = _head_norm(of_ref[...] + ob_ref[...], glag_ref[...]) * jax.nn.silu(r_ref[...].astype(F32))
    pa = jnp.dot(oa, wpa_ref[...], preferred_element_type=F32)
    pb = jnp.dot(ob.astype(BF16), wpb_ref[...], preferred_element_type=F32)
    sg = sg_ref[...].astype(F32)
    merged = sg[:, :D_MODEL] * pa + sg[:, D_MODEL:] * pb
    x1 = x_ref[...] + jnp.dot(merged.astype(BF16), wout_ref[...], preferred_element_type=F32)
    x1_ref[...] = x1
    h2 = _rms(x1, nffn_ref[...])
    h2_sc[...] = h2
    for cp in h2_stores(step):
        cp.start()

    logits = lax.dot_general(rwt_ref[...], h2.astype(BF16), NT_DIMS, preferred_element_type=F32) + rb_ref[...]
    eidx = lax.broadcasted_iota(I32, logits.shape, 0)
    vals, sels = [], []
    for k in range(TOP_K):
        mk = jnp.max(logits, axis=0, keepdims=True)
        ik = jnp.min(jnp.where(logits == mk, eidx, N_EXPERTS), axis=0, keepdims=True)
        sel = eidx == ik
        logits = jnp.where(sel, -jnp.inf, logits)
        vals.append(mk)
        sels.append(sel)
        ti_ref[k:k + 1, :] = ik
    ex = [jnp.exp(v - vals[0]) for v in vals]
    denom = ex[0] + ex[1] + ex[2] + ex[3]
    for k in range(TOP_K):
        tg_ref[k:k + 1, :] = ex[k] / denom
    tg_ref[TOP_K:, :] = jnp.zeros((tg_ref.shape[0] - TOP_K, tm), F32)

    multi = (sels[0] | sels[1] | sels[2] | sels[3])
    multi_f = jnp.where(multi, 1.0, 0.0)
    before = lax.broadcasted_iota(I32, (tm, tm), 0) < lax.broadcasted_iota(I32, (tm, tm), 1)
    cum = jnp.dot(multi_f.astype(BF16), jnp.where(before, 1.0, 0.0).astype(BF16), preferred_element_type=F32)
    tot = carry_sc[:, 0:1] + cum
    for k in range(TOP_K):
        rk_ref[k:k + 1, :] = jnp.sum(jnp.where(sels[k], tot, 0.0), axis=0, keepdims=True).astype(I32)
    carry_sc[...] = carry_sc[...] + jnp.sum(multi_f, axis=1, keepdims=True)
    cnt_ref[...] = carry_sc[...]

    @pl.when(step == pl.num_programs(0) - 1)
    def _():
        for cp in h2_stores(step):
            cp.wait()


def _merge(x2, oa, of, ob, lvr, sg, subg, glag, wpa, wpb, wout, nffn, rwt, rb, lam_init):
    T, D = x2.shape
    tm = TOKEN_TILE
    row = lambda n: pl.BlockSpec((tm, n), lambda i: (i, 0))
    col = lambda n: pl.BlockSpec((n, tm), lambda i: (0, i))
    out_shape = (
        jax.ShapeDtypeStruct((T, D), F32),
        jax.ShapeDtypeStruct((T,) + ROW_TILE, F32),
        jax.ShapeDtypeStruct((TOP_K, T), I32),
        jax.ShapeDtypeStruct((TOP_K, T), I32),
        jax.ShapeDtypeStruct((SUBLANES, T), F32),
        jax.ShapeDtypeStruct((N_EXPERTS, LANES), F32),
    )
    return pl.pallas_call(
        functools.partial(_merge_kernel, lam_init=lam_init),
        grid=(T // tm,),
        in_specs=[row(D), row(DA_WIDTH), row(GLA_WIDTH), row(GLA_WIDTH),
                  pl.BlockSpec((tm, GLA_WIDTH), lambda i: (i, 1)), row(2 * D),
                  _full(subg.shape), _full(glag.shape), _full(wpa.shape), _full(wpb.shape),
                  _full(wout.shape), _full(nffn.shape), _full(rwt.shape), _full(rb.shape)],
        out_specs=(row(D), pl.BlockSpec(memory_space=pl.ANY),
                   col(TOP_K), col(TOP_K), col(SUBLANES), _full((N_EXPERTS, LANES))),
        out_shape=out_shape,
        scratch_shapes=[pltpu.VMEM((N_EXPERTS, LANES), F32), pltpu.VMEM((tm, D), F32),
                        pltpu.SemaphoreType.DMA(())],
        compiler_params=_cparams(("arbitrary",)),
        name="merge_router",
    )(x2, oa, of, ob, lvr, sg, subg, glag, wpa, wpb, wout, nffn, rwt, rb)


SC_INDEX_WINDOW = 128
SC_GATHER_ROWS = 16


def _sc_scatter_rows(rows, dest, n_slots):
    T = rows.shape[0]
    info = plsc.get_sparse_core_info()
    workers = info.num_cores * info.num_subcores
    per_worker = T // workers
    assert T % workers == 0 and per_worker % SC_INDEX_WINDOW == 0
    windows = T // SC_INDEX_WINDOW
    steps = SC_INDEX_WINDOW // SC_GATHER_ROWS
    mesh = plsc.VectorSubcoreMesh(core_axis_name="c", subcore_axis_name="s")

    @functools.partial(
        pl.kernel, mesh=mesh,
        out_type=jax.ShapeDtypeStruct((n_slots,) + rows.shape[1:], rows.dtype),
        scratch_types=[pltpu.VMEM((steps, SC_GATHER_ROWS), I32)] * TOP_K
        + [pltpu.VMEM((SC_GATHER_ROWS,) + rows.shape[1:], rows.dtype)] * 2 + [pltpu.SemaphoreType.DMA] * 4)
    def scatter(rows_hbm, dest_hbm, out_hbm, *scratch):
        idx_v, bufs = scratch[:TOP_K], scratch[TOP_K:TOP_K + 2]
        load_sem, send_sem = scratch[TOP_K + 2:TOP_K + 4], scratch[TOP_K + 4:TOP_K + 6]
        worker = lax.axis_index("s") * info.num_cores + lax.axis_index("c")
        first = worker * (per_worker // SC_INDEX_WINDOW)

        @pl.loop(0, per_worker // SC_INDEX_WINDOW)
        def _(j):
            window = first + j
            for k in range(TOP_K):
                pltpu.sync_copy(dest_hbm.at[k * windows + window], idx_v[k])

            def load(c):
                src = rows_hbm.at[pl.ds(window * SC_INDEX_WINDOW + c * SC_GATHER_ROWS, SC_GATHER_ROWS)]
                return pltpu.async_copy(src, bufs[c % 2], load_sem[c % 2])

            loads = {0: load(0)}
            sends = {}
            for c in range(steps):
                if c + 1 < steps:
                    for cp in sends.pop(c - 1, ()):
                        cp.wait()
                    loads[c + 1] = load(c + 1)
                loads.pop(c).wait()
                sends[c] = [pltpu.async_copy(bufs[c % 2], out_hbm.at[idx_v[k].at[c]], send_sem[c % 2])
                            for k in range(TOP_K)]
            for pending in sends.values():
                for cp in pending:
                    cp.wait()

    return scatter(rows, dest.reshape(TOP_K * windows, steps, SC_GATHER_ROWS))


def _deinterleave_bf16(w_ref, o_ref):
    n = 2 * LANES
    src = lax.broadcasted_iota(I32, (n, n), 0)
    dst = lax.broadcasted_iota(I32, (n, n), 1)
    perm = jnp.where(src == jnp.where(dst < LANES, 2 * dst, 2 * (dst - LANES) + 1), 1.0, 0.0).astype(BF16)
    for c in range(w_ref.shape[1] // n):
        w = w_ref[:, c * n:(c + 1) * n].astype(BF16)
        o_ref[:, c * n:(c + 1) * n] = jnp.dot(w, perm, preferred_element_type=F32).astype(BF16)


def _expert_kernel(blk_ref, nused_ref, nvalid_ref, xs_ref, w1f_ref, b1_ref, w2f_ref, b2_ref, ys_ref,
                   xbuf, ybuf, w1_ref, w2_ref, in_sem, out_sem):
    i = pl.program_id(0)
    n = pl.num_programs(0)
    bm = xbuf.shape[1]
    slot = i % 2

    @pl.when((i == 0) | (blk_ref[i] != blk_ref[jnp.maximum(i - 1, 0)]))
    def _():
        _deinterleave_bf16(w1f_ref.at[0], w1_ref)
        w2_ref[...] = w2f_ref[0].astype(BF16)

    def rows_of(ref, blk):
        return ref.at[pl.ds(pl.multiple_of(blk * bm, bm), bm)]

    def loads(blk, s):
        return _relayout_copies(xbuf.at[s], rows_of(xs_ref, blk), in_sem.at[s], to_tiled=False)

    def stores(blk, s):
        return _relayout_copies(ybuf.at[s], rows_of(ys_ref, blk), out_sem.at[s], to_tiled=True)

    @pl.when(i == 0)
    def _():
        for cp in loads(0, 0):
            cp.start()

    @pl.when(i + 1 < n)
    def _():
        for cp in loads(i + 1, 1 - slot):
            cp.start()

    for cp in loads(i, slot):
        cp.wait()

    @pl.when(i >= 2)
    def _():
        for cp in stores(i - 2, slot):
            cp.wait()

    @pl.when(i < nused_ref[0])
    def _():
        live = lax.broadcasted_iota(I32, (bm, 1), 0) < nvalid_ref[i]
        x = jnp.where(live, xbuf[slot], 0.0).astype(BF16)
        step = 2 * MXU_TILE

        def hidden(c):
            cols = slice(c * step, (c + 1) * step)
            return jnp.dot(x, w1_ref[:, cols], preferred_element_type=F32) + b1_ref[0, :, cols]

        def down(c, hid):
            acts = []
            for o in range(0, step, 2 * LANES):
                gate = jnp.minimum(hid[:, o:o + LANES], SWIGLU_LIMIT)
                up = jnp.clip(hid[:, o + LANES:o + 2 * LANES], -SWIGLU_LIMIT, SWIGLU_LIMIT)
                acts.append(((up + 1.0) * (gate * jax.nn.sigmoid(SWIGLU_ALPHA * gate))).astype(BF16))
            rows = slice(c * MXU_TILE, (c + 1) * MXU_TILE)
            return jnp.dot(jnp.concatenate(acts, axis=1), w2_ref[rows, :], preferred_element_type=F32)

        n_piece = w1_ref.shape[1] // step
        y = b2_ref[0]
        hid = hidden(0)
        for c in range(n_piece):
            nxt = hidden(c + 1) if c + 1 < n_piece else None
            y = y + down(c, hid)
            hid = nxt
        ybuf[slot] = y

    @pl.when(i >= nused_ref[0])
    def _():
        ybuf[slot] = jnp.zeros(ybuf.shape[1:], F32)

    for cp in stores(i, slot):
        cp.start()

    @pl.when(i == n - 1)
    def _():
        for cp in stores(i, slot):
            cp.wait()

    @pl.when((i == n - 1) & (n >= 2))
    def _():
        for cp in stores(i - 1, 1 - slot):
            cp.wait()


def _experts(xs, blk_e, nused, nvalid, w1, b1, w2, b2):
    P = xs.shape[0]
    D = w1.shape[1]
    bm = EXPERT_ROWS
    nb = P // bm
    per_expert = lambda a: pl.BlockSpec((1,) + a.shape[1:], lambda i, blk, nu, nv: (blk[i], 0, 0))
    return pl.pallas_call(
        _expert_kernel,
        grid_spec=pltpu.PrefetchScalarGridSpec(
            num_scalar_prefetch=3,
            grid=(nb,),
            in_specs=[pl.BlockSpec(memory_space=pl.ANY), per_expert(w1), per_expert(b1), per_expert(w2),
                      per_expert(b2)],
            out_specs=pl.BlockSpec(memory_space=pl.ANY),
            scratch_shapes=[pltpu.VMEM((2, bm, D), F32), pltpu.VMEM((2, bm, D), F32),
                            pltpu.VMEM(w1.shape[1:], BF16), pltpu.VMEM(w2.shape[1:], BF16),
                            pltpu.SemaphoreType.DMA((2,)), pltpu.SemaphoreType.DMA((2,))],
        ),
        out_shape=jax.ShapeDtypeStruct((P,) + ROW_TILE, F32),
        compiler_params=_cparams(("arbitrary",)),
        name="moe_experts",
    )(blk_e, nused, nvalid, xs, w1, b1, w2, b2)


def _sc_gather_rows(table, idx):
    n = idx.shape[0]
    info = plsc.get_sparse_core_info()
    workers = info.num_cores * info.num_subcores
    per_worker = n // workers
    assert n % workers == 0 and per_worker % SC_INDEX_WINDOW == 0
    mesh = plsc.VectorSubcoreMesh(core_axis_name="c", subcore_axis_name="s")

    @functools.partial(
        pl.kernel, mesh=mesh,
        out_type=jax.ShapeDtypeStruct((n,) + table.shape[1:], table.dtype),
        scratch_types=[pltpu.VMEM((SC_INDEX_WINDOW,), I32)]
        + [pltpu.VMEM((SC_GATHER_ROWS,) + table.shape[1:], table.dtype)] * 2 + [pltpu.SemaphoreType.DMA] * 4)
    def gather(table_hbm, idx_hbm, out_hbm, idx_v, *scratch):
        bufs, fetch_sem, store_sem = scratch[0:2], scratch[2:4], scratch[4:6]
        steps = SC_INDEX_WINDOW // SC_GATHER_ROWS
        worker = lax.axis_index("s") * info.num_cores + lax.axis_index("c")
        first = worker * (per_worker // SC_INDEX_WINDOW)

        @pl.loop(0, per_worker // SC_INDEX_WINDOW)
        def _(j):
            window = first + j
            pltpu.sync_copy(idx_hbm.at[window], idx_v)

            def fetch(c):
                src = table_hbm.at[idx_v.at[pl.ds(c * SC_GATHER_ROWS, SC_GATHER_ROWS)]]
                return pltpu.async_copy(src, bufs[c % 2], fetch_sem[c % 2])

            fetches = {0: fetch(0)}
            stores = {}
            for c in range(steps):
                if c + 1 < steps:
                    if c - 1 in stores:
                        stores.pop(c - 1).wait()
                    fetches[c + 1] = fetch(c + 1)
                fetches.pop(c).wait()
                dst = out_hbm.at[pl.ds(window * SC_INDEX_WINDOW + c * SC_GATHER_ROWS, SC_GATHER_ROWS)]
                stores[c] = pltpu.async_copy(bufs[c % 2], dst, store_sem[c % 2])
            for cp in stores.values():
                cp.wait()

    return gather(table, idx.reshape(n // SC_INDEX_WINDOW, SC_INDEX_WINDOW))


def _finish_kernel(rows_ref, tg_ref, x1_ref, gfin_ref, y_ref, buf, sem, *, final):
    i = pl.program_id(0)
    n = pl.num_programs(0)
    tm = x1_ref.shape[0]
    slot = i % 2

    def loads(tile, s):
        cps = []
        for k in range(TOP_K):
            start = pl.multiple_of(k * (n * tm) + tile * tm, tm)
            cps += _relayout_copies(buf.at[s, k], rows_ref.at[pl.ds(start, tm)], sem.at[s], to_tiled=False)
        return cps

    @pl.when(i == 0)
    def _():
        for cp in loads(0, 0):
            cp.start()

    @pl.when(i + 1 < n)
    def _():
        for cp in loads(i + 1, 1 - slot):
            cp.start()

    for cp in loads(i, slot):
        cp.wait()
    gates = tg_ref[...].T
    x = x1_ref[...]
    for k in range(TOP_K):
        x = x + buf[slot, k] * gates[:, k:k + 1]
    y_ref[...] = _rms(x, gfin_ref[...]) if final else x


def _finish(rows, tg, x1, gfin, final, tm=TOKEN_TILE):
    T, D = x1.shape
    return pl.pallas_call(
        functools.partial(_finish_kernel, final=final),
        grid=(T // tm,),
        in_specs=[pl.BlockSpec(memory_space=pl.ANY),
                  pl.BlockSpec((SUBLANES, tm), lambda i: (0, i)),
                  pl.BlockSpec((tm, D), lambda i: (i, 0)),
                  _full(gfin.shape)],
        out_specs=pl.BlockSpec((tm, D), lambda i: (i, 0)),
        out_shape=jax.ShapeDtypeStruct((T, D), F32),
        scratch_shapes=[pltpu.VMEM((2, TOP_K, tm, D), F32), pltpu.SemaphoreType.DMA((2,))],
        compiler_params=_cparams(("arbitrary",)),
        name="moe_finish",
    )(rows, tg, x1, gfin)


def _moe(h2, ti, rk, tg, cnt, x1, moe_w, gfin, final):
    T = x1.shape[0]
    bm = EXPERT_ROWS
    n_assign = T * TOP_K
    nb = -(-(n_assign + N_EXPERTS * (bm - 1)) // bm)
    counts = cnt[:, 0].astype(I32)
    padded = (counts + bm - 1) // bm * bm
    pad_end = jnp.cumsum(padded).astype(I32)
    pad_start = pad_end - padded
    experts = jnp.arange(N_EXPERTS, dtype=I32)[:, None, None]
    dest = rk + jnp.sum(jnp.where(ti[None] == experts, pad_start[:, None, None], 0), axis=0)
    nused = (pad_end[-1:] // bm).astype(I32)
    blk_start = jnp.minimum(jnp.arange(nb, dtype=I32), nused[0] - 1) * bm
    blk_e = jnp.minimum(jnp.sum((pad_end[None, :] <= blk_start[:, None]).astype(I32), axis=1), N_EXPERTS - 1)
    live_end = pad_start + counts
    nvalid = jnp.clip(live_end[blk_e] - jnp.arange(nb, dtype=I32) * bm, 0, bm).astype(I32)
    xs = _sc_scatter_rows(h2, dest, nb * bm)
    ys = _experts(xs, blk_e, nused, nvalid, *moe_w)
    rows = _sc_gather_rows(ys, dest.reshape(-1))
    return _finish(rows, tg, x1, gfin, final)


def _split_in_proj(w):
    sizes = (DA_QK_WIDTH, DA_QK_WIDTH, DA_WIDTH, GLA_KEY_WIDTH, GLA_KEY_WIDTH, GLA_WIDTH, GLA_WIDTH,
             GLA_GATE_RANK, GLA_GATE_RANK, D_MODEL, D_MODEL)
    assert w.shape[-1] == sum(sizes)
    out, o = [], 0
    for s in sizes:
        out.append(w[:, o:o + s])
        o += s
    return out


def _prep_layer(l, norm_mix_g, w_in, lambda_q1, lambda_k1, lambda_q2, lambda_k2, diff_subln_g,
                gla_gate_up_f, gla_gate_bias_f, gla_gate_up_b, gla_gate_bias_b, gla_norm_g,
                w_proj_a, w_proj_b, w_out, norm_ffn_g, router_w, router_b, moe_w1, moe_b1, moe_w2, moe_b2):
    dq, dk, dv, lq, lk, lv, lr, laf, lab, ga, gb = _split_in_proj(w_in[l])
    bf = lambda a: a.astype(BF16)
    r = GLA_GATE_RANK
    gup = jnp.zeros((LANES, 2 * GLA_KEY_WIDTH), F32)
    gup = gup.at[0:r, 0:GLA_KEY_WIDTH].set(gla_gate_up_f[l]).at[r:2 * r, GLA_KEY_WIDTH:].set(gla_gate_up_b[l])
    lamv = jnp.zeros((SUBLANES, LANES), F32)
    for i, v in enumerate((lambda_q1, lambda_k1, lambda_q2, lambda_k2)):
        lamv = lamv.at[i, 0:DA_HEAD_DIM].set(v[l].astype(F32))
    inproj = dict(
        norm_g=norm_mix_g[l][None, :],
        wqk=bf(jnp.concatenate([dq * (DA_HEAD_DIM ** -0.5), dk], 1)),
        wvt=bf(dv.T),
        wlqk=bf(jnp.concatenate([lq, lk], 1)),
        wlvr=bf(jnp.concatenate([lv, lr], 1)),
        wg=bf(jnp.concatenate([ga, gb], 1)),
        wlow=bf(jnp.concatenate([laf, lab, jnp.zeros((D_MODEL, LANES - 2 * r), F32)], 1)),
        gup=bf(gup),
        gbias=jnp.concatenate([gla_gate_bias_f[l], gla_gate_bias_b[l]])[None, :],
    )
    merge = dict(
        subg=diff_subln_g[l][None, :], glag=gla_norm_g[l][None, :],
        wpa=bf(w_proj_a[l]), wpb=bf(w_proj_b[l]), wout=bf(w_out[l]),
        nffn=norm_ffn_g[l][None, :], rwt=bf(router_w[l].T), rb=router_b[l][:, None],
    )
    b1 = moe_b1[l].reshape(N_EXPERTS, -1, LANES, 2).transpose(0, 1, 3, 2).reshape(N_EXPERTS, 1, -1)
    moe_w = (moe_w1[l], b1, moe_w2[l], moe_b2[l][:, None, :])
    return inproj, lamv, merge, moe_w


def _trunk(x, layers, norm_final_g):
    B, L, D = x.shape
    T = B * L
    cos_t, sin_t = _rope_tables(L)
    x2 = x.reshape(T, D)
    gfin = norm_final_g[None, :]
    for l, (inproj, lamv, merge, moe_w) in enumerate(layers):
        lam_init = 0.8 - 0.6 * math.exp(-0.3 * l)
        qk, vt, lqk, lvr, gfb, sg = _inproj(x2, L, cos_t=cos_t, sin_t=sin_t, **inproj)
        oa = _diff_attention(qk.reshape(B, L, -1), vt, lamv, lam_init)
        of, ob = _gla(lqk.reshape(B, L, -1), lvr.reshape(B, L, -1), gfb.reshape(B, L, -1))
        x1, h2, ti, rk, tg, cnt = _merge(x2, oa.reshape(T, -1), of.reshape(T, -1), ob.reshape(T, -1),
                                         lvr, sg, lam_init=lam_init, **merge)
        x2 = _moe(h2, ti, rk, tg, cnt, x1, moe_w, gfin, final=(l == len(layers) - 1))
    return x2.reshape(B, L, D)


def kernel(x_prompt, x_sample, norm_mix_g, w_in, lambda_q1, lambda_k1, lambda_q2, lambda_k2, diff_subln_g,
           gla_gate_up_f, gla_gate_bias_f, gla_gate_up_b, gla_gate_bias_b, gla_norm_g, w_proj_a, w_proj_b,
           w_out, norm_ffn_g, router_w, router_b, moe_w1, moe_b1, moe_w2, moe_b2, norm_final_g):
    depth = w_in.shape[0]
    layers = [_prep_layer(l, norm_mix_g, w_in, lambda_q1, lambda_k1, lambda_q2, lambda_k2, diff_subln_g,
                          gla_gate_up_f, gla_gate_bias_f, gla_gate_up_b, gla_gate_bias_b, gla_norm_g,
                          w_proj_a, w_proj_b, w_out, norm_ffn_g, router_w, router_b,
                          moe_w1, moe_b1, moe_w2, moe_b2) for l in range(depth)]
    return _trunk(x_prompt, layers, norm_final_g), _trunk(x_sample, layers, norm_final_g)
```

```python
import functools
import math

import jax
import jax.numpy as jnp
from jax import lax
from jax.experimental import pallas as pl
from jax.experimental.pallas import tpu as pltpu
from jax.experimental.pallas import tpu_sc as plsc

F32 = jnp.float32
BF16 = jnp.bfloat16
I32 = jnp.int32

D_MODEL = 1024
DA_HEADS = 4
DA_HEAD_DIM = 64
DA_VDIM = 128
DA_QK_WIDTH = DA_HEADS * 2 * DA_HEAD_DIM
DA_WIDTH = DA_HEADS * DA_VDIM
ROPE_THETA = 10000.0
GLA_HEADS = 4
GLA_KDIM = 64
GLA_VDIM = 128
GLA_KEY_WIDTH = GLA_HEADS * GLA_KDIM
GLA_WIDTH = GLA_HEADS * GLA_VDIM
GLA_GATE_RANK = 16
GLA_GATE_NORMALIZER = 16.0
GLA_CHUNK = 64
N_EXPERTS = 32
TOP_K = 4
SWIGLU_ALPHA = 1.702
SWIGLU_LIMIT = 7.0
NORM_EPS = 1e-5
LANES = 128
SUBLANES = 8

BF16_SUBLANES = 16
MXU_TILE = 256
LOG2_E = 1.4426950408889634
VT_ROWS = DA_VDIM + BF16_SUBLANES

VMEM_LIMIT = 56 * 1024 * 1024

TOKEN_TILE = 512
ATTN_KEY_CHUNK = 512
EXPERT_ROWS = 512

NT_DIMS = (((1,), (1,)), ((), ()))
TN_DIMS = (((0,), (0,)), ((), ()))


def _cparams(sem):
    return pltpu.CompilerParams(dimension_semantics=sem, vmem_limit_bytes=VMEM_LIMIT)


def _full(shape):
    nd = len(shape)
    return pl.BlockSpec(shape, lambda *_: (0,) * nd)


def _rms(x, g):
    return x * lax.rsqrt(jnp.mean(x * x, axis=-1, keepdims=True) + NORM_EPS) * g


ROW_TILE = (SUBLANES, LANES)


def _relayout_copies(flat_ref, tiled_ref, sem, to_tiled):
    cps = []
    for c in range(SUBLANES):
        flat = flat_ref.at[:, pl.ds(c * LANES, LANES)]
        tiled = tiled_ref.at[:, c, :]
        cps.append(pltpu.make_async_copy(flat, tiled, sem) if to_tiled else pltpu.make_async_copy(tiled, flat, sem))
    return cps


def _log_sigmoid(x):
    return jnp.minimum(x, 0.0) - jnp.log1p(jnp.exp(-jnp.abs(x)))


def _inproj_kernel(x_ref, g_ref, cos_ref, sin_ref, wqk_ref, wvt_ref, wlqk_ref, wlvr_ref,
                   wg_ref, wlow_ref, gup_ref, gbias_ref,
                   qk_ref, vt_ref, lqk_ref, lvr_ref, gfb_ref, sg_ref):
    h = _rms(x_ref[...], g_ref[...]).astype(BF16)

    cos = cos_ref[...]
    sin_signed = sin_ref[...]
    first_half = (lax.broadcasted_iota(I32, (1, LANES), 1) % DA_HEAD_DIM) < (DA_HEAD_DIM // 2)
    for c in range(2 * DA_QK_WIDTH // LANES):
        z = jnp.dot(h, wqk_ref[:, c * LANES:(c + 1) * LANES], preferred_element_type=F32)
        if c < DA_QK_WIDTH // LANES:
            z = z * LOG2_E
        partner = jnp.where(first_half, pltpu.roll(z, LANES - DA_HEAD_DIM // 2, 1),
                            pltpu.roll(z, DA_HEAD_DIM // 2, 1))
        qk_ref[:, c * LANES:(c + 1) * LANES] = (z * cos + partner * sin_signed).astype(BF16)

    vt = lax.dot_general(wvt_ref[...], h, NT_DIMS, preferred_element_type=F32).astype(BF16)
    for hd in range(DA_HEADS):
        vt_ref[0, hd * VT_ROWS:hd * VT_ROWS + DA_VDIM, :] = vt[hd * DA_VDIM:(hd + 1) * DA_VDIM]
        vt_ref[0, hd * VT_ROWS + DA_VDIM:(hd + 1) * VT_ROWS, :] = jnp.ones((VT_ROWS - DA_VDIM, vt.shape[1]), BF16)
    lqk_ref[...] = jnp.dot(h, wlqk_ref[...], preferred_element_type=F32)
    lvr_ref[...] = jnp.dot(h, wlvr_ref[...], preferred_element_type=F32).astype(BF16)
    sg_ref[...] = jax.nn.sigmoid(jnp.dot(h, wg_ref[...], preferred_element_type=F32)).astype(BF16)
    low = jnp.dot(h, wlow_ref[...], preferred_element_type=F32).astype(BF16)
    pre = jnp.dot(low, gup_ref[...], preferred_element_type=F32) + gbias_ref[...]
    gfb_ref[...] = _log_sigmoid(pre) * (1.0 / GLA_GATE_NORMALIZER)


def _inproj(x2, seq_len, norm_g, wqk, wvt, wlqk, wlvr, wg, wlow, gup, gbias, cos_t, sin_t):
    T, D = x2.shape
    tm = TOKEN_TILE
    assert T % tm == 0 and seq_len % tm == 0
    nl = seq_len // tm
    row = lambda n: pl.BlockSpec((tm, n), lambda i: (i, 0))
    out_shape = (
        jax.ShapeDtypeStruct((T, 2 * DA_QK_WIDTH), BF16),
        jax.ShapeDtypeStruct((T // tm, DA_HEADS * VT_ROWS, tm), BF16),
        jax.ShapeDtypeStruct((T, 2 * GLA_KEY_WIDTH), F32),
        jax.ShapeDtypeStruct((T, 2 * GLA_WIDTH), BF16),
        jax.ShapeDtypeStruct((T, 2 * GLA_KEY_WIDTH), F32),
        jax.ShapeDtypeStruct((T, 2 * D_MODEL), BF16),
    )
    return pl.pallas_call(
        _inproj_kernel,
        grid=(T // tm,),
        in_specs=[row(D), _full(norm_g.shape),
                  pl.BlockSpec((tm, LANES), lambda i: (i % nl, 0)),
                  pl.BlockSpec((tm, LANES), lambda i: (i % nl, 0)),
                  _full(wqk.shape), _full(wvt.shape), _full(wlqk.shape), _full(wlvr.shape),
                  _full(wg.shape), _full(wlow.shape), _full(gup.shape), _full(gbias.shape)],
        out_specs=(row(2 * DA_QK_WIDTH), pl.BlockSpec((1, DA_HEADS * VT_ROWS, tm), lambda i: (i, 0, 0)),
                   row(2 * GLA_KEY_WIDTH), row(2 * GLA_WIDTH), row(2 * GLA_KEY_WIDTH), row(2 * D_MODEL)),
        out_shape=out_shape,
        compiler_params=_cparams(("parallel",)),
        name="inproj",
    )(x2, norm_g, cos_t, sin_t, wqk, wvt, wlqk, wlvr, wg, wlow, gup, gbias)


def _rope_tables(seq_len):
    d = DA_HEAD_DIM
    inv = ROPE_THETA ** (-jnp.arange(0, d, 2, dtype=F32) / d)
    ang = jnp.arange(seq_len, dtype=F32)[:, None] * inv[None, :]
    cos = jnp.concatenate([jnp.cos(ang)] * (2 * LANES // d), -1)
    sin = jnp.sin(ang)
    sin_signed = jnp.concatenate([-sin, sin] * (LANES // d), -1)
    return cos, sin_signed


def _attn_kernel(lamv_ref, q_ref, k_ref, vt_ref, o_ref, acc_sc, sa_sc, sb_sc, *, lam_init):
    n_vt, _, vt_w = vt_ref.shape
    tk = ATTN_KEY_CHUNK
    nk = n_vt * vt_w // tk
    tq = q_ref.shape[1]
    q = q_ref[0]
    lane = lax.broadcasted_iota(I32, (1, LANES), 1)
    zeros = jnp.zeros_like(q)
    qms = (jnp.where(lane < DA_HEAD_DIM, q, zeros), jnp.where(lane >= DA_HEAD_DIM, q, zeros))
    acc_sc[...] = jnp.zeros(acc_sc.shape, F32)

    def scores(j, dst_ref):
        kj = k_ref[0, j * tk:(j + 1) * tk, :]
        col_max = []
        for mp in range(2):
            st = lax.dot_general(kj, qms[mp], NT_DIMS, preferred_element_type=F32)
            dst_ref[mp] = st
            col_max.append(jnp.max(st, axis=0, keepdims=True))
        return col_max

    def softmax_pv(j, src_ref, ms, col_max):
        def v_tile(r):
            key = j * tk + r
            return vt_ref[key // vt_w, :, key % vt_w:key % vt_w + MXU_TILE]

        new = []
        for mp in range(2):
            m_new = jnp.maximum(ms[mp], col_max[mp])
            alpha = jnp.exp2(ms[mp] - m_new)
            for cols in (slice(c, c + MXU_TILE) for c in range(0, tq, MXU_TILE)):
                part = None
                for r in range(0, tk, MXU_TILE):
                    p = jnp.exp2((src_ref[mp, r:r + MXU_TILE, cols] - m_new[:, cols]).astype(BF16))
                    d = jnp.dot(v_tile(r), p, preferred_element_type=F32)
                    part = d if part is None else part + d
                acc_sc[mp, :, cols] = acc_sc[mp, :, cols] * alpha[:, cols] + part
            new.append(m_new)
        return tuple(new)

    bufs = (sa_sc, sb_sc)
    col_max = scores(0, bufs[0])
    ms = tuple(jnp.full((1, tq), -jnp.inf, F32) for _ in range(2))
    for j in range(nk):
        nxt_max = scores(j + 1, bufs[(j + 1) % 2]) if j + 1 < nk else None
        ms = softmax_pv(j, bufs[j % 2], ms, col_max)
        col_max = nxt_max

    lv = lamv_ref[...]
    lam = (jnp.exp(jnp.sum(lv[0:1] * lv[1:2], axis=-1, keepdims=True))
           - jnp.exp(jnp.sum(lv[2:3] * lv[3:4], axis=-1, keepdims=True)) + lam_init)
    a0, a1 = acc_sc[0], acc_sc[1]
    o_ref[0] = (a0[:DA_VDIM] / a0[DA_VDIM:DA_VDIM + 1] - lam * (a1[:DA_VDIM] / a1[DA_VDIM:DA_VDIM + 1])).T


def _diff_attention(qk3, vt3, lamv, lam_init, tq=512):
    B, L, _ = qk3.shape
    vt_w = vt3.shape[-1]
    tk = ATTN_KEY_CHUNK
    assert L % tk == 0 and tk % vt_w == 0
    return pl.pallas_call(
        functools.partial(_attn_kernel, lam_init=lam_init),
        grid=(B, DA_HEADS, L // tq),
        in_specs=[_full(lamv.shape),
                  pl.BlockSpec((1, tq, LANES), lambda b, h, i: (b, i, h)),
                  pl.BlockSpec((1, L, LANES), lambda b, h, i: (b, 0, DA_HEADS + h)),
                  pl.BlockSpec((L // vt_w, VT_ROWS, vt_w), lambda b, h, i: (b, h, 0))],
        out_specs=pl.BlockSpec((1, tq, DA_VDIM), lambda b, h, i: (b, i, h)),
        out_shape=jax.ShapeDtypeStruct((B, L, DA_WIDTH), F32),
        scratch_shapes=[pltpu.VMEM((2, VT_ROWS, tq), F32), pltpu.VMEM((2, tk, tq), F32),
                        pltpu.VMEM((2, tk, tq), F32)],
        compiler_params=_cparams(("parallel", "parallel", "arbitrary")),
        name="diff_attention",
    )(lamv, qk3, qk3, vt3)


def _split3_bf16(x):
    hi = x.astype(BF16)
    r1 = x - hi.astype(F32)
    mid = r1.astype(BF16)
    return hi, mid, (r1 - mid.astype(F32)).astype(BF16)


def _gla_local(qk_ref, vr_ref, g_ref, *, reverse):
    C = GLA_CHUNK
    tb = qk_ref.shape[1]
    nchunk = tb // C
    row = lax.broadcasted_iota(I32, (tb, tb), 0)
    col = lax.broadcasted_iota(I32, (tb, tb), 1)
    same_chunk = (row // C) == (col // C)
    tri = same_chunk & ((col >= row) if reverse else (col <= row))
    tri_b = jnp.where(tri, 1.0, 0.0).astype(BF16)
    lane = lax.broadcasted_iota(I32, (1, LANES), 1)
    goff = GLA_KEY_WIDTH if reverse else 0

    g = g_ref[0, :, goff:goff + GLA_KEY_WIDTH]
    q = qk_ref[0, :, 0:GLA_KEY_WIDTH] * (GLA_KDIM ** -0.5)
    k = qk_ref[0, :, GLA_KEY_WIDTH:2 * GLA_KEY_WIDTH]
    v = vr_ref[0]
    b = sum(jnp.dot(tri_b, part, preferred_element_type=F32) for part in _split3_bf16(g))
    last_row, ref_row = (0, C // 2) if reverse else (C - 1, C // 2 - 1)
    per_chunk = lambda r: jnp.concatenate(
        [jnp.broadcast_to(b[c * C + r:c * C + r + 1], (C, GLA_KEY_WIDTH)) for c in range(nchunk)], axis=0)
    b_last, b_ref = per_chunk(last_row), per_chunk(ref_row)
    q_in = q * jnp.exp(b - b_ref)
    k_in = (k * jnp.exp(b_ref - b)).astype(BF16)
    k_dec = k * jnp.exp(b_last - b)
    q_dec = q * jnp.exp(b)
    dec = [jnp.exp(b[c * C + last_row:c * C + last_row + 1]) for c in range(nchunk)]

    heads = []
    for h in range(GLA_HEADS):
        ps = slice((h // 2) * LANES, (h // 2 + 1) * LANES)
        keep = (lane < GLA_KDIM) if h % 2 == 0 else (lane >= GLA_KDIM)
        zero = jnp.zeros((tb, LANES), F32)
        heads.append(dict(
            ps=ps, vs=slice(h * GLA_VDIM, (h + 1) * GLA_VDIM),
            q_in=jnp.where(keep, q_in[:, ps], zero).astype(BF16),
            q_dec=jnp.where(keep, q_dec[:, ps], zero).astype(BF16),
            k_dec=jnp.where(keep, k_dec[:, ps], zero).astype(BF16)))
    scores = [lax.dot_general(hd["q_in"], k_in[:, hd["ps"]], NT_DIMS, preferred_element_type=F32) for hd in heads]
    kv_t = [[lax.dot_general(v[c * C:(c + 1) * C, hd["vs"]], hd["k_dec"][c * C:(c + 1) * C], TN_DIMS,
                             preferred_element_type=F32) for hd in heads] for c in range(nchunk)]
    intra = [jnp.dot(jnp.where(tri, s, 0.0).astype(BF16), v[:, hd["vs"]], preferred_element_type=F32)
             for s, hd in zip(scores, heads)]
    return heads, dec, kv_t, intra


def _gla_scan(heads, dec, kv_t, st_ref, *, reverse):
    nchunk = len(dec)
    state = [st_ref[h] for h in range(GLA_HEADS)]
    entering = [None] * nchunk
    for c in (range(nchunk - 1, -1, -1) if reverse else range(nchunk)):
        entering[c] = [s.astype(BF16) for s in state]
        state = [s * dec[c][:, hd["ps"]] + kv_t[c][h] for h, (s, hd) in enumerate(zip(state, heads))]
    for h in range(GLA_HEADS):
        st_ref[h] = state[h]
    return entering


def _gla_emit(heads, intra, entering, o_ref):
    C = GLA_CHUNK
    for c, states in enumerate(entering):
        rows = slice(c * C, (c + 1) * C)
        for hd, o_intra, s_t in zip(heads, intra, states):
            inter = lax.dot_general(hd["q_dec"][rows], s_t, NT_DIMS, preferred_element_type=F32)
            o_ref[0, rows, hd["vs"]] = o_intra[rows] + inter


def _gla_kernel(qkf_ref, vrf_ref, gf_ref, qkb_ref, vrb_ref, gb_ref, of_ref, ob_ref, sf_sc, sb_sc):
    @pl.when(pl.program_id(1) == 0)
    def _():
        sf_sc[...] = jnp.zeros(sf_sc.shape, F32)
        sb_sc[...] = jnp.zeros(sb_sc.shape, F32)

    hf, df, kf, inf = _gla_local(qkf_ref, vrf_ref, gf_ref, reverse=False)
    hb, db, kb, inb = _gla_local(qkb_ref, vrb_ref, gb_ref, reverse=True)
    ef = _gla_scan(hf, df, kf, sf_sc, reverse=False)
    eb = _gla_scan(hb, db, kb, sb_sc, reverse=True)
    _gla_emit(hf, inf, ef, of_ref)
    _gla_emit(hb, inb, eb, ob_ref)


def _gla(lqk3, lvr3, gfb3, tb=256):
    B, L, _ = lqk3.shape
    nb = L // tb
    fwd = lambda n: pl.BlockSpec((1, tb, n), lambda b, i: (b, i, 0))
    bwd = lambda n: pl.BlockSpec((1, tb, n), lambda b, i: (b, nb - 1 - i, 0))
    out = jax.ShapeDtypeStruct((B, L, GLA_WIDTH), F32)
    return pl.pallas_call(
        _gla_kernel,
        grid=(B, nb),
        in_specs=[fwd(2 * GLA_KEY_WIDTH), fwd(GLA_WIDTH), fwd(2 * GLA_KEY_WIDTH),
                  bwd(2 * GLA_KEY_WIDTH), bwd(GLA_WIDTH), bwd(2 * GLA_KEY_WIDTH)],
        out_specs=(fwd(GLA_WIDTH), bwd(GLA_WIDTH)),
        out_shape=(out, out),
        scratch_shapes=[pltpu.VMEM((GLA_HEADS, GLA_VDIM, LANES), F32),
                        pltpu.VMEM((GLA_HEADS, GLA_VDIM, LANES), F32)],
        compiler_params=_cparams(("parallel", "arbitrary")),
        name="gla",
    )(lqk3, lvr3, gfb3, lqk3, lvr3, gfb3)


def _head_norm(z, g):
    return jnp.concatenate(
        [_rms(z[:, h * LANES:(h + 1) * LANES], g) for h in range(z.shape[1] // LANES)], axis=1)


def _merge_kernel(x_ref, oa_ref, of_ref, ob_ref, r_ref, sg_ref, subg_ref, glag_ref, wpa_ref, wpb_ref,
                  wout_ref, nffn_ref, rwt_ref, rb_ref,
                  x1_ref, h2_ref, ti_ref, rk_ref, tg_ref, cnt_ref, carry_sc, h2_sc, h2_sem, *, lam_init):
    tm = x_ref.shape[0]
    step = pl.program_id(0)

    def h2_stores(tile):
        rows = h2_ref.at[pl.ds(pl.multiple_of(tile * tm, tm), tm)]
        return _relayout_copies(h2_sc, rows, h2_sem, to_tiled=True)

    @pl.when(step == 0)
    def _():
        carry_sc[...] = jnp.zeros(carry_sc.shape, F32)

    @pl.when(step > 0)
    def _():
        for cp in h2_stores(step - 1):
            cp.wait()

    oa = (_head_norm(oa_ref[...], subg_ref[...]) * (1.0 - lam_init)).astype(BF16)
    ob = _head_norm(of_ref[...] + ob_ref[...], glag_ref[...]) * jax.nn.silu(r_ref[...].astype(F32))
    pa = jnp.dot(oa, wpa_ref[...], preferred_element_type=F32)
    pb = jnp.dot(ob.astype(BF16), wpb_ref[...], preferred_element_type=F32)
    sg = sg_ref[...].astype(F32)
    merged = sg[:, :D_MODEL] * pa + sg[:, D_MODEL:] * pb
    x1 = x_ref[...] + jnp.dot(merged.astype(BF16), wout_ref[...], preferred_element_type=F32)
    x1_ref[...] = x1
    h2 = _rms(x1, nffn_ref[...])
    h2_sc[...] = h2
    for cp in h2_stores(step):
        cp.start()

    logits = lax.dot_general(rwt_ref[...], h2.astype(BF16), NT_DIMS, preferred_element_type=F32) + rb_ref[...]
    eidx = lax.broadcasted_iota(I32, logits.shape, 0)
    vals, sels = [], []
    for k in range(TOP_K):
        mk = jnp.max(logits, axis=0, keepdims=True)
        ik = jnp.min(jnp.where(logits == mk, eidx, N_EXPERTS), axis=0, keepdims=True)
        sel = eidx == ik
        logits = jnp.where(sel, -jnp.inf, logits)
        vals.append(mk)
        sels.append(sel)
        ti_ref[k:k + 1, :] = ik
    ex = [jnp.exp(v - vals[0]) for v in vals]
    denom = ex[0] + ex[1] + ex[2] + ex[3]
    for k in range(TOP_K):
        tg_ref[k:k + 1, :] = ex[k] / denom
    tg_ref[TOP_K:, :] = jnp.zeros((tg_ref.shape[0] - TOP_K, tm), F32)

    multi = (sels[0] | sels[1] | sels[2] | sels[3])
    multi_f = jnp.where(multi, 1.0, 0.0)
    before = lax.broadcasted_iota(I32, (tm, tm), 0) < lax.broadcasted_iota(I32, (tm, tm), 1)
    cum = jnp.dot(multi_f.astype(BF16), jnp.where(before, 1.0, 0.0).astype(BF16), preferred_element_type=F32)
    tot = carry_sc[:, 0:1] + cum
    for k in range(TOP_K):
        rk_ref[k:k + 1, :] = jnp.sum(jnp.where(sels[k], tot, 0.0), axis=0, keepdims=True).astype(I32)
    carry_sc[...] = carry_sc[...] + jnp.sum(multi_f, axis=1, keepdims=True)
    cnt_ref[...] = carry_sc[...]

    @pl.when(step == pl.num_programs(0) - 1)
    def _():
        for cp in h2_stores(step):
            cp.wait()


def _merge(x2, oa, of, ob, lvr, sg, subg, glag, wpa, wpb, wout, nffn, rwt, rb, lam_init):
    T, D = x2.shape
    tm = TOKEN_TILE
    row = lambda n: pl.BlockSpec((tm, n), lambda i: (i, 0))
    col = lambda n: pl.BlockSpec((n, tm), lambda i: (0, i))
    out_shape = (
        jax.ShapeDtypeStruct((T, D), F32),
        jax.ShapeDtypeStruct((T,) + ROW_TILE, F32),
        jax.ShapeDtypeStruct((TOP_K, T), I32),
        jax.ShapeDtypeStruct((TOP_K, T), I32),
        jax.ShapeDtypeStruct((SUBLANES, T), F32),
        jax.ShapeDtypeStruct((N_EXPERTS, LANES), F32),
    )
    return pl.pallas_call(
        functools.partial(_merge_kernel, lam_init=lam_init),
        grid=(T // tm,),
        in_specs=[row(D), row(DA_WIDTH), row(GLA_WIDTH), row(GLA_WIDTH),
                  pl.BlockSpec((tm, GLA_WIDTH), lambda i: (i, 1)), row(2 * D),
                  _full(subg.shape), _full(glag.shape), _full(wpa.shape), _full(wpb.shape),
                  _full(wout.shape), _full(nffn.shape), _full(rwt.shape), _full(rb.shape)],
        out_specs=(row(D), pl.BlockSpec(memory_space=pl.ANY),
                   col(TOP_K), col(TOP_K), col(SUBLANES), _full((N_EXPERTS, LANES))),
        out_shape=out_shape,
        scratch_shapes=[pltpu.VMEM((N_EXPERTS, LANES), F32), pltpu.VMEM((tm, D), F32),
                        pltpu.SemaphoreType.DMA(())],
        compiler_params=_cparams(("arbitrary",)),
        name="merge_router",
    )(x2, oa, of, ob, lvr, sg, subg, glag, wpa, wpb, wout, nffn, rwt, rb)


SC_INDEX_WINDOW = 128
SC_GATHER_ROWS = 16


def _sc_scatter_rows(rows, dest, n_slots):
    T = rows.shape[0]
    info = plsc.get_sparse_core_info()
    workers = info.num_cores * info.num_subcores
    per_worker = T // workers
    assert T % workers == 0 and per_worker % SC_INDEX_WINDOW == 0
    windows = T // SC_INDEX_WINDOW
    steps = SC_INDEX_WINDOW // SC_GATHER_ROWS
    mesh = plsc.VectorSubcoreMesh(core_axis_name="c", subcore_axis_name="s")

    @functools.partial(
        pl.kernel, mesh=mesh,
        out_type=jax.ShapeDtypeStruct((n_slots,) + rows.shape[1:], rows.dtype),
        scratch_types=[pltpu.VMEM((steps, SC_GATHER_ROWS), I32)] * TOP_K
        + [pltpu.VMEM((SC_GATHER_ROWS,) + rows.shape[1:], rows.dtype)] * 2 + [pltpu.SemaphoreType.DMA] * 4)
    def scatter(rows_hbm, dest_hbm, out_hbm, *scratch):
        idx_v, bufs = scratch[:TOP_K], scratch[TOP_K:TOP_K + 2]
        load_sem, send_sem = scratch[TOP_K + 2:TOP_K + 4], scratch[TOP_K + 4:TOP_K + 6]
        worker = lax.axis_index("s") * info.num_cores + lax.axis_index("c")
        first = worker * (per_worker // SC_INDEX_WINDOW)

        @pl.loop(0, per_worker // SC_INDEX_WINDOW)
        def _(j):
            window = first + j
            for k in range(TOP_K):
                pltpu.sync_copy(dest_hbm.at[k * windows + window], idx_v[k])

            def load(c):
                src = rows_hbm.at[pl.ds(window * SC_INDEX_WINDOW + c * SC_GATHER_ROWS, SC_GATHER_ROWS)]
                return pltpu.async_copy(src, bufs[c % 2], load_sem[c % 2])

            loads = {0: load(0)}
            sends = {}
            for c in range(steps):
                if c + 1 < steps:
                    for cp in sends.pop(c - 1, ()):
                        cp.wait()
                    loads[c + 1] = load(c + 1)
                loads.pop(c).wait()
                sends[c] = [pltpu.async_copy(bufs[c % 2], out_hbm.at[idx_v[k].at[c]], send_sem[c % 2])
                            for k in range(TOP_K)]
            for pending in sends.values():
                for cp in pending:
                    cp.wait()

    return scatter(rows, dest.reshape(TOP_K * windows, steps, SC_GATHER_ROWS))


def _deinterleave_bf16(w_ref, o_ref):
    n = 2 * LANES
    src = lax.broadcasted_iota(I32, (n, n), 0)
    dst = lax.broadcasted_iota(I32, (n, n), 1)
    perm = jnp.where(src == jnp.where(dst < LANES, 2 * dst, 2 * (dst - LANES) + 1), 1.0, 0.0).astype(BF16)
    for c in range(w_ref.shape[1] // n):
        w = w_ref[:, c * n:(c + 1) * n].astype(BF16)
        o_ref[:, c * n:(c + 1) * n] = jnp.dot(w, perm, preferred_element_type=F32).astype(BF16)


def _expert_kernel(blk_ref, nused_ref, nvalid_ref, xs_ref, w1f_ref, b1_ref, w2f_ref, b2_ref, ys_ref,
                   xbuf, ybuf, w1_ref, w2_ref, in_sem, out_sem):
    i = pl.program_id(0)
    n = pl.num_programs(0)
    bm = xbuf.shape[1]
    slot = i % 2

    @pl.when((i == 0) | (blk_ref[i] != blk_ref[jnp.maximum(i - 1, 0)]))
    def _():
        _deinterleave_bf16(w1f_ref.at[0], w1_ref)
        w2_ref[...] = w2f_ref[0].astype(BF16)

    def rows_of(ref, blk):
        return ref.at[pl.ds(pl.multiple_of(blk * bm, bm), bm)]

    def loads(blk, s):
        return _relayout_copies(xbuf.at[s], rows_of(xs_ref, blk), in_sem.at[s], to_tiled=False)

    def stores(blk, s):
        return _relayout_copies(ybuf.at[s], rows_of(ys_ref, blk), out_sem.at[s], to_tiled=True)

    @pl.when(i == 0)
    def _():
        for cp in loads(0, 0):
            cp.start()

    @pl.when(i + 1 < n)
    def _():
        for cp in loads(i + 1, 1 - slot):
            cp.start()

    for cp in loads(i, slot):
        cp.wait()

    @pl.when(i >= 2)
    def _():
        for cp in stores(i - 2, slot):
            cp.wait()

    @pl.when(i < nused_ref[0])
    def _():
        live = lax.broadcasted_iota(I32, (bm, 1), 0) < nvalid_ref[i]
        x = jnp.where(live, xbuf[slot], 0.0).astype(BF16)
        step = 2 * MXU_TILE

        def hidden(c):
            cols = slice(c * step, (c + 1) * step)
            return jnp.dot(x, w1_ref[:, cols], preferred_element_type=F32) + b1_ref[0, :, cols]

        def down(c, hid):
            acts = []
            for o in range(0, step, 2 * LANES):
                gate = jnp.minimum(hid[:, o:o + LANES], SWIGLU_LIMIT)
                up = jnp.clip(hid[:, o + LANES:o + 2 * LANES], -SWIGLU_LIMIT, SWIGLU_LIMIT)
                acts.append(((up + 1.0) * (gate * jax.nn.sigmoid(SWIGLU_ALPHA * gate))).astype(BF16))
            rows = slice(c * MXU_TILE, (c + 1) * MXU_TILE)
            return jnp.dot(jnp.concatenate(acts, axis=1), w2_ref[rows, :], preferred_element_type=F32)

        n_piece = w1_ref.shape[1] // step
        y = b2_ref[0]
        hid = hidden(0)
        for c in range(n_piece):
            nxt = hidden(c + 1) if c + 1 < n_piece else None
            y = y + down(c, hid)
            hid = nxt
        ybuf[slot] = y

    @pl.when(i >= nused_ref[0])
    def _():
        ybuf[slot] = jnp.zeros(ybuf.shape[1:], F32)

    for cp in stores(i, slot):
        cp.start()

    @pl.when(i == n - 1)
    def _():
        for cp in stores(i, slot):
            cp.wait()

    @pl.when((i == n - 1) & (n >= 2))
    def _():
        for cp in stores(i - 1, 1 - slot):
            cp.wait()


def _experts(xs, blk_e, nused, nvalid, w1, b1, w2, b2):
    P = xs.shape[0]
    D = w1.shape[1]
    bm = EXPERT_ROWS
    nb = P // bm
    per_expert = lambda a: pl.BlockSpec((1,) + a.shape[1:], lambda i, blk, nu, nv: (blk[i], 0, 0))
    return pl.pallas_call(
        _expert_kernel,
        grid_spec=pltpu.PrefetchScalarGridSpec(
            num_scalar_prefetch=3,
            grid=(nb,),
            in_specs=[pl.BlockSpec(memory_space=pl.ANY), per_expert(w1), per_expert(b1), per_expert(w2),
                      per_expert(b2)],
            out_specs=pl.BlockSpec(memory_space=pl.ANY),
            scratch_shapes=[pltpu.VMEM((2, bm, D), F32), pltpu.VMEM((2, bm, D), F32),
                            pltpu.VMEM(w1.shape[1:], BF16), pltpu.VMEM(w2.shape[1:], BF16),
                            pltpu.SemaphoreType.DMA((2,)), pltpu.SemaphoreType.DMA((2,))],
        ),
        out_shape=jax.ShapeDtypeStruct((P,) + ROW_TILE, F32),
        compiler_params=_cparams(("arbitrary",)),
        name="moe_experts",
    )(blk_e, nused, nvalid, xs, w1, b1, w2, b2)


def _vreg_tile_shape(rows, cols):
    return (rows // SUBLANES, cols // LANES, SUBLANES, LANES)


def _load_vreg_tiles(ref):
    rows = ref.shape[0] * SUBLANES
    return jnp.concatenate([ref[:, b, :, :].reshape(rows, LANES) for b in range(ref.shape[1])], axis=1)


def _sc_gather_rows(table, idx):
    n = idx.shape[0]
    cols = table.shape[1] * table.shape[2]
    info = plsc.get_sparse_core_info()
    workers = info.num_cores * info.num_subcores
    per_worker = n // workers
    assert n % workers == 0 and per_worker % SC_INDEX_WINDOW == 0
    mesh = plsc.VectorSubcoreMesh(core_axis_name="c", subcore_axis_name="s")

    @functools.partial(
        pl.kernel, mesh=mesh,
        out_type=jax.ShapeDtypeStruct(_vreg_tile_shape(n, cols), table.dtype),
        scratch_types=[pltpu.VMEM((SC_INDEX_WINDOW,), I32)]
        + [pltpu.VMEM((SC_GATHER_ROWS,) + table.shape[1:], table.dtype)] * 2 + [pltpu.SemaphoreType.DMA] * 4)
    def gather(table_hbm, idx_hbm, out_hbm, idx_v, *scratch):
        bufs, fetch_sem, store_sem = scratch[0:2], scratch[2:4], scratch[4:6]
        steps = SC_INDEX_WINDOW // SC_GATHER_ROWS
        worker = lax.axis_index("s") * info.num_cores + lax.axis_index("c")
        first = worker * (per_worker // SC_INDEX_WINDOW)

        @pl.loop(0, per_worker // SC_INDEX_WINDOW)
        def _(j):
            window = first + j
            pltpu.sync_copy(idx_hbm.at[window], idx_v)

            def fetch(c):
                src = table_hbm.at[idx_v.at[pl.ds(c * SC_GATHER_ROWS, SC_GATHER_ROWS)]]
                return pltpu.async_copy(src, bufs[c % 2], fetch_sem[c % 2])

            def store(c):
                group = (window * SC_INDEX_WINDOW + c * SC_GATHER_ROWS) // SUBLANES
                return [pltpu.async_copy(bufs[c % 2].at[r], out_hbm.at[group + r // SUBLANES, :, r % SUBLANES, :],
                                         store_sem[c % 2]) for r in range(SC_GATHER_ROWS)]

            fetches = {0: fetch(0)}
            stores = {}
            for c in range(steps):
                if c + 1 < steps:
                    for cp in stores.pop(c - 1, ()):
                        cp.wait()
                    fetches[c + 1] = fetch(c + 1)
                fetches.pop(c).wait()
                stores[c] = store(c)
            for pending in stores.values():
                for cp in pending:
                    cp.wait()

    return gather(table, idx.reshape(n // SC_INDEX_WINDOW, SC_INDEX_WINDOW))


def _finish_kernel(*refs, final):
    row_refs, (tg_ref, x1_ref, gfin_ref, y_ref) = refs[:TOP_K], refs[TOP_K:]
    gates = tg_ref[...].T
    x = x1_ref[...]
    for k in range(TOP_K):
        x = x + _load_vreg_tiles(row_refs[k]) * gates[:, k:k + 1]
    y_ref[...] = _rms(x, gfin_ref[...]) if final else x


def _finish(rows, tg, x1, gfin, final, tm=TOKEN_TILE):
    T, D = x1.shape
    tiles = T // tm
    rows_spec = lambda k: pl.BlockSpec(_vreg_tile_shape(tm, D), lambda i: (k * tiles + i, 0, 0, 0))
    return pl.pallas_call(
        functools.partial(_finish_kernel, final=final),
        grid=(tiles,),
        in_specs=[rows_spec(k) for k in range(TOP_K)]
        + [pl.BlockSpec((SUBLANES, tm), lambda i: (0, i)),
           pl.BlockSpec((tm, D), lambda i: (i, 0)),
           _full(gfin.shape)],
        out_specs=pl.BlockSpec((tm, D), lambda i: (i, 0)),
        out_shape=jax.ShapeDtypeStruct((T, D), F32),
        compiler_params=_cparams(("parallel",)),
        name="moe_finish",
    )(*([rows] * TOP_K), tg, x1, gfin)


def _moe(h2, ti, rk, tg, cnt, x1, moe_w, gfin, final):
    T = x1.shape[0]
    bm = EXPERT_ROWS
    n_assign = T * TOP_K
    nb = -(-(n_assign + N_EXPERTS * (bm - 1)) // bm)
    counts = cnt[:, 0].astype(I32)
    padded = (counts + bm - 1) // bm * bm
    pad_end = jnp.cumsum(padded).astype(I32)
    pad_start = pad_end - padded
    experts = jnp.arange(N_EXPERTS, dtype=I32)[:, None, None]
    dest = rk + jnp.sum(jnp.where(ti[None] == experts, pad_start[:, None, None], 0), axis=0)
    nused = (pad_end[-1:] // bm).astype(I32)
    blk_start = jnp.minimum(jnp.arange(nb, dtype=I32), nused[0] - 1) * bm
    blk_e = jnp.minimum(jnp.sum((pad_end[None, :] <= blk_start[:, None]).astype(I32), axis=1), N_EXPERTS - 1)
    live_end = pad_start + counts
    nvalid = jnp.clip(live_end[blk_e] - jnp.arange(nb, dtype=I32) * bm, 0, bm).astype(I32)
    xs = _sc_scatter_rows(h2, dest, nb * bm)
    ys = _experts(xs, blk_e, nused, nvalid, *moe_w)
    rows = _sc_gather_rows(ys, dest.reshape(-1))
    return _finish(rows, tg, x1, gfin, final)


def _split_in_proj(w):
    sizes = (DA_QK_WIDTH, DA_QK_WIDTH, DA_WIDTH, GLA_KEY_WIDTH, GLA_KEY_WIDTH, GLA_WIDTH, GLA_WIDTH,
             GLA_GATE_RANK, GLA_GATE_RANK, D_MODEL, D_MODEL)
    assert w.shape[-1] == sum(sizes)
    out, o = [], 0
    for s in sizes:
        out.append(w[:, o:o + s])
        o += s
    return out


def _prep_layer(l, norm_mix_g, w_in, lambda_q1, lambda_k1, lambda_q2, lambda_k2, diff_subln_g,
                gla_gate_up_f, gla_gate_bias_f, gla_gate_up_b, gla_gate_bias_b, gla_norm_g,
                w_proj_a, w_proj_b, w_out, norm_ffn_g, router_w, router_b, moe_w1, moe_b1, moe_w2, moe_b2):
    dq, dk, dv, lq, lk, lv, lr, laf, lab, ga, gb = _split_in_proj(w_in[l])
    bf = lambda a: a.astype(BF16)
    r = GLA_GATE_RANK
    gup = jnp.zeros((LANES, 2 * GLA_KEY_WIDTH), F32)
    gup = gup.at[0:r, 0:GLA_KEY_WIDTH].set(gla_gate_up_f[l]).at[r:2 * r, GLA_KEY_WIDTH:].set(gla_gate_up_b[l])
    lamv = jnp.zeros((SUBLANES, LANES), F32)
    for i, v in enumerate((lambda_q1, lambda_k1, lambda_q2, lambda_k2)):
        lamv = lamv.at[i, 0:DA_HEAD_DIM].set(v[l].astype(F32))
    inproj = dict(
        norm_g=norm_mix_g[l][None, :],
        wqk=bf(jnp.concatenate([dq * (DA_HEAD_DIM ** -0.5), dk], 1)),
        wvt=bf(dv.T),
        wlqk=bf(jnp.concatenate([lq, lk], 1)),
        wlvr=bf(jnp.concatenate([lv, lr], 1)),
        wg=bf(jnp.concatenate([ga, gb], 1)),
        wlow=bf(jnp.concatenate([laf, lab, jnp.zeros((D_MODEL, LANES - 2 * r), F32)], 1)),
        gup=bf(gup),
        gbias=jnp.concatenate([gla_gate_bias_f[l], gla_gate_bias_b[l]])[None, :],
    )
    merge = dict(
        subg=diff_subln_g[l][None, :], glag=gla_norm_g[l][None, :],
        wpa=bf(w_proj_a[l]), wpb=bf(w_proj_b[l]), wout=bf(w_out[l]),
        nffn=norm_ffn_g[l][None, :], rwt=bf(router_w[l].T), rb=router_b[l][:, None],
    )
    b1 = moe_b1[l].reshape(N_EXPERTS, -1, LANES, 2).transpose(0, 1, 3, 2).reshape(N_EXPERTS, 1, -1)
    moe_w = (moe_w1[l], b1, moe_w2[l], moe_b2[l][:, None, :])
    return inproj, lamv, merge, moe_w


def _trunk(x, layers, norm_final_g):
    B, L, D = x.shape
    T = B * L
    cos_t, sin_t = _rope_tables(L)
    x2 = x.reshape(T, D)
    gfin = norm_final_g[None, :]
    for l, (inproj, lamv, merge, moe_w) in enumerate(layers):
        lam_init = 0.8 - 0.6 * math.exp(-0.3 * l)
        qk, vt, lqk, lvr, gfb, sg = _inproj(x2, L, cos_t=cos_t, sin_t=sin_t, **inproj)
        oa = _diff_attention(qk.reshape(B, L, -1), vt, lamv, lam_init)
        of, ob = _gla(lqk.reshape(B, L, -1), lvr.reshape(B, L, -1), gfb.reshape(B, L, -1))
        x1, h2, ti, rk, tg, cnt = _merge(x2, oa.reshape(T, -1), of.reshape(T, -1), ob.reshape(T, -1),
                                         lvr, sg, lam_init=lam_init, **merge)
        x2 = _moe(h2, ti, rk, tg, cnt, x1, moe_w, gfin, final=(l == len(layers) - 1))
    return x2.reshape(B, L, D)


def kernel(x_prompt, x_sample, norm_mix_g, w_in, lambda_q1, lambda_k1, lambda_q2, lambda_k2, diff_subln_g,
           gla_gate_up_f, gla_gate_bias_f, gla_gate_up_b, gla_gate_bias_b, gla_norm_g, w_proj_a, w_proj_b,
           w_out, norm_ffn_g, router_w, router_b, moe_w1, moe_b1, moe_w2, moe_b2, norm_final_g):
    depth = w_in.shape[0]
    layers = [_prep_layer(l, norm_mix_g, w_in, lambda_q1, lambda_k1, lambda_q2, lambda_k2, diff_subln_g,
                          gla_gate_up_f, gla_gate_bias_f, gla_gate_up_b, gla_gate_bias_b, gla_norm_g,
                          w_proj_a, w_proj_b, w_out, norm_ffn_g, router_w, router_b,
                          moe_w1, moe_b1, moe_w2, moe_b2) for l in range(depth)]
    return _trunk(x_prompt, layers, norm_final_g), _trunk(x_sample, layers, norm_final_g)
```

```python
import functools
import math

import jax
import jax.numpy as jnp
from jax import lax
from jax.experimental import pallas as pl
from jax.experimental.pallas import tpu as pltpu
from jax.experimental.pallas import tpu_sc as plsc

F32 = jnp.float32
BF16 = jnp.bfloat16
I32 = jnp.int32

D_MODEL = 1024
DA_HEADS = 4
DA_HEAD_DIM = 64
DA_VDIM = 128
DA_QK_WIDTH = DA_HEADS * 2 * DA_HEAD_DIM
DA_WIDTH = DA_HEADS * DA_VDIM
ROPE_THETA = 10000.0
GLA_HEADS = 4
GLA_KDIM = 64
GLA_VDIM = 128
GLA_KEY_WIDTH = GLA_HEADS * GLA_KDIM
GLA_WIDTH = GLA_HEADS * GLA_VDIM
GLA_GATE_RANK = 16
GLA_GATE_NORMALIZER = 16.0
GLA_CHUNK = 64
N_EXPERTS = 32
TOP_K = 4
SWIGLU_ALPHA = 1.702
SWIGLU_LIMIT = 7.0
NORM_EPS = 1e-5
LANES = 128
SUBLANES = 8

BF16_SUBLANES = 16
MXU_TILE = 256
LOG2_E = 1.4426950408889634
VT_ROWS = DA_VDIM + BF16_SUBLANES

VMEM_LIMIT = 56 * 1024 * 1024

TOKEN_TILE = 512
ATTN_KEY_CHUNK = 512
EXPERT_ROWS = 512

NT_DIMS = (((1,), (1,)), ((), ()))
TN_DIMS = (((0,), (0,)), ((), ()))


def _cparams(sem):
    return pltpu.CompilerParams(dimension_semantics=sem, vmem_limit_bytes=VMEM_LIMIT)


def _full(shape):
    nd = len(shape)
    return pl.BlockSpec(shape, lambda *_: (0,) * nd)


def _rms(x, g):
    return x * lax.rsqrt(jnp.mean(x * x, axis=-1, keepdims=True) + NORM_EPS) * g


ROW_TILE = (SUBLANES, LANES)


def _relayout_copies(flat_ref, tiled_ref, sem, to_tiled):
    cps = []
    for c in range(SUBLANES):
        flat = flat_ref.at[:, pl.ds(c * LANES, LANES)]
        tiled = tiled_ref.at[:, c, :]
        cps.append(pltpu.make_async_copy(flat, tiled, sem) if to_tiled else pltpu.make_async_copy(tiled, flat, sem))
    return cps


def _log_sigmoid(x):
    return jnp.minimum(x, 0.0) - jnp.log1p(jnp.exp(-jnp.abs(x)))


def _inproj_kernel(x_ref, g_ref, cos_ref, sin_ref, wqk_ref, wvt_ref, wlqk_ref, wlvr_ref,
                   wg_ref, wlow_ref, gup_ref, gbias_ref,
                   qk_ref, vt_ref, lqk_ref, lvr_ref, gfb_ref, sg_ref):
    h = _rms(x_ref[...], g_ref[...]).astype(BF16)

    cos = cos_ref[...]
    sin_signed = sin_ref[...]
    first_half = (lax.broadcasted_iota(I32, (1, LANES), 1) % DA_HEAD_DIM) < (DA_HEAD_DIM // 2)
    for c in range(2 * DA_QK_WIDTH // LANES):
        z = jnp.dot(h, wqk_ref[:, c * LANES:(c + 1) * LANES], preferred_element_type=F32)
        if c < DA_QK_WIDTH // LANES:
            z = z * LOG2_E
        partner = jnp.where(first_half, pltpu.roll(z, LANES - DA_HEAD_DIM // 2, 1),
                            pltpu.roll(z, DA_HEAD_DIM // 2, 1))
        qk_ref[:, c * LANES:(c + 1) * LANES] = (z * cos + partner * sin_signed).astype(BF16)

    vt = lax.dot_general(wvt_ref[...], h, NT_DIMS, preferred_element_type=F32).astype(BF16)
    for hd in range(DA_HEADS):
        vt_ref[0, hd * VT_ROWS:hd * VT_ROWS + DA_VDIM, :] = vt[hd * DA_VDIM:(hd + 1) * DA_VDIM]
        vt_ref[0, hd * VT_ROWS + DA_VDIM:(hd + 1) * VT_ROWS, :] = jnp.ones((VT_ROWS - DA_VDIM, vt.shape[1]), BF16)
    lqk_ref[...] = jnp.dot(h, wlqk_ref[...], preferred_element_type=F32)
    lvr_ref[...] = jnp.dot(h, wlvr_ref[...], preferred_element_type=F32).astype(BF16)
    sg_ref[...] = jax.nn.sigmoid(jnp.dot(h, wg_ref[...], preferred_element_type=F32)).astype(BF16)
    low = jnp.dot(h, wlow_ref[...], preferred_element_type=F32).astype(BF16)
    pre = jnp.dot(low, gup_ref[...], preferred_element_type=F32) + gbias_ref[...]
    gfb_ref[...] = _log_sigmoid(pre) * (1.0 / GLA_GATE_NORMALIZER)


def _inproj(x2, seq_len, norm_g, wqk, wvt, wlqk, wlvr, wg, wlow, gup, gbias, cos_t, sin_t):
    T, D = x2.shape
    tm = TOKEN_TILE
    assert T % tm == 0 and seq_len % tm == 0
    nl = seq_len // tm
    row = lambda n: pl.BlockSpec((tm, n), lambda i: (i, 0))
    out_shape = (
        jax.ShapeDtypeStruct((T, 2 * DA_QK_WIDTH), BF16),
        jax.ShapeDtypeStruct((T // tm, DA_HEADS * VT_ROWS, tm), BF16),
        jax.ShapeDtypeStruct((T, 2 * GLA_KEY_WIDTH), F32),
        jax.ShapeDtypeStruct((T, 2 * GLA_WIDTH), BF16),
        jax.ShapeDtypeStruct((T, 2 * GLA_KEY_WIDTH), F32),
        jax.ShapeDtypeStruct((T, 2 * D_MODEL), BF16),
    )
    return pl.pallas_call(
        _inproj_kernel,
        grid=(T // tm,),
        in_specs=[row(D), _full(norm_g.shape),
                  pl.BlockSpec((tm, LANES), lambda i: (i % nl, 0)),
                  pl.BlockSpec((tm, LANES), lambda i: (i % nl, 0)),
                  _full(wqk.shape), _full(wvt.shape), _full(wlqk.shape), _full(wlvr.shape),
                  _full(wg.shape), _full(wlow.shape), _full(gup.shape), _full(gbias.shape)],
        out_specs=(row(2 * DA_QK_WIDTH), pl.BlockSpec((1, DA_HEADS * VT_ROWS, tm), lambda i: (i, 0, 0)),
                   row(2 * GLA_KEY_WIDTH), row(2 * GLA_WIDTH), row(2 * GLA_KEY_WIDTH), row(2 * D_MODEL)),
        out_shape=out_shape,
        compiler_params=_cparams(("parallel",)),
        name="inproj",
    )(x2, norm_g, cos_t, sin_t, wqk, wvt, wlqk, wlvr, wg, wlow, gup, gbias)


def _rope_tables(seq_len):
    d = DA_HEAD_DIM
    inv = ROPE_THETA ** (-jnp.arange(0, d, 2, dtype=F32) / d)
    ang = jnp.arange(seq_len, dtype=F32)[:, None] * inv[None, :]
    cos = jnp.concatenate([jnp.cos(ang)] * (2 * LANES // d), -1)
    sin = jnp.sin(ang)
    sin_signed = jnp.concatenate([-sin, sin] * (LANES // d), -1)
    return cos, sin_signed


def _attn_kernel(lamv_ref, q_ref, k_ref, vt_ref, o_ref, acc_sc, sa_sc, sb_sc, *, lam_init):
    n_vt, _, vt_w = vt_ref.shape
    tk = ATTN_KEY_CHUNK
    nk = n_vt * vt_w // tk
    tq = q_ref.shape[1]
    q = q_ref[0]
    lane = lax.broadcasted_iota(I32, (1, LANES), 1)
    zeros = jnp.zeros_like(q)
    qms = (jnp.where(lane < DA_HEAD_DIM, q, zeros), jnp.where(lane >= DA_HEAD_DIM, q, zeros))
    acc_sc[...] = jnp.zeros(acc_sc.shape, F32)

    def scores(j, dst_ref):
        kj = k_ref[0, j * tk:(j + 1) * tk, :]
        col_max = []
        for mp in range(2):
            st = lax.dot_general(kj, qms[mp], NT_DIMS, preferred_element_type=F32)
            dst_ref[mp] = st
            col_max.append(jnp.max(st, axis=0, keepdims=True))
        return col_max

    def softmax_pv(j, src_ref, ms, col_max):
        def v_tile(r):
            key = j * tk + r
            return vt_ref[key // vt_w, :, key % vt_w:key % vt_w + MXU_TILE]

        new = []
        for mp in range(2):
            m_new = jnp.maximum(ms[mp], col_max[mp])
            alpha = jnp.exp2(ms[mp] - m_new)
            for cols in (slice(c, c + MXU_TILE) for c in range(0, tq, MXU_TILE)):
                part = None
                for r in range(0, tk, MXU_TILE):
                    p = jnp.exp2((src_ref[mp, r:r + MXU_TILE, cols] - m_new[:, cols]).astype(BF16))
                    d = jnp.dot(v_tile(r), p, preferred_element_type=F32)
                    part = d if part is None else part + d
                acc_sc[mp, :, cols] = acc_sc[mp, :, cols] * alpha[:, cols] + part
            new.append(m_new)
        return tuple(new)

    bufs = (sa_sc, sb_sc)
    col_max = scores(0, bufs[0])
    ms = tuple(jnp.full((1, tq), -jnp.inf, F32) for _ in range(2))
    for j in range(nk):
        nxt_max = scores(j + 1, bufs[(j + 1) % 2]) if j + 1 < nk else None
        ms = softmax_pv(j, bufs[j % 2], ms, col_max)
        col_max = nxt_max

    lv = lamv_ref[...]
    lam = (jnp.exp(jnp.sum(lv[0:1] * lv[1:2], axis=-1, keepdims=True))
           - jnp.exp(jnp.sum(lv[2:3] * lv[3:4], axis=-1, keepdims=True)) + lam_init)
    a0, a1 = acc_sc[0], acc_sc[1]
    o_ref[0] = (a0[:DA_VDIM] / a0[DA_VDIM:DA_VDIM + 1] - lam * (a1[:DA_VDIM] / a1[DA_VDIM:DA_VDIM + 1])).T


def _diff_attention(qk3, vt3, lamv, lam_init, tq=1024):
    B, L, _ = qk3.shape
    vt_w = vt3.shape[-1]
    tk = ATTN_KEY_CHUNK
    assert L % tk == 0 and tk % vt_w == 0
    return pl.pallas_call(
        functools.partial(_attn_kernel, lam_init=lam_init),
        grid=(B, DA_HEADS, L // tq),
        in_specs=[_full(lamv.shape),
                  pl.BlockSpec((1, tq, LANES), lambda b, h, i: (b, i, h)),
                  pl.BlockSpec((1, L, LANES), lambda b, h, i: (b, 0, DA_HEADS + h)),
                  pl.BlockSpec((L // vt_w, VT_ROWS, vt_w), lambda b, h, i: (b, h, 0))],
        out_specs=pl.BlockSpec((1, tq, DA_VDIM), lambda b, h, i: (b, i, h)),
        out_shape=jax.ShapeDtypeStruct((B, L, DA_WIDTH), F32),
        scratch_shapes=[pltpu.VMEM((2, VT_ROWS, tq), F32), pltpu.VMEM((2, tk, tq), F32),
                        pltpu.VMEM((2, tk, tq), F32)],
        compiler_params=_cparams(("parallel", "parallel", "arbitrary")),
        name="diff_attention",
    )(lamv, qk3, qk3, vt3)


def _split3_bf16(x):
    hi = x.astype(BF16)
    r1 = x - hi.astype(F32)
    mid = r1.astype(BF16)
    return hi, mid, (r1 - mid.astype(F32)).astype(BF16)


def _gla_local(qk_ref, vr_ref, g_ref, *, reverse):
    C = GLA_CHUNK
    tb = qk_ref.shape[1]
    nchunk = tb // C
    row = lax.broadcasted_iota(I32, (tb, tb), 0)
    col = lax.broadcasted_iota(I32, (tb, tb), 1)
    same_chunk = (row // C) == (col // C)
    tri = same_chunk & ((col >= row) if reverse else (col <= row))
    tri_b = jnp.where(tri, 1.0, 0.0).astype(BF16)
    lane = lax.broadcasted_iota(I32, (1, LANES), 1)
    goff = GLA_KEY_WIDTH if reverse else 0

    g = g_ref[0, :, goff:goff + GLA_KEY_WIDTH]
    q = qk_ref[0, :, 0:GLA_KEY_WIDTH] * (GLA_KDIM ** -0.5)
    k = qk_ref[0, :, GLA_KEY_WIDTH:2 * GLA_KEY_WIDTH]
    v = vr_ref[0]
    b = sum(jnp.dot(tri_b, part, preferred_element_type=F32) for part in _split3_bf16(g))
    last_row, ref_row = (0, C // 2) if reverse else (C - 1, C // 2 - 1)
    per_chunk = lambda r: jnp.concatenate(
        [jnp.broadcast_to(b[c * C + r:c * C + r + 1], (C, GLA_KEY_WIDTH)) for c in range(nchunk)], axis=0)
    b_last, b_ref = per_chunk(last_row), per_chunk(ref_row)
    q_in = q * jnp.exp(b - b_ref)
    k_in = (k * jnp.exp(b_ref - b)).astype(BF16)
    k_dec = k * jnp.exp(b_last - b)
    q_dec = q * jnp.exp(b)
    dec = [jnp.exp(b[c * C + last_row:c * C + last_row + 1]) for c in range(nchunk)]

    heads = []
    for h in range(GLA_HEADS):
        ps = slice((h // 2) * LANES, (h // 2 + 1) * LANES)
        keep = (lane < GLA_KDIM) if h % 2 == 0 else (lane >= GLA_KDIM)
        zero = jnp.zeros((tb, LANES), F32)
        heads.append(dict(
            ps=ps, vs=slice(h * GLA_VDIM, (h + 1) * GLA_VDIM),
            q_in=jnp.where(keep, q_in[:, ps], zero).astype(BF16),
            q_dec=jnp.where(keep, q_dec[:, ps], zero).astype(BF16),
            k_dec=jnp.where(keep, k_dec[:, ps], zero).astype(BF16)))
    scores = [lax.dot_general(hd["q_in"], k_in[:, hd["ps"]], NT_DIMS, preferred_element_type=F32) for hd in heads]
    kv_t = [[lax.dot_general(v[c * C:(c + 1) * C, hd["vs"]], hd["k_dec"][c * C:(c + 1) * C], TN_DIMS,
                             preferred_element_type=F32) for hd in heads] for c in range(nchunk)]
    intra = [jnp.dot(jnp.where(tri, s, 0.0).astype(BF16), v[:, hd["vs"]], preferred_element_type=F32)
             for s, hd in zip(scores, heads)]
    return heads, dec, kv_t, intra


def _gla_scan(heads, dec, kv_t, st_ref, *, reverse):
    nchunk = len(dec)
    state = [st_ref[h] for h in range(GLA_HEADS)]
    entering = [None] * nchunk
    for c in (range(nchunk - 1, -1, -1) if reverse else range(nchunk)):
        entering[c] = [s.astype(BF16) for s in state]
        state = [s * dec[c][:, hd["ps"]] + kv_t[c][h] for h, (s, hd) in enumerate(zip(state, heads))]
    for h in range(GLA_HEADS):
        st_ref[h] = state[h]
    return entering


def _gla_emit(heads, intra, entering, o_ref):
    C = GLA_CHUNK
    for c, states in enumerate(entering):
        rows = slice(c * C, (c + 1) * C)
        for hd, o_intra, s_t in zip(heads, intra, states):
            inter = lax.dot_general(hd["q_dec"][rows], s_t, NT_DIMS, preferred_element_type=F32)
            o_ref[0, rows, hd["vs"]] = o_intra[rows] + inter


def _gla_kernel(qkf_ref, vrf_ref, gf_ref, qkb_ref, vrb_ref, gb_ref, of_ref, ob_ref, sf_sc, sb_sc):
    @pl.when(pl.program_id(1) == 0)
    def _():
        sf_sc[...] = jnp.zeros(sf_sc.shape, F32)
        sb_sc[...] = jnp.zeros(sb_sc.shape, F32)

    hf, df, kf, inf = _gla_local(qkf_ref, vrf_ref, gf_ref, reverse=False)
    hb, db, kb, inb = _gla_local(qkb_ref, vrb_ref, gb_ref, reverse=True)
    ef = _gla_scan(hf, df, kf, sf_sc, reverse=False)
    eb = _gla_scan(hb, db, kb, sb_sc, reverse=True)
    _gla_emit(hf, inf, ef, of_ref)
    _gla_emit(hb, inb, eb, ob_ref)


def _gla(lqk3, lvr3, gfb3, tb=256):
    B, L, _ = lqk3.shape
    nb = L // tb
    fwd = lambda n: pl.BlockSpec((1, tb, n), lambda b, i: (b, i, 0))
    bwd = lambda n: pl.BlockSpec((1, tb, n), lambda b, i: (b, nb - 1 - i, 0))
    out = jax.ShapeDtypeStruct((B, L, GLA_WIDTH), F32)
    return pl.pallas_call(
        _gla_kernel,
        grid=(B, nb),
        in_specs=[fwd(2 * GLA_KEY_WIDTH), fwd(GLA_WIDTH), fwd(2 * GLA_KEY_WIDTH),
                  bwd(2 * GLA_KEY_WIDTH), bwd(GLA_WIDTH), bwd(2 * GLA_KEY_WIDTH)],
        out_specs=(fwd(GLA_WIDTH), bwd(GLA_WIDTH)),
        out_shape=(out, out),
        scratch_shapes=[pltpu.VMEM((GLA_HEADS, GLA_VDIM, LANES), F32),
                        pltpu.VMEM((GLA_HEADS, GLA_VDIM, LANES), F32)],
        compiler_params=_cparams(("parallel", "arbitrary")),
        name="gla",
    )(lqk3, lvr3, gfb3, lqk3, lvr3, gfb3)


def _head_norm(z, g):
    return jnp.concatenate(
        [_rms(z[:, h * LANES:(h + 1) * LANES], g) for h in range(z.shape[1] // LANES)], axis=1)


def _merge_kernel(x_ref, oa_ref, of_ref, ob_ref, r_ref, sg_ref, subg_ref, glag_ref, wpa_ref, wpb_ref,
                  wout_ref, nffn_ref, rwt_ref, rb_ref,
                  x1_ref, h2_ref, ti_ref, rk_ref, tg_ref, cnt_ref, carry_sc, h2_sc, h2_sem, *, lam_init):
    tm = x_ref.shape[0]
    step = pl.program_id(0)

    def h2_stores(tile):
        rows = h2_ref.at[pl.ds(pl.multiple_of(tile * tm, tm), tm)]
        return _relayout_copies(h2_sc, rows, h2_sem, to_tiled=True)

    @pl.when(step == 0)
    def _():
        carry_sc[...] = jnp.zeros(carry_sc.shape, F32)

    @pl.when(step > 0)
    def _():
        for cp in h2_stores(step - 1):
            cp.wait()

    oa = (_head_norm(oa_ref[...], subg_ref[...]) * (1.0 - lam_init)).astype(BF16)
    ob = _head_norm(of_ref[...] + ob_ref[...], glag_ref[...]) * jax.nn.silu(r_ref[...].astype(F32))
    pa = jnp.dot(oa, wpa_ref[...], preferred_element_type=F32)
    pb = jnp.dot(ob.astype(BF16), wpb_ref[...], preferred_element_type=F32)
    sg = sg_ref[...].astype(F32)
    merged = sg[:, :D_MODEL] * pa + sg[:, D_MODEL:] * pb
    x1 = x_ref[...] + jnp.dot(merged.astype(BF16), wout_ref[...], preferred_element_type=F32)
    x1_ref[...] = x1
    h2 = _rms(x1, nffn_ref[...])
    h2_sc[...] = h2
    for cp in h2_stores(step):
        cp.start()

    logits = lax.dot_general(rwt_ref[...], h2.astype(BF16), NT_DIMS, preferred_element_type=F32) + rb_ref[...]
    eidx = lax.broadcasted_iota(I32, logits.shape, 0)
    vals, sels = [], []
    for k in range(TOP_K):
        mk = jnp.max(logits, axis=0, keepdims=True)
        ik = jnp.min(jnp.where(logits == mk, eidx, N_EXPERTS), axis=0, keepdims=True)
        sel = eidx == ik
        logits = jnp.where(sel, -jnp.inf, logits)
        vals.append(mk)
        sels.append(sel)
        ti_ref[k:k + 1, :] = ik
    ex = [jnp.exp(v - vals[0]) for v in vals]
    denom = ex[0] + ex[1] + ex[2] + ex[3]
    for k in range(TOP_K):
        tg_ref[k:k + 1, :] = ex[k] / denom
    tg_ref[TOP_K:, :] = jnp.zeros((tg_ref.shape[0] - TOP_K, tm), F32)

    multi = (sels[0] | sels[1] | sels[2] | sels[3])
    multi_f = jnp.where(multi, 1.0, 0.0)
    before = lax.broadcasted_iota(I32, (tm, tm), 0) < lax.broadcasted_iota(I32, (tm, tm), 1)
    cum = jnp.dot(multi_f.astype(BF16), jnp.where(before, 1.0, 0.0).astype(BF16), preferred_element_type=F32)
    tot = carry_sc[:, 0:1] + cum
    for k in range(TOP_K):
        rk_ref[k:k + 1, :] = jnp.sum(jnp.where(sels[k], tot, 0.0), axis=0, keepdims=True).astype(I32)
    carry_sc[...] = carry_sc[...] + jnp.sum(multi_f, axis=1, keepdims=True)
    cnt_ref[...] = carry_sc[...]

    @pl.when(step == pl.num_programs(0) - 1)
    def _():
        for cp in h2_stores(step):
            cp.wait()


def _merge(x2, oa, of, ob, lvr, sg, subg, glag, wpa, wpb, wout, nffn, rwt, rb, lam_init):
    T, D = x2.shape
    tm = TOKEN_TILE
    row = lambda n: pl.BlockSpec((tm, n), lambda i: (i, 0))
    col = lambda n: pl.BlockSpec((n, tm), lambda i: (0, i))
    out_shape = (
        jax.ShapeDtypeStruct((T, D), F32),
        jax.ShapeDtypeStruct((T,) + ROW_TILE, F32),
        jax.ShapeDtypeStruct((TOP_K, T), I32),
        jax.ShapeDtypeStruct((TOP_K, T), I32),
        jax.ShapeDtypeStruct((SUBLANES, T), F32),
        jax.ShapeDtypeStruct((N_EXPERTS, LANES), F32),
    )
    return pl.pallas_call(
        functools.partial(_merge_kernel, lam_init=lam_init),
        grid=(T // tm,),
        in_specs=[row(D), row(DA_WIDTH), row(GLA_WIDTH), row(GLA_WIDTH),
                  pl.BlockSpec((tm, GLA_WIDTH), lambda i: (i, 1)), row(2 * D),
                  _full(subg.shape), _full(glag.shape), _full(wpa.shape), _full(wpb.shape),
                  _full(wout.shape), _full(nffn.shape), _full(rwt.shape), _full(rb.shape)],
        out_specs=(row(D), pl.BlockSpec(memory_space=pl.ANY),
                   col(TOP_K), col(TOP_K), col(SUBLANES), _full((N_EXPERTS, LANES))),
        out_shape=out_shape,
        scratch_shapes=[pltpu.VMEM((N_EXPERTS, LANES), F32), pltpu.VMEM((tm, D), F32),
                        pltpu.SemaphoreType.DMA(())],
        compiler_params=_cparams(("arbitrary",)),
        name="merge_router",
    )(x2, oa, of, ob, lvr, sg, subg, glag, wpa, wpb, wout, nffn, rwt, rb)


SC_INDEX_WINDOW = 128
SC_GATHER_ROWS = 16


def _sc_scatter_rows(rows, dest, n_slots):
    T = rows.shape[0]
    info = plsc.get_sparse_core_info()
    workers = info.num_cores * info.num_subcores
    per_worker = T // workers
    assert T % workers == 0 and per_worker % SC_INDEX_WINDOW == 0
    windows = T // SC_INDEX_WINDOW
    steps = SC_INDEX_WINDOW // SC_GATHER_ROWS
    mesh = plsc.VectorSubcoreMesh(core_axis_name="c", subcore_axis_name="s")

    @functools.partial(
        pl.kernel, mesh=mesh,
        out_type=jax.ShapeDtypeStruct((n_slots,) + rows.shape[1:], rows.dtype),
        scratch_types=[pltpu.VMEM((steps, SC_GATHER_ROWS), I32)] * TOP_K
        + [pltpu.VMEM((SC_GATHER_ROWS,) + rows.shape[1:], rows.dtype)] * 2 + [pltpu.SemaphoreType.DMA] * 4)
    def scatter(rows_hbm, dest_hbm, out_hbm, *scratch):
        idx_v, bufs = scratch[:TOP_K], scratch[TOP_K:TOP_K + 2]
        load_sem, send_sem = scratch[TOP_K + 2:TOP_K + 4], scratch[TOP_K + 4:TOP_K + 6]
        worker = lax.axis_index("s") * info.num_cores + lax.axis_index("c")
        first = worker * (per_worker // SC_INDEX_WINDOW)

        @pl.loop(0, per_worker // SC_INDEX_WINDOW)
        def _(j):
            window = first + j
            for k in range(TOP_K):
                pltpu.sync_copy(dest_hbm.at[k * windows + window], idx_v[k])

            def load(c):
                src = rows_hbm.at[pl.ds(window * SC_INDEX_WINDOW + c * SC_GATHER_ROWS, SC_GATHER_ROWS)]
                return pltpu.async_copy(src, bufs[c % 2], load_sem[c % 2])

            loads = {0: load(0)}
            sends = {}
            for c in range(steps):
                if c + 1 < steps:
                    for cp in sends.pop(c - 1, ()):
                        cp.wait()
                    loads[c + 1] = load(c + 1)
                loads.pop(c).wait()
                sends[c] = [pltpu.async_copy(bufs[c % 2], out_hbm.at[idx_v[k].at[c]], send_sem[c % 2])
                            for k in range(TOP_K)]
            for pending in sends.values():
                for cp in pending:
                    cp.wait()

    return scatter(rows, dest.reshape(TOP_K * windows, steps, SC_GATHER_ROWS))


def _deinterleave_bf16(w_ref, o_ref):
    n = 2 * LANES
    src = lax.broadcasted_iota(I32, (n, n), 0)
    dst = lax.broadcasted_iota(I32, (n, n), 1)
    perm = jnp.where(src == jnp.where(dst < LANES, 2 * dst, 2 * (dst - LANES) + 1), 1.0, 0.0).astype(BF16)
    for c in range(w_ref.shape[1] // n):
        w = w_ref[:, c * n:(c + 1) * n].astype(BF16)
        o_ref[:, c * n:(c + 1) * n] = jnp.dot(w, perm, preferred_element_type=F32).astype(BF16)


def _expert_kernel(blk_ref, nused_ref, nvalid_ref, xs_ref, w1f_ref, b1_ref, w2f_ref, b2_ref, ys_ref,
                   xbuf, ybuf, w1_ref, w2_ref, in_sem, out_sem):
    i = pl.program_id(0)
    n = pl.num_programs(0)
    bm = xbuf.shape[1]
    slot = i % 2

    @pl.when((i == 0) | (blk_ref[i] != blk_ref[jnp.maximum(i - 1, 0)]))
    def _():
        _deinterleave_bf16(w1f_ref.at[0], w1_ref)
        w2_ref[...] = w2f_ref[0].astype(BF16)

    def rows_of(ref, blk):
        return ref.at[pl.ds(pl.multiple_of(blk * bm, bm), bm)]

    def loads(blk, s):
        return _relayout_copies(xbuf.at[s], rows_of(xs_ref, blk), in_sem.at[s], to_tiled=False)

    def stores(blk, s):
        return _relayout_copies(ybuf.at[s], rows_of(ys_ref, blk), out_sem.at[s], to_tiled=True)

    @pl.when(i == 0)
    def _():
        for cp in loads(0, 0):
            cp.start()

    @pl.when(i + 1 < n)
    def _():
        for cp in loads(i + 1, 1 - slot):
            cp.start()

    for cp in loads(i, slot):
        cp.wait()

    @pl.when(i >= 2)
    def _():
        for cp in stores(i - 2, slot):
            cp.wait()

    @pl.when(i < nused_ref[0])
    def _():
        live = lax.broadcasted_iota(I32, (bm, 1), 0) < nvalid_ref[i]
        x = jnp.where(live, xbuf[slot], 0.0).astype(BF16)
        step = 2 * MXU_TILE

        def hidden(c):
            cols = slice(c * step, (c + 1) * step)
            return jnp.dot(x, w1_ref[:, cols], preferred_element_type=F32) + b1_ref[0, :, cols]

        def down(c, hid):
            acts = []
            for o in range(0, step, 2 * LANES):
                gate = jnp.minimum(hid[:, o:o + LANES], SWIGLU_LIMIT)
                up = jnp.clip(hid[:, o + LANES:o + 2 * LANES], -SWIGLU_LIMIT, SWIGLU_LIMIT)
                acts.append(((up + 1.0) * (gate * jax.nn.sigmoid(SWIGLU_ALPHA * gate))).astype(BF16))
            rows = slice(c * MXU_TILE, (c + 1) * MXU_TILE)
            return jnp.dot(jnp.concatenate(acts, axis=1), w2_ref[rows, :], preferred_element_type=F32)

        n_piece = w1_ref.shape[1] // step
        y = b2_ref[0]
        hid = hidden(0)
        for c in range(n_piece):
            nxt = hidden(c + 1) if c + 1 < n_piece else None
            y = y + down(c, hid)
            hid = nxt
        ybuf[slot] = y

    @pl.when(i >= nused_ref[0])
    def _():
        ybuf[slot] = jnp.zeros(ybuf.shape[1:], F32)

    for cp in stores(i, slot):
        cp.start()

    @pl.when(i == n - 1)
    def _():
        for cp in stores(i, slot):
            cp.wait()

    @pl.when((i == n - 1) & (n >= 2))
    def _():
        for cp in stores(i - 1, 1 - slot):
            cp.wait()


def _experts(xs, blk_e, nused, nvalid, w1, b1, w2, b2):
    P = xs.shape[0]
    D = w1.shape[1]
    bm = EXPERT_ROWS
    nb = P // bm
    per_expert = lambda a: pl.BlockSpec((1,) + a.shape[1:], lambda i, blk, nu, nv: (blk[i], 0, 0))
    return pl.pallas_call(
        _expert_kernel,
        grid_spec=pltpu.PrefetchScalarGridSpec(
            num_scalar_prefetch=3,
            grid=(nb,),
            in_specs=[pl.BlockSpec(memory_space=pl.ANY), per_expert(w1), per_expert(b1), per_expert(w2),
                      per_expert(b2)],
            out_specs=pl.BlockSpec(memory_space=pl.ANY),
            scratch_shapes=[pltpu.VMEM((2, bm, D), F32), pltpu.VMEM((2, bm, D), F32),
                            pltpu.VMEM(w1.shape[1:], BF16), pltpu.VMEM(w2.shape[1:], BF16),
                            pltpu.SemaphoreType.DMA((2,)), pltpu.SemaphoreType.DMA((2,))],
        ),
        out_shape=jax.ShapeDtypeStruct((P,) + ROW_TILE, F32),
        compiler_params=_cparams(("arbitrary",)),
        name="moe_experts",
    )(blk_e, nused, nvalid, xs, w1, b1, w2, b2)


def _vreg_tile_shape(rows, cols):
    return (rows // SUBLANES, cols // LANES, SUBLANES, LANES)


def _load_vreg_tiles(ref):
    rows = ref.shape[0] * SUBLANES
    return jnp.concatenate([ref[:, b, :, :].reshape(rows, LANES) for b in range(ref.shape[1])], axis=1)


def _sc_gather_rows(table, idx):
    n = idx.shape[0]
    cols = table.shape[1] * table.shape[2]
    info = plsc.get_sparse_core_info()
    workers = info.num_cores * info.num_subcores
    per_worker = n // workers
    assert n % workers == 0 and per_worker % SC_INDEX_WINDOW == 0
    mesh = plsc.VectorSubcoreMesh(core_axis_name="c", subcore_axis_name="s")

    @functools.partial(
        pl.kernel, mesh=mesh,
        out_type=jax.ShapeDtypeStruct(_vreg_tile_shape(n, cols), table.dtype),
        scratch_types=[pltpu.VMEM((SC_INDEX_WINDOW,), I32)]
        + [pltpu.VMEM((SC_GATHER_ROWS,) + table.shape[1:], table.dtype)] * 2 + [pltpu.SemaphoreType.DMA] * 4)
    def gather(table_hbm, idx_hbm, out_hbm, idx_v, *scratch):
        bufs, fetch_sem, store_sem = scratch[0:2], scratch[2:4], scratch[4:6]
        steps = SC_INDEX_WINDOW // SC_GATHER_ROWS
        worker = lax.axis_index("s") * info.num_cores + lax.axis_index("c")
        first = worker * (per_worker // SC_INDEX_WINDOW)

        @pl.loop(0, per_worker // SC_INDEX_WINDOW)
        def _(j):
            window = first + j
            pltpu.sync_copy(idx_hbm.at[window], idx_v)

            def fetch(c):
                src = table_hbm.at[idx_v.at[pl.ds(c * SC_GATHER_ROWS, SC_GATHER_ROWS)]]
                return pltpu.async_copy(src, bufs[c % 2], fetch_sem[c % 2])

            def store(c):
                group = (window * SC_INDEX_WINDOW + c * SC_GATHER_ROWS) // SUBLANES
                return [pltpu.async_copy(bufs[c % 2].at[r], out_hbm.at[group + r // SUBLANES, :, r % SUBLANES, :],
                                         store_sem[c % 2]) for r in range(SC_GATHER_ROWS)]

            fetches = {0: fetch(0)}
            stores = {}
            for c in range(steps):
                if c + 1 < steps:
                    for cp in stores.pop(c - 1, ()):
                        cp.wait()
                    fetches[c + 1] = fetch(c + 1)
                fetches.pop(c).wait()
                stores[c] = store(c)
            for pending in stores.values():
                for cp in pending:
                    cp.wait()

    return gather(table, idx.reshape(n // SC_INDEX_WINDOW, SC_INDEX_WINDOW))


def _finish_kernel(*refs, final):
    row_refs, (tg_ref, x1_ref, gfin_ref, y_ref) = refs[:TOP_K], refs[TOP_K:]
    gates = tg_ref[...].T
    x = x1_ref[...]
    for k in range(TOP_K):
        x = x + _load_vreg_tiles(row_refs[k]) * gates[:, k:k + 1]
    y_ref[...] = _rms(x, gfin_ref[...]) if final else x


def _finish(rows, tg, x1, gfin, final, tm=TOKEN_TILE):
    T, D = x1.shape
    tiles = T // tm
    rows_spec = lambda k: pl.BlockSpec(_vreg_tile_shape(tm, D), lambda i: (k * tiles + i, 0, 0, 0))
    return pl.pallas_call(
        functools.partial(_finish_kernel, final=final),
        grid=(tiles,),
        in_specs=[rows_spec(k) for k in range(TOP_K)]
        + [pl.BlockSpec((SUBLANES, tm), lambda i: (0, i)),
           pl.BlockSpec((tm, D), lambda i: (i, 0)),
           _full(gfin.shape)],
        out_specs=pl.BlockSpec((tm, D), lambda i: (i, 0)),
        out_shape=jax.ShapeDtypeStruct((T, D), F32),
        compiler_params=_cparams(("parallel",)),
        name="moe_finish",
    )(*([rows] * TOP_K), tg, x1, gfin)


def _moe(h2, ti, rk, tg, cnt, x1, moe_w, gfin, final):
    T = x1.shape[0]
    bm = EXPERT_ROWS
    n_assign = T * TOP_K
    nb = -(-(n_assign + N_EXPERTS * (bm - 1)) // bm)
    counts = cnt[:, 0].astype(I32)
    padded = (counts + bm - 1) // bm * bm
    pad_end = jnp.cumsum(padded).astype(I32)
    pad_start = pad_end - padded
    experts = jnp.arange(N_EXPERTS, dtype=I32)[:, None, None]
    dest = rk + jnp.sum(jnp.where(ti[None] == experts, pad_start[:, None, None], 0), axis=0)
    nused = (pad_end[-1:] // bm).astype(I32)
    blk_start = jnp.minimum(jnp.arange(nb, dtype=I32), nused[0] - 1) * bm
    blk_e = jnp.minimum(jnp.sum((pad_end[None, :] <= blk_start[:, None]).astype(I32), axis=1), N_EXPERTS - 1)
    live_end = pad_start + counts
    nvalid = jnp.clip(live_end[blk_e] - jnp.arange(nb, dtype=I32) * bm, 0, bm).astype(I32)
    xs = _sc_scatter_rows(h2, dest, nb * bm)
    ys = _experts(xs, blk_e, nused, nvalid, *moe_w)
    rows = _sc_gather_rows(ys, dest.reshape(-1))
    return _finish(rows, tg, x1, gfin, final)


def _split_in_proj(w):
    sizes = (DA_QK_WIDTH, DA_QK_WIDTH, DA_WIDTH, GLA_KEY_WIDTH, GLA_KEY_WIDTH, GLA_WIDTH, GLA_WIDTH,
             GLA_GATE_RANK, GLA_GATE_RANK, D_MODEL, D_MODEL)
    assert w.shape[-1] == sum(sizes)
    out, o = [], 0
    for s in sizes:
        out.append(w[:, o:o + s])
        o += s
    return out


def _prep_layer(l, norm_mix_g, w_in, lambda_q1, lambda_k1, lambda_q2, lambda_k2, diff_subln_g,
                gla_gate_up_f, gla_gate_bias_f, gla_gate_up_b, gla_gate_bias_b, gla_norm_g,
                w_proj_a, w_proj_b, w_out, norm_ffn_g, router_w, router_b, moe_w1, moe_b1, moe_w2, moe_b2):
    dq, dk, dv, lq, lk, lv, lr, laf, lab, ga, gb = _split_in_proj(w_in[l])
    bf = lambda a: a.astype(BF16)
    r = GLA_GATE_RANK
    gup = jnp.zeros((LANES, 2 * GLA_KEY_WIDTH), F32)
    gup = gup.at[0:r, 0:GLA_KEY_WIDTH].set(gla_gate_up_f[l]).at[r:2 * r, GLA_KEY_WIDTH:].set(gla_gate_up_b[l])
    lamv = jnp.zeros((SUBLANES, LANES), F32)
    for i, v in enumerate((lambda_q1, lambda_k1, lambda_q2, lambda_k2)):
        lamv = lamv.at[i, 0:DA_HEAD_DIM].set(v[l].astype(F32))
    inproj = dict(
        norm_g=norm_mix_g[l][None, :],
        wqk=bf(jnp.concatenate([dq * (DA_HEAD_DIM ** -0.5), dk], 1)),
        wvt=bf(dv.T),
        wlqk=bf(jnp.concatenate([lq, lk], 1)),
        wlvr=bf(jnp.concatenate([lv, lr], 1)),
        wg=bf(jnp.concatenate([ga, gb], 1)),
        wlow=bf(jnp.concatenate([laf, lab, jnp.zeros((D_MODEL, LANES - 2 * r), F32)], 1)),
        gup=bf(gup),
        gbias=jnp.concatenate([gla_gate_bias_f[l], gla_gate_bias_b[l]])[None, :],
    )
    merge = dict(
        subg=diff_subln_g[l][None, :], glag=gla_norm_g[l][None, :],
        wpa=bf(w_proj_a[l]), wpb=bf(w_proj_b[l]), wout=bf(w_out[l]),
        nffn=norm_ffn_g[l][None, :], rwt=bf(router_w[l].T), rb=router_b[l][:, None],
    )
    b1 = moe_b1[l].reshape(N_EXPERTS, -1, LANES, 2).transpose(0, 1, 3, 2).reshape(N_EXPERTS, 1, -1)
    moe_w = (moe_w1[l], b1, moe_w2[l], moe_b2[l][:, None, :])
    return inproj, lamv, merge, moe_w


def _trunk(x, layers, norm_final_g):
    B, L, D = x.shape
    T = B * L
    cos_t, sin_t = _rope_tables(L)
    x2 = x.reshape(T, D)
    gfin = norm_final_g[None, :]
    for l, (inproj, lamv, merge, moe_w) in enumerate(layers):
        lam_init = 0.8 - 0.6 * math.exp(-0.3 * l)
        qk, vt, lqk, lvr, gfb, sg = _inproj(x2, L, cos_t=cos_t, sin_t=sin_t, **inproj)
        oa = _diff_attention(qk.reshape(B, L, -1), vt, lamv, lam_init)
        of, ob = _gla(lqk.reshape(B, L, -1), lvr.reshape(B, L, -1), gfb.reshape(B, L, -1))
        x1, h2, ti, rk, tg, cnt = _merge(x2, oa.reshape(T, -1), of.reshape(T, -1), ob.reshape(T, -1),
                                         lvr, sg, lam_init=lam_init, **merge)
        x2 = _moe(h2, ti, rk, tg, cnt, x1, moe_w, gfin, final=(l == len(layers) - 1))
    return x2.reshape(B, L, D)


def kernel(x_prompt, x_sample, norm_mix_g, w_in, lambda_q1, lambda_k1, lambda_q2, lambda_k2, diff_subln_g,
           gla_gate_up_f, gla_gate_bias_f, gla_gate_up_b, gla_gate_bias_b, gla_norm_g, w_proj_a, w_proj_b,
           w_out, norm_ffn_g, router_w, router_b, moe_w1, moe_b1, moe_w2, moe_b2, norm_final_g):
    depth = w_in.shape[0]
    layers = [_prep_layer(l, norm_mix_g, w_in, lambda_q1, lambda_k1, lambda_q2, lambda_k2, diff_subln_g,
                          gla_gate_up_f, gla_gate_bias_f, gla_gate_up_b, gla_gate_bias_b, gla_norm_g,
                          w_proj_a, w_proj_b, w_out, norm_ffn_g, router_w, router_b,
                          moe_w1, moe_b1, moe_w2, moe_b2) for l in range(depth)]
    return _trunk(x_prompt, layers, norm_final_g), _trunk(x_sample, layers, norm_final_g)
```

```python
import functools
import math

import jax
import jax.numpy as jnp
from jax import lax
from jax.experimental import pallas as pl
from jax.experimental.pallas import tpu as pltpu
from jax.experimental.pallas import tpu_sc as plsc

F32 = jnp.float32
BF16 = jnp.bfloat16
I32 = jnp.int32

D_MODEL = 1024
DA_HEADS = 4
DA_HEAD_DIM = 64
DA_VDIM = 128
DA_QK_WIDTH = DA_HEADS * 2 * DA_HEAD_DIM
DA_WIDTH = DA_HEADS * DA_VDIM
ROPE_THETA = 10000.0
GLA_HEADS = 4
GLA_KDIM = 64
GLA_VDIM = 128
GLA_KEY_WIDTH = GLA_HEADS * GLA_KDIM
GLA_WIDTH = GLA_HEADS * GLA_VDIM
GLA_GATE_RANK = 16
GLA_GATE_NORMALIZER = 16.0
GLA_CHUNK = 64
N_EXPERTS = 32
TOP_K = 4
SWIGLU_ALPHA = 1.702
SWIGLU_LIMIT = 7.0
NORM_EPS = 1e-5
LANES = 128
SUBLANES = 8

BF16_SUBLANES = 16
MXU_TILE = 256
LOG2_E = 1.4426950408889634
VT_ROWS = DA_VDIM + BF16_SUBLANES

VMEM_LIMIT = 56 * 1024 * 1024

TOKEN_TILE = 512
ATTN_KEY_CHUNK = 512
EXPERT_ROWS = 512

NT_DIMS = (((1,), (1,)), ((), ()))
TN_DIMS = (((0,), (0,)), ((), ()))


def _cparams(sem):
    return pltpu.CompilerParams(dimension_semantics=sem, vmem_limit_bytes=VMEM_LIMIT)


def _full(shape):
    nd = len(shape)
    return pl.BlockSpec(shape, lambda *_: (0,) * nd)


def _rms(x, g):
    return x * lax.rsqrt(jnp.mean(x * x, axis=-1, keepdims=True) + NORM_EPS) * g


ROW_TILE = (SUBLANES, LANES)


def _relayout_copies(flat_ref, tiled_ref, sem, to_tiled):
    cps = []
    for c in range(SUBLANES):
        flat = flat_ref.at[:, pl.ds(c * LANES, LANES)]
        tiled = tiled_ref.at[:, c, :]
        cps.append(pltpu.make_async_copy(flat, tiled, sem) if to_tiled else pltpu.make_async_copy(tiled, flat, sem))
    return cps


def _log_sigmoid(x):
    return jnp.minimum(x, 0.0) - jnp.log1p(jnp.exp(-jnp.abs(x)))


def _inproj_kernel(x_ref, g_ref, cos_ref, sin_ref, wqk_ref, wvt_ref, wlqk_ref, wlvr_ref,
                   wg_ref, wlow_ref, gup_ref, gbias_ref,
                   qk_ref, vt_ref, lqk_ref, lvr_ref, gfb_ref, sg_ref):
    h = _rms(x_ref[...], g_ref[...]).astype(BF16)

    cos = cos_ref[...]
    sin_signed = sin_ref[...]
    first_half = (lax.broadcasted_iota(I32, (1, LANES), 1) % DA_HEAD_DIM) < (DA_HEAD_DIM // 2)
    for c in range(2 * DA_QK_WIDTH // LANES):
        z = jnp.dot(h, wqk_ref[:, c * LANES:(c + 1) * LANES], preferred_element_type=F32)
        if c < DA_QK_WIDTH // LANES:
            z = z * LOG2_E
        partner = jnp.where(first_half, pltpu.roll(z, LANES - DA_HEAD_DIM // 2, 1),
                            pltpu.roll(z, DA_HEAD_DIM // 2, 1))
        qk_ref[:, c * LANES:(c + 1) * LANES] = (z * cos + partner * sin_signed).astype(BF16)

    vt = lax.dot_general(wvt_ref[...], h, NT_DIMS, preferred_element_type=F32).astype(BF16)
    for hd in range(DA_HEADS):
        vt_ref[0, hd * VT_ROWS:hd * VT_ROWS + DA_VDIM, :] = vt[hd * DA_VDIM:(hd + 1) * DA_VDIM]
        vt_ref[0, hd * VT_ROWS + DA_VDIM:(hd + 1) * VT_ROWS, :] = jnp.ones((VT_ROWS - DA_VDIM, vt.shape[1]), BF16)
    lqk_ref[...] = jnp.dot(h, wlqk_ref[...], preferred_element_type=F32)
    lvr_ref[...] = jnp.dot(h, wlvr_ref[...], preferred_element_type=F32).astype(BF16)
    sg_ref[...] = jax.nn.sigmoid(jnp.dot(h, wg_ref[...], preferred_element_type=F32)).astype(BF16)
    low = jnp.dot(h, wlow_ref[...], preferred_element_type=F32).astype(BF16)
    pre = jnp.dot(low, gup_ref[...], preferred_element_type=F32) + gbias_ref[...]
    gfb_ref[...] = _log_sigmoid(pre) * (1.0 / GLA_GATE_NORMALIZER)


def _inproj(x2, seq_len, norm_g, wqk, wvt, wlqk, wlvr, wg, wlow, gup, gbias, cos_t, sin_t):
    T, D = x2.shape
    tm = TOKEN_TILE
    assert T % tm == 0 and seq_len % tm == 0
    nl = seq_len // tm
    row = lambda n: pl.BlockSpec((tm, n), lambda i: (i, 0))
    out_shape = (
        jax.ShapeDtypeStruct((T, 2 * DA_QK_WIDTH), BF16),
        jax.ShapeDtypeStruct((T // tm, DA_HEADS * VT_ROWS, tm), BF16),
        jax.ShapeDtypeStruct((T, 2 * GLA_KEY_WIDTH), F32),
        jax.ShapeDtypeStruct((T, 2 * GLA_WIDTH), BF16),
        jax.ShapeDtypeStruct((T, 2 * GLA_KEY_WIDTH), F32),
        jax.ShapeDtypeStruct((T, 2 * D_MODEL), BF16),
    )
    return pl.pallas_call(
        _inproj_kernel,
        grid=(T // tm,),
        in_specs=[row(D), _full(norm_g.shape),
                  pl.BlockSpec((tm, LANES), lambda i: (i % nl, 0)),
                  pl.BlockSpec((tm, LANES), lambda i: (i % nl, 0)),
                  _full(wqk.shape), _full(wvt.shape), _full(wlqk.shape), _full(wlvr.shape),
                  _full(wg.shape), _full(wlow.shape), _full(gup.shape), _full(gbias.shape)],
        out_specs=(row(2 * DA_QK_WIDTH), pl.BlockSpec((1, DA_HEADS * VT_ROWS, tm), lambda i: (i, 0, 0)),
                   row(2 * GLA_KEY_WIDTH), row(2 * GLA_WIDTH), row(2 * GLA_KEY_WIDTH), row(2 * D_MODEL)),
        out_shape=out_shape,
        compiler_params=_cparams(("parallel",)),
        name="inproj",
    )(x2, norm_g, cos_t, sin_t, wqk, wvt, wlqk, wlvr, wg, wlow, gup, gbias)


def _rope_tables(seq_len):
    d = DA_HEAD_DIM
    inv = ROPE_THETA ** (-jnp.arange(0, d, 2, dtype=F32) / d)
    ang = jnp.arange(seq_len, dtype=F32)[:, None] * inv[None, :]
    cos = jnp.concatenate([jnp.cos(ang)] * (2 * LANES // d), -1)
    sin = jnp.sin(ang)
    sin_signed = jnp.concatenate([-sin, sin] * (LANES // d), -1)
    return cos, sin_signed


def _attn_kernel(lamv_ref, q_ref, k_ref, vt_ref, o_ref, acc_sc, sa_sc, sb_sc, *, lam_init):
    n_vt, _, vt_w = vt_ref.shape
    tk = ATTN_KEY_CHUNK
    nk = n_vt * vt_w // tk
    tq = q_ref.shape[1]
    q = q_ref[0]
    lane = lax.broadcasted_iota(I32, (1, LANES), 1)
    zeros = jnp.zeros_like(q)
    qms = (jnp.where(lane < DA_HEAD_DIM, q, zeros), jnp.where(lane >= DA_HEAD_DIM, q, zeros))
    acc_sc[...] = jnp.zeros(acc_sc.shape, F32)

    def scores(j, dst_ref):
        kj = k_ref[0, j * tk:(j + 1) * tk, :]
        col_max = []
        for mp in range(2):
            st = lax.dot_general(kj, qms[mp], NT_DIMS, preferred_element_type=F32)
            dst_ref[mp] = st
            col_max.append(jnp.max(st, axis=0, keepdims=True))
        return col_max

    def softmax_pv(j, src_ref, ms, col_max):
        def v_tile(r):
            key = j * tk + r
            return vt_ref[key // vt_w, :, key % vt_w:key % vt_w + MXU_TILE]

        new = []
        for mp in range(2):
            m_new = jnp.maximum(ms[mp], col_max[mp])
            alpha = jnp.exp2(ms[mp] - m_new)
            for cols in (slice(c, c + MXU_TILE) for c in range(0, tq, MXU_TILE)):
                part = None
                for r in range(0, tk, MXU_TILE):
                    p = jnp.exp2((src_ref[mp, r:r + MXU_TILE, cols] - m_new[:, cols]).astype(BF16))
                    d = jnp.dot(v_tile(r), p, preferred_element_type=F32)
                    part = d if part is None else part + d
                acc_sc[mp, :, cols] = acc_sc[mp, :, cols] * alpha[:, cols] + part
            new.append(m_new)
        return tuple(new)

    bufs = (sa_sc, sb_sc)
    col_max = scores(0, bufs[0])
    ms = tuple(jnp.full((1, tq), -jnp.inf, F32) for _ in range(2))
    for j in range(nk):
        nxt_max = scores(j + 1, bufs[(j + 1) % 2]) if j + 1 < nk else None
        ms = softmax_pv(j, bufs[j % 2], ms, col_max)
        col_max = nxt_max

    lv = lamv_ref[...]
    lam = (jnp.exp(jnp.sum(lv[0:1] * lv[1:2], axis=-1, keepdims=True))
           - jnp.exp(jnp.sum(lv[2:3] * lv[3:4], axis=-1, keepdims=True)) + lam_init)
    a0, a1 = acc_sc[0], acc_sc[1]
    o_ref[0] = (a0[:DA_VDIM] / a0[DA_VDIM:DA_VDIM + 1] - lam * (a1[:DA_VDIM] / a1[DA_VDIM:DA_VDIM + 1])).T


def _diff_attention(qk3, vt3, lamv, lam_init, tq=1024):
    B, L, _ = qk3.shape
    vt_w = vt3.shape[-1]
    tk = ATTN_KEY_CHUNK
    assert L % tk == 0 and tk % vt_w == 0
    return pl.pallas_call(
        functools.partial(_attn_kernel, lam_init=lam_init),
        grid=(B, DA_HEADS, L // tq),
        in_specs=[_full(lamv.shape),
                  pl.BlockSpec((1, tq, LANES), lambda b, h, i: (b, i, h)),
                  pl.BlockSpec((1, L, LANES), lambda b, h, i: (b, 0, DA_HEADS + h)),
                  pl.BlockSpec((L // vt_w, VT_ROWS, vt_w), lambda b, h, i: (b, h, 0))],
        out_specs=pl.BlockSpec((1, tq, DA_VDIM), lambda b, h, i: (b, i, h)),
        out_shape=jax.ShapeDtypeStruct((B, L, DA_WIDTH), F32),
        scratch_shapes=[pltpu.VMEM((2, VT_ROWS, tq), F32), pltpu.VMEM((2, tk, tq), F32),
                        pltpu.VMEM((2, tk, tq), F32)],
        compiler_params=_cparams(("parallel", "parallel", "arbitrary")),
        name="diff_attention",
    )(lamv, qk3, qk3, vt3)


def _split3_bf16(x):
    hi = x.astype(BF16)
    r1 = x - hi.astype(F32)
    mid = r1.astype(BF16)
    return hi, mid, (r1 - mid.astype(F32)).astype(BF16)


def _gla_local(qk_ref, vr_ref, g_ref, bi, *, reverse):
    C = GLA_CHUNK
    tb = qk_ref.shape[1]
    nchunk = tb // C
    row = lax.broadcasted_iota(I32, (tb, tb), 0)
    col = lax.broadcasted_iota(I32, (tb, tb), 1)
    same_chunk = (row // C) == (col // C)
    tri = same_chunk & ((col >= row) if reverse else (col <= row))
    tri_b = jnp.where(tri, 1.0, 0.0).astype(BF16)
    lane = lax.broadcasted_iota(I32, (1, LANES), 1)
    goff = GLA_KEY_WIDTH if reverse else 0

    g = g_ref[bi, :, goff:goff + GLA_KEY_WIDTH]
    q = qk_ref[bi, :, 0:GLA_KEY_WIDTH] * (GLA_KDIM ** -0.5)
    k = qk_ref[bi, :, GLA_KEY_WIDTH:2 * GLA_KEY_WIDTH]
    v = vr_ref[bi]
    b = sum(jnp.dot(tri_b, part, preferred_element_type=F32) for part in _split3_bf16(g))
    last_row, ref_row = (0, C // 2) if reverse else (C - 1, C // 2 - 1)
    per_chunk = lambda r: jnp.concatenate(
        [jnp.broadcast_to(b[c * C + r:c * C + r + 1], (C, GLA_KEY_WIDTH)) for c in range(nchunk)], axis=0)
    b_last, b_ref = per_chunk(last_row), per_chunk(ref_row)
    q_in = q * jnp.exp(b - b_ref)
    k_in = (k * jnp.exp(b_ref - b)).astype(BF16)
    k_dec = k * jnp.exp(b_last - b)
    q_dec = q * jnp.exp(b)
    dec = [jnp.exp(b[c * C + last_row:c * C + last_row + 1]) for c in range(nchunk)]

    heads = []
    for h in range(GLA_HEADS):
        ps = slice((h // 2) * LANES, (h // 2 + 1) * LANES)
        keep = (lane < GLA_KDIM) if h % 2 == 0 else (lane >= GLA_KDIM)
        zero = jnp.zeros((tb, LANES), F32)
        heads.append(dict(
            ps=ps, vs=slice(h * GLA_VDIM, (h + 1) * GLA_VDIM),
            q_in=jnp.where(keep, q_in[:, ps], zero).astype(BF16),
            q_dec=jnp.where(keep, q_dec[:, ps], zero).astype(BF16),
            k_dec=jnp.where(keep, k_dec[:, ps], zero).astype(BF16)))
    scores = [lax.dot_general(hd["q_in"], k_in[:, hd["ps"]], NT_DIMS, preferred_element_type=F32) for hd in heads]
    kv_t = [[lax.dot_general(v[c * C:(c + 1) * C, hd["vs"]], hd["k_dec"][c * C:(c + 1) * C], TN_DIMS,
                             preferred_element_type=F32) for hd in heads] for c in range(nchunk)]
    intra = [jnp.dot(jnp.where(tri, s, 0.0).astype(BF16), v[:, hd["vs"]], preferred_element_type=F32)
             for s, hd in zip(scores, heads)]
    return heads, dec, kv_t, intra


def _gla_scan(heads, dec, kv_t, st_ref, bi, *, reverse):
    nchunk = len(dec)
    state = [st_ref[bi, h] for h in range(GLA_HEADS)]
    entering = [None] * nchunk
    for c in (range(nchunk - 1, -1, -1) if reverse else range(nchunk)):
        entering[c] = [s.astype(BF16) for s in state]
        state = [s * dec[c][:, hd["ps"]] + kv_t[c][h] for h, (s, hd) in enumerate(zip(state, heads))]
    for h in range(GLA_HEADS):
        st_ref[bi, h] = state[h]
    return entering


def _gla_emit(heads, intra, entering, o_ref, bi):
    C = GLA_CHUNK
    for c, states in enumerate(entering):
        rows = slice(c * C, (c + 1) * C)
        for hd, o_intra, s_t in zip(heads, intra, states):
            inter = lax.dot_general(hd["q_dec"][rows], s_t, NT_DIMS, preferred_element_type=F32)
            o_ref[bi, rows, hd["vs"]] = o_intra[rows] + inter


def _gla_kernel(qkf_ref, vrf_ref, gf_ref, qkb_ref, vrb_ref, gb_ref, of_ref, ob_ref, sf_sc, sb_sc):
    @pl.when(pl.program_id(1) == 0)
    def _():
        sf_sc[...] = jnp.zeros(sf_sc.shape, F32)
        sb_sc[...] = jnp.zeros(sb_sc.shape, F32)

    rows = range(qkf_ref.shape[0])
    local_f = [_gla_local(qkf_ref, vrf_ref, gf_ref, bi, reverse=False) for bi in rows]
    local_b = [_gla_local(qkb_ref, vrb_ref, gb_ref, bi, reverse=True) for bi in rows]
    enter_f = [_gla_scan(h, d, kv, sf_sc, bi, reverse=False) for bi, (h, d, kv, _) in zip(rows, local_f)]
    enter_b = [_gla_scan(h, d, kv, sb_sc, bi, reverse=True) for bi, (h, d, kv, _) in zip(rows, local_b)]
    for bi in rows:
        _gla_emit(local_f[bi][0], local_f[bi][3], enter_f[bi], of_ref, bi)
        _gla_emit(local_b[bi][0], local_b[bi][3], enter_b[bi], ob_ref, bi)


def _gla(lqk3, lvr3, gfb3, tb=256, rows=2):
    B, L, _ = lqk3.shape
    nb = L // tb
    assert B % rows == 0
    fwd = lambda n: pl.BlockSpec((rows, tb, n), lambda b, i: (b, i, 0))
    bwd = lambda n: pl.BlockSpec((rows, tb, n), lambda b, i: (b, nb - 1 - i, 0))
    out = jax.ShapeDtypeStruct((B, L, GLA_WIDTH), F32)
    return pl.pallas_call(
        _gla_kernel,
        grid=(B // rows, nb),
        in_specs=[fwd(2 * GLA_KEY_WIDTH), fwd(GLA_WIDTH), fwd(2 * GLA_KEY_WIDTH),
                  bwd(2 * GLA_KEY_WIDTH), bwd(GLA_WIDTH), bwd(2 * GLA_KEY_WIDTH)],
        out_specs=(fwd(GLA_WIDTH), bwd(GLA_WIDTH)),
        out_shape=(out, out),
        scratch_shapes=[pltpu.VMEM((rows, GLA_HEADS, GLA_VDIM, LANES), F32),
                        pltpu.VMEM((rows, GLA_HEADS, GLA_VDIM, LANES), F32)],
        compiler_params=_cparams(("parallel", "arbitrary")),
        name="gla",
    )(lqk3, lvr3, gfb3, lqk3, lvr3, gfb3)


def _head_norm(z, g):
    return jnp.concatenate(
        [_rms(z[:, h * LANES:(h + 1) * LANES], g) for h in range(z.shape[1] // LANES)], axis=1)


def _merge_kernel(x_ref, oa_ref, of_ref, ob_ref, r_ref, sg_ref, subg_ref, glag_ref, wpa_ref, wpb_ref,
                  wout_ref, nffn_ref, rwt_ref, rb_ref,
                  x1_ref, h2_ref, ti_ref, rk_ref, tg_ref, cnt_ref, carry_sc, h2_sc, h2_sem, *, lam_init):
    tm = x_ref.shape[0]
    step = pl.program_id(0)

    def h2_stores(tile):
        rows = h2_ref.at[pl.ds(pl.multiple_of(tile * tm, tm), tm)]
        return _relayout_copies(h2_sc, rows, h2_sem, to_tiled=True)

    @pl.when(step == 0)
    def _():
        carry_sc[...] = jnp.zeros(carry_sc.shape, F32)

    @pl.when(step > 0)
    def _():
        for cp in h2_stores(step - 1):
            cp.wait()

    oa = (_head_norm(oa_ref[...], subg_ref[...]) * (1.0 - lam_init)).astype(BF16)
    ob = _head_norm(of_ref[...] + ob_ref[...], glag_ref[...]) * jax.nn.silu(r_ref[...].astype(F32))
    pa = jnp.dot(oa, wpa_ref[...], preferred_element_type=F32)
    pb = jnp.dot(ob.astype(BF16), wpb_ref[...], preferred_element_type=F32)
    sg = sg_ref[...].astype(F32)
    merged = sg[:, :D_MODEL] * pa + sg[:, D_MODEL:] * pb
    x1 = x_ref[...] + jnp.dot(merged.astype(BF16), wout_ref[...], preferred_element_type=F32)
    x1_ref[...] = x1
    h2 = _rms(x1, nffn_ref[...])
    h2_sc[...] = h2
    for cp in h2_stores(step):
        cp.start()

    logits = lax.dot_general(rwt_ref[...], h2.astype(BF16), NT_DIMS, preferred_element_type=F32) + rb_ref[...]
    eidx = lax.broadcasted_iota(I32, logits.shape, 0)
    vals, sels = [], []
    for k in range(TOP_K):
        mk = jnp.max(logits, axis=0, keepdims=True)
        ik = jnp.min(jnp.where(logits == mk, eidx, N_EXPERTS), axis=0, keepdims=True)
        sel = eidx == ik
        logits = jnp.where(sel, -jnp.inf, logits)
        vals.append(mk)
        sels.append(sel)
        ti_ref[k:k + 1, :] = ik
    ex = [jnp.exp(v - vals[0]) for v in vals]
    denom = ex[0] + ex[1] + ex[2] + ex[3]
    for k in range(TOP_K):
        tg_ref[k:k + 1, :] = ex[k] / denom
    tg_ref[TOP_K:, :] = jnp.zeros((tg_ref.shape[0] - TOP_K, tm), F32)

    multi = (sels[0] | sels[1] | sels[2] | sels[3])
    multi_f = jnp.where(multi, 1.0, 0.0)
    before = lax.broadcasted_iota(I32, (tm, tm), 0) < lax.broadcasted_iota(I32, (tm, tm), 1)
    cum = jnp.dot(multi_f.astype(BF16), jnp.where(before, 1.0, 0.0).astype(BF16), preferred_element_type=F32)
    tot = carry_sc[:, 0:1] + cum
    for k in range(TOP_K):
        rk_ref[k:k + 1, :] = jnp.sum(jnp.where(sels[k], tot, 0.0), axis=0, keepdims=True).astype(I32)
    carry_sc[...] = carry_sc[...] + jnp.sum(multi_f, axis=1, keepdims=True)
    cnt_ref[...] = carry_sc[...]

    @pl.when(step == pl.num_programs(0) - 1)
    def _():
        for cp in h2_stores(step):
            cp.wait()


def _merge(x2, oa, of, ob, lvr, sg, subg, glag, wpa, wpb, wout, nffn, rwt, rb, lam_init):
    T, D = x2.shape
    tm = TOKEN_TILE
    row = lambda n: pl.BlockSpec((tm, n), lambda i: (i, 0))
    col = lambda n: pl.BlockSpec((n, tm), lambda i: (0, i))
    out_shape = (
        jax.ShapeDtypeStruct((T, D), F32),
        jax.ShapeDtypeStruct((T,) + ROW_TILE, F32),
        jax.ShapeDtypeStruct((TOP_K, T), I32),
        jax.ShapeDtypeStruct((TOP_K, T), I32),
        jax.ShapeDtypeStruct((SUBLANES, T), F32),
        jax.ShapeDtypeStruct((N_EXPERTS, LANES), F32),
    )
    return pl.pallas_call(
        functools.partial(_merge_kernel, lam_init=lam_init),
        grid=(T // tm,),
        in_specs=[row(D), row(DA_WIDTH), row(GLA_WIDTH), row(GLA_WIDTH),
                  pl.BlockSpec((tm, GLA_WIDTH), lambda i: (i, 1)), row(2 * D),
                  _full(subg.shape), _full(glag.shape), _full(wpa.shape), _full(wpb.shape),
                  _full(wout.shape), _full(nffn.shape), _full(rwt.shape), _full(rb.shape)],
        out_specs=(row(D), pl.BlockSpec(memory_space=pl.ANY),
                   col(TOP_K), col(TOP_K), col(SUBLANES), _full((N_EXPERTS, LANES))),
        out_shape=out_shape,
        scratch_shapes=[pltpu.VMEM((N_EXPERTS, LANES), F32), pltpu.VMEM((tm, D), F32),
                        pltpu.SemaphoreType.DMA(())],
        compiler_params=_cparams(("arbitrary",)),
        name="merge_router",
    )(x2, oa, of, ob, lvr, sg, subg, glag, wpa, wpb, wout, nffn, rwt, rb)


SC_INDEX_WINDOW = 128
SC_GATHER_ROWS = 16


def _sc_scatter_rows(rows, dest, n_slots):
    T = rows.shape[0]
    info = plsc.get_sparse_core_info()
    workers = info.num_cores * info.num_subcores
    per_worker = T // workers
    assert T % workers == 0 and per_worker % SC_INDEX_WINDOW == 0
    windows = T // SC_INDEX_WINDOW
    steps = SC_INDEX_WINDOW // SC_GATHER_ROWS
    mesh = plsc.VectorSubcoreMesh(core_axis_name="c", subcore_axis_name="s")

    @functools.partial(
        pl.kernel, mesh=mesh,
        out_type=jax.ShapeDtypeStruct((n_slots,) + rows.shape[1:], rows.dtype),
        scratch_types=[pltpu.VMEM((steps, SC_GATHER_ROWS), I32)] * TOP_K
        + [pltpu.VMEM((SC_GATHER_ROWS,) + rows.shape[1:], rows.dtype)] * 2 + [pltpu.SemaphoreType.DMA] * 4)
    def scatter(rows_hbm, dest_hbm, out_hbm, *scratch):
        idx_v, bufs = scratch[:TOP_K], scratch[TOP_K:TOP_K + 2]
        load_sem, send_sem = scratch[TOP_K + 2:TOP_K + 4], scratch[TOP_K + 4:TOP_K + 6]
        worker = lax.axis_index("s") * info.num_cores + lax.axis_index("c")
        first = worker * (per_worker // SC_INDEX_WINDOW)

        @pl.loop(0, per_worker // SC_INDEX_WINDOW)
        def _(j):
            window = first + j
            for k in range(TOP_K):
                pltpu.sync_copy(dest_hbm.at[k * windows + window], idx_v[k])

            def load(c):
                src = rows_hbm.at[pl.ds(window * SC_INDEX_WINDOW + c * SC_GATHER_ROWS, SC_GATHER_ROWS)]
                return pltpu.async_copy(src, bufs[c % 2], load_sem[c % 2])

            loads = {0: load(0)}
            sends = {}
            for c in range(steps):
                if c + 1 < steps:
                    for cp in sends.pop(c - 1, ()):
                        cp.wait()
                    loads[c + 1] = load(c + 1)
                loads.pop(c).wait()
                sends[c] = [pltpu.async_copy(bufs[c % 2], out_hbm.at[idx_v[k].at[c]], send_sem[c % 2])
                            for k in range(TOP_K)]
            for pending in sends.values():
                for cp in pending:
                    cp.wait()

    return scatter(rows, dest.reshape(TOP_K * windows, steps, SC_GATHER_ROWS))


def _deinterleave_bf16(w_ref, o_ref):
    n = 2 * LANES
    src = lax.broadcasted_iota(I32, (n, n), 0)
    dst = lax.broadcasted_iota(I32, (n, n), 1)
    perm = jnp.where(src == jnp.where(dst < LANES, 2 * dst, 2 * (dst - LANES) + 1), 1.0, 0.0).astype(BF16)
    for c in range(w_ref.shape[1] // n):
        w = w_ref[:, c * n:(c + 1) * n].astype(BF16)
        o_ref[:, c * n:(c + 1) * n] = jnp.dot(w, perm, preferred_element_type=F32).astype(BF16)


def _expert_kernel(blk_ref, nused_ref, nvalid_ref, xs_ref, w1f_ref, b1_ref, w2f_ref, b2_ref, ys_ref,
                   xbuf, ybuf, w1_ref, w2_ref, in_sem, out_sem):
    i = pl.program_id(0)
    n = pl.num_programs(0)
    bm = xbuf.shape[1]
    slot = i % 2

    @pl.when((i == 0) | (blk_ref[i] != blk_ref[jnp.maximum(i - 1, 0)]))
    def _():
        _deinterleave_bf16(w1f_ref.at[0], w1_ref)
        w2_ref[...] = w2f_ref[0].astype(BF16)

    def rows_of(ref, blk):
        return ref.at[pl.ds(pl.multiple_of(blk * bm, bm), bm)]

    def loads(blk, s):
        return _relayout_copies(xbuf.at[s], rows_of(xs_ref, blk), in_sem.at[s], to_tiled=False)

    def stores(blk, s):
        return _relayout_copies(ybuf.at[s], rows_of(ys_ref, blk), out_sem.at[s], to_tiled=True)

    @pl.when(i == 0)
    def _():
        for cp in loads(0, 0):
            cp.start()

    @pl.when(i + 1 < n)
    def _():
        for cp in loads(i + 1, 1 - slot):
            cp.start()

    for cp in loads(i, slot):
        cp.wait()

    @pl.when(i >= 2)
    def _():
        for cp in stores(i - 2, slot):
            cp.wait()

    @pl.when(i < nused_ref[0])
    def _():
        live = lax.broadcasted_iota(I32, (bm, 1), 0) < nvalid_ref[i]
        x = jnp.where(live, xbuf[slot], 0.0).astype(BF16)
        step = 2 * MXU_TILE

        def hidden(c):
            cols = slice(c * step, (c + 1) * step)
            return jnp.dot(x, w1_ref[:, cols], preferred_element_type=F32) + b1_ref[0, :, cols]

        def down(c, hid):
            acts = []
            for o in range(0, step, 2 * LANES):
                gate = jnp.minimum(hid[:, o:o + LANES], SWIGLU_LIMIT)
                up = jnp.clip(hid[:, o + LANES:o + 2 * LANES], -SWIGLU_LIMIT, SWIGLU_LIMIT)
                acts.append(((up + 1.0) * (gate * jax.nn.sigmoid(SWIGLU_ALPHA * gate))).astype(BF16))
            rows = slice(c * MXU_TILE, (c + 1) * MXU_TILE)
            return jnp.dot(jnp.concatenate(acts, axis=1), w2_ref[rows, :], preferred_element_type=F32)

        n_piece = w1_ref.shape[1] // step
        y = b2_ref[0]
        hid = hidden(0)
        for c in range(n_piece):
            nxt = hidden(c + 1) if c + 1 < n_piece else None
            y = y + down(c, hid)
            hid = nxt
        ybuf[slot] = y

    @pl.when(i >= nused_ref[0])
    def _():
        ybuf[slot] = jnp.zeros(ybuf.shape[1:], F32)

    for cp in stores(i, slot):
        cp.start()

    @pl.when(i == n - 1)
    def _():
        for cp in stores(i, slot):
            cp.wait()

    @pl.when((i == n - 1) & (n >= 2))
    def _():
        for cp in stores(i - 1, 1 - slot):
            cp.wait()


def _experts(xs, blk_e, nused, nvalid, w1, b1, w2, b2):
    P = xs.shape[0]
    D = w1.shape[1]
    bm = EXPERT_ROWS
    nb = P // bm
    per_expert = lambda a: pl.BlockSpec((1,) + a.shape[1:], lambda i, blk, nu, nv: (blk[i], 0, 0))
    return pl.pallas_call(
        _expert_kernel,
        grid_spec=pltpu.PrefetchScalarGridSpec(
            num_scalar_prefetch=3,
            grid=(nb,),
            in_specs=[pl.BlockSpec(memory_space=pl.ANY), per_expert(w1), per_expert(b1), per_expert(w2),
                      per_expert(b2)],
            out_specs=pl.BlockSpec(memory_space=pl.ANY),
            scratch_shapes=[pltpu.VMEM((2, bm, D), F32), pltpu.VMEM((2, bm, D), F32),
                            pltpu.VMEM(w1.shape[1:], BF16), pltpu.VMEM(w2.shape[1:], BF16),
                            pltpu.SemaphoreType.DMA((2,)), pltpu.SemaphoreType.DMA((2,))],
        ),
        out_shape=jax.ShapeDtypeStruct((P,) + ROW_TILE, F32),
        compiler_params=_cparams(("arbitrary",)),
        name="moe_experts",
    )(blk_e, nused, nvalid, xs, w1, b1, w2, b2)


def _vreg_tile_shape(rows, cols):
    return (rows // SUBLANES, cols // LANES, SUBLANES, LANES)


def _load_vreg_tiles(ref):
    rows = ref.shape[0] * SUBLANES
    return jnp.concatenate([ref[:, b, :, :].reshape(rows, LANES) for b in range(ref.shape[1])], axis=1)


def _sc_gather_rows(table, idx):
    n = idx.shape[0]
    cols = table.shape[1] * table.shape[2]
    info = plsc.get_sparse_core_info()
    workers = info.num_cores * info.num_subcores
    per_worker = n // workers
    assert n % workers == 0 and per_worker % SC_INDEX_WINDOW == 0
    mesh = plsc.VectorSubcoreMesh(core_axis_name="c", subcore_axis_name="s")

    @functools.partial(
        pl.kernel, mesh=mesh,
        out_type=jax.ShapeDtypeStruct(_vreg_tile_shape(n, cols), table.dtype),
        scratch_types=[pltpu.VMEM((SC_INDEX_WINDOW,), I32)]
        + [pltpu.VMEM((SC_GATHER_ROWS,) + table.shape[1:], table.dtype)] * 2 + [pltpu.SemaphoreType.DMA] * 4)
    def gather(table_hbm, idx_hbm, out_hbm, idx_v, *scratch):
        bufs, fetch_sem, store_sem = scratch[0:2], scratch[2:4], scratch[4:6]
        steps = SC_INDEX_WINDOW // SC_GATHER_ROWS
        worker = lax.axis_index("s") * info.num_cores + lax.axis_index("c")
        first = worker * (per_worker // SC_INDEX_WINDOW)

        @pl.loop(0, per_worker // SC_INDEX_WINDOW)
        def _(j):
            window = first + j
            pltpu.sync_copy(idx_hbm.at[window], idx_v)

            def fetch(c):
                src = table_hbm.at[idx_v.at[pl.ds(c * SC_GATHER_ROWS, SC_GATHER_ROWS)]]
                return pltpu.async_copy(src, bufs[c % 2], fetch_sem[c % 2])

            def store(c):
                group = (window * SC_INDEX_WINDOW + c * SC_GATHER_ROWS) // SUBLANES
                return [pltpu.async_copy(bufs[c % 2].at[r], out_hbm.at[group + r // SUBLANES, :, r % SUBLANES, :],
                                         store_sem[c % 2]) for r in range(SC_GATHER_ROWS)]

            fetches = {0: fetch(0)}
            stores = {}
            for c in range(steps):
                if c + 1 < steps:
                    for cp in stores.pop(c - 1, ()):
                        cp.wait()
                    fetches[c + 1] = fetch(c + 1)
                fetches.pop(c).wait()
                stores[c] = store(c)
            for pending in stores.values():
                for cp in pending:
                    cp.wait()

    return gather(table, idx.reshape(n // SC_INDEX_WINDOW, SC_INDEX_WINDOW))


def _finish_kernel(*refs, final):
    row_refs, (tg_ref, x1_ref, gfin_ref, y_ref) = refs[:TOP_K], refs[TOP_K:]
    gates = tg_ref[...].T
    x = x1_ref[...]
    for k in range(TOP_K):
        x = x + _load_vreg_tiles(row_refs[k]) * gates[:, k:k + 1]
    y_ref[...] = _rms(x, gfin_ref[...]) if final else x


def _finish(rows, tg, x1, gfin, final, tm=TOKEN_TILE):
    T, D = x1.shape
    tiles = T // tm
    rows_spec = lambda k: pl.BlockSpec(_vreg_tile_shape(tm, D), lambda i: (k * tiles + i, 0, 0, 0))
    return pl.pallas_call(
        functools.partial(_finish_kernel, final=final),
        grid=(tiles,),
        in_specs=[rows_spec(k) for k in range(TOP_K)]
        + [pl.BlockSpec((SUBLANES, tm), lambda i: (0, i)),
           pl.BlockSpec((tm, D), lambda i: (i, 0)),
           _full(gfin.shape)],
        out_specs=pl.BlockSpec((tm, D), lambda i: (i, 0)),
        out_shape=jax.ShapeDtypeStruct((T, D), F32),
        compiler_params=_cparams(("parallel",)),
        name="moe_finish",
    )(*([rows] * TOP_K), tg, x1, gfin)


def _moe(h2, ti, rk, tg, cnt, x1, moe_w, gfin, final):
    T = x1.shape[0]
    bm = EXPERT_ROWS
    n_assign = T * TOP_K
    nb = -(-(n_assign + N_EXPERTS * (bm - 1)) // bm)
    counts = cnt[:, 0].astype(I32)
    padded = (counts + bm - 1) // bm * bm
    pad_end = jnp.cumsum(padded).astype(I32)
    pad_start = pad_end - padded
    experts = jnp.arange(N_EXPERTS, dtype=I32)[:, None, None]
    dest = rk + jnp.sum(jnp.where(ti[None] == experts, pad_start[:, None, None], 0), axis=0)
    nused = (pad_end[-1:] // bm).astype(I32)
    blk_start = jnp.minimum(jnp.arange(nb, dtype=I32), nused[0] - 1) * bm
    blk_e = jnp.minimum(jnp.sum((pad_end[None, :] <= blk_start[:, None]).astype(I32), axis=1), N_EXPERTS - 1)
    live_end = pad_start + counts
    nvalid = jnp.clip(live_end[blk_e] - jnp.arange(nb, dtype=I32) * bm, 0, bm).astype(I32)
    xs = _sc_scatter_rows(h2, dest, nb * bm)
    ys = _experts(xs, blk_e, nused, nvalid, *moe_w)
    rows = _sc_gather_rows(ys, dest.reshape(-1))
    return _finish(rows, tg, x1, gfin, final)


def _split_in_proj(w):
    sizes = (DA_QK_WIDTH, DA_QK_WIDTH, DA_WIDTH, GLA_KEY_WIDTH, GLA_KEY_WIDTH, GLA_WIDTH, GLA_WIDTH,
             GLA_GATE_RANK, GLA_GATE_RANK, D_MODEL, D_MODEL)
    assert w.shape[-1] == sum(sizes)
    out, o = [], 0
    for s in sizes:
        out.append(w[:, o:o + s])
        o += s
    return out


def _prep_layer(l, norm_mix_g, w_in, lambda_q1, lambda_k1, lambda_q2, lambda_k2, diff_subln_g,
                gla_gate_up_f, gla_gate_bias_f, gla_gate_up_b, gla_gate_bias_b, gla_norm_g,
                w_proj_a, w_proj_b, w_out, norm_ffn_g, router_w, router_b, moe_w1, moe_b1, moe_w2, moe_b2):
    dq, dk, dv, lq, lk, lv, lr, laf, lab, ga, gb = _split_in_proj(w_in[l])
    bf = lambda a: a.astype(BF16)
    r = GLA_GATE_RANK
    gup = jnp.zeros((LANES, 2 * GLA_KEY_WIDTH), F32)
    gup = gup.at[0:r, 0:GLA_KEY_WIDTH].set(gla_gate_up_f[l]).at[r:2 * r, GLA_KEY_WIDTH:].set(gla_gate_up_b[l])
    lamv = jnp.zeros((SUBLANES, LANES), F32)
    for i, v in enumerate((lambda_q1, lambda_k1, lambda_q2, lambda_k2)):
        lamv = lamv.at[i, 0:DA_HEAD_DIM].set(v[l].astype(F32))
    inproj = dict(
        norm_g=norm_mix_g[l][None, :],
        wqk=bf(jnp.concatenate([dq * (DA_HEAD_DIM ** -0.5), dk], 1)),
        wvt=bf(dv.T),
        wlqk=bf(jnp.concatenate([lq, lk], 1)),
        wlvr=bf(jnp.concatenate([lv, lr], 1)),
        wg=bf(jnp.concatenate([ga, gb], 1)),
        wlow=bf(jnp.concatenate([laf, lab, jnp.zeros((D_MODEL, LANES - 2 * r), F32)], 1)),
        gup=bf(gup),
        gbias=jnp.concatenate([gla_gate_bias_f[l], gla_gate_bias_b[l]])[None, :],
    )
    merge = dict(
        subg=diff_subln_g[l][None, :], glag=gla_norm_g[l][None, :],
        wpa=bf(w_proj_a[l]), wpb=bf(w_proj_b[l]), wout=bf(w_out[l]),
        nffn=norm_ffn_g[l][None, :], rwt=bf(router_w[l].T), rb=router_b[l][:, None],
    )
    b1 = moe_b1[l].reshape(N_EXPERTS, -1, LANES, 2).transpose(0, 1, 3, 2).reshape(N_EXPERTS, 1, -1)
    moe_w = (moe_w1[l], b1, moe_w2[l], moe_b2[l][:, None, :])
    return inproj, lamv, merge, moe_w


def _trunk(x, layers, norm_final_g):
    B, L, D = x.shape
    T = B * L
    cos_t, sin_t = _rope_tables(L)
    x2 = x.reshape(T, D)
    gfin = norm_final_g[None, :]
    for l, (inproj, lamv, merge, moe_w) in enumerate(layers):
        lam_init = 0.8 - 0.6 * math.exp(-0.3 * l)
        qk, vt, lqk, lvr, gfb, sg = _inproj(x2, L, cos_t=cos_t, sin_t=sin_t, **inproj)
        oa = _diff_attention(qk.reshape(B, L, -1), vt, lamv, lam_init)
        of, ob = _gla(lqk.reshape(B, L, -1), lvr.reshape(B, L, -1), gfb.reshape(B, L, -1))
        x1, h2, ti, rk, tg, cnt = _merge(x2, oa.reshape(T, -1), of.reshape(T, -1), ob.reshape(T, -1),
                                         lvr, sg, lam_init=lam_init, **merge)
        x2 = _moe(h2, ti, rk, tg, cnt, x1, moe_w, gfin, final=(l == len(layers) - 1))
    return x2.reshape(B, L, D)


def kernel(x_prompt, x_sample, norm_mix_g, w_in, lambda_q1, lambda_k1, lambda_q2, lambda_k2, diff_subln_g,
           gla_gate_up_f, gla_gate_bias_f, gla_gate_up_b, gla_gate_bias_b, gla_norm_g, w_proj_a, w_proj_b,
           w_out, norm_ffn_g, router_w, router_b, moe_w1, moe_b1, moe_w2, moe_b2, norm_final_g):
    depth = w_in.shape[0]
    layers = [_prep_layer(l, norm_mix_g, w_in, lambda_q1, lambda_k1, lambda_q2, lambda_k2, diff_subln_g,
                          gla_gate_up_f, gla_gate_bias_f, gla_gate_up_b, gla_gate_bias_b, gla_norm_g,
                          w_proj_a, w_proj_b, w_out, norm_ffn_g, router_w, router_b,
                          moe_w1, moe_b1, moe_w2, moe_b2) for l in range(depth)]
    return _trunk(x_prompt, layers, norm_final_g), _trunk(x_sample, layers, norm_final_g)
```

```python
import functools
import math

import jax
import jax.numpy as jnp
from jax import lax
from jax.experimental import pallas as pl
from jax.experimental.pallas import tpu as pltpu
from jax.experimental.pallas import tpu_sc as plsc

F32 = jnp.float32
BF16 = jnp.bfloat16
I32 = jnp.int32

D_MODEL = 1024
DA_HEADS = 4
DA_HEAD_DIM = 64
DA_VDIM = 128
DA_QK_WIDTH = DA_HEADS * 2 * DA_HEAD_DIM
DA_WIDTH = DA_HEADS * DA_VDIM
ROPE_THETA = 10000.0
GLA_HEADS = 4
GLA_KDIM = 64
GLA_VDIM = 128
GLA_KEY_WIDTH = GLA_HEADS * GLA_KDIM
GLA_WIDTH = GLA_HEADS * GLA_VDIM
GLA_GATE_RANK = 16
GLA_GATE_NORMALIZER = 16.0
GLA_CHUNK = 64
N_EXPERTS = 32
TOP_K = 4
SWIGLU_ALPHA = 1.702
SWIGLU_LIMIT = 7.0
NORM_EPS = 1e-5
LANES = 128
SUBLANES = 8

BF16_SUBLANES = 16
MXU_TILE = 256
LOG2_E = 1.4426950408889634
VT_ROWS = DA_VDIM + BF16_SUBLANES

VMEM_LIMIT = 56 * 1024 * 1024

TOKEN_TILE = 512
ATTN_KEY_CHUNK = 512
EXPERT_ROWS = 512

NT_DIMS = (((1,), (1,)), ((), ()))
TN_DIMS = (((0,), (0,)), ((), ()))


def _cparams(sem):
    return pltpu.CompilerParams(dimension_semantics=sem, vmem_limit_bytes=VMEM_LIMIT)


def _full(shape):
    nd = len(shape)
    return pl.BlockSpec(shape, lambda *_: (0,) * nd)


def _rms(x, g):
    return x * lax.rsqrt(jnp.mean(x * x, axis=-1, keepdims=True) + NORM_EPS) * g


ROW_TILE = (SUBLANES, LANES)


def _relayout_copies(flat_ref, tiled_ref, sem, to_tiled):
    cps = []
    for c in range(SUBLANES):
        flat = flat_ref.at[:, pl.ds(c * LANES, LANES)]
        tiled = tiled_ref.at[:, c, :]
        cps.append(pltpu.make_async_copy(flat, tiled, sem) if to_tiled else pltpu.make_async_copy(tiled, flat, sem))
    return cps


def _log_sigmoid(x):
    return jnp.minimum(x, 0.0) - jnp.log1p(jnp.exp(-jnp.abs(x)))


def _inproj_kernel(x_ref, g_ref, cos_ref, sin_ref, wqk_ref, wvt_ref, wlqk_ref, wlvr_ref,
                   wg_ref, wlow_ref, gup_ref, gbias_ref,
                   qk_ref, vt_ref, lqk_ref, lvr_ref, gfb_ref, sg_ref):
    h = _rms(x_ref[...], g_ref[...]).astype(BF16)

    cos = cos_ref[...]
    sin_signed = sin_ref[...]
    first_half = (lax.broadcasted_iota(I32, (1, LANES), 1) % DA_HEAD_DIM) < (DA_HEAD_DIM // 2)
    for c in range(2 * DA_QK_WIDTH // LANES):
        z = jnp.dot(h, wqk_ref[:, c * LANES:(c + 1) * LANES], preferred_element_type=F32)
        if c < DA_QK_WIDTH // LANES:
            z = z * LOG2_E
        partner = jnp.where(first_half, pltpu.roll(z, LANES - DA_HEAD_DIM // 2, 1),
                            pltpu.roll(z, DA_HEAD_DIM // 2, 1))
        qk_ref[:, c * LANES:(c + 1) * LANES] = (z * cos + partner * sin_signed).astype(BF16)

    vt = lax.dot_general(wvt_ref[...], h, NT_DIMS, preferred_element_type=F32).astype(BF16)
    for hd in range(DA_HEADS):
        vt_ref[0, hd * VT_ROWS:hd * VT_ROWS + DA_VDIM, :] = vt[hd * DA_VDIM:(hd + 1) * DA_VDIM]
        vt_ref[0, hd * VT_ROWS + DA_VDIM:(hd + 1) * VT_ROWS, :] = jnp.ones((VT_ROWS - DA_VDIM, vt.shape[1]), BF16)
    lqk_ref[...] = jnp.dot(h, wlqk_ref[...], preferred_element_type=F32)
    lvr_ref[...] = jnp.dot(h, wlvr_ref[...], preferred_element_type=F32).astype(BF16)
    sg_ref[...] = jax.nn.sigmoid(jnp.dot(h, wg_ref[...], preferred_element_type=F32)).astype(BF16)
    low = jnp.dot(h, wlow_ref[...], preferred_element_type=F32).astype(BF16)
    pre = jnp.dot(low, gup_ref[...], preferred_element_type=F32) + gbias_ref[...]
    gfb_ref[...] = _log_sigmoid(pre) * (1.0 / GLA_GATE_NORMALIZER)


def _inproj(x2, seq_len, norm_g, wqk, wvt, wlqk, wlvr, wg, wlow, gup, gbias, cos_t, sin_t):
    T, D = x2.shape
    tm = TOKEN_TILE
    assert T % tm == 0 and seq_len % tm == 0
    nl = seq_len // tm
    row = lambda n: pl.BlockSpec((tm, n), lambda i: (i, 0))
    out_shape = (
        jax.ShapeDtypeStruct((T, 2 * DA_QK_WIDTH), BF16),
        jax.ShapeDtypeStruct((T // tm, DA_HEADS * VT_ROWS, tm), BF16),
        jax.ShapeDtypeStruct((T, 2 * GLA_KEY_WIDTH), F32),
        jax.ShapeDtypeStruct((T, 2 * GLA_WIDTH), BF16),
        jax.ShapeDtypeStruct((T, 2 * GLA_KEY_WIDTH), F32),
        jax.ShapeDtypeStruct((T, 2 * D_MODEL), BF16),
    )
    return pl.pallas_call(
        _inproj_kernel,
        grid=(T // tm,),
        in_specs=[row(D), _full(norm_g.shape),
                  pl.BlockSpec((tm, LANES), lambda i: (i % nl, 0)),
                  pl.BlockSpec((tm, LANES), lambda i: (i % nl, 0)),
                  _full(wqk.shape), _full(wvt.shape), _full(wlqk.shape), _full(wlvr.shape),
                  _full(wg.shape), _full(wlow.shape), _full(gup.shape), _full(gbias.shape)],
        out_specs=(row(2 * DA_QK_WIDTH), pl.BlockSpec((1, DA_HEADS * VT_ROWS, tm), lambda i: (i, 0, 0)),
                   row(2 * GLA_KEY_WIDTH), row(2 * GLA_WIDTH), row(2 * GLA_KEY_WIDTH), row(2 * D_MODEL)),
        out_shape=out_shape,
        compiler_params=_cparams(("parallel",)),
        name="inproj",
    )(x2, norm_g, cos_t, sin_t, wqk, wvt, wlqk, wlvr, wg, wlow, gup, gbias)


def _rope_tables(seq_len):
    d = DA_HEAD_DIM
    inv = ROPE_THETA ** (-jnp.arange(0, d, 2, dtype=F32) / d)
    ang = jnp.arange(seq_len, dtype=F32)[:, None] * inv[None, :]
    cos = jnp.concatenate([jnp.cos(ang)] * (2 * LANES // d), -1)
    sin = jnp.sin(ang)
    sin_signed = jnp.concatenate([-sin, sin] * (LANES // d), -1)
    return cos, sin_signed


def _attn_kernel(lamv_ref, q_ref, k_ref, vt_ref, o_ref, acc_sc, sa_sc, sb_sc, *, lam_init):
    n_vt, _, vt_w = vt_ref.shape
    tk = ATTN_KEY_CHUNK
    nk = n_vt * vt_w // tk
    tq = q_ref.shape[1]
    q = q_ref[0]
    lane = lax.broadcasted_iota(I32, (1, LANES), 1)
    zeros = jnp.zeros_like(q)
    qms = (jnp.where(lane < DA_HEAD_DIM, q, zeros), jnp.where(lane >= DA_HEAD_DIM, q, zeros))
    acc_sc[...] = jnp.zeros(acc_sc.shape, F32)

    def scores(j, dst_ref):
        kj = k_ref[0, j * tk:(j + 1) * tk, :]
        col_max = []
        for mp in range(2):
            st = lax.dot_general(kj, qms[mp], NT_DIMS, preferred_element_type=F32)
            dst_ref[mp] = st
            col_max.append(jnp.max(st, axis=0, keepdims=True))
        return col_max

    def softmax_pv(j, src_ref, ms, col_max):
        def v_tile(r):
            key = j * tk + r
            return vt_ref[key // vt_w, :, key % vt_w:key % vt_w + MXU_TILE]

        new = []
        for mp in range(2):
            m_new = jnp.maximum(ms[mp], col_max[mp])
            alpha = jnp.exp2(ms[mp] - m_new)
            for cols in (slice(c, c + MXU_TILE) for c in range(0, tq, MXU_TILE)):
                part = None
                for r in range(0, tk, MXU_TILE):
                    p = jnp.exp2((src_ref[mp, r:r + MXU_TILE, cols] - m_new[:, cols]).astype(BF16))
                    d = jnp.dot(v_tile(r), p, preferred_element_type=F32)
                    part = d if part is None else part + d
                acc_sc[mp, :, cols] = acc_sc[mp, :, cols] * alpha[:, cols] + part
            new.append(m_new)
        return tuple(new)

    bufs = (sa_sc, sb_sc)
    col_max = scores(0, bufs[0])
    ms = tuple(jnp.full((1, tq), -jnp.inf, F32) for _ in range(2))
    for j in range(nk):
        nxt_max = scores(j + 1, bufs[(j + 1) % 2]) if j + 1 < nk else None
        ms = softmax_pv(j, bufs[j % 2], ms, col_max)
        col_max = nxt_max

    lv = lamv_ref[...]
    lam = (jnp.exp(jnp.sum(lv[0:1] * lv[1:2], axis=-1, keepdims=True))
           - jnp.exp(jnp.sum(lv[2:3] * lv[3:4], axis=-1, keepdims=True)) + lam_init)
    a0, a1 = acc_sc[0], acc_sc[1]
    o_ref[0] = (a0[:DA_VDIM] / a0[DA_VDIM:DA_VDIM + 1] - lam * (a1[:DA_VDIM] / a1[DA_VDIM:DA_VDIM + 1])).T


def _diff_attention(qk3, vt3, lamv, lam_init, tq=1024):
    B, L, _ = qk3.shape
    vt_w = vt3.shape[-1]
    tk = ATTN_KEY_CHUNK
    assert L % tk == 0 and tk % vt_w == 0
    return pl.pallas_call(
        functools.partial(_attn_kernel, lam_init=lam_init),
        grid=(B, DA_HEADS, L // tq),
        in_specs=[_full(lamv.shape),
                  pl.BlockSpec((1, tq, LANES), lambda b, h, i: (b, i, h)),
                  pl.BlockSpec((1, L, LANES), lambda b, h, i: (b, 0, DA_HEADS + h)),
                  pl.BlockSpec((L // vt_w, VT_ROWS, vt_w), lambda b, h, i: (b, h, 0))],
        out_specs=pl.BlockSpec((1, tq, DA_VDIM), lambda b, h, i: (b, i, h)),
        out_shape=jax.ShapeDtypeStruct((B, L, DA_WIDTH), F32),
        scratch_shapes=[pltpu.VMEM((2, VT_ROWS, tq), F32), pltpu.VMEM((2, tk, tq), F32),
                        pltpu.VMEM((2, tk, tq), F32)],
        compiler_params=_cparams(("parallel", "parallel", "arbitrary")),
        name="diff_attention",
    )(lamv, qk3, qk3, vt3)


def _split3_bf16(x):
    hi = x.astype(BF16)
    r1 = x - hi.astype(F32)
    mid = r1.astype(BF16)
    return hi, mid, (r1 - mid.astype(F32)).astype(BF16)


def _gla_local(qk_ref, vr_ref, g_ref, bi, *, reverse):
    C = GLA_CHUNK
    tb = qk_ref.shape[1]
    nchunk = tb // C
    row = lax.broadcasted_iota(I32, (tb, tb), 0)
    col = lax.broadcasted_iota(I32, (tb, tb), 1)
    same_chunk = (row // C) == (col // C)
    tri = same_chunk & ((col >= row) if reverse else (col <= row))
    tri_b = jnp.where(tri, 1.0, 0.0).astype(BF16)
    lane = lax.broadcasted_iota(I32, (1, LANES), 1)
    goff = GLA_KEY_WIDTH if reverse else 0

    g = g_ref[bi, :, goff:goff + GLA_KEY_WIDTH]
    q = qk_ref[bi, :, 0:GLA_KEY_WIDTH] * (GLA_KDIM ** -0.5)
    k = qk_ref[bi, :, GLA_KEY_WIDTH:2 * GLA_KEY_WIDTH]
    v = vr_ref[bi]
    b = sum(jnp.dot(tri_b, part, preferred_element_type=F32) for part in _split3_bf16(g))
    last_row, ref_row = (0, C // 2) if reverse else (C - 1, C // 2 - 1)
    per_chunk = lambda r: jnp.concatenate(
        [jnp.broadcast_to(b[c * C + r:c * C + r + 1], (C, GLA_KEY_WIDTH)) for c in range(nchunk)], axis=0)
    b_last, b_ref = per_chunk(last_row), per_chunk(ref_row)
    q_in = q * jnp.exp(b - b_ref)
    k_in = (k * jnp.exp(b_ref - b)).astype(BF16)
    k_dec = k * jnp.exp(b_last - b)
    q_dec = q * jnp.exp(b)
    dec = [jnp.exp(b[c * C + last_row:c * C + last_row + 1]) for c in range(nchunk)]

    heads = []
    for h in range(GLA_HEADS):
        ps = slice((h // 2) * LANES, (h // 2 + 1) * LANES)
        keep = (lane < GLA_KDIM) if h % 2 == 0 else (lane >= GLA_KDIM)
        zero = jnp.zeros((tb, LANES), F32)
        heads.append(dict(
            ps=ps, vs=slice(h * GLA_VDIM, (h + 1) * GLA_VDIM),
            q_in=jnp.where(keep, q_in[:, ps], zero).astype(BF16),
            q_dec=jnp.where(keep, q_dec[:, ps], zero).astype(BF16),
            k_dec=jnp.where(keep, k_dec[:, ps], zero).astype(BF16)))
    scores = [lax.dot_general(hd["q_in"], k_in[:, hd["ps"]], NT_DIMS, preferred_element_type=F32) for hd in heads]
    kv_t = [[lax.dot_general(v[c * C:(c + 1) * C, hd["vs"]], hd["k_dec"][c * C:(c + 1) * C], TN_DIMS,
                             preferred_element_type=F32) for hd in heads] for c in range(nchunk)]
    intra = [jnp.dot(jnp.where(tri, s, 0.0).astype(BF16), v[:, hd["vs"]], preferred_element_type=F32)
             for s, hd in zip(scores, heads)]
    return heads, dec, kv_t, intra


def _gla_scan(heads, dec, kv_t, st_ref, bi, *, reverse):
    nchunk = len(dec)
    state = [st_ref[bi, h] for h in range(GLA_HEADS)]
    entering = [None] * nchunk
    for c in (range(nchunk - 1, -1, -1) if reverse else range(nchunk)):
        entering[c] = [s.astype(BF16) for s in state]
        state = [s * dec[c][:, hd["ps"]] + kv_t[c][h] for h, (s, hd) in enumerate(zip(state, heads))]
    for h in range(GLA_HEADS):
        st_ref[bi, h] = state[h]
    return entering


def _gla_emit(heads, intra, entering, o_ref, bi):
    C = GLA_CHUNK
    for c, states in enumerate(entering):
        rows = slice(c * C, (c + 1) * C)
        for hd, o_intra, s_t in zip(heads, intra, states):
            inter = lax.dot_general(hd["q_dec"][rows], s_t, NT_DIMS, preferred_element_type=F32)
            o_ref[bi, rows, hd["vs"]] = o_intra[rows] + inter


def _gla_kernel(qkf_ref, vrf_ref, gf_ref, qkb_ref, vrb_ref, gb_ref, of_ref, ob_ref, sf_sc, sb_sc):
    @pl.when(pl.program_id(1) == 0)
    def _():
        sf_sc[...] = jnp.zeros(sf_sc.shape, F32)
        sb_sc[...] = jnp.zeros(sb_sc.shape, F32)

    rows = range(qkf_ref.shape[0])
    local_f = [_gla_local(qkf_ref, vrf_ref, gf_ref, bi, reverse=False) for bi in rows]
    local_b = [_gla_local(qkb_ref, vrb_ref, gb_ref, bi, reverse=True) for bi in rows]
    enter_f = [_gla_scan(h, d, kv, sf_sc, bi, reverse=False) for bi, (h, d, kv, _) in zip(rows, local_f)]
    enter_b = [_gla_scan(h, d, kv, sb_sc, bi, reverse=True) for bi, (h, d, kv, _) in zip(rows, local_b)]
    for bi in rows:
        _gla_emit(local_f[bi][0], local_f[bi][3], enter_f[bi], of_ref, bi)
        _gla_emit(local_b[bi][0], local_b[bi][3], enter_b[bi], ob_ref, bi)


def _gla(lqk3, lvr3, gfb3, tb=256, rows=2):
    B, L, _ = lqk3.shape
    nb = L // tb
    assert B % rows == 0
    fwd = lambda n: pl.BlockSpec((rows, tb, n), lambda b, i: (b, i, 0))
    bwd = lambda n: pl.BlockSpec((rows, tb, n), lambda b, i: (b, nb - 1 - i, 0))
    out = jax.ShapeDtypeStruct((B, L, GLA_WIDTH), F32)
    return pl.pallas_call(
        _gla_kernel,
        grid=(B // rows, nb),
        in_specs=[fwd(2 * GLA_KEY_WIDTH), fwd(GLA_WIDTH), fwd(2 * GLA_KEY_WIDTH),
                  bwd(2 * GLA_KEY_WIDTH), bwd(GLA_WIDTH), bwd(2 * GLA_KEY_WIDTH)],
        out_specs=(fwd(GLA_WIDTH), bwd(GLA_WIDTH)),
        out_shape=(out, out),
        scratch_shapes=[pltpu.VMEM((rows, GLA_HEADS, GLA_VDIM, LANES), F32),
                        pltpu.VMEM((rows, GLA_HEADS, GLA_VDIM, LANES), F32)],
        compiler_params=_cparams(("parallel", "arbitrary")),
        name="gla",
    )(lqk3, lvr3, gfb3, lqk3, lvr3, gfb3)


def _head_norm(z, g):
    return jnp.concatenate(
        [_rms(z[:, h * LANES:(h + 1) * LANES], g) for h in range(z.shape[1] // LANES)], axis=1)


def _merge_kernel(x_ref, oa_ref, of_ref, ob_ref, r_ref, sg_ref, subg_ref, glag_ref, wpa_ref, wpb_ref,
                  wout_ref, nffn_ref, rwt_ref, rb_ref,
                  x1_ref, h2_ref, ti_ref, rk_ref, tg_ref, cnt_ref, carry_sc, h2_sc, h2_sem, *, lam_init):
    tm = x_ref.shape[0]
    step = pl.program_id(0)

    def h2_stores(tile):
        rows = h2_ref.at[pl.ds(pl.multiple_of(tile * tm, tm), tm)]
        return _relayout_copies(h2_sc, rows, h2_sem, to_tiled=True)

    @pl.when(step == 0)
    def _():
        carry_sc[...] = jnp.zeros(carry_sc.shape, F32)

    @pl.when(step > 0)
    def _():
        for cp in h2_stores(step - 1):
            cp.wait()

    oa = (_head_norm(oa_ref[...], subg_ref[...]) * (1.0 - lam_init)).astype(BF16)
    ob = _head_norm(of_ref[...] + ob_ref[...], glag_ref[...]) * jax.nn.silu(r_ref[...].astype(F32))
    pa = jnp.dot(oa, wpa_ref[...], preferred_element_type=F32)
    pb = jnp.dot(ob.astype(BF16), wpb_ref[...], preferred_element_type=F32)
    sg = sg_ref[...].astype(F32)
    merged = sg[:, :D_MODEL] * pa + sg[:, D_MODEL:] * pb
    x1 = x_ref[...] + jnp.dot(merged.astype(BF16), wout_ref[...], preferred_element_type=F32)
    x1_ref[...] = x1
    h2 = _rms(x1, nffn_ref[...])
    h2_sc[...] = h2
    for cp in h2_stores(step):
        cp.start()

    logits = lax.dot_general(rwt_ref[...], h2.astype(BF16), NT_DIMS, preferred_element_type=F32) + rb_ref[...]
    eidx = lax.broadcasted_iota(I32, logits.shape, 0)
    vals, sels = [], []
    for k in range(TOP_K):
        mk = jnp.max(logits, axis=0, keepdims=True)
        ik = jnp.min(jnp.where(logits == mk, eidx, N_EXPERTS), axis=0, keepdims=True)
        sel = eidx == ik
        logits = jnp.where(sel, -jnp.inf, logits)
        vals.append(mk)
        sels.append(sel)
        ti_ref[k:k + 1, :] = ik
    ex = [jnp.exp(v - vals[0]) for v in vals]
    denom = ex[0] + ex[1] + ex[2] + ex[3]
    for k in range(TOP_K):
        tg_ref[k:k + 1, :] = ex[k] / denom
    tg_ref[TOP_K:, :] = jnp.zeros((tg_ref.shape[0] - TOP_K, tm), F32)

    multi = (sels[0] | sels[1] | sels[2] | sels[3])
    multi_f = jnp.where(multi, 1.0, 0.0)
    before = lax.broadcasted_iota(I32, (tm, tm), 0) < lax.broadcasted_iota(I32, (tm, tm), 1)
    cum = jnp.dot(multi_f.astype(BF16), jnp.where(before, 1.0, 0.0).astype(BF16), preferred_element_type=F32)
    tot = carry_sc[:, 0:1] + cum
    for k in range(TOP_K):
        rk_ref[k:k + 1, :] = jnp.sum(jnp.where(sels[k], tot, 0.0), axis=0, keepdims=True).astype(I32)
    carry_sc[...] = carry_sc[...] + jnp.sum(multi_f, axis=1, keepdims=True)
    cnt_ref[...] = carry_sc[...]

    @pl.when(step == pl.num_programs(0) - 1)
    def _():
        for cp in h2_stores(step):
            cp.wait()


def _merge(x2, oa, of, ob, lvr, sg, subg, glag, wpa, wpb, wout, nffn, rwt, rb, lam_init):
    T, D = x2.shape
    tm = TOKEN_TILE
    row = lambda n: pl.BlockSpec((tm, n), lambda i: (i, 0))
    col = lambda n: pl.BlockSpec((n, tm), lambda i: (0, i))
    out_shape = (
        jax.ShapeDtypeStruct((T, D), F32),
        jax.ShapeDtypeStruct((T,) + ROW_TILE, F32),
        jax.ShapeDtypeStruct((TOP_K, T), I32),
        jax.ShapeDtypeStruct((TOP_K, T), I32),
        jax.ShapeDtypeStruct((SUBLANES, T), F32),
        jax.ShapeDtypeStruct((N_EXPERTS, LANES), F32),
    )
    return pl.pallas_call(
        functools.partial(_merge_kernel, lam_init=lam_init),
        grid=(T // tm,),
        in_specs=[row(D), row(DA_WIDTH), row(GLA_WIDTH), row(GLA_WIDTH),
                  pl.BlockSpec((tm, GLA_WIDTH), lambda i: (i, 1)), row(2 * D),
                  _full(subg.shape), _full(glag.shape), _full(wpa.shape), _full(wpb.shape),
                  _full(wout.shape), _full(nffn.shape), _full(rwt.shape), _full(rb.shape)],
        out_specs=(row(D), pl.BlockSpec(memory_space=pl.ANY),
                   col(TOP_K), col(TOP_K), col(SUBLANES), _full((N_EXPERTS, LANES))),
        out_shape=out_shape,
        scratch_shapes=[pltpu.VMEM((N_EXPERTS, LANES), F32), pltpu.VMEM((tm, D), F32),
                        pltpu.SemaphoreType.DMA(())],
        compiler_params=_cparams(("arbitrary",)),
        name="merge_router",
    )(x2, oa, of, ob, lvr, sg, subg, glag, wpa, wpb, wout, nffn, rwt, rb)


SC_INDEX_WINDOW = 128
SC_GATHER_ROWS = 16


def _sc_scatter_rows(rows, dest, n_slots):
    T = rows.shape[0]
    info = plsc.get_sparse_core_info()
    workers = info.num_cores * info.num_subcores
    per_worker = T // workers
    assert T % workers == 0 and per_worker % SC_INDEX_WINDOW == 0
    windows = T // SC_INDEX_WINDOW
    steps = SC_INDEX_WINDOW // SC_GATHER_ROWS
    mesh = plsc.VectorSubcoreMesh(core_axis_name="c", subcore_axis_name="s")

    @functools.partial(
        pl.kernel, mesh=mesh,
        out_type=jax.ShapeDtypeStruct((n_slots,) + rows.shape[1:], rows.dtype),
        scratch_types=[pltpu.VMEM((steps, SC_GATHER_ROWS), I32)] * TOP_K
        + [pltpu.VMEM((SC_GATHER_ROWS,) + rows.shape[1:], rows.dtype)] * 2 + [pltpu.SemaphoreType.DMA] * 4)
    def scatter(rows_hbm, dest_hbm, out_hbm, *scratch):
        idx_v, bufs = scratch[:TOP_K], scratch[TOP_K:TOP_K + 2]
        load_sem, send_sem = scratch[TOP_K + 2:TOP_K + 4], scratch[TOP_K + 4:TOP_K + 6]
        worker = lax.axis_index("s") * info.num_cores + lax.axis_index("c")
        first = worker * (per_worker // SC_INDEX_WINDOW)

        @pl.loop(0, per_worker // SC_INDEX_WINDOW)
        def _(j):
            window = first + j
            for k in range(TOP_K):
                pltpu.sync_copy(dest_hbm.at[k * windows + window], idx_v[k])

            def load(c):
                src = rows_hbm.at[pl.ds(window * SC_INDEX_WINDOW + c * SC_GATHER_ROWS, SC_GATHER_ROWS)]
                return pltpu.async_copy(src, bufs[c % 2], load_sem[c % 2])

            loads = {0: load(0)}
            sends = {}
            for c in range(steps):
                if c + 1 < steps:
                    for cp in sends.pop(c - 1, ()):
                        cp.wait()
                    loads[c + 1] = load(c + 1)
                loads.pop(c).wait()
                sends[c] = [pltpu.async_copy(bufs[c % 2], out_hbm.at[idx_v[k].at[c]], send_sem[c % 2])
                            for k in range(TOP_K)]
            for pending in sends.values():
                for cp in pending:
                    cp.wait()

    return scatter(rows, dest.reshape(TOP_K * windows, steps, SC_GATHER_ROWS))


def _deinterleave_bf16(w_ref, o_ref):
    n = 2 * LANES
    src = lax.broadcasted_iota(I32, (n, n), 0)
    dst = lax.broadcasted_iota(I32, (n, n), 1)
    perm = jnp.where(src == jnp.where(dst < LANES, 2 * dst, 2 * (dst - LANES) + 1), 1.0, 0.0).astype(BF16)
    for c in range(w_ref.shape[1] // n):
        w = w_ref[:, c * n:(c + 1) * n].astype(BF16)
        o_ref[:, c * n:(c + 1) * n] = jnp.dot(w, perm, preferred_element_type=F32).astype(BF16)


def _expert_kernel(blk_ref, nused_ref, nvalid_ref, xs_ref, w1f_ref, b1_ref, w2f_ref, b2_ref, ys_ref,
                   xbuf, ybuf, w1_ref, w2_ref, in_sem, out_sem):
    i = pl.program_id(0)
    n = pl.num_programs(0)
    bm = xbuf.shape[1]
    slot = i % 2

    @pl.when((i == 0) | (blk_ref[i] != blk_ref[jnp.maximum(i - 1, 0)]))
    def _():
        _deinterleave_bf16(w1f_ref.at[0], w1_ref)
        w2_ref[...] = w2f_ref[0].astype(BF16)

    def rows_of(ref, blk):
        return ref.at[pl.ds(pl.multiple_of(blk * bm, bm), bm)]

    def loads(blk, s):
        return _relayout_copies(xbuf.at[s], rows_of(xs_ref, blk), in_sem.at[s], to_tiled=False)

    def stores(blk, s):
        return _relayout_copies(ybuf.at[s], rows_of(ys_ref, blk), out_sem.at[s], to_tiled=True)

    @pl.when(i == 0)
    def _():
        for cp in loads(0, 0):
            cp.start()

    @pl.when(i + 1 < n)
    def _():
        for cp in loads(i + 1, 1 - slot):
            cp.start()

    for cp in loads(i, slot):
        cp.wait()

    @pl.when(i >= 2)
    def _():
        for cp in stores(i - 2, slot):
            cp.wait()

    @pl.when(i < nused_ref[0])
    def _():
        live = lax.broadcasted_iota(I32, (bm, 1), 0) < nvalid_ref[i]
        x = jnp.where(live, xbuf[slot], 0.0).astype(BF16)
        step = 2 * MXU_TILE

        def hidden(c):
            cols = slice(c * step, (c + 1) * step)
            return jnp.dot(x, w1_ref[:, cols], preferred_element_type=F32) + b1_ref[0, :, cols]

        def down(c, hid):
            acts = []
            for o in range(0, step, 2 * LANES):
                gate = jnp.minimum(hid[:, o:o + LANES], SWIGLU_LIMIT)
                up = jnp.clip(hid[:, o + LANES:o + 2 * LANES], -SWIGLU_LIMIT, SWIGLU_LIMIT)
                acts.append(((up + 1.0) * (gate * jax.nn.sigmoid(SWIGLU_ALPHA * gate))).astype(BF16))
            rows = slice(c * MXU_TILE, (c + 1) * MXU_TILE)
            return jnp.dot(jnp.concatenate(acts, axis=1), w2_ref[rows, :], preferred_element_type=F32)

        n_piece = w1_ref.shape[1] // step
        y = b2_ref[0]
        hid = hidden(0)
        for c in range(n_piece):
            nxt = hidden(c + 1) if c + 1 < n_piece else None
            y = y + down(c, hid)
            hid = nxt
        ybuf[slot] = y

    @pl.when(i >= nused_ref[0])
    def _():
        ybuf[slot] = jnp.zeros(ybuf.shape[1:], F32)

    for cp in stores(i, slot):
        cp.start()

    @pl.when(i == n - 1)
    def _():
        for cp in stores(i, slot):
            cp.wait()

    @pl.when((i == n - 1) & (n >= 2))
    def _():
        for cp in stores(i - 1, 1 - slot):
            cp.wait()


def _experts(xs, blk_e, nused, nvalid, w1, b1, w2, b2):
    P = xs.shape[0]
    D = w1.shape[1]
    bm = EXPERT_ROWS
    nb = P // bm
    per_expert = lambda a: pl.BlockSpec((1,) + a.shape[1:], lambda i, blk, nu, nv: (blk[i], 0, 0))
    return pl.pallas_call(
        _expert_kernel,
        grid_spec=pltpu.PrefetchScalarGridSpec(
            num_scalar_prefetch=3,
            grid=(nb,),
            in_specs=[pl.BlockSpec(memory_space=pl.ANY), per_expert(w1), per_expert(b1), per_expert(w2),
                      per_expert(b2)],
            out_specs=pl.BlockSpec(memory_space=pl.ANY),
            scratch_shapes=[pltpu.VMEM((2, bm, D), F32), pltpu.VMEM((2, bm, D), F32),
                            pltpu.VMEM(w1.shape[1:], BF16), pltpu.VMEM(w2.shape[1:], BF16),
                            pltpu.SemaphoreType.DMA((2,)), pltpu.SemaphoreType.DMA((2,))],
        ),
        out_shape=jax.ShapeDtypeStruct((P,) + ROW_TILE, F32),
        compiler_params=_cparams(("arbitrary",)),
        name="moe_experts",
    )(blk_e, nused, nvalid, xs, w1, b1, w2, b2)


def _vreg_tile_shape(rows, cols):
    return (rows // SUBLANES, cols // LANES, SUBLANES, LANES)


def _load_vreg_tiles(ref):
    rows = ref.shape[0] * SUBLANES
    return jnp.concatenate([ref[:, b, :, :].reshape(rows, LANES) for b in range(ref.shape[1])], axis=1)


def _sc_gather_rows(table, idx):
    n = idx.shape[0]
    cols = table.shape[1] * table.shape[2]
    info = plsc.get_sparse_core_info()
    workers = info.num_cores * info.num_subcores
    per_worker = n // workers
    assert n % workers == 0 and per_worker % SC_INDEX_WINDOW == 0
    mesh = plsc.VectorSubcoreMesh(core_axis_name="c", subcore_axis_name="s")

    @functools.partial(
        pl.kernel, mesh=mesh,
        out_type=jax.ShapeDtypeStruct(_vreg_tile_shape(n, cols), table.dtype),
        scratch_types=[pltpu.VMEM((SC_INDEX_WINDOW,), I32)]
        + [pltpu.VMEM((SC_GATHER_ROWS,) + table.shape[1:], table.dtype)] * 2 + [pltpu.SemaphoreType.DMA] * 4)
    def gather(table_hbm, idx_hbm, out_hbm, idx_v, *scratch):
        bufs, fetch_sem, store_sem = scratch[0:2], scratch[2:4], scratch[4:6]
        steps = SC_INDEX_WINDOW // SC_GATHER_ROWS
        worker = lax.axis_index("s") * info.num_cores + lax.axis_index("c")
        first = worker * (per_worker // SC_INDEX_WINDOW)

        @pl.loop(0, per_worker // SC_INDEX_WINDOW)
        def _(j):
            window = first + j
            pltpu.sync_copy(idx_hbm.at[window], idx_v)

            def fetch(c):
                src = table_hbm.at[idx_v.at[pl.ds(c * SC_GATHER_ROWS, SC_GATHER_ROWS)]]
                return pltpu.async_copy(src, bufs[c % 2], fetch_sem[c % 2])

            def store(c):
                group = (window * SC_INDEX_WINDOW + c * SC_GATHER_ROWS) // SUBLANES
                return [pltpu.async_copy(bufs[c % 2].at[r], out_hbm.at[group + r // SUBLANES, :, r % SUBLANES, :],
                                         store_sem[c % 2]) for r in range(SC_GATHER_ROWS)]

            fetches = {0: fetch(0)}
            stores = {}
            for c in range(steps):
                if c + 1 < steps:
                    for cp in stores.pop(c - 1, ()):
                        cp.wait()
                    fetches[c + 1] = fetch(c + 1)
                fetches.pop(c).wait()
                stores[c] = store(c)
            for pending in stores.values():
                for cp in pending:
                    cp.wait()

    return gather(table, idx.reshape(n // SC_INDEX_WINDOW, SC_INDEX_WINDOW))


FINISH_SLOTS = 3


def _finish_kernel(rows_ref, tg_ref, x1_ref, gfin_ref, y_ref, buf, sem, *, tiles, final):
    i = pl.program_id(0)
    groups = buf.shape[2]

    def loads(tile, slot):
        return [pltpu.make_async_copy(rows_ref.at[pl.ds((k * tiles + tile) * groups, groups)], buf.at[slot, k],
                                      sem.at[slot]) for k in range(TOP_K)]

    @pl.when(i == 0)
    def _():
        for t in range(min(FINISH_SLOTS - 1, tiles)):
            for cp in loads(t, t):
                cp.start()

    ahead = i + FINISH_SLOTS - 1

    @pl.when(ahead < tiles)
    def _():
        for cp in loads(ahead, ahead % FINISH_SLOTS):
            cp.start()

    slot = i % FINISH_SLOTS
    for cp in loads(i, slot):
        cp.wait()
    gates = tg_ref[...].T
    x = x1_ref[...]
    for k in range(TOP_K):
        x = x + _load_vreg_tiles(buf.at[slot, k]) * gates[:, k:k + 1]
    y_ref[...] = _rms(x, gfin_ref[...]) if final else x


def _finish(rows, tg, x1, gfin, final, tm=TOKEN_TILE):
    T, D = x1.shape
    tiles = T // tm
    return pl.pallas_call(
        functools.partial(_finish_kernel, tiles=tiles, final=final),
        grid=(tiles,),
        in_specs=[pl.BlockSpec(memory_space=pl.ANY),
                  pl.BlockSpec((SUBLANES, tm), lambda i: (0, i)),
                  pl.BlockSpec((tm, D), lambda i: (i, 0)),
                  _full(gfin.shape)],
        out_specs=pl.BlockSpec((tm, D), lambda i: (i, 0)),
        out_shape=jax.ShapeDtypeStruct((T, D), F32),
        scratch_shapes=[pltpu.VMEM((FINISH_SLOTS, TOP_K) + _vreg_tile_shape(tm, D), F32),
                        pltpu.SemaphoreType.DMA((FINISH_SLOTS,))],
        compiler_params=_cparams(("arbitrary",)),
        name="moe_finish",
    )(rows, tg, x1, gfin)


def _moe(h2, ti, rk, tg, cnt, x1, moe_w, gfin, final):
    T = x1.shape[0]
    bm = EXPERT_ROWS
    n_assign = T * TOP_K
    nb = -(-(n_assign + N_EXPERTS * (bm - 1)) // bm)
    counts = cnt[:, 0].astype(I32)
    padded = (counts + bm - 1) // bm * bm
    pad_end = jnp.cumsum(padded).astype(I32)
    pad_start = pad_end - padded
    experts = jnp.arange(N_EXPERTS, dtype=I32)[:, None, None]
    dest = rk + jnp.sum(jnp.where(ti[None] == experts, pad_start[:, None, None], 0), axis=0)
    nused = (pad_end[-1:] // bm).astype(I32)
    blk_start = jnp.minimum(jnp.arange(nb, dtype=I32), nused[0] - 1) * bm
    blk_e = jnp.minimum(jnp.sum((pad_end[None, :] <= blk_start[:, None]).astype(I32), axis=1), N_EXPERTS - 1)
    live_end = pad_start + counts
    nvalid = jnp.clip(live_end[blk_e] - jnp.arange(nb, dtype=I32) * bm, 0, bm).astype(I32)
    xs = _sc_scatter_rows(h2, dest, nb * bm)
    ys = _experts(xs, blk_e, nused, nvalid, *moe_w)
    rows = _sc_gather_rows(ys, dest.reshape(-1))
    return _finish(rows, tg, x1, gfin, final)


def _split_in_proj(w):
    sizes = (DA_QK_WIDTH, DA_QK_WIDTH, DA_WIDTH, GLA_KEY_WIDTH, GLA_KEY_WIDTH, GLA_WIDTH, GLA_WIDTH,
             GLA_GATE_RANK, GLA_GATE_RANK, D_MODEL, D_MODEL)
    assert w.shape[-1] == sum(sizes)
    out, o = [], 0
    for s in sizes:
        out.append(w[:, o:o + s])
        o += s
    return out


def _prep_layer(l, norm_mix_g, w_in, lambda_q1, lambda_k1, lambda_q2, lambda_k2, diff_subln_g,
                gla_gate_up_f, gla_gate_bias_f, gla_gate_up_b, gla_gate_bias_b, gla_norm_g,
                w_proj_a, w_proj_b, w_out, norm_ffn_g, router_w, router_b, moe_w1, moe_b1, moe_w2, moe_b2):
    dq, dk, dv, lq, lk, lv, lr, laf, lab, ga, gb = _split_in_proj(w_in[l])
    bf = lambda a: a.astype(BF16)
    r = GLA_GATE_RANK
    gup = jnp.zeros((LANES, 2 * GLA_KEY_WIDTH), F32)
    gup = gup.at[0:r, 0:GLA_KEY_WIDTH].set(gla_gate_up_f[l]).at[r:2 * r, GLA_KEY_WIDTH:].set(gla_gate_up_b[l])
    lamv = jnp.zeros((SUBLANES, LANES), F32)
    for i, v in enumerate((lambda_q1, lambda_k1, lambda_q2, lambda_k2)):
        lamv = lamv.at[i, 0:DA_HEAD_DIM].set(v[l].astype(F32))
    inproj = dict(
        norm_g=norm_mix_g[l][None, :],
        wqk=bf(jnp.concatenate([dq * (DA_HEAD_DIM ** -0.5), dk], 1)),
        wvt=bf(dv.T),
        wlqk=bf(jnp.concatenate([lq, lk], 1)),
        wlvr=bf(jnp.concatenate([lv, lr], 1)),
        wg=bf(jnp.concatenate([ga, gb], 1)),
        wlow=bf(jnp.concatenate([laf, lab, jnp.zeros((D_MODEL, LANES - 2 * r), F32)], 1)),
        gup=bf(gup),
        gbias=jnp.concatenate([gla_gate_bias_f[l], gla_gate_bias_b[l]])[None, :],
    )
    merge = dict(
        subg=diff_subln_g[l][None, :], glag=gla_norm_g[l][None, :],
        wpa=bf(w_proj_a[l]), wpb=bf(w_proj_b[l]), wout=bf(w_out[l]),
        nffn=norm_ffn_g[l][None, :], rwt=bf(router_w[l].T), rb=router_b[l][:, None],
    )
    b1 = moe_b1[l].reshape(N_EXPERTS, -1, LANES, 2).transpose(0, 1, 3, 2).reshape(N_EXPERTS, 1, -1)
    moe_w = (moe_w1[l], b1, moe_w2[l], moe_b2[l][:, None, :])
    return inproj, lamv, merge, moe_w


def _trunk(x, layers, norm_final_g):
    B, L, D = x.shape
    T = B * L
    cos_t, sin_t = _rope_tables(L)
    x2 = x.reshape(T, D)
    gfin = norm_final_g[None, :]
    for l, (inproj, lamv, merge, moe_w) in enumerate(layers):
        lam_init = 0.8 - 0.6 * math.exp(-0.3 * l)
        qk, vt, lqk, lvr, gfb, sg = _inproj(x2, L, cos_t=cos_t, sin_t=sin_t, **inproj)
        oa = _diff_attention(qk.reshape(B, L, -1), vt, lamv, lam_init)
        of, ob = _gla(lqk.reshape(B, L, -1), lvr.reshape(B, L, -1), gfb.reshape(B, L, -1))
        x1, h2, ti, rk, tg, cnt = _merge(x2, oa.reshape(T, -1), of.reshape(T, -1), ob.reshape(T, -1),
                                         lvr, sg, lam_init=lam_init, **merge)
        x2 = _moe(h2, ti, rk, tg, cnt, x1, moe_w, gfin, final=(l == len(layers) - 1))
    return x2.reshape(B, L, D)


def kernel(x_prompt, x_sample, norm_mix_g, w_in, lambda_q1, lambda_k1, lambda_q2, lambda_k2, diff_subln_g,
           gla_gate_up_f, gla_gate_bias_f, gla_gate_up_b, gla_gate_bias_b, gla_norm_g, w_proj_a, w_proj_b,
           w_out, norm_ffn_g, router_w, router_b, moe_w1, moe_b1, moe_w2, moe_b2, norm_final_g):
    depth = w_in.shape[0]
    layers = [_prep_layer(l, norm_mix_g, w_in, lambda_q1, lambda_k1, lambda_q2, lambda_k2, diff_subln_g,
                          gla_gate_up_f, gla_gate_bias_f, gla_gate_up_b, gla_gate_bias_b, gla_norm_g,
                          w_proj_a, w_proj_b, w_out, norm_ffn_g, router_w, router_b,
                          moe_w1, moe_b1, moe_w2, moe_b2) for l in range(depth)]
    return _trunk(x_prompt, layers, norm_final_g), _trunk(x_sample, layers, norm_final_g)
```

```python
import functools
import math

import jax
import jax.numpy as jnp
from jax import lax
from jax.experimental import pallas as pl
from jax.experimental.pallas import tpu as pltpu
from jax.experimental.pallas import tpu_sc as plsc

F32 = jnp.float32
BF16 = jnp.bfloat16
I32 = jnp.int32

D_MODEL = 1024
DA_HEADS = 4
DA_HEAD_DIM = 64
DA_VDIM = 128
DA_QK_WIDTH = DA_HEADS * 2 * DA_HEAD_DIM
DA_WIDTH = DA_HEADS * DA_VDIM
ROPE_THETA = 10000.0
GLA_HEADS = 4
GLA_KDIM = 64
GLA_VDIM = 128
GLA_KEY_WIDTH = GLA_HEADS * GLA_KDIM
GLA_WIDTH = GLA_HEADS * GLA_VDIM
GLA_GATE_RANK = 16
GLA_GATE_NORMALIZER = 16.0
GLA_CHUNK = 64
N_EXPERTS = 32
TOP_K = 4
SWIGLU_ALPHA = 1.702
SWIGLU_LIMIT = 7.0
NORM_EPS = 1e-5
LANES = 128
SUBLANES = 8

BF16_SUBLANES = 16
MXU_TILE = 256
LOG2_E = 1.4426950408889634
VT_ROWS = DA_VDIM + BF16_SUBLANES

VMEM_LIMIT = 56 * 1024 * 1024

TOKEN_TILE = 512
ATTN_KEY_CHUNK = 512
EXPERT_ROWS = 512

NT_DIMS = (((1,), (1,)), ((), ()))
TN_DIMS = (((0,), (0,)), ((), ()))


def _cparams(sem):
    return pltpu.CompilerParams(dimension_semantics=sem, vmem_limit_bytes=VMEM_LIMIT)


def _full(shape):
    nd = len(shape)
    return pl.BlockSpec(shape, lambda *_: (0,) * nd)


def _rms(x, g):
    return x * lax.rsqrt(jnp.mean(x * x, axis=-1, keepdims=True) + NORM_EPS) * g


ROW_TILE = (SUBLANES, LANES)


def _relayout_copies(flat_ref, tiled_ref, sem, to_tiled):
    cps = []
    for c in range(SUBLANES):
        flat = flat_ref.at[:, pl.ds(c * LANES, LANES)]
        tiled = tiled_ref.at[:, c, :]
        cps.append(pltpu.make_async_copy(flat, tiled, sem) if to_tiled else pltpu.make_async_copy(tiled, flat, sem))
    return cps


def _log_sigmoid(x):
    return jnp.minimum(x, 0.0) - jnp.log1p(jnp.exp(-jnp.abs(x)))


def _inproj_kernel(x_ref, g_ref, cos_ref, sin_ref, wqk_ref, wvt_ref, wlqk_ref, wlvr_ref,
                   wg_ref, wlow_ref, gup_ref, gbias_ref,
                   qk_ref, vt_ref, lqk_ref, lvr_ref, gfb_ref, sg_ref):
    h = _rms(x_ref[...], g_ref[...]).astype(BF16)

    cos = cos_ref[...]
    sin_signed = sin_ref[...]
    first_half = (lax.broadcasted_iota(I32, (1, LANES), 1) % DA_HEAD_DIM) < (DA_HEAD_DIM // 2)
    for c in range(2 * DA_QK_WIDTH // LANES):
        z = jnp.dot(h, wqk_ref[:, c * LANES:(c + 1) * LANES], preferred_element_type=F32)
        if c < DA_QK_WIDTH // LANES:
            z = z * LOG2_E
        partner = jnp.where(first_half, pltpu.roll(z, LANES - DA_HEAD_DIM // 2, 1),
                            pltpu.roll(z, DA_HEAD_DIM // 2, 1))
        qk_ref[:, c * LANES:(c + 1) * LANES] = (z * cos + partner * sin_signed).astype(BF16)

    vt = lax.dot_general(wvt_ref[...], h, NT_DIMS, preferred_element_type=F32).astype(BF16)
    for hd in range(DA_HEADS):
        vt_ref[0, hd * VT_ROWS:hd * VT_ROWS + DA_VDIM, :] = vt[hd * DA_VDIM:(hd + 1) * DA_VDIM]
        vt_ref[0, hd * VT_ROWS + DA_VDIM:(hd + 1) * VT_ROWS, :] = jnp.ones((VT_ROWS - DA_VDIM, vt.shape[1]), BF16)
    lqk_ref[...] = jnp.dot(h, wlqk_ref[...], preferred_element_type=F32)
    lvr_ref[...] = jnp.dot(h, wlvr_ref[...], preferred_element_type=F32).astype(BF16)
    sg_ref[...] = jax.nn.sigmoid(jnp.dot(h, wg_ref[...], preferred_element_type=F32)).astype(BF16)
    low = jnp.dot(h, wlow_ref[...], preferred_element_type=F32).astype(BF16)
    pre = jnp.dot(low, gup_ref[...], preferred_element_type=F32) + gbias_ref[...]
    gfb_ref[...] = _log_sigmoid(pre) * (1.0 / GLA_GATE_NORMALIZER)


def _inproj(x2, seq_len, norm_g, wqk, wvt, wlqk, wlvr, wg, wlow, gup, gbias, cos_t, sin_t):
    T, D = x2.shape
    tm = TOKEN_TILE
    assert T % tm == 0 and seq_len % tm == 0
    nl = seq_len // tm
    row = lambda n: pl.BlockSpec((tm, n), lambda i: (i, 0))
    out_shape = (
        jax.ShapeDtypeStruct((T, 2 * DA_QK_WIDTH), BF16),
        jax.ShapeDtypeStruct((T // tm, DA_HEADS * VT_ROWS, tm), BF16),
        jax.ShapeDtypeStruct((T, 2 * GLA_KEY_WIDTH), F32),
        jax.ShapeDtypeStruct((T, 2 * GLA_WIDTH), BF16),
        jax.ShapeDtypeStruct((T, 2 * GLA_KEY_WIDTH), F32),
        jax.ShapeDtypeStruct((T, 2 * D_MODEL), BF16),
    )
    return pl.pallas_call(
        _inproj_kernel,
        grid=(T // tm,),
        in_specs=[row(D), _full(norm_g.shape),
                  pl.BlockSpec((tm, LANES), lambda i: (i % nl, 0)),
                  pl.BlockSpec((tm, LANES), lambda i: (i % nl, 0)),
                  _full(wqk.shape), _full(wvt.shape), _full(wlqk.shape), _full(wlvr.shape),
                  _full(wg.shape), _full(wlow.shape), _full(gup.shape), _full(gbias.shape)],
        out_specs=(row(2 * DA_QK_WIDTH), pl.BlockSpec((1, DA_HEADS * VT_ROWS, tm), lambda i: (i, 0, 0)),
                   row(2 * GLA_KEY_WIDTH), row(2 * GLA_WIDTH), row(2 * GLA_KEY_WIDTH), row(2 * D_MODEL)),
        out_shape=out_shape,
        compiler_params=_cparams(("parallel",)),
        name="inproj",
    )(x2, norm_g, cos_t, sin_t, wqk, wvt, wlqk, wlvr, wg, wlow, gup, gbias)


def _rope_tables(seq_len):
    d = DA_HEAD_DIM
    inv = ROPE_THETA ** (-jnp.arange(0, d, 2, dtype=F32) / d)
    ang = jnp.arange(seq_len, dtype=F32)[:, None] * inv[None, :]
    cos = jnp.concatenate([jnp.cos(ang)] * (2 * LANES // d), -1)
    sin = jnp.sin(ang)
    sin_signed = jnp.concatenate([-sin, sin] * (LANES // d), -1)
    return cos, sin_signed


def _attn_kernel(lamv_ref, q_ref, k_ref, vt_ref, o_ref, acc_sc, sa_sc, sb_sc, *, lam_init):
    n_vt, _, vt_w = vt_ref.shape
    tk = ATTN_KEY_CHUNK
    nk = n_vt * vt_w // tk
    tq = q_ref.shape[1]
    q = q_ref[0]
    lane = lax.broadcasted_iota(I32, (1, LANES), 1)
    zeros = jnp.zeros_like(q)
    qms = (jnp.where(lane < DA_HEAD_DIM, q, zeros), jnp.where(lane >= DA_HEAD_DIM, q, zeros))
    acc_sc[...] = jnp.zeros(acc_sc.shape, F32)

    def scores(j, dst_ref):
        kj = k_ref[0, j * tk:(j + 1) * tk, :]
        col_max = []
        for mp in range(2):
            st = lax.dot_general(kj, qms[mp], NT_DIMS, preferred_element_type=F32)
            dst_ref[mp] = st
            col_max.append(jnp.max(st, axis=0, keepdims=True))
        return col_max

    def softmax_pv(j, src_ref, ms, col_max):
        def v_tile(r):
            key = j * tk + r
            return vt_ref[key // vt_w, :, key % vt_w:key % vt_w + MXU_TILE]

        new = []
        for mp in range(2):
            m_new = jnp.maximum(ms[mp], col_max[mp])
            alpha = jnp.exp2(ms[mp] - m_new)
            for cols in (slice(c, c + MXU_TILE) for c in range(0, tq, MXU_TILE)):
                part = None
                for r in range(0, tk, MXU_TILE):
                    p = jnp.exp2((src_ref[mp, r:r + MXU_TILE, cols] - m_new[:, cols]).astype(BF16))
                    d = jnp.dot(v_tile(r), p, preferred_element_type=F32)
                    part = d if part is None else part + d
                acc_sc[mp, :, cols] = acc_sc[mp, :, cols] * alpha[:, cols] + part
            new.append(m_new)
        return tuple(new)

    bufs = (sa_sc, sb_sc)
    col_max = scores(0, bufs[0])
    ms = tuple(jnp.full((1, tq), -jnp.inf, F32) for _ in range(2))
    for j in range(nk):
        nxt_max = scores(j + 1, bufs[(j + 1) % 2]) if j + 1 < nk else None
        ms = softmax_pv(j, bufs[j % 2], ms, col_max)
        col_max = nxt_max

    lv = lamv_ref[...]
    lam = (jnp.exp(jnp.sum(lv[0:1] * lv[1:2], axis=-1, keepdims=True))
           - jnp.exp(jnp.sum(lv[2:3] * lv[3:4], axis=-1, keepdims=True)) + lam_init)
    a0, a1 = acc_sc[0], acc_sc[1]
    o_ref[0] = (a0[:DA_VDIM] / a0[DA_VDIM:DA_VDIM + 1] - lam * (a1[:DA_VDIM] / a1[DA_VDIM:DA_VDIM + 1])).T


def _diff_attention(qk3, vt3, lamv, lam_init, tq=1024):
    B, L, _ = qk3.shape
    vt_w = vt3.shape[-1]
    tk = ATTN_KEY_CHUNK
    assert L % tk == 0 and tk % vt_w == 0
    return pl.pallas_call(
        functools.partial(_attn_kernel, lam_init=lam_init),
        grid=(B, DA_HEADS, L // tq),
        in_specs=[_full(lamv.shape),
                  pl.BlockSpec((1, tq, LANES), lambda b, h, i: (b, i, h)),
                  pl.BlockSpec((1, L, LANES), lambda b, h, i: (b, 0, DA_HEADS + h)),
                  pl.BlockSpec((L // vt_w, VT_ROWS, vt_w), lambda b, h, i: (b, h, 0))],
        out_specs=pl.BlockSpec((1, tq, DA_VDIM), lambda b, h, i: (b, i, h)),
        out_shape=jax.ShapeDtypeStruct((B, L, DA_WIDTH), F32),
        scratch_shapes=[pltpu.VMEM((2, VT_ROWS, tq), F32), pltpu.VMEM((2, tk, tq), F32),
                        pltpu.VMEM((2, tk, tq), F32)],
        compiler_params=_cparams(("parallel", "parallel", "arbitrary")),
        name="diff_attention",
    )(lamv, qk3, qk3, vt3)


def _split3_bf16(x):
    hi = x.astype(BF16)
    r1 = x - hi.astype(F32)
    mid = r1.astype(BF16)
    return hi, mid, (r1 - mid.astype(F32)).astype(BF16)


def _gla_local(qk_ref, vr_ref, g_ref, bi, *, reverse):
    C = GLA_CHUNK
    tb = qk_ref.shape[1]
    nchunk = tb // C
    row = lax.broadcasted_iota(I32, (tb, tb), 0)
    col = lax.broadcasted_iota(I32, (tb, tb), 1)
    same_chunk = (row // C) == (col // C)
    tri = same_chunk & ((col >= row) if reverse else (col <= row))
    tri_b = jnp.where(tri, 1.0, 0.0).astype(BF16)
    lane = lax.broadcasted_iota(I32, (1, LANES), 1)
    goff = GLA_KEY_WIDTH if reverse else 0

    g = g_ref[bi, :, goff:goff + GLA_KEY_WIDTH]
    q = qk_ref[bi, :, 0:GLA_KEY_WIDTH] * (GLA_KDIM ** -0.5)
    k = qk_ref[bi, :, GLA_KEY_WIDTH:2 * GLA_KEY_WIDTH]
    v = vr_ref[bi]
    b = sum(jnp.dot(tri_b, part, preferred_element_type=F32) for part in _split3_bf16(g))
    last_row, ref_row = (0, C // 2) if reverse else (C - 1, C // 2 - 1)
    per_chunk = lambda r: jnp.concatenate(
        [jnp.broadcast_to(b[c * C + r:c * C + r + 1], (C, GLA_KEY_WIDTH)) for c in range(nchunk)], axis=0)
    b_last, b_ref = per_chunk(last_row), per_chunk(ref_row)
    q_in = q * jnp.exp(b - b_ref)
    k_in = (k * jnp.exp(b_ref - b)).astype(BF16)
    k_dec = k * jnp.exp(b_last - b)
    q_dec = q * jnp.exp(b)
    dec = [jnp.exp(b[c * C + last_row:c * C + last_row + 1]) for c in range(nchunk)]

    heads = []
    for h in range(GLA_HEADS):
        ps = slice((h // 2) * LANES, (h // 2 + 1) * LANES)
        keep = (lane < GLA_KDIM) if h % 2 == 0 else (lane >= GLA_KDIM)
        zero = jnp.zeros((tb, LANES), F32)
        heads.append(dict(
            ps=ps, vs=slice(h * GLA_VDIM, (h + 1) * GLA_VDIM),
            q_in=jnp.where(keep, q_in[:, ps], zero).astype(BF16),
            q_dec=jnp.where(keep, q_dec[:, ps], zero).astype(BF16),
            k_dec=jnp.where(keep, k_dec[:, ps], zero).astype(BF16)))
    scores = [lax.dot_general(hd["q_in"], k_in[:, hd["ps"]], NT_DIMS, preferred_element_type=F32) for hd in heads]
    kv_t = [[lax.dot_general(v[c * C:(c + 1) * C, hd["vs"]], hd["k_dec"][c * C:(c + 1) * C], TN_DIMS,
                             preferred_element_type=F32) for hd in heads] for c in range(nchunk)]
    intra = [jnp.dot(jnp.where(tri, s, 0.0).astype(BF16), v[:, hd["vs"]], preferred_element_type=F32)
             for s, hd in zip(scores, heads)]
    return heads, dec, kv_t, intra


def _gla_scan(heads, dec, kv_t, st_ref, bi, *, reverse):
    nchunk = len(dec)
    state = [st_ref[bi, h] for h in range(GLA_HEADS)]
    entering = [None] * nchunk
    for c in (range(nchunk - 1, -1, -1) if reverse else range(nchunk)):
        entering[c] = [s.astype(BF16) for s in state]
        state = [s * dec[c][:, hd["ps"]] + kv_t[c][h] for h, (s, hd) in enumerate(zip(state, heads))]
    for h in range(GLA_HEADS):
        st_ref[bi, h] = state[h]
    return entering


def _gla_emit(heads, intra, entering, o_ref, bi):
    C = GLA_CHUNK
    for c, states in enumerate(entering):
        rows = slice(c * C, (c + 1) * C)
        for hd, o_intra, s_t in zip(heads, intra, states):
            inter = lax.dot_general(hd["q_dec"][rows], s_t, NT_DIMS, preferred_element_type=F32)
            o_ref[bi, rows, hd["vs"]] = o_intra[rows] + inter


def _gla_kernel(qkf_ref, vrf_ref, gf_ref, qkb_ref, vrb_ref, gb_ref, of_ref, ob_ref, sf_sc, sb_sc):
    @pl.when(pl.program_id(1) == 0)
    def _():
        sf_sc[...] = jnp.zeros(sf_sc.shape, F32)
        sb_sc[...] = jnp.zeros(sb_sc.shape, F32)

    rows = range(qkf_ref.shape[0])
    local_f = [_gla_local(qkf_ref, vrf_ref, gf_ref, bi, reverse=False) for bi in rows]
    local_b = [_gla_local(qkb_ref, vrb_ref, gb_ref, bi, reverse=True) for bi in rows]
    enter_f = [_gla_scan(h, d, kv, sf_sc, bi, reverse=False) for bi, (h, d, kv, _) in zip(rows, local_f)]
    enter_b = [_gla_scan(h, d, kv, sb_sc, bi, reverse=True) for bi, (h, d, kv, _) in zip(rows, local_b)]
    for bi in rows:
        _gla_emit(local_f[bi][0], local_f[bi][3], enter_f[bi], of_ref, bi)
        _gla_emit(local_b[bi][0], local_b[bi][3], enter_b[bi], ob_ref, bi)


def _gla(lqk3, lvr3, gfb3, tb=256, rows=2):
    B, L, _ = lqk3.shape
    nb = L // tb
    assert B % rows == 0
    fwd = lambda n: pl.BlockSpec((rows, tb, n), lambda b, i: (b, i, 0))
    bwd = lambda n: pl.BlockSpec((rows, tb, n), lambda b, i: (b, nb - 1 - i, 0))
    out = jax.ShapeDtypeStruct((B, L, GLA_WIDTH), F32)
    return pl.pallas_call(
        _gla_kernel,
        grid=(B // rows, nb),
        in_specs=[fwd(2 * GLA_KEY_WIDTH), fwd(GLA_WIDTH), fwd(2 * GLA_KEY_WIDTH),
                  bwd(2 * GLA_KEY_WIDTH), bwd(GLA_WIDTH), bwd(2 * GLA_KEY_WIDTH)],
        out_specs=(fwd(GLA_WIDTH), bwd(GLA_WIDTH)),
        out_shape=(out, out),
        scratch_shapes=[pltpu.VMEM((rows, GLA_HEADS, GLA_VDIM, LANES), F32),
                        pltpu.VMEM((rows, GLA_HEADS, GLA_VDIM, LANES), F32)],
        compiler_params=_cparams(("parallel", "arbitrary")),
        name="gla",
    )(lqk3, lvr3, gfb3, lqk3, lvr3, gfb3)


def _head_norm(z, g):
    return jnp.concatenate(
        [_rms(z[:, h * LANES:(h + 1) * LANES], g) for h in range(z.shape[1] // LANES)], axis=1)


def _merge_kernel(x_ref, oa_ref, of_ref, ob_ref, r_ref, sg_ref, subg_ref, glag_ref, wpa_ref, wpb_ref,
                  wout_ref, nffn_ref, rwt_ref, rb_ref,
                  x1_ref, h2_ref, ti_ref, rk_ref, tg_ref, cnt_ref, carry_sc, h2_sc, h2_sem, *, lam_init):
    tm = x_ref.shape[0]
    step = pl.program_id(0)

    def h2_stores(tile):
        rows = h2_ref.at[pl.ds(pl.multiple_of(tile * tm, tm), tm)]
        return _relayout_copies(h2_sc, rows, h2_sem, to_tiled=True)

    @pl.when(step == 0)
    def _():
        carry_sc[...] = jnp.zeros(carry_sc.shape, F32)

    @pl.when(step > 0)
    def _():
        for cp in h2_stores(step - 1):
            cp.wait()

    oa = (_head_norm(oa_ref[...], subg_ref[...]) * (1.0 - lam_init)).astype(BF16)
    ob = _head_norm(of_ref[...] + ob_ref[...], glag_ref[...]) * jax.nn.silu(r_ref[...].astype(F32))
    pa = jnp.dot(oa, wpa_ref[...], preferred_element_type=F32)
    pb = jnp.dot(ob.astype(BF16), wpb_ref[...], preferred_element_type=F32)
    sg = sg_ref[...].astype(F32)
    merged = sg[:, :D_MODEL] * pa + sg[:, D_MODEL:] * pb
    x1 = x_ref[...] + jnp.dot(merged.astype(BF16), wout_ref[...], preferred_element_type=F32)
    x1_ref[...] = x1
    h2 = _rms(x1, nffn_ref[...])
    h2_sc[...] = h2

    logits = lax.dot_general(rwt_ref[...], h2.astype(BF16), NT_DIMS, preferred_element_type=F32) + rb_ref[...]
    eidx = lax.broadcasted_iota(I32, logits.shape, 0)
    vals, sels = [], []
    for k in range(TOP_K):
        mk = jnp.max(logits, axis=0, keepdims=True)
        ik = jnp.min(jnp.where(logits == mk, eidx, N_EXPERTS), axis=0, keepdims=True)
        sel = eidx == ik
        logits = jnp.where(sel, -jnp.inf, logits)
        vals.append(mk)
        sels.append(sel)
        ti_ref[k:k + 1, :] = ik
    ex = [jnp.exp(v - vals[0]) for v in vals]
    denom = ex[0] + ex[1] + ex[2] + ex[3]
    for k in range(TOP_K):
        tg_ref[k:k + 1, :] = ex[k] / denom
    tg_ref[TOP_K:, :] = jnp.zeros((tg_ref.shape[0] - TOP_K, tm), F32)

    multi = (sels[0] | sels[1] | sels[2] | sels[3])
    multi_f = jnp.where(multi, 1.0, 0.0)
    before = lax.broadcasted_iota(I32, (tm, tm), 0) < lax.broadcasted_iota(I32, (tm, tm), 1)
    cum = jnp.dot(multi_f.astype(BF16), jnp.where(before, 1.0, 0.0).astype(BF16), preferred_element_type=F32)
    tot = carry_sc[:, 0:1] + cum
    for k in range(TOP_K):
        rk_ref[k:k + 1, :] = jnp.sum(jnp.where(sels[k], tot, 0.0), axis=0, keepdims=True).astype(I32)
    carry_sc[...] = carry_sc[...] + jnp.sum(multi_f, axis=1, keepdims=True)
    cnt_ref[...] = carry_sc[...]

    for cp in h2_stores(step):
        cp.start()

    @pl.when(step == pl.num_programs(0) - 1)
    def _():
        for cp in h2_stores(step):
            cp.wait()


def _merge(x2, oa, of, ob, lvr, sg, subg, glag, wpa, wpb, wout, nffn, rwt, rb, lam_init):
    T, D = x2.shape
    tm = TOKEN_TILE
    row = lambda n: pl.BlockSpec((tm, n), lambda i: (i, 0))
    col = lambda n: pl.BlockSpec((n, tm), lambda i: (0, i))
    out_shape = (
        jax.ShapeDtypeStruct((T, D), F32),
        jax.ShapeDtypeStruct((T,) + ROW_TILE, F32),
        jax.ShapeDtypeStruct((TOP_K, T), I32),
        jax.ShapeDtypeStruct((TOP_K, T), I32),
        jax.ShapeDtypeStruct((SUBLANES, T), F32),
        jax.ShapeDtypeStruct((N_EXPERTS, LANES), F32),
    )
    return pl.pallas_call(
        functools.partial(_merge_kernel, lam_init=lam_init),
        grid=(T // tm,),
        in_specs=[row(D), row(DA_WIDTH), row(GLA_WIDTH), row(GLA_WIDTH),
                  pl.BlockSpec((tm, GLA_WIDTH), lambda i: (i, 1)), row(2 * D),
                  _full(subg.shape), _full(glag.shape), _full(wpa.shape), _full(wpb.shape),
                  _full(wout.shape), _full(nffn.shape), _full(rwt.shape), _full(rb.shape)],
        out_specs=(row(D), pl.BlockSpec(memory_space=pl.ANY),
                   col(TOP_K), col(TOP_K), col(SUBLANES), _full((N_EXPERTS, LANES))),
        out_shape=out_shape,
        scratch_shapes=[pltpu.VMEM((N_EXPERTS, LANES), F32), pltpu.VMEM((tm, D), F32),
                        pltpu.SemaphoreType.DMA(())],
        compiler_params=_cparams(("arbitrary",)),
        name="merge_router",
    )(x2, oa, of, ob, lvr, sg, subg, glag, wpa, wpb, wout, nffn, rwt, rb)


SC_INDEX_WINDOW = 128
SC_GATHER_ROWS = 16


def _sc_scatter_rows(rows, dest, n_slots):
    T = rows.shape[0]
    info = plsc.get_sparse_core_info()
    workers = info.num_cores * info.num_subcores
    per_worker = T // workers
    assert T % workers == 0 and per_worker % SC_INDEX_WINDOW == 0
    windows = T // SC_INDEX_WINDOW
    steps = SC_INDEX_WINDOW // SC_GATHER_ROWS
    mesh = plsc.VectorSubcoreMesh(core_axis_name="c", subcore_axis_name="s")

    @functools.partial(
        pl.kernel, mesh=mesh,
        out_type=jax.ShapeDtypeStruct((n_slots,) + rows.shape[1:], rows.dtype),
        scratch_types=[pltpu.VMEM((steps, SC_GATHER_ROWS), I32)] * TOP_K
        + [pltpu.VMEM((SC_GATHER_ROWS,) + rows.shape[1:], rows.dtype)] * 2 + [pltpu.SemaphoreType.DMA] * 4)
    def scatter(rows_hbm, dest_hbm, out_hbm, *scratch):
        idx_v, bufs = scratch[:TOP_K], scratch[TOP_K:TOP_K + 2]
        load_sem, send_sem = scratch[TOP_K + 2:TOP_K + 4], scratch[TOP_K + 4:TOP_K + 6]
        worker = lax.axis_index("s") * info.num_cores + lax.axis_index("c")
        first = worker * (per_worker // SC_INDEX_WINDOW)

        @pl.loop(0, per_worker // SC_INDEX_WINDOW)
        def _(j):
            window = first + j
            for k in range(TOP_K):
                pltpu.sync_copy(dest_hbm.at[k * windows + window], idx_v[k])

            def load(c):
                src = rows_hbm.at[pl.ds(window * SC_INDEX_WINDOW + c * SC_GATHER_ROWS, SC_GATHER_ROWS)]
                return pltpu.async_copy(src, bufs[c % 2], load_sem[c % 2])

            loads = {0: load(0)}
            sends = {}
            for c in range(steps):
                if c + 1 < steps:
                    for cp in sends.pop(c - 1, ()):
                        cp.wait()
                    loads[c + 1] = load(c + 1)
                loads.pop(c).wait()
                sends[c] = [pltpu.async_copy(bufs[c % 2], out_hbm.at[idx_v[k].at[c]], send_sem[c % 2])
                            for k in range(TOP_K)]
            for pending in sends.values():
                for cp in pending:
                    cp.wait()

    return scatter(rows, dest.reshape(TOP_K * windows, steps, SC_GATHER_ROWS))


def _deinterleave_bf16(w_ref, o_ref):
    n = 2 * LANES
    src = lax.broadcasted_iota(I32, (n, n), 0)
    dst = lax.broadcasted_iota(I32, (n, n), 1)
    perm = jnp.where(src == jnp.where(dst < LANES, 2 * dst, 2 * (dst - LANES) + 1), 1.0, 0.0).astype(BF16)
    for c in range(w_ref.shape[1] // n):
        w = w_ref[:, c * n:(c + 1) * n].astype(BF16)
        o_ref[:, c * n:(c + 1) * n] = jnp.dot(w, perm, preferred_element_type=F32).astype(BF16)


def _expert_kernel(blk_ref, nused_ref, nvalid_ref, xs_ref, w1f_ref, b1_ref, w2f_ref, b2_ref, ys_ref,
                   xbuf, ybuf, w1_ref, w2_ref, in_sem, out_sem):
    i = pl.program_id(0)
    n = pl.num_programs(0)
    bm = xbuf.shape[1]
    slot = i % 2

    @pl.when((i == 0) | (blk_ref[i] != blk_ref[jnp.maximum(i - 1, 0)]))
    def _():
        _deinterleave_bf16(w1f_ref.at[0], w1_ref)
        w2_ref[...] = w2f_ref[0].astype(BF16)

    def rows_of(ref, blk):
        return ref.at[pl.ds(pl.multiple_of(blk * bm, bm), bm)]

    def loads(blk, s):
        return _relayout_copies(xbuf.at[s], rows_of(xs_ref, blk), in_sem.at[s], to_tiled=False)

    def stores(blk, s):
        return _relayout_copies(ybuf.at[s], rows_of(ys_ref, blk), out_sem.at[s], to_tiled=True)

    @pl.when(i == 0)
    def _():
        for cp in loads(0, 0):
            cp.start()

    @pl.when(i + 1 < n)
    def _():
        for cp in loads(i + 1, 1 - slot):
            cp.start()

    for cp in loads(i, slot):
        cp.wait()

    @pl.when(i >= 2)
    def _():
        for cp in stores(i - 2, slot):
            cp.wait()

    @pl.when(i < nused_ref[0])
    def _():
        live = lax.broadcasted_iota(I32, (bm, 1), 0) < nvalid_ref[i]
        x = jnp.where(live, xbuf[slot], 0.0).astype(BF16)
        step = 2 * MXU_TILE

        def hidden(c):
            cols = slice(c * step, (c + 1) * step)
            return jnp.dot(x, w1_ref[:, cols], preferred_element_type=F32) + b1_ref[0, :, cols]

        def down(c, hid):
            acts = []
            for o in range(0, step, 2 * LANES):
                gate = jnp.minimum(hid[:, o:o + LANES], SWIGLU_LIMIT)
                up = jnp.clip(hid[:, o + LANES:o + 2 * LANES], -SWIGLU_LIMIT, SWIGLU_LIMIT)
                acts.append(((up + 1.0) * (gate * jax.nn.sigmoid(SWIGLU_ALPHA * gate))).astype(BF16))
            rows = slice(c * MXU_TILE, (c + 1) * MXU_TILE)
            return jnp.dot(jnp.concatenate(acts, axis=1), w2_ref[rows, :], preferred_element_type=F32)

        n_piece = w1_ref.shape[1] // step
        y = b2_ref[0]
        hid = hidden(0)
        for c in range(n_piece):
            nxt = hidden(c + 1) if c + 1 < n_piece else None
            y = y + down(c, hid)
            hid = nxt
        ybuf[slot] = y

    @pl.when(i >= nused_ref[0])
    def _():
        ybuf[slot] = jnp.zeros(ybuf.shape[1:], F32)

    for cp in stores(i, slot):
        cp.start()

    @pl.when(i == n - 1)
    def _():
        for cp in stores(i, slot):
            cp.wait()

    @pl.when((i == n - 1) & (n >= 2))
    def _():
        for cp in stores(i - 1, 1 - slot):
            cp.wait()


def _experts(xs, blk_e, nused, nvalid, w1, b1, w2, b2):
    P = xs.shape[0]
    D = w1.shape[1]
    bm = EXPERT_ROWS
    nb = P // bm
    per_expert = lambda a: pl.BlockSpec((1,) + a.shape[1:], lambda i, blk, nu, nv: (blk[i], 0, 0))
    return pl.pallas_call(
        _expert_kernel,
        grid_spec=pltpu.PrefetchScalarGridSpec(
            num_scalar_prefetch=3,
            grid=(nb,),
            in_specs=[pl.BlockSpec(memory_space=pl.ANY), per_expert(w1), per_expert(b1), per_expert(w2),
                      per_expert(b2)],
            out_specs=pl.BlockSpec(memory_space=pl.ANY),
            scratch_shapes=[pltpu.VMEM((2, bm, D), F32), pltpu.VMEM((2, bm, D), F32),
                            pltpu.VMEM(w1.shape[1:], BF16), pltpu.VMEM(w2.shape[1:], BF16),
                            pltpu.SemaphoreType.DMA((2,)), pltpu.SemaphoreType.DMA((2,))],
        ),
        out_shape=jax.ShapeDtypeStruct((P,) + ROW_TILE, F32),
        compiler_params=_cparams(("arbitrary",)),
        name="moe_experts",
    )(blk_e, nused, nvalid, xs, w1, b1, w2, b2)


def _vreg_tile_shape(rows, cols):
    return (rows // SUBLANES, cols // LANES, SUBLANES, LANES)


def _load_vreg_tiles(ref):
    rows = ref.shape[0] * SUBLANES
    return jnp.concatenate([ref[:, b, :, :].reshape(rows, LANES) for b in range(ref.shape[1])], axis=1)


def _sc_gather_rows(table, idx):
    n = idx.shape[0]
    cols = table.shape[1] * table.shape[2]
    info = plsc.get_sparse_core_info()
    workers = info.num_cores * info.num_subcores
    per_worker = n // workers
    assert n % workers == 0 and per_worker % SC_INDEX_WINDOW == 0
    mesh = plsc.VectorSubcoreMesh(core_axis_name="c", subcore_axis_name="s")

    @functools.partial(
        pl.kernel, mesh=mesh,
        out_type=jax.ShapeDtypeStruct(_vreg_tile_shape(n, cols), table.dtype),
        scratch_types=[pltpu.VMEM((SC_INDEX_WINDOW,), I32)]
        + [pltpu.VMEM((SC_GATHER_ROWS,) + table.shape[1:], table.dtype)] * 2 + [pltpu.SemaphoreType.DMA] * 4)
    def gather(table_hbm, idx_hbm, out_hbm, idx_v, *scratch):
        bufs, fetch_sem, store_sem = scratch[0:2], scratch[2:4], scratch[4:6]
        steps = SC_INDEX_WINDOW // SC_GATHER_ROWS
        worker = lax.axis_index("s") * info.num_cores + lax.axis_index("c")
        first = worker * (per_worker // SC_INDEX_WINDOW)

        @pl.loop(0, per_worker // SC_INDEX_WINDOW)
        def _(j):
            window = first + j
            pltpu.sync_copy(idx_hbm.at[window], idx_v)

            def fetch(c):
                src = table_hbm.at[idx_v.at[pl.ds(c * SC_GATHER_ROWS, SC_GATHER_ROWS)]]
                return pltpu.async_copy(src, bufs[c % 2], fetch_sem[c % 2])

            def store(c):
                group = (window * SC_INDEX_WINDOW + c * SC_GATHER_ROWS) // SUBLANES
                return [pltpu.async_copy(bufs[c % 2].at[r], out_hbm.at[group + r // SUBLANES, :, r % SUBLANES, :],
                                         store_sem[c % 2]) for r in range(SC_GATHER_ROWS)]

            fetches = {0: fetch(0)}
            stores = {}
            for c in range(steps):
                if c + 1 < steps:
                    for cp in stores.pop(c - 1, ()):
                        cp.wait()
                    fetches[c + 1] = fetch(c + 1)
                fetches.pop(c).wait()
                stores[c] = store(c)
            for pending in stores.values():
                for cp in pending:
                    cp.wait()

    return gather(table, idx.reshape(n // SC_INDEX_WINDOW, SC_INDEX_WINDOW))


FINISH_SLOTS = 3


def _finish_kernel(rows_ref, tg_ref, x1_ref, gfin_ref, y_ref, buf, sem, *, tiles, final):
    i = pl.program_id(0)
    groups = buf.shape[2]

    def loads(tile, slot):
        return [pltpu.make_async_copy(rows_ref.at[pl.ds((k * tiles + tile) * groups, groups)], buf.at[slot, k],
                                      sem.at[slot]) for k in range(TOP_K)]

    @pl.when(i == 0)
    def _():
        for t in range(min(FINISH_SLOTS - 1, tiles)):
            for cp in loads(t, t):
                cp.start()

    ahead = i + FINISH_SLOTS - 1

    @pl.when(ahead < tiles)
    def _():
        for cp in loads(ahead, ahead % FINISH_SLOTS):
            cp.start()

    slot = i % FINISH_SLOTS
    for cp in loads(i, slot):
        cp.wait()
    gates = tg_ref[...].T
    x = x1_ref[...]
    for k in range(TOP_K):
        x = x + _load_vreg_tiles(buf.at[slot, k]) * gates[:, k:k + 1]
    y_ref[...] = _rms(x, gfin_ref[...]) if final else x


def _finish(rows, tg, x1, gfin, final, tm=TOKEN_TILE):
    T, D = x1.shape
    tiles = T // tm
    return pl.pallas_call(
        functools.partial(_finish_kernel, tiles=tiles, final=final),
        grid=(tiles,),
        in_specs=[pl.BlockSpec(memory_space=pl.ANY),
                  pl.BlockSpec((SUBLANES, tm), lambda i: (0, i)),
                  pl.BlockSpec((tm, D), lambda i: (i, 0)),
                  _full(gfin.shape)],
        out_specs=pl.BlockSpec((tm, D), lambda i: (i, 0)),
        out_shape=jax.ShapeDtypeStruct((T, D), F32),
        scratch_shapes=[pltpu.VMEM((FINISH_SLOTS, TOP_K) + _vreg_tile_shape(tm, D), F32),
                        pltpu.SemaphoreType.DMA((FINISH_SLOTS,))],
        compiler_params=_cparams(("arbitrary",)),
        name="moe_finish",
    )(rows, tg, x1, gfin)


def _moe(h2, ti, rk, tg, cnt, x1, moe_w, gfin, final):
    T = x1.shape[0]
    bm = EXPERT_ROWS
    n_assign = T * TOP_K
    nb = -(-(n_assign + N_EXPERTS * (bm - 1)) // bm)
    counts = cnt[:, 0].astype(I32)
    padded = (counts + bm - 1) // bm * bm
    pad_end = jnp.cumsum(padded).astype(I32)
    pad_start = pad_end - padded
    experts = jnp.arange(N_EXPERTS, dtype=I32)[:, None, None]
    dest = rk + jnp.sum(jnp.where(ti[None] == experts, pad_start[:, None, None], 0), axis=0)
    nused = (pad_end[-1:] // bm).astype(I32)
    blk_start = jnp.minimum(jnp.arange(nb, dtype=I32), nused[0] - 1) * bm
    blk_e = jnp.minimum(jnp.sum((pad_end[None, :] <= blk_start[:, None]).astype(I32), axis=1), N_EXPERTS - 1)
    live_end = pad_start + counts
    nvalid = jnp.clip(live_end[blk_e] - jnp.arange(nb, dtype=I32) * bm, 0, bm).astype(I32)
    xs = _sc_scatter_rows(h2, dest, nb * bm)
    ys = _experts(xs, blk_e, nused, nvalid, *moe_w)
    rows = _sc_gather_rows(ys, dest.reshape(-1))
    return _finish(rows, tg, x1, gfin, final)


def _split_in_proj(w):
    sizes = (DA_QK_WIDTH, DA_QK_WIDTH, DA_WIDTH, GLA_KEY_WIDTH, GLA_KEY_WIDTH, GLA_WIDTH, GLA_WIDTH,
             GLA_GATE_RANK, GLA_GATE_RANK, D_MODEL, D_MODEL)
    assert w.shape[-1] == sum(sizes)
    out, o = [], 0
    for s in sizes:
        out.append(w[:, o:o + s])
        o += s
    return out


def _prep_layer(l, norm_mix_g, w_in, lambda_q1, lambda_k1, lambda_q2, lambda_k2, diff_subln_g,
                gla_gate_up_f, gla_gate_bias_f, gla_gate_up_b, gla_gate_bias_b, gla_norm_g,
                w_proj_a, w_proj_b, w_out, norm_ffn_g, router_w, router_b, moe_w1, moe_b1, moe_w2, moe_b2):
    dq, dk, dv, lq, lk, lv, lr, laf, lab, ga, gb = _split_in_proj(w_in[l])
    bf = lambda a: a.astype(BF16)
    r = GLA_GATE_RANK
    gup = jnp.zeros((LANES, 2 * GLA_KEY_WIDTH), F32)
    gup = gup.at[0:r, 0:GLA_KEY_WIDTH].set(gla_gate_up_f[l]).at[r:2 * r, GLA_KEY_WIDTH:].set(gla_gate_up_b[l])
    lamv = jnp.zeros((SUBLANES, LANES), F32)
    for i, v in enumerate((lambda_q1, lambda_k1, lambda_q2, lambda_k2)):
        lamv = lamv.at[i, 0:DA_HEAD_DIM].set(v[l].astype(F32))
    inproj = dict(
        norm_g=norm_mix_g[l][None, :],
        wqk=bf(jnp.concatenate([dq * (DA_HEAD_DIM ** -0.5), dk], 1)),
        wvt=bf(dv.T),
        wlqk=bf(jnp.concatenate([lq, lk], 1)),
        wlvr=bf(jnp.concatenate([lv, lr], 1)),
        wg=bf(jnp.concatenate([ga, gb], 1)),
        wlow=bf(jnp.concatenate([laf, lab, jnp.zeros((D_MODEL, LANES - 2 * r), F32)], 1)),
        gup=bf(gup),
        gbias=jnp.concatenate([gla_gate_bias_f[l], gla_gate_bias_b[l]])[None, :],
    )
    merge = dict(
        subg=diff_subln_g[l][None, :], glag=gla_norm_g[l][None, :],
        wpa=bf(w_proj_a[l]), wpb=bf(w_proj_b[l]), wout=bf(w_out[l]),
        nffn=norm_ffn_g[l][None, :], rwt=bf(router_w[l].T), rb=router_b[l][:, None],
    )
    b1 = moe_b1[l].reshape(N_EXPERTS, -1, LANES, 2).transpose(0, 1, 3, 2).reshape(N_EXPERTS, 1, -1)
    moe_w = (moe_w1[l], b1, moe_w2[l], moe_b2[l][:, None, :])
    return inproj, lamv, merge, moe_w


def _trunk(x, layers, norm_final_g):
    B, L, D = x.shape
    T = B * L
    cos_t, sin_t = _rope_tables(L)
    x2 = x.reshape(T, D)
    gfin = norm_final_g[None, :]
    for l, (inproj, lamv, merge, moe_w) in enumerate(layers):
        lam_init = 0.8 - 0.6 * math.exp(-0.3 * l)
        qk, vt, lqk, lvr, gfb, sg = _inproj(x2, L, cos_t=cos_t, sin_t=sin_t, **inproj)
        oa = _diff_attention(qk.reshape(B, L, -1), vt, lamv, lam_init)
        of, ob = _gla(lqk.reshape(B, L, -1), lvr.reshape(B, L, -1), gfb.reshape(B, L, -1))
        x1, h2, ti, rk, tg, cnt = _merge(x2, oa.reshape(T, -1), of.reshape(T, -1), ob.reshape(T, -1),
                                         lvr, sg, lam_init=lam_init, **merge)
        x2 = _moe(h2, ti, rk, tg, cnt, x1, moe_w, gfin, final=(l == len(layers) - 1))
    return x2.reshape(B, L, D)


def kernel(x_prompt, x_sample, norm_mix_g, w_in, lambda_q1, lambda_k1, lambda_q2, lambda_k2, diff_subln_g,
           gla_gate_up_f, gla_gate_bias_f, gla_gate_up_b, gla_gate_bias_b, gla_norm_g, w_proj_a, w_proj_b,
           w_out, norm_ffn_g, router_w, router_b, moe_w1, moe_b1, moe_w2, moe_b2, norm_final_g):
    depth = w_in.shape[0]
    layers = [_prep_layer(l, norm_mix_g, w_in, lambda_q1, lambda_k1, lambda_q2, lambda_k2, diff_subln_g,
                          gla_gate_up_f, gla_gate_bias_f, gla_gate_up_b, gla_gate_bias_b, gla_norm_g,
                          w_proj_a, w_proj_b, w_out, norm_ffn_g, router_w, router_b,
                          moe_w1, moe_b1, moe_w2, moe_b2) for l in range(depth)]
    return _trunk(x_prompt, layers, norm_final_g), _trunk(x_sample, layers, norm_final_g)
```
